```python
import math
import jax, jax.numpy as jnp
from jax import lax
import numpy as np

D_MODEL = 1024
BATCH = 32
SEQ = 256
DEPTH = 2
DEC_BATCH = 4
DEC_SEQ = 2048
PAST_LEN = 512

GRID_W = 64
MIX_W = D_MODEL
DIFF_W = MIX_W // 4
DIFF_HEADS = 4
DIFF_V = DIFF_W // DIFF_HEADS
DIFF_QK = DIFF_V // 2
GQA_W = MIX_W // 4
GQA_HD = 64
GQA_Q_HEADS = GQA_W // GQA_HD
GQA_KV_HEADS = 2
SSD_INNER = MIX_W - DIFF_W - GQA_W
SSD_HD = 64
SSD_HEADS = SSD_INNER // SSD_HD
SSD_GROUPS = 2
SSD_STATE = 64
SSD_CONV = 5
SSD_CHUNK = 128
XBC_W = SSD_INNER + 2 * SSD_GROUPS * SSD_STATE
Q_BLOCK = 128
D_FF = -(-8 * D_MODEL // (3 * 256)) * 256
ROPE_THETA = 10000.0
EPS = 1e-5
ALPHA = (2 * DEPTH) ** 0.25
BETA = (8 * DEPTH) ** -0.25
PROJ_SIZES = (DIFF_HEADS * 2 * DIFF_QK, DIFF_HEADS * 2 * DIFF_QK, DIFF_W,
              GQA_W, GQA_KV_HEADS * GQA_HD, GQA_KV_HEADS * GQA_HD,
              SSD_INNER, XBC_W, 2 * SSD_HEADS)
IN_W = sum(PROJ_SIZES)

kernel_name = 'hybrid_diffusion_prefix_trunk_step'


def layer_norm(x, g, b):
    xf = x.astype(jnp.float32)
    mu = jnp.mean(xf, -1, keepdims=True)
    var = jnp.mean(jnp.square(xf - mu), -1, keepdims=True)
    return ((xf - mu) * lax.rsqrt(var + EPS) * g + b).astype(x.dtype)


def rms_norm(x, g):
    xf = x.astype(jnp.float32)
    return (xf * lax.rsqrt(jnp.mean(xf * xf, -1, keepdims=True) + EPS) * g).astype(x.dtype)


def axial_rope(rows, dim):
    row = jnp.repeat(jnp.arange(rows), GRID_W).astype(jnp.float32)
    col = jnp.tile(jnp.arange(GRID_W), rows).astype(jnp.float32)
    n_freq = dim // 4
    inv = ROPE_THETA ** (-jnp.arange(n_freq, dtype=jnp.float32) / n_freq)
    ang = jnp.concatenate([row[:, None] * inv, col[:, None] * inv], -1)
    return (jnp.cos(ang), jnp.sin(ang))


def apply_rope(x, cos, sin):
    shape = (1, x.shape[1]) + (1,) * (x.ndim - 3) + (x.shape[-1] // 2,)
    c, s = cos.reshape(shape), sin.reshape(shape)
    xf = x.astype(jnp.float32)
    x1, x2 = xf[..., 0::2], xf[..., 1::2]
    return jnp.stack([x1 * c - x2 * s, x1 * s + x2 * c], -1).reshape(x.shape).astype(x.dtype)


def sweep_query_blocks(fn, q):
    b, t = q.shape[:2]
    nb = t // Q_BLOCK
    qb = jnp.moveaxis(q.reshape((b, nb, Q_BLOCK) + q.shape[2:]), 1, 0)
    ob = lax.map(fn, qb)
    return jnp.moveaxis(ob, 0, 1).reshape((b, t) + ob.shape[3:])


def diff_attention(q, k, v, lam, subln_g, lam_init):
    scale = DIFF_QK ** -0.5
    def block(qb):
        s = jnp.einsum('bqhmd,bkhmd->bhmqk', qb, k).astype(jnp.float32) * scale
        p = jax.nn.softmax(s, axis=-1)
        w = p[:, :, 0] - lam * p[:, :, 1]
        return jnp.einsum('bhqk,bkhe->bqhe', w.astype(v.dtype), v)
    o = sweep_query_blocks(block, q)
    o = rms_norm(o, subln_g) * (1.0 - lam_init)
    return o.reshape(o.shape[:2] + (DIFF_W,))


def gqa_attention(q, k, v):
    b, t = q.shape[:2]
    scale = GQA_HD ** -0.5
    qg = q.reshape(b, t, GQA_KV_HEADS, GQA_Q_HEADS // GQA_KV_HEADS, GQA_HD)
    def block(qb):
        s = jnp.einsum('bqkgd,bskd->bkgqs', qb, k).astype(jnp.float32) * scale
        p = jax.nn.softmax(s, axis=-1).astype(v.dtype)
        return jnp.einsum('bkgqs,bskd->bqkgd', p, v)
    o = sweep_query_blocks(block, qg)
    return o.reshape(b, t, GQA_W)


def depthwise_conv(x, w, bias):
    y = lax.conv_general_dilated(x, w[:, None, :].astype(x.dtype), window_strides=(1,),
                                 padding=[(SSD_CONV // 2, SSD_CONV // 2)],
                                 dimension_numbers=('NWC', 'WIO', 'NWC'),
                                 feature_group_count=x.shape[-1])
    return y + bias


def ssd_scan(x, dt, a, bm, cm, init):
    f32 = jnp.float32
    b, t, h, p = x.shape
    n = bm.shape[-1]
    nc, L = t // SSD_CHUNK, SSD_CHUNK
    xd = (x.astype(f32) * dt[..., None]).reshape(b, nc, L, h, p)
    bc = bm.astype(f32).reshape(b, nc, L, h, n)
    cc = cm.astype(f32).reshape(b, nc, L, h, n)
    a_cs = jnp.cumsum((dt * a).reshape(b, nc, L, h).transpose(0, 3, 1, 2), axis=-1)
    lower = jnp.tril(jnp.ones((L, L), bool))
    decay = jnp.exp(jnp.where(lower, a_cs[..., :, None] - a_cs[..., None, :], -jnp.inf))
    scores = jnp.einsum('bclhn,bcshn->bhcls', cc, bc) * decay
    y_diag = jnp.einsum('bhcls,bcshp->bclhp', scores, xd)
    to_end = jnp.exp(a_cs[..., -1:] - a_cs)
    chunk_states = jnp.einsum('bclhn,bhcl,bclhp->bchpn', bc, to_end, xd)
    chunk_decay = jnp.exp(a_cs[..., -1])
    def step(s, inp):
        st, dec = inp
        return s * dec[:, :, None, None] + st, s
    final, s_in = lax.scan(step, init.astype(f32),
                           (jnp.moveaxis(chunk_states, 1, 0), jnp.moveaxis(chunk_decay, 2, 0)))
    s_in = jnp.moveaxis(s_in, 0, 1)
    y_off = jnp.einsum('bclhn,bchpn->bclhp', cc, s_in) * jnp.exp(a_cs).transpose(0, 2, 3, 1)[..., None]
    return (y_diag + y_off).reshape(b, t, h, p), final


def ssd_mixer(z, xbc, dt_raw, lp, init_f, init_b):
    b, t, _ = z.shape
    xbc = jax.nn.silu(depthwise_conv(xbc, lp['conv_w'], lp['conv_b']))
    xh, bm, cm = jnp.split(xbc, [SSD_INNER, SSD_INNER + SSD_GROUPS * SSD_STATE], axis=-1)
    rep = SSD_HEADS // SSD_GROUPS
    xh = xh.reshape(b, t, SSD_HEADS, SSD_HD)
    bm = jnp.repeat(bm.reshape(b, t, SSD_GROUPS, SSD_STATE), rep, axis=2)
    cm = jnp.repeat(cm.reshape(b, t, SSD_GROUPS, SSD_STATE), rep, axis=2)
    dt = jax.nn.softplus((dt_raw + lp['dt_bias']).astype(jnp.float32))
    a = -jnp.exp(lp['A_log'].astype(jnp.float32))
    flip = lambda u: jnp.flip(u, 1)
    y_f, s_f = ssd_scan(xh, dt[:, :, 0], a[0], bm, cm, init_f)
    y_b, s_b = ssd_scan(flip(xh), flip(dt[:, :, 1]), a[1], flip(bm), flip(cm), init_b)
    y = y_f + flip(y_b) + xh.astype(jnp.float32) * lp['D'].astype(jnp.float32)[:, None]
    y = y.reshape(b, t, SSD_INNER) * jax.nn.silu(z.astype(jnp.float32))
    return rms_norm(y, lp['ssd_norm_g']).astype(z.dtype), s_f, s_b


def mixing_sublayer(h, lp, lam_init, ctx=None, rope=None):
    b, t, _ = h.shape
    proj = jnp.einsum('btd,de->bte', h, lp['w_in'])
    cuts = np.cumsum(PROJ_SIZES)[:-1].tolist()
    dq, dk, dv, gq, gk, gv, z, xbc, dt_raw = jnp.split(proj, cuts, axis=-1)
    dq = dq.reshape(b, t, DIFF_HEADS, 2, DIFF_QK)
    dk = dk.reshape(b, t, DIFF_HEADS, 2, DIFF_QK)
    dv = dv.reshape(b, t, DIFF_HEADS, DIFF_V)
    gq = rms_norm(gq.reshape(b, t, GQA_Q_HEADS, GQA_HD), lp['qk_g'][0])
    gk = rms_norm(gk.reshape(b, t, GQA_KV_HEADS, GQA_HD), lp['qk_g'][1])
    gv = gv.reshape(b, t, GQA_KV_HEADS, GQA_HD)
    dt_raw = dt_raw.reshape(b, t, 2, SSD_HEADS)
    lq1, lk1, lq2, lk2 = lp['lam'][0], lp['lam'][1], lp['lam'][2], lp['lam'][3]
    lam = (jnp.exp(jnp.sum(lq1 * lk1).astype(jnp.float32))
           - jnp.exp(jnp.sum(lq2 * lk2).astype(jnp.float32)) + lam_init)
    if ctx is None:
        kd, vd, kg, vg = dk, dv, gk, gv
        init_f = jnp.zeros((b, SSD_HEADS, SSD_HD, SSD_STATE), jnp.float32)
        init_b = init_f
    else:
        c_dk, c_dv, c_gk, c_gv, init_f, init_b = ctx
        cos_d, sin_d, cos_g, sin_g = rope
        dq = apply_rope(dq, cos_d, sin_d)
        kd = jnp.concatenate([apply_rope(dk, cos_d, sin_d),
                              c_dk.reshape(b, -1, DIFF_HEADS, 2, DIFF_QK).astype(dk.dtype)], axis=1)
        vd = jnp.concatenate([dv, c_dv.astype(dv.dtype)], axis=1)
        gq = apply_rope(gq, cos_g, sin_g)
        kg = jnp.concatenate([apply_rope(gk, cos_g, sin_g), c_gk.astype(gk.dtype)], axis=1)
        vg = jnp.concatenate([gv, c_gv.astype(gv.dtype)], axis=1)
    o_diff = diff_attention(dq, kd, vd, lam, lp['subln_g'], lam_init)
    o_gqa = gqa_attention(gq, kg, vg)
    o_ssd, s_f, s_b = ssd_mixer(z, xbc, dt_raw, lp, init_f, init_b)
    out = jnp.einsum('bte,ed->btd', jnp.concatenate([o_diff, o_gqa, o_ssd], -1), lp['w_out'])
    if ctx is None:
        cache = (dk.reshape(b, t, DIFF_HEADS, 2 * DIFF_QK), dv, gk, gv,
                 s_f.astype(h.dtype), s_b.astype(h.dtype))
    else:
        cache = None
    return out, cache


def swiglu(h, w_in, w_out):
    g, u = jnp.split(jnp.einsum('btd,df->btf', h, w_in), 2, axis=-1)
    return jnp.einsum('btf,fd->btd', jax.nn.silu(g) * u, w_out)


def trunk_layer(x, mod, lp, lam_init, ctx=None, rope=None):
    shift1, scale1, gate1, shift2, scale2, gate2 = jnp.split(mod[:, None, :], 6, axis=-1)
    o, cache = mixing_sublayer(x * (1 + scale1) + shift1, lp, lam_init, ctx, rope)
    x = layer_norm(ALPHA * x + gate1 * o, lp['ln_g'][0], lp['ln_b'][0])
    f = swiglu(x * (1 + scale2) + shift2, lp['w_ffn_in'], lp['w_ffn_out'])
    x = layer_norm(ALPHA * x + gate2 * f, lp['ln_g'][1], lp['ln_b'][1])
    return x, cache


def setup_inputs(seed: int = 0) -> dict:
    key = jax.random.key(seed)
    ks = jax.random.split(key, 27)
    f32 = jnp.float32
    def nrm(k, shape, s=1.0):
        return s * jax.random.normal(k, shape, f32)
    dt0 = jnp.exp(jax.random.uniform(ks[20], (DEPTH, 2, SSD_HEADS), f32, math.log(1e-3), math.log(1e-1)))
    return {
        'x_prompt': nrm(ks[0], (BATCH, SEQ, D_MODEL)),
        'x_sample': nrm(ks[1], (DEC_BATCH, DEC_SEQ, D_MODEL)),
        'cache_diff_k': nrm(ks[2], (DEC_BATCH, DEPTH, PAST_LEN, DIFF_HEADS, 2 * DIFF_QK)),
        'cache_diff_v': nrm(ks[3], (DEC_BATCH, DEPTH, PAST_LEN, DIFF_HEADS, DIFF_V)),
        'cache_gqa_k': nrm(ks[4], (DEC_BATCH, DEPTH, PAST_LEN, GQA_KV_HEADS, GQA_HD)),
        'cache_gqa_v': nrm(ks[5], (DEC_BATCH, DEPTH, PAST_LEN, GQA_KV_HEADS, GQA_HD)),
        'state_ssd_fwd': nrm(ks[6], (DEC_BATCH, DEPTH, SSD_HEADS, SSD_HD, SSD_STATE), 0.1),
        'state_ssd_bwd': nrm(ks[7], (DEC_BATCH, DEPTH, SSD_HEADS, SSD_HD, SSD_STATE), 0.1),
        'c': nrm(ks[8], (DEC_BATCH, D_MODEL)),
        'c_ctx': nrm(ks[9], (D_MODEL,)),
        'w_ada': nrm(ks[10], (DEPTH, D_MODEL, 6 * D_MODEL), 0.5 * D_MODEL ** -0.5),
        'b_ada': nrm(ks[11], (DEPTH, 6 * D_MODEL), 0.01),
        'w_in': nrm(ks[12], (DEPTH, D_MODEL, IN_W), D_MODEL ** -0.5),
        'w_out': nrm(ks[13], (DEPTH, MIX_W, D_MODEL), BETA * MIX_W ** -0.5),
        'diff_lambda': nrm(ks[14], (DEPTH, 4, DIFF_QK), 0.1),
        'diff_subln_g': 1.0 + nrm(ks[15], (DEPTH, DIFF_V), 0.02),
        'qk_norm_g': 1.0 + nrm(ks[16], (DEPTH, 2, GQA_HD), 0.02),
        'ssd_conv_w': nrm(ks[17], (DEPTH, SSD_CONV, XBC_W), SSD_CONV ** -0.5),
        'ssd_conv_b': nrm(ks[18], (DEPTH, XBC_W), 0.01),
        'ssd_A_log': jnp.log(jax.random.uniform(ks[19], (DEPTH, 2, SSD_HEADS), f32, 1.0, 16.0)),
        'ssd_dt_bias': dt0 + jnp.log(-jnp.expm1(-dt0)),
        'ssd_D': 1.0 + nrm(ks[21], (DEPTH, SSD_HEADS), 0.02),
        'ssd_norm_g': 1.0 + nrm(ks[22], (DEPTH, SSD_INNER), 0.02),
        'ln_g': 1.0 + nrm(ks[23], (DEPTH, 2, D_MODEL), 0.02),
        'ln_b': nrm(ks[24], (DEPTH, 2, D_MODEL), 0.01),
        'w_ffn_in': nrm(ks[25], (DEPTH, D_MODEL, 2 * D_FF), D_MODEL ** -0.5),
        'w_ffn_out': nrm(ks[26], (DEPTH, D_FF, D_MODEL), BETA * D_FF ** -0.5),
    }


def reference(x_prompt, x_sample, cache_diff_k, cache_diff_v, cache_gqa_k, cache_gqa_v,
              state_ssd_fwd, state_ssd_bwd, c, c_ctx, w_ada, b_ada, w_in, w_out, diff_lambda,
              diff_subln_g, qk_norm_g, ssd_conv_w, ssd_conv_b, ssd_A_log, ssd_dt_bias, ssd_D,
              ssd_norm_g, ln_g, ln_b, w_ffn_in, w_ffn_out):
    rows = x_sample.shape[1] // GRID_W
    rope = axial_rope(rows, DIFF_QK) + axial_rope(rows, GQA_HD)
    xp, xs = x_prompt, x_sample
    new_dk, new_dv, new_gk, new_gv, new_sf, new_sb = [], [], [], [], [], []
    for l in range(DEPTH):
        lam_init = 0.8 - 0.6 * math.exp(-0.3 * l)
        lp = dict(w_in=w_in[l], w_out=w_out[l], lam=diff_lambda[l], subln_g=diff_subln_g[l],
                  qk_g=qk_norm_g[l], conv_w=ssd_conv_w[l], conv_b=ssd_conv_b[l], A_log=ssd_A_log[l],
                  dt_bias=ssd_dt_bias[l], D=ssd_D[l], ssd_norm_g=ssd_norm_g[l], ln_g=ln_g[l],
                  ln_b=ln_b[l], w_ffn_in=w_ffn_in[l], w_ffn_out=w_ffn_out[l])
        mod_ctx = (jnp.dot(jax.nn.silu(c_ctx), w_ada[l]) + b_ada[l])[None]
        mod_lat = jnp.dot(jax.nn.silu(c), w_ada[l]) + b_ada[l]
        xp, cache = trunk_layer(xp, mod_ctx, lp, lam_init)
        new_dk.append(cache[0]); new_dv.append(cache[1]); new_gk.append(cache[2])
        new_gv.append(cache[3]); new_sf.append(cache[4]); new_sb.append(cache[5])
        ctx = (cache_diff_k[:, l], cache_diff_v[:, l], cache_gqa_k[:, l], cache_gqa_v[:, l],
               state_ssd_fwd[:, l], state_ssd_bwd[:, l])
        xs, _ = trunk_layer(xs, mod_lat, lp, lam_init, ctx=ctx, rope=rope)
    return (xp, xs, jnp.stack(new_dk, 1), jnp.stack(new_dv, 1), jnp.stack(new_gk, 1),
            jnp.stack(new_gv, 1), jnp.stack(new_sf, 1), jnp.stack(new_sb, 1))
```

```python
import functools
import math

import jax
import jax.numpy as jnp
from jax import lax
from jax.experimental import pallas as pl
from jax.experimental.pallas import tpu as pltpu

F32 = jnp.float32
BF16 = jnp.bfloat16

GRID_W = 64
DIFF_HEADS = 4
DIFF_QK = 32
DIFF_V = 64
DIFF_W = DIFF_HEADS * DIFF_V
GQA_HD = 64
GQA_Q_HEADS = 4
GQA_KV_HEADS = 2
GQA_W = GQA_Q_HEADS * GQA_HD
GQA_KV_W = GQA_KV_HEADS * GQA_HD
SSD_HD = 64
SSD_HEADS = 8
SSD_INNER = SSD_HEADS * SSD_HD
SSD_GROUPS = 2
SSD_STATE = 64
SSD_BC_W = SSD_GROUPS * SSD_STATE
SSD_CONV = 5
SSD_CHUNK = 128
XBC_W = SSD_INNER + 2 * SSD_BC_W
DT_W = 2 * SSD_HEADS
ROPE_THETA = 10000.0
EPS = 1e-5
LANES = 128
VMEM_LIMIT = 56 * 1024 * 1024

_C_DQ, _C_DK, _C_DV, _C_GQ, _C_GK, _C_GV, _C_Z, _C_XBC, _C_DT, _C_END = (
    0, 256, 512, 768, 1024, 1152, 1280, 1792, 2560, 2576)
IN_W_PAD = _C_DT + LANES


def _dot(a, b):
    return jnp.dot(a, b, preferred_element_type=F32)


def _dot_nt(a, b):
    return lax.dot_general(a, b, (((1,), (1,)), ((), ())), preferred_element_type=F32)


def _split3(a):
    a1 = a.astype(BF16)
    r1 = a - a1.astype(F32)
    a2 = r1.astype(BF16)
    a3 = (r1 - a2.astype(F32)).astype(BF16)
    return a1, a2, a3


def _dot3_l(a, b_exact):
    a1, a2, a3 = _split3(a)
    return _dot(a1, b_exact) + (_dot(a2, b_exact) + _dot(a3, b_exact))


def _dot3_r(a_exact, b):
    b1, b2, b3 = _split3(b)
    return _dot(a_exact, b1) + (_dot(a_exact, b2) + _dot(a_exact, b3))


def _sigmoid(x):
    return 1.0 / (1.0 + jnp.exp(-x))


def _silu(x):
    return x * _sigmoid(x)


def _layer_norm(x, g, b):
    mu = jnp.mean(x, axis=-1, keepdims=True)
    xc = x - mu
    var = jnp.mean(xc * xc, axis=-1, keepdims=True)
    return xc * lax.rsqrt(var + EPS) * g + b


def _group_avg_matrix(width, group):
    sh = int(math.log2(group))
    r = lax.shift_right_logical(lax.broadcasted_iota(jnp.int32, (width, width), 0), sh)
    c = lax.shift_right_logical(lax.broadcasted_iota(jnp.int32, (width, width), 1), sh)
    return jnp.where(r == c, 1.0 / group, 0.0).astype(BF16)


def _group_mean_sq(x, gmat):
    xx = x * x
    hi = xx.astype(BF16)
    lo = (xx - hi.astype(F32)).astype(BF16)
    return _dot(hi, gmat) + _dot(lo, gmat)


def _rope(x, cos, sin_signed):
    w = x.shape[-1]
    lane = lax.broadcasted_iota(jnp.int32, x.shape, 1)
    nxt = pltpu.roll(x, w - 1, 1)
    prv = pltpu.roll(x, 1, 1)
    partner = jnp.where((lane & 1) == 0, nxt, prv)
    return x * cos + partner * sin_signed


def _lane_mask(shape, lo, hi):
    lane = lax.broadcasted_iota(jnp.int32, shape, 1)
    return (lane >= lo) & (lane < hi)


def _mod_kernel(c_ref, w_ref, b_ref, o_ref):
    a = _silu(c_ref[...])
    a_hi = a.astype(BF16)
    a_lo = (a - a_hi.astype(F32)).astype(BF16)
    w = w_ref[...]
    w_hi = w.astype(BF16)
    w_lo = (w - w_hi.astype(F32)).astype(BF16)
    o_ref[...] = _dot(a_hi, w_hi) + (_dot(a_lo, w_hi) + _dot(a_hi, w_lo)) + b_ref[...]


def _modulation(cvec, w_ada, b_ada):
    depth, d, n = w_ada.shape
    tn = 1536
    rows = cvec.shape[0]
    return pl.pallas_call(
        _mod_kernel,
        grid=(depth, n // tn),
        in_specs=[
            pl.BlockSpec((rows, d), lambda l, j: (0, 0)),
            pl.BlockSpec((None, d, tn), lambda l, j: (l, 0, j)),
            pl.BlockSpec((None, 1, tn), lambda l, j: (l, 0, j)),
        ],
        out_specs=pl.BlockSpec((None, rows, tn), lambda l, j: (l, 0, j)),
        out_shape=jax.ShapeDtypeStruct((depth, rows, n), F32),
        compiler_params=pltpu.CompilerParams(
            dimension_semantics=("arbitrary", "arbitrary"), vmem_limit_bytes=VMEM_LIMIT),
    )(cvec, w_ada, b_ada.reshape(depth, 1, n))


def _inproj_kernel(x_ref, sh_ref, sc_ref, w_ref, qa_ref, kv_ref, z_ref, xbc_ref, dt_ref):
    h = (x_ref[...] * (1.0 + sc_ref[...]) + sh_ref[...]).astype(BF16)

    def mm(lo, hi):
        return _dot(h, w_ref[:, lo:hi])

    qa_ref[:, 0:DIFF_W] = mm(_C_DQ, _C_DK)
    qa_ref[:, DIFF_W:DIFF_W + GQA_W] = mm(_C_GQ, _C_GK)
    kv_ref[:, 0:2 * DIFF_W] = mm(_C_DK, _C_GQ)
    kv_ref[:, 2 * DIFF_W:2 * DIFF_W + 2 * GQA_KV_W] = mm(_C_GK, _C_Z)
    z_ref[...] = mm(_C_Z, _C_XBC)
    xbc_ref[...] = mm(_C_XBC, _C_DT)
    dt_ref[...] = mm(_C_DT, IN_W_PAD)


def _mod_spec(layer, which, row_fn, d):
    return pl.BlockSpec((None, None, 1, d), lambda i: (layer, row_fn(i), 0, which))


def _inproj(x, mods, layer, row_fn, w_in_bf, tm):
    t, d = x.shape
    widths = (DIFF_W + GQA_W, 2 * DIFF_W + 2 * GQA_KV_W, SSD_INNER, XBC_W, LANES)
    return pl.pallas_call(
        _inproj_kernel,
        grid=(t // tm,),
        in_specs=[
            pl.BlockSpec((tm, d), lambda i: (i, 0)),
            _mod_spec(layer, 0, row_fn, d),
            _mod_spec(layer, 1, row_fn, d),
            pl.BlockSpec((d, IN_W_PAD), lambda i: (0, 0)),
        ],
        out_specs=[pl.BlockSpec((tm, w), lambda i: (i, 0)) for w in widths],
        out_shape=[jax.ShapeDtypeStruct((t, w), F32) for w in widths],
        compiler_params=pltpu.CompilerParams(
            dimension_semantics=("arbitrary",), vmem_limit_bytes=VMEM_LIMIT),
    )(x, mods, mods, w_in_bf)


def _softmax_parts(s):
    m = jnp.max(s, axis=-1, keepdims=True)
    e = jnp.exp(s - m)
    return e, jnp.sum(e, axis=-1, keepdims=True)


def _attn_core(qd, qg, kd, vd, kg, kg_sw, vg, vg_sw, lam, subln_g, lam_init):
    tq = qd.shape[0]
    o_d = jnp.zeros((tq, DIFF_W), F32)
    for h in range(DIFF_HEADS):
        parts = []
        for m in range(2):
            lo = (2 * h + m) * DIFF_QK
            qm = jnp.where(_lane_mask(qd.shape, lo, lo + DIFF_QK), qd, 0.0).astype(BF16)
            parts.append(_softmax_parts(_dot_nt(qm, kd)))
        (e0, l0), (e1, l1) = parts
        w = e0 * (1.0 / l0) - e1 * (lam / l1)
        of = _dot(w.astype(BF16), vd)
        o_d = jnp.where(_lane_mask(o_d.shape, h * DIFF_V, (h + 1) * DIFF_V), of, o_d)
    ms = _group_mean_sq(o_d, _group_avg_matrix(DIFF_W, DIFF_V))
    o_d = o_d * lax.rsqrt(ms + EPS) * subln_g * (1.0 - lam_init)

    halves = []
    for qh, first, second in ((qg[:, 0:LANES], (kg, vg), (kg_sw, vg_sw)),
                              (qg[:, LANES:2 * LANES], (kg_sw, vg_sw), (kg, vg))):
        acc = jnp.zeros((tq, LANES), F32)
        for lo, (kk, vv) in ((0, first), (GQA_HD, second)):
            msk = _lane_mask(qh.shape, lo, lo + GQA_HD)
            qm = jnp.where(msk, qh, 0.0).astype(BF16)
            e, l = _softmax_parts(_dot_nt(qm, kk))
            p = (e * (1.0 / l)).astype(BF16)
            acc = jnp.where(msk, _dot(p, vv), acc)
        halves.append(acc)
    return o_d, halves[0], halves[1]


def _diff_lambda(lam_ref, lam_init):
    lp = lam_ref[...]
    s1 = jnp.sum(lp[0:1, :] * lp[1:2, :], axis=-1, keepdims=True)
    s2 = jnp.sum(lp[2:3, :] * lp[3:4, :], axis=-1, keepdims=True)
    return jnp.exp(s1) - jnp.exp(s2) + lam_init


def _swap_halves(x):
    return pltpu.roll(x, GQA_HD, 1)


def _attn_ctx_kernel(lam_init, qa_ref, kv_ref, lam_ref, subln_ref, gq_ref, gk_ref,
                     o_ref, gkn_ref):
    lam = _diff_lambda(lam_ref, lam_init)
    qa = qa_ref[...]
    kv = kv_ref[...]
    qd = qa[:, 0:DIFF_W] * (DIFF_QK ** -0.5)
    gq = qa[:, DIFF_W:DIFF_W + GQA_W]
    gq = gq * lax.rsqrt(_group_mean_sq(gq, _group_avg_matrix(GQA_W, GQA_HD)) + EPS) * gq_ref[...]
    gk = kv[:, 2 * DIFF_W:2 * DIFF_W + GQA_KV_W]
    gk = gk * lax.rsqrt(_group_mean_sq(gk, _group_avg_matrix(GQA_KV_W, GQA_HD)) + EPS) * gk_ref[...]
    gkn_ref[...] = gk
    gv = kv[:, 2 * DIFF_W + GQA_KV_W:2 * DIFF_W + 2 * GQA_KV_W]
    o_d, o_g0, o_g1 = _attn_core(
        qd, gq * (GQA_HD ** -0.5),
        kv[:, 0:DIFF_W].astype(BF16), kv[:, DIFF_W:2 * DIFF_W].astype(BF16),
        gk.astype(BF16), _swap_halves(gk).astype(BF16),
        gv.astype(BF16), _swap_halves(gv).astype(BF16),
        lam, subln_ref[...], lam_init)
    o_ref[:, 0:DIFF_W] = o_d
    o_ref[:, DIFF_W:DIFF_W + LANES] = o_g0
    o_ref[:, DIFF_W + LANES:DIFF_W + 2 * LANES] = o_g1


def _attn_ctx(qa, kv, lam_p, subln_t, gq_t, gk_t, lam_init, batch, seq):
    t = qa.shape[0]
    full = lambda a: pl.BlockSpec(a.shape, lambda b: (0,) * a.ndim)
    return pl.pallas_call(
        functools.partial(_attn_ctx_kernel, lam_init),
        grid=(batch,),
        in_specs=[
            pl.BlockSpec((seq, qa.shape[1]), lambda b: (b, 0)),
            pl.BlockSpec((seq, kv.shape[1]), lambda b: (b, 0)),
            full(lam_p), full(subln_t), full(gq_t), full(gk_t),
        ],
        out_specs=[
            pl.BlockSpec((seq, DIFF_W + GQA_W), lambda b: (b, 0)),
            pl.BlockSpec((seq, GQA_KV_W), lambda b: (b, 0)),
        ],
        out_shape=[jax.ShapeDtypeStruct((t, DIFF_W + GQA_W), F32),
                   jax.ShapeDtypeStruct((t, GQA_KV_W), F32)],
        compiler_params=pltpu.CompilerParams(
            dimension_semantics=("arbitrary",), vmem_limit_bytes=VMEM_LIMIT),
    )(qa, kv, lam_p, subln_t, gq_t, gk_t)


def _attn_lat_kernel(lam_init, seq, past, kblk,
                     qa_ref, kv_ref, cdk_ref, cdv_ref, cgk_ref, cgv_ref,
                     cosq_ref, sinq_ref, cosk_ref, sink_ref,
                     lam_ref, subln_ref, gq_ref, gk_ref,
                     o_ref, kd_s, vd_s, kg_s, kgsw_s, vg_s, vgsw_s):
    @pl.when(pl.program_id(1) == 0)
    def _prepare_keys():
        gmat = _group_avg_matrix(GQA_KV_W, GQA_HD)

        def body(i, carry):
            r = pl.ds(pl.multiple_of(i * kblk, kblk), kblk)
            kv = kv_ref[r, :]
            ck = cosk_ref[r, :]
            sk = sink_ref[r, :]
            dk = _rope(kv[:, 0:DIFF_W], ck[:, 0:DIFF_W], sk[:, 0:DIFF_W])
            kd_s[r, :] = dk.astype(BF16)
            vd_s[r, :] = kv[:, DIFF_W:2 * DIFF_W].astype(BF16)
            gk = kv[:, 2 * DIFF_W:2 * DIFF_W + GQA_KV_W]
            gk = gk * lax.rsqrt(_group_mean_sq(gk, gmat) + EPS) * gk_ref[...]
            gk = _rope(gk, ck[:, DIFF_W:DIFF_W + GQA_KV_W], sk[:, DIFF_W:DIFF_W + GQA_KV_W])
            kg_s[r, :] = gk.astype(BF16)
            kgsw_s[r, :] = _swap_halves(gk).astype(BF16)
            gv = kv[:, 2 * DIFF_W + GQA_KV_W:2 * DIFF_W + 2 * GQA_KV_W]
            vg_s[r, :] = gv.astype(BF16)
            vgsw_s[r, :] = _swap_halves(gv).astype(BF16)
            return carry

        lax.fori_loop(0, seq // kblk, body, 0)
        c = pl.ds(seq, past)
        kd_s[c, :] = cdk_ref[...].astype(BF16)
        vd_s[c, :] = cdv_ref[...].astype(BF16)
        cgk = cgk_ref[...]
        cgv = cgv_ref[...]
        kg_s[c, :] = cgk.astype(BF16)
        kgsw_s[c, :] = _swap_halves(cgk).astype(BF16)
        vg_s[c, :] = cgv.astype(BF16)
        vgsw_s[c, :] = _swap_halves(cgv).astype(BF16)

    lam = _diff_lambda(lam_ref, lam_init)
    qa = qa_ref[...]
    cq = cosq_ref[...]
    sq = sinq_ref[...]
    qd = _rope(qa[:, 0:DIFF_W], cq[:, 0:DIFF_W], sq[:, 0:DIFF_W]) * (DIFF_QK ** -0.5)
    gq = qa[:, DIFF_W:DIFF_W + GQA_W]
    gq = gq * lax.rsqrt(_group_mean_sq(gq, _group_avg_matrix(GQA_W, GQA_HD)) + EPS) * gq_ref[...]
    gq = _rope(gq, cq[:, DIFF_W:DIFF_W + GQA_W], sq[:, DIFF_W:DIFF_W + GQA_W]) * (GQA_HD ** -0.5)
    o_d, o_g0, o_g1 = _attn_core(qd, gq, kd_s[...], vd_s[...], kg_s[...], kgsw_s[...],
                                 vg_s[...], vgsw_s[...], lam, subln_ref[...], lam_init)
    o_ref[:, 0:DIFF_W] = o_d
    o_ref[:, DIFF_W:DIFF_W + LANES] = o_g0
    o_ref[:, DIFF_W + LANES:DIFF_W + 2 * LANES] = o_g1


def _attn_lat(qa, kv, caches, layer, tables, lam_p, subln_t, gq_t, gk_t, lam_init,
              batch, seq, tq):
    cdk, cdv, cgk, cgv = caches
    cosq, sinq, cosk, sink = tables
    past = cdk.shape[2]
    t = qa.shape[0]
    nq = seq // tq
    tk = seq + past
    kblk = min(256, seq)
    full = lambda a: pl.BlockSpec(a.shape, lambda b, q: (0,) * a.ndim)
    cache = lambda a: pl.BlockSpec((None, None, past, a.shape[3]), lambda b, q: (b, layer, 0, 0))
    return pl.pallas_call(
        functools.partial(_attn_lat_kernel, lam_init, seq, past, kblk),
        grid=(batch, nq),
        in_specs=[
            pl.BlockSpec((tq, qa.shape[1]), lambda b, q: (b * nq + q, 0)),
            pl.BlockSpec((seq, kv.shape[1]), lambda b, q: (b, 0)),
            cache(cdk), cache(cdv), cache(cgk), cache(cgv),
            pl.BlockSpec((tq, cosq.shape[1]), lambda b, q: (q, 0)),
            pl.BlockSpec((tq, sinq.shape[1]), lambda b, q: (q, 0)),
            full(cosk), full(sink),
            full(lam_p), full(subln_t), full(gq_t), full(gk_t),
        ],
        out_specs=pl.BlockSpec((tq, DIFF_W + GQA_W), lambda b, q: (b * nq + q, 0)),
        out_shape=jax.ShapeDtypeStruct((t, DIFF_W + GQA_W), F32),
        scratch_shapes=[
            pltpu.VMEM((tk, DIFF_W), BF16), pltpu.VMEM((tk, DIFF_W), BF16),
            pltpu.VMEM((tk, GQA_KV_W), BF16), pltpu.VMEM((tk, GQA_KV_W), BF16),
            pltpu.VMEM((tk, GQA_KV_W), BF16), pltpu.VMEM((tk, GQA_KV_W), BF16),
        ],
        compiler_params=pltpu.CompilerParams(
            dimension_semantics=("arbitrary", "arbitrary"), vmem_limit_bytes=VMEM_LIMIT),
    )(qa, kv, cdk, cdv, cgk, cgv, cosq, sinq, cosk, sink, lam_p, subln_t, gq_t, gk_t)


def _softplus(x):
    return jnp.maximum(x, 0.0) + jnp.log1p(jnp.exp(-jnp.abs(x)))


def _ssd_kernel(has_init, seq, *refs):
    if has_init:
        (z_ref, xbc_ref, dt_ref, cw_ref, cb_ref, alog_ref, dtb_ref, dexp_ref, ng_ref,
         sf0_ref, sb0_ref, o_ref, sf_ref, sb_ref,
         xs_s, bm_s, cm_s, dt_s, halo_s, st_s) = refs
    else:
        (z_ref, xbc_ref, dt_ref, cw_ref, cb_ref, alog_ref, dtb_ref, dexp_ref, ng_ref,
         o_ref, sf_ref, sb_ref,
         xs_s, bm_s, cm_s, dt_s, halo_s, st_s) = refs
        sf0_ref = sb0_ref = None
    L = SSD_CHUNK
    nchunks = seq // L
    halo = 8

    ri = lax.broadcasted_iota(jnp.int32, (L, L), 0)
    ci = lax.broadcasted_iota(jnp.int32, (L, L), 1)
    lower = ri >= ci
    upper = ri <= ci
    lane_row = lax.broadcasted_iota(jnp.int32, (1, LANES), 1)
    a_row = jnp.where(lane_row < DT_W, -jnp.exp(alog_ref[...]), 0.0)
    ej = lax.broadcasted_iota(jnp.int32, (LANES, SSD_INNER), 0)
    eh = lax.shift_right_logical(lax.broadcasted_iota(jnp.int32, (LANES, SSD_INNER), 1), 6)
    group_mask = jnp.where(lax.shift_right_logical(ej, 6) == lax.shift_right_logical(eh, 2), 1.0, 0.0)

    def chunk_rows(c):
        return pl.ds(pl.multiple_of(c * L, L), L)

    def conv_body(c, carry):
        r0 = c * L
        rows = chunk_rows(c)
        prev = xbc_ref[pl.ds(pl.multiple_of(jnp.maximum(r0 - halo, 0), halo), halo), :]
        nxt = xbc_ref[pl.ds(pl.multiple_of(jnp.minimum(r0 + L, seq - halo), halo), halo), :]
        halo_s[0:halo, :] = jnp.where(c > 0, prev, 0.0)
        halo_s[halo:halo + L, :] = xbc_ref[rows, :]
        halo_s[halo + L:2 * halo + L, :] = jnp.where(c < nchunks - 1, nxt, 0.0)
        acc = cb_ref[...] + halo_s[pl.ds(halo - SSD_CONV // 2, L), :] * cw_ref[0:1, :]
        for j in range(1, SSD_CONV):
            acc = acc + halo_s[pl.ds(halo - SSD_CONV // 2 + j, L), :] * cw_ref[j:j + 1, :]
        act = _silu(acc)
        xs_s[rows, :] = act[:, 0:SSD_INNER]
        bm_s[rows, :] = act[:, SSD_INNER:SSD_INNER + SSD_BC_W]
        cm_s[rows, :] = act[:, SSD_INNER + SSD_BC_W:XBC_W]
        dt_s[rows, :] = _softplus(dt_ref[rows, :] + dtb_ref[...])
        return carry

    lax.fori_loop(0, nchunks, conv_body, 0)

    def scan(direction):
        tri = jnp.where(lower if direction == 0 else upper, 1.0, 0.0).astype(BF16)
        causal = lower if direction == 0 else upper
        expand = jnp.where(ej == eh + direction * SSD_HEADS, 1.0, 0.0).astype(BF16)
        edge_row = L - 1 if direction == 0 else 0

        def body(i, carry):
            c = i if direction == 0 else nchunks - 1 - i
            rows = chunk_rows(c)
            x_c = xs_s[rows, :]
            b_c = bm_s[rows, :]
            c_c = cm_s[rows, :]
            dt_c = dt_s[rows, :]
            dta = dt_c * a_row
            acs = _dot3_r(tri, dta)
            acs_t = acs.T
            dt_e = _dot3_l(dt_c, expand)
            acs_e = _dot3_l(acs, expand)
            edge = acs_e[edge_row:edge_row + 1, :]
            xd = x_c * dt_e
            xd_b = xd.astype(BF16)
            b_b = b_c.astype(BF16)
            st = st_s[...]
            y = _dot(c_c.astype(BF16), st.astype(BF16)) * jnp.exp(acs_e)
            pairs = []
            for g in range(SSD_GROUPS):
                cg = jnp.where(_lane_mask(c_c.shape, g * SSD_STATE, (g + 1) * SSD_STATE), c_c, 0.0)
                cb = _dot_nt(cg.astype(BF16), b_b)
                for hp in range(2):
                    pair = g * 2 + hp
                    res = []
                    for k in range(2):
                        j = direction * SSD_HEADS + pair * 2 + k
                        diff = acs[:, j:j + 1] - acs_t[j:j + 1, :]
                        dec = jnp.where(causal, jnp.exp(jnp.minimum(diff, 0.0)), 0.0)
                        sc = (cb * dec).astype(BF16)
                        res.append(_dot(sc, xd_b[:, pair * LANES:(pair + 1) * LANES]))
                    pairs.append(jnp.where(_lane_mask(res[0].shape, 0, SSD_HD), res[0], res[1]))
            y = y + jnp.concatenate(pairs, axis=-1)
            xdw = (xd * jnp.exp(edge - acs_e)).astype(BF16)
            s_new = _dot(b_c.T.astype(BF16), xdw) * group_mask
            st_s[...] = st * jnp.exp(edge) + s_new
            if direction == 0:
                o_ref[rows, :] = y + x_c * dexp_ref[...]
            else:
                yt = (o_ref[rows, :] + y) * _silu(z_ref[rows, :])
                ms = jnp.mean(yt * yt, axis=-1, keepdims=True)
                o_ref[rows, :] = yt * lax.rsqrt(ms + EPS) * ng_ref[...]
            return carry

        lax.fori_loop(0, nchunks, body, 0)

    def load_state(ref):
        r = lax.broadcasted_iota(jnp.int32, (SSD_STATE, LANES), 0)
        c = lax.broadcasted_iota(jnp.int32, (SSD_STATE, LANES), 1)
        dup = jnp.where((c & (SSD_STATE - 1)) == r, 1.0, 0.0).astype(BF16)
        return _dot3_l(ref[...], dup).T * group_mask

    def store_state(ref):
        st_t = st_s[...].T
        ref[...] = (st_t + _swap_halves(st_t))[:, 0:SSD_STATE]

    st_s[...] = load_state(sf0_ref) if has_init else jnp.zeros(st_s.shape, F32)
    scan(0)
    store_state(sf_ref)
    st_s[...] = load_state(sb0_ref) if has_init else jnp.zeros(st_s.shape, F32)
    scan(1)
    store_state(sb_ref)


def _ssd(z, xbc, dt, params, init, layer, batch, seq):
    cw, cb, alog, dtb, dexp, ng = params
    t = z.shape[0]
    has_init = init is not None
    full = lambda a: pl.BlockSpec(a.shape, lambda b: (0,) * a.ndim)
    st_spec = pl.BlockSpec((None, SSD_INNER, SSD_STATE), lambda b: (b, 0, 0))
    in_specs = [
        pl.BlockSpec((seq, SSD_INNER), lambda b: (b, 0)),
        pl.BlockSpec((seq, XBC_W), lambda b: (b, 0)),
        pl.BlockSpec((seq, LANES), lambda b: (b, 0)),
        full(cw), full(cb), full(alog), full(dtb), full(dexp), full(ng),
    ]
    args = [z, xbc, dt, cw, cb, alog, dtb, dexp, ng]
    if has_init:
        init_spec = pl.BlockSpec((None, None, SSD_INNER, SSD_STATE), lambda b: (b, layer, 0, 0))
        in_specs += [init_spec, init_spec]
        args += list(init)
    st_shape = jax.ShapeDtypeStruct((batch, SSD_INNER, SSD_STATE), F32)
    return pl.pallas_call(
        functools.partial(_ssd_kernel, has_init, seq),
        grid=(batch,),
        in_specs=in_specs,
        out_specs=[pl.BlockSpec((seq, SSD_INNER), lambda b: (b, 0)), st_spec, st_spec],
        out_shape=[jax.ShapeDtypeStruct((t, SSD_INNER), F32), st_shape, st_shape],
        scratch_shapes=[
            pltpu.VMEM((seq, SSD_INNER), F32), pltpu.VMEM((seq, SSD_BC_W), F32),
            pltpu.VMEM((seq, SSD_BC_W), F32), pltpu.VMEM((seq, LANES), F32),
            pltpu.VMEM((SSD_CHUNK + 16, XBC_W), F32), pltpu.VMEM((LANES, SSD_INNER), F32),
        ],
        compiler_params=pltpu.CompilerParams(
            dimension_semantics=("arbitrary",), vmem_limit_bytes=VMEM_LIMIT),
    )(*args)


def _post_kernel(alpha, d_ff, ff_chunk, x_ref, oa_ref, os_ref, g1_ref, sh2_ref, sc2_ref, g2_ref,
                 wo_ref, wfi_ref, wfo_ref, lng_ref, lnb_ref, y_ref):
    x = x_ref[...]
    wa = oa_ref.shape[1]
    o = _dot(oa_ref[...].astype(BF16), wo_ref[0:wa, :]) + _dot(os_ref[...].astype(BF16), wo_ref[wa:, :])
    x1 = _layer_norm(alpha * x + g1_ref[...] * o, lng_ref[0:1, :], lnb_ref[0:1, :])
    h2 = (x1 * (1.0 + sc2_ref[...]) + sh2_ref[...]).astype(BF16)
    f = jnp.zeros(x.shape, F32)
    for c in range(d_ff // ff_chunk):
        lo = c * ff_chunk
        g = _dot(h2, wfi_ref[:, lo:lo + ff_chunk])
        u = _dot(h2, wfi_ref[:, d_ff + lo:d_ff + lo + ff_chunk])
        f = f + _dot((_silu(g) * u).astype(BF16), wfo_ref[lo:lo + ff_chunk, :])
    y_ref[...] = _layer_norm(alpha * x1 + g2_ref[...] * f, lng_ref[1:2, :], lnb_ref[1:2, :])


def _post(x, oa, os_, mods, layer, row_fn, wo_bf, wfi_bf, wfo_bf, lng, lnb, alpha, tm):
    t, d = x.shape
    d_ff = wfo_bf.shape[0]
    ff_chunk = d_ff // 2
    resident = lambda a: pl.BlockSpec(a.shape, lambda i: (0,) * a.ndim, pipeline_mode=pl.Buffered(1))
    return pl.pallas_call(
        functools.partial(_post_kernel, alpha, d_ff, ff_chunk),
        grid=(t // tm,),
        in_specs=[
            pl.BlockSpec((tm, d), lambda i: (i, 0)),
            pl.BlockSpec((tm, oa.shape[1]), lambda i: (i, 0)),
            pl.BlockSpec((tm, os_.shape[1]), lambda i: (i, 0)),
            _mod_spec(layer, 2, row_fn, d),
            _mod_spec(layer, 3, row_fn, d),
            _mod_spec(layer, 4, row_fn, d),
            _mod_spec(layer, 5, row_fn, d),
            resident(wo_bf), resident(wfi_bf), resident(wfo_bf),
            pl.BlockSpec(lng.shape, lambda i: (0, 0)),
            pl.BlockSpec(lnb.shape, lambda i: (0, 0)),
        ],
        out_specs=pl.BlockSpec((tm, d), lambda i: (i, 0)),
        out_shape=jax.ShapeDtypeStruct((t, d), F32),
        compiler_params=pltpu.CompilerParams(
            dimension_semantics=("arbitrary",), vmem_limit_bytes=VMEM_LIMIT),
    )(x, oa, os_, mods, mods, mods, mods, wo_bf, wfi_bf, wfo_bf, lng, lnb)


def _rope_tables(rows, dim, copies):
    row = jnp.repeat(jnp.arange(rows), GRID_W).astype(F32)
    col = jnp.tile(jnp.arange(GRID_W), rows).astype(F32)
    n_freq = dim // 4
    inv = ROPE_THETA ** (-jnp.arange(n_freq, dtype=F32) / n_freq)
    ang = jnp.concatenate([row[:, None] * inv, col[:, None] * inv], -1)
    cos, sin = jnp.cos(ang), jnp.sin(ang)
    cos_full = jnp.repeat(cos, 2, axis=-1)
    sin_signed = jnp.stack([-sin, sin], axis=-1).reshape(sin.shape[0], dim)
    return jnp.tile(cos_full, (1, copies)), jnp.tile(sin_signed, (1, copies))


def kernel(x_prompt, x_sample, cache_diff_k, cache_diff_v, cache_gqa_k, cache_gqa_v, state_ssd_fwd, state_ssd_bwd, c, c_ctx, w_ada, b_ada, w_in, w_out, diff_lambda, diff_subln_g, qk_norm_g, ssd_conv_w, ssd_conv_b, ssd_A_log, ssd_dt_bias, ssd_D, ssd_norm_g, ln_g, ln_b, w_ffn_in, w_ffn_out):
    batch, seq, d = x_prompt.shape
    dec_batch, dec_seq, _ = x_sample.shape
    depth = w_in.shape[0]
    past = cache_diff_k.shape[2]
    alpha = (2 * depth) ** 0.25
    rows = dec_seq // GRID_W

    n_vec = 1 + dec_batch
    n_pad = -(-n_vec // 8) * 8
    cvec = jnp.concatenate([c_ctx[None, :], c, jnp.zeros((n_pad - n_vec, d), F32)], axis=0)
    mods = _modulation(cvec, w_ada, b_ada).reshape(depth, n_pad, 1, 6 * d)

    cos_d, sin_d = _rope_tables(rows, DIFF_QK, DIFF_W // DIFF_QK)
    cos_g, sin_g = _rope_tables(rows, GQA_HD, GQA_Q_HEADS)
    cosq = jnp.concatenate([cos_d, cos_g], axis=-1)
    sinq = jnp.concatenate([sin_d, sin_g], axis=-1)
    cosk = jnp.concatenate([cos_d, cos_g[:, :GQA_KV_W]], axis=-1)
    sink = jnp.concatenate([sin_d, sin_g[:, :GQA_KV_W]], axis=-1)
    tables = (cosq, sinq, cosk, sink)

    caches = (cache_diff_k.reshape(dec_batch, depth, past, DIFF_W),
              cache_diff_v.reshape(dec_batch, depth, past, DIFF_W),
              cache_gqa_k.reshape(dec_batch, depth, past, GQA_KV_W),
              cache_gqa_v.reshape(dec_batch, depth, past, GQA_KV_W))

    xp = x_prompt.reshape(batch * seq, d)
    xs = x_sample.reshape(dec_batch * dec_seq, d)
    tm_ctx = min(512, batch * seq)
    tm_lat = min(512, dec_seq)
    tq = min(256, dec_seq)
    ctx_row = lambda i: 0
    lat_row = lambda i: 1 + (i * tm_lat) // dec_seq

    new_dk, new_dv, new_gk, new_gv, new_sf, new_sb = [], [], [], [], [], []
    for l in range(depth):
        lam_init = 0.8 - 0.6 * math.exp(-0.3 * l)
        w_in_bf = jnp.pad(w_in[l], ((0, 0), (0, IN_W_PAD - w_in.shape[2]))).astype(BF16)
        wo_bf = w_out[l].astype(BF16)
        wfi_bf = w_ffn_in[l].astype(BF16)
        wfo_bf = w_ffn_out[l].astype(BF16)
        lam_p = diff_lambda[l]
        subln_t = jnp.tile(diff_subln_g[l], DIFF_HEADS)[None, :]
        gq_t = jnp.tile(qk_norm_g[l, 0], GQA_Q_HEADS)[None, :]
        gk_t = jnp.tile(qk_norm_g[l, 1], GQA_KV_HEADS)[None, :]
        pad_row = lambda v: jnp.pad(v.reshape(1, -1), ((0, 0), (0, LANES - v.size)))
        ssd_params = (ssd_conv_w[l], ssd_conv_b[l][None, :], pad_row(ssd_A_log[l]),
                      pad_row(ssd_dt_bias[l]), jnp.repeat(ssd_D[l], SSD_HD)[None, :],
                      ssd_norm_g[l][None, :])

        qa, kv, z, xbc, dt = _inproj(xp, mods, l, ctx_row, w_in_bf, tm_ctx)
        oa, gkn = _attn_ctx(qa, kv, lam_p, subln_t, gq_t, gk_t, lam_init, batch, seq)
        os_, sf, sb = _ssd(z, xbc, dt, ssd_params, None, l, batch, seq)
        xp = _post(xp, oa, os_, mods, l, ctx_row, wo_bf, wfi_bf, wfo_bf, ln_g[l], ln_b[l], alpha, tm_ctx)
        new_dk.append(kv[:, 0:DIFF_W].reshape(batch, seq, DIFF_HEADS, 2 * DIFF_QK))
        new_dv.append(kv[:, DIFF_W:2 * DIFF_W].reshape(batch, seq, DIFF_HEADS, DIFF_V))
        new_gk.append(gkn.reshape(batch, seq, GQA_KV_HEADS, GQA_HD))
        new_gv.append(kv[:, 2 * DIFF_W + GQA_KV_W:].reshape(batch, seq, GQA_KV_HEADS, GQA_HD))
        new_sf.append(sf.reshape(batch, SSD_HEADS, SSD_HD, SSD_STATE))
        new_sb.append(sb.reshape(batch, SSD_HEADS, SSD_HD, SSD_STATE))

        qa, kv, z, xbc, dt = _inproj(xs, mods, l, lat_row, w_in_bf, tm_lat)
        oa = _attn_lat(qa, kv, caches, l, tables, lam_p, subln_t, gq_t, gk_t, lam_init,
                       dec_batch, dec_seq, tq)
        init = (state_ssd_fwd.reshape(dec_batch, depth, SSD_INNER, SSD_STATE),
                state_ssd_bwd.reshape(dec_batch, depth, SSD_INNER, SSD_STATE))
        os_, _, _ = _ssd(z, xbc, dt, ssd_params, init, l, dec_batch, dec_seq)
        xs = _post(xs, oa, os_, mods, l, lat_row, wo_bf, wfi_bf, wfo_bf, ln_g[l], ln_b[l], alpha, tm_lat)

    return (xp.reshape(batch, seq, d), xs.reshape(dec_batch, dec_seq, d),
            jnp.stack(new_dk, 1), jnp.stack(new_dv, 1), jnp.stack(new_gk, 1),
            jnp.stack(new_gv, 1), jnp.stack(new_sf, 1), jnp.stack(new_sb, 1))
```

```python
import functools
import math

import jax
import jax.numpy as jnp
from jax import lax
from jax.experimental import pallas as pl
from jax.experimental.pallas import tpu as pltpu

F32 = jnp.float32
BF16 = jnp.bfloat16

GRID_W = 64
DIFF_HEADS = 4
DIFF_QK = 32
DIFF_V = 64
DIFF_W = DIFF_HEADS * DIFF_V
GQA_HD = 64
GQA_Q_HEADS = 4
GQA_KV_HEADS = 2
GQA_W = GQA_Q_HEADS * GQA_HD
GQA_KV_W = GQA_KV_HEADS * GQA_HD
SSD_HD = 64
SSD_HEADS = 8
SSD_INNER = SSD_HEADS * SSD_HD
SSD_GROUPS = 2
SSD_STATE = 64
SSD_BC_W = SSD_GROUPS * SSD_STATE
SSD_CONV = 5
SSD_CHUNK = 128
XBC_W = SSD_INNER + 2 * SSD_BC_W
DT_W = 2 * SSD_HEADS
ROPE_THETA = 10000.0
EPS = 1e-5
LANES = 128
VMEM_LIMIT = 56 * 1024 * 1024

_C_DQ, _C_DK, _C_DV, _C_GQ, _C_GK, _C_GV, _C_Z, _C_XBC, _C_DT, _C_END = (
    0, 256, 512, 768, 1024, 1152, 1280, 1792, 2560, 2576)
IN_W_PAD = _C_DT + LANES


def _dot(a, b):
    return jnp.dot(a, b, preferred_element_type=F32)


def _dot_nt(a, b):
    return lax.dot_general(a, b, (((1,), (1,)), ((), ())), preferred_element_type=F32)


def _split3(a):
    a1 = a.astype(BF16)
    r1 = a - a1.astype(F32)
    a2 = r1.astype(BF16)
    a3 = (r1 - a2.astype(F32)).astype(BF16)
    return a1, a2, a3


def _dot3_l(a, b_exact):
    a1, a2, a3 = _split3(a)
    return _dot(a1, b_exact) + (_dot(a2, b_exact) + _dot(a3, b_exact))


def _dot3_r(a_exact, b):
    b1, b2, b3 = _split3(b)
    return _dot(a_exact, b1) + (_dot(a_exact, b2) + _dot(a_exact, b3))


def _sigmoid(x):
    return 1.0 / (1.0 + jnp.exp(-x))


def _silu(x):
    return x * _sigmoid(x)


def _layer_norm(x, g, b):
    mu = jnp.mean(x, axis=-1, keepdims=True)
    xc = x - mu
    var = jnp.mean(xc * xc, axis=-1, keepdims=True)
    return xc * lax.rsqrt(var + EPS) * g + b


def _group_avg_matrix(width, group):
    sh = int(math.log2(group))
    r = lax.shift_right_logical(lax.broadcasted_iota(jnp.int32, (width, width), 0), sh)
    c = lax.shift_right_logical(lax.broadcasted_iota(jnp.int32, (width, width), 1), sh)
    return jnp.where(r == c, 1.0 / group, 0.0).astype(BF16)


def _group_mean_sq(x, gmat):
    xx = x * x
    hi = xx.astype(BF16)
    lo = (xx - hi.astype(F32)).astype(BF16)
    return _dot(hi, gmat) + _dot(lo, gmat)


def _rope(x, cos, sin_signed):
    w = x.shape[-1]
    lane = lax.broadcasted_iota(jnp.int32, x.shape, 1)
    nxt = pltpu.roll(x, w - 1, 1)
    prv = pltpu.roll(x, 1, 1)
    partner = jnp.where((lane & 1) == 0, nxt, prv)
    return x * cos + partner * sin_signed


def _lane_mask(shape, lo, hi):
    lane = lax.broadcasted_iota(jnp.int32, shape, 1)
    return (lane >= lo) & (lane < hi)


def _mod_kernel(c_ref, w_ref, b_ref, o_ref):
    a = _silu(c_ref[...])
    a_hi = a.astype(BF16)
    a_lo = (a - a_hi.astype(F32)).astype(BF16)
    w = w_ref[...]
    w_hi = w.astype(BF16)
    w_lo = (w - w_hi.astype(F32)).astype(BF16)
    o_ref[...] = _dot(a_hi, w_hi) + (_dot(a_lo, w_hi) + _dot(a_hi, w_lo)) + b_ref[...]


def _modulation(cvec, w_ada, b_ada):
    depth, d, n = w_ada.shape
    tn = 1536
    rows = cvec.shape[0]
    return pl.pallas_call(
        _mod_kernel,
        grid=(depth, n // tn),
        in_specs=[
            pl.BlockSpec((rows, d), lambda l, j: (0, 0)),
            pl.BlockSpec((None, d, tn), lambda l, j: (l, 0, j)),
            pl.BlockSpec((None, 1, tn), lambda l, j: (l, 0, j)),
        ],
        out_specs=pl.BlockSpec((None, rows, tn), lambda l, j: (l, 0, j)),
        out_shape=jax.ShapeDtypeStruct((depth, rows, n), F32),
        compiler_params=pltpu.CompilerParams(
            dimension_semantics=("arbitrary", "arbitrary"), vmem_limit_bytes=VMEM_LIMIT),
    )(cvec, w_ada, b_ada.reshape(depth, 1, n))


def _inproj_kernel(x_ref, sh_ref, sc_ref, w_ref, qa_ref, kv_ref, z_ref, xbc_ref, dt_ref):
    h = (x_ref[...] * (1.0 + sc_ref[...]) + sh_ref[...]).astype(BF16)

    def mm(lo, hi):
        return _dot(h, w_ref[:, lo:hi])

    qa_ref[:, 0:DIFF_W] = mm(_C_DQ, _C_DK)
    qa_ref[:, DIFF_W:DIFF_W + GQA_W] = mm(_C_GQ, _C_GK)
    kv_ref[:, 0:2 * DIFF_W] = mm(_C_DK, _C_GQ)
    kv_ref[:, 2 * DIFF_W:2 * DIFF_W + 2 * GQA_KV_W] = mm(_C_GK, _C_Z)
    z_ref[...] = mm(_C_Z, _C_XBC)
    xbc_ref[...] = mm(_C_XBC, _C_DT)
    dt_ref[...] = mm(_C_DT, IN_W_PAD)


def _mod_spec(layer, which, row_fn, d):
    return pl.BlockSpec((None, None, 1, d), lambda i: (layer, row_fn(i), 0, which))


def _inproj(x, mods, layer, row_fn, w_in_bf, tm):
    t, d = x.shape
    widths = (DIFF_W + GQA_W, 2 * DIFF_W + 2 * GQA_KV_W, SSD_INNER, XBC_W, LANES)
    return pl.pallas_call(
        _inproj_kernel,
        grid=(t // tm,),
        in_specs=[
            pl.BlockSpec((tm, d), lambda i: (i, 0)),
            _mod_spec(layer, 0, row_fn, d),
            _mod_spec(layer, 1, row_fn, d),
            pl.BlockSpec((d, IN_W_PAD), lambda i: (0, 0)),
        ],
        out_specs=[pl.BlockSpec((tm, w), lambda i: (i, 0)) for w in widths],
        out_shape=[jax.ShapeDtypeStruct((t, w), F32) for w in widths],
        compiler_params=pltpu.CompilerParams(
            dimension_semantics=("arbitrary",), vmem_limit_bytes=VMEM_LIMIT),
    )(x, mods, mods, w_in_bf)


LOG2E = 1.4426950408889634
NEG_BIG = -1e30
N_SCORE_HEADS = 2 * DIFF_HEADS + GQA_Q_HEADS


def _diff_lambda(lam_ref, lam_init):
    lp = lam_ref[...]
    s1 = jnp.sum(lp[0:1, :] * lp[1:2, :], axis=-1, keepdims=True)
    s2 = jnp.sum(lp[2:3, :] * lp[3:4, :], axis=-1, keepdims=True)
    return jnp.exp(s1) - jnp.exp(s2) + lam_init


def _swap_halves(x):
    return pltpu.roll(x, GQA_HD, 1)


def _attn_kernel(cfg, *refs):
    lam_init, seq, past, kb, rope, emit_gk = cfg
    it = iter(refs)
    qa_ref, kv_ref = next(it), next(it)
    if past:
        cdk_ref, cdv_ref, cgk_ref, cgv_ref = next(it), next(it), next(it), next(it)
    if rope:
        cosq_ref, sinq_ref, cosk_ref, sink_ref = next(it), next(it), next(it), next(it)
    lam_ref, gain_ref, gq_ref, gk_ref = next(it), next(it), next(it), next(it)
    o_ref = next(it)
    gkn_ref = next(it) if emit_gk else None
    kd_s, kg_s, vdt_s, vgt_s, wq_s, wg_s, s0_s, s1_s, p_s, m_s, l_s, alpha_s, acc_s = it
    tq = qa_ref.shape[0]
    n_blocks = (seq + past) // kb

    @pl.when(pl.program_id(1) == 0)
    def _prepare_keys():
        gmat = _group_avg_matrix(GQA_KV_W, GQA_HD)
        for i in range(seq // kb):
            rows = slice(i * kb, (i + 1) * kb)
            kv = kv_ref[rows, :]
            dk = kv[:, 0:DIFF_W]
            gk = kv[:, 2 * DIFF_W:2 * DIFF_W + GQA_KV_W]
            gk = gk * lax.rsqrt(_group_mean_sq(gk, gmat) + EPS) * gk_ref[...]
            if emit_gk:
                gkn_ref[rows, :] = gk
            if rope:
                ck = cosk_ref[rows, :]
                sk = sink_ref[rows, :]
                dk = _rope(dk, ck[:, 0:DIFF_W], sk[:, 0:DIFF_W])
                gk = _rope(gk, ck[:, DIFF_W:DIFF_W + GQA_KV_W], sk[:, DIFF_W:DIFF_W + GQA_KV_W])
            kd_s[rows, :] = dk.astype(BF16)
            kg_s[rows, :] = gk.astype(BF16)
            vdt_s[i] = kv[:, DIFF_W:2 * DIFF_W].T.astype(BF16)
            vgt_s[i] = kv[:, 2 * DIFF_W + GQA_KV_W:2 * DIFF_W + 2 * GQA_KV_W].T.astype(BF16)
        for j in range(past // kb):
            src = slice(j * kb, (j + 1) * kb)
            dst = slice(seq + j * kb, seq + (j + 1) * kb)
            kd_s[dst, :] = cdk_ref[src, :].astype(BF16)
            kg_s[dst, :] = cgk_ref[src, :].astype(BF16)
            vdt_s[seq // kb + j] = cdv_ref[src, :].T.astype(BF16)
            vgt_s[seq // kb + j] = cgv_ref[src, :].T.astype(BF16)

    qa = qa_ref[...]
    qd = qa[:, 0:DIFF_W]
    gq = qa[:, DIFF_W:DIFF_W + GQA_W]
    gq = gq * lax.rsqrt(_group_mean_sq(gq, _group_avg_matrix(GQA_W, GQA_HD)) + EPS) * gq_ref[...]
    if rope:
        cq = cosq_ref[...]
        sq = sinq_ref[...]
        qd = _rope(qd, cq[:, 0:DIFF_W], sq[:, 0:DIFF_W])
        gq = _rope(gq, cq[:, DIFF_W:DIFF_W + GQA_W], sq[:, DIFF_W:DIFF_W + GQA_W])
    qd_t = (qd * (DIFF_QK ** -0.5 * LOG2E)).T
    gq_t = (gq * (GQA_HD ** -0.5 * LOG2E)).T
    row = lax.broadcasted_iota(jnp.int32, (DIFF_W, tq), 0)
    for hm in range(2 * DIFF_HEADS):
        lo = hm * DIFF_QK
        wq_s[:, hm * tq:(hm + 1) * tq] = jnp.where((row >= lo) & (row < lo + DIFF_QK), qd_t, 0.0).astype(BF16)
    zeros = jnp.zeros((GQA_HD, tq), F32)
    for h in range(GQA_Q_HEADS):
        piece = gq_t[h * GQA_HD:(h + 1) * GQA_HD, :]
        pair = [piece, zeros] if h // (GQA_Q_HEADS // GQA_KV_HEADS) == 0 else [zeros, piece]
        wg_s[:, h * tq:(h + 1) * tq] = jnp.concatenate(pair, axis=0).astype(BF16)
    m_s[...] = jnp.full(m_s.shape, NEG_BIG, F32)
    l_s[...] = jnp.zeros(l_s.shape, F32)
    acc_s[...] = jnp.zeros(acc_s.shape, F32)
    n_diff = 2 * DIFF_HEADS * tq
    n_all = N_SCORE_HEADS * tq

    def scores(j, s_buf):
        rows = pl.ds(j * kb if isinstance(j, int) else pl.multiple_of(j * kb, kb), kb)
        s_buf[:, 0:n_diff] = _dot(kd_s[rows, :], wq_s[...])
        s_buf[:, n_diff:n_all] = _dot(kg_s[rows, :], wg_s[...])

    def accumulate(j, s_buf):
        for c in range(n_all // LANES):
            cols = slice(c * LANES, (c + 1) * LANES)
            s = s_buf[:, cols]
            m_old = m_s[:, cols]
            m_new = jnp.maximum(m_old, jnp.max(s, axis=0, keepdims=True))
            alpha = jnp.exp2(m_old - m_new)
            p = jnp.exp2(s - m_new)
            l_s[:, cols] = alpha * l_s[:, cols] + jnp.sum(p, axis=0, keepdims=True)
            m_s[:, cols] = m_new
            alpha_s[:, cols] = alpha
            p_s[:, cols] = p.astype(BF16)
        v_d = vdt_s[j]
        v_g = vgt_s[j]
        for idx in range(N_SCORE_HEADS):
            cols = slice(idx * tq, (idx + 1) * tq)
            if idx < 2 * DIFF_HEADS:
                vh = idx // 2
                v_t = v_d[vh * DIFF_V:(vh + 1) * DIFF_V, :]
            else:
                vh = (idx - 2 * DIFF_HEADS) // (GQA_Q_HEADS // GQA_KV_HEADS)
                v_t = v_g[vh * GQA_HD:(vh + 1) * GQA_HD, :]
            acc_s[idx] = alpha_s[:, cols] * acc_s[idx] + _dot(v_t, p_s[:, cols])

    scores(0, s0_s)
    if n_blocks > 1:
        def block_pair(i, carry):
            scores(2 * i + 1, s1_s)
            accumulate(2 * i, s0_s)
            scores(2 * i + 2, s0_s)
            accumulate(2 * i + 1, s1_s)
            return carry

        lax.fori_loop(0, n_blocks // 2 - 1, block_pair, 0)
        scores(n_blocks - 1, s1_s)
        accumulate(n_blocks - 2, s0_s)
        accumulate(n_blocks - 1, s1_s)
    else:
        accumulate(0, s0_s)

    lam = _diff_lambda(lam_ref, lam_init)
    outs = []
    for h in range(DIFF_HEADS):
        c0 = slice(2 * h * tq, (2 * h + 1) * tq)
        c1 = slice((2 * h + 1) * tq, (2 * h + 2) * tq)
        o = acc_s[2 * h] * (1.0 / l_s[:, c0]) - acc_s[2 * h + 1] * (lam / l_s[:, c1])
        ms = jnp.mean(o * o, axis=0, keepdims=True)
        outs.append(o * (lax.rsqrt(ms + EPS) * (1.0 - lam_init)))
    for h in range(GQA_Q_HEADS):
        idx = 2 * DIFF_HEADS + h
        outs.append(acc_s[idx] * (1.0 / l_s[:, idx * tq:(idx + 1) * tq]))
    o_ref[...] = jnp.concatenate(outs, axis=0).T * gain_ref[...]


def _attention(qa, kv, caches, layer, tables, lam_p, gain, gq_t, gk_t, lam_init, batch, seq, tq,
               emit_gk):
    t = qa.shape[0]
    nq = seq // tq
    past = caches[0].shape[2] if caches is not None else 0
    kb = min(256, seq)
    assert seq % kb == 0 and past % kb == 0 and seq % tq == 0
    n_blocks = (seq + past) // kb
    assert n_blocks == 1 or n_blocks % 2 == 0
    n_all = N_SCORE_HEADS * tq
    full = lambda a: pl.BlockSpec(a.shape, lambda b, q: (0,) * a.ndim)
    in_specs = [pl.BlockSpec((tq, qa.shape[1]), lambda b, q: (b * nq + q, 0)),
                pl.BlockSpec((seq, kv.shape[1]), lambda b, q: (b, 0))]
    args = [qa, kv]
    if caches is not None:
        in_specs += [pl.BlockSpec((None, None, past, a.shape[3]), lambda b, q: (b, layer, 0, 0))
                     for a in caches]
        args += list(caches)
    if tables is not None:
        cosq, sinq, cosk, sink = tables
        in_specs += [pl.BlockSpec((tq, cosq.shape[1]), lambda b, q: (q, 0)),
                     pl.BlockSpec((tq, sinq.shape[1]), lambda b, q: (q, 0)),
                     full(cosk), full(sink)]
        args += [cosq, sinq, cosk, sink]
    in_specs += [full(lam_p), full(gain), full(gq_t), full(gk_t)]
    args += [lam_p, gain, gq_t, gk_t]
    out_specs = [pl.BlockSpec((tq, DIFF_W + GQA_W), lambda b, q: (b * nq + q, 0))]
    out_shape = [jax.ShapeDtypeStruct((t, DIFF_W + GQA_W), F32)]
    if emit_gk:
        out_specs.append(pl.BlockSpec((seq, GQA_KV_W), lambda b, q: (b, 0)))
        out_shape.append(jax.ShapeDtypeStruct((t, GQA_KV_W), F32))
    cfg = (lam_init, seq, past, kb, tables is not None, emit_gk)
    return pl.pallas_call(
        functools.partial(_attn_kernel, cfg),
        grid=(batch, nq),
        in_specs=in_specs,
        out_specs=out_specs,
        out_shape=out_shape,
        scratch_shapes=[
            pltpu.VMEM((seq + past, DIFF_W), BF16), pltpu.VMEM((seq + past, GQA_KV_W), BF16),
            pltpu.VMEM((n_blocks, DIFF_W, kb), BF16), pltpu.VMEM((n_blocks, GQA_KV_W, kb), BF16),
            pltpu.VMEM((DIFF_W, 2 * DIFF_HEADS * tq), BF16), pltpu.VMEM((GQA_KV_W, GQA_Q_HEADS * tq), BF16),
            pltpu.VMEM((kb, n_all), F32), pltpu.VMEM((kb, n_all), F32), pltpu.VMEM((kb, n_all), BF16),
            pltpu.VMEM((1, n_all), F32), pltpu.VMEM((1, n_all), F32), pltpu.VMEM((1, n_all), F32),
            pltpu.VMEM((N_SCORE_HEADS, GQA_HD, tq), F32),
        ],
        compiler_params=pltpu.CompilerParams(
            dimension_semantics=("arbitrary", "arbitrary"), vmem_limit_bytes=VMEM_LIMIT),
    )(*args)


def _softplus(x):
    return jnp.maximum(x, 0.0) + jnp.log1p(jnp.exp(-jnp.abs(x)))


def _ssd_kernel(has_init, seq, *refs):
    if has_init:
        (z_ref, xbc_ref, dt_ref, cw_ref, cb_ref, alog_ref, dtb_ref, dexp_ref, ng_ref,
         sf0_ref, sb0_ref, o_ref, sf_ref, sb_ref,
         xs_s, bm_s, cm_s, dt_s, halo_s, st_s) = refs
    else:
        (z_ref, xbc_ref, dt_ref, cw_ref, cb_ref, alog_ref, dtb_ref, dexp_ref, ng_ref,
         o_ref, sf_ref, sb_ref,
         xs_s, bm_s, cm_s, dt_s, halo_s, st_s) = refs
        sf0_ref = sb0_ref = None
    L = SSD_CHUNK
    nchunks = seq // L
    halo = 8

    ri = lax.broadcasted_iota(jnp.int32, (L, L), 0)
    ci = lax.broadcasted_iota(jnp.int32, (L, L), 1)
    lower = ri >= ci
    upper = ri <= ci
    lane_row = lax.broadcasted_iota(jnp.int32, (1, LANES), 1)
    a_row = jnp.where(lane_row < DT_W, -jnp.exp(alog_ref[...]), 0.0)
    ej = lax.broadcasted_iota(jnp.int32, (LANES, SSD_INNER), 0)
    eh = lax.shift_right_logical(lax.broadcasted_iota(jnp.int32, (LANES, SSD_INNER), 1), 6)
    group_mask = jnp.where(lax.shift_right_logical(ej, 6) == lax.shift_right_logical(eh, 2), 1.0, 0.0)

    def chunk_rows(c):
        return pl.ds(pl.multiple_of(c * L, L), L)

    def conv_body(c, carry):
        r0 = c * L
        rows = chunk_rows(c)
        prev = xbc_ref[pl.ds(pl.multiple_of(jnp.maximum(r0 - halo, 0), halo), halo), :]
        nxt = xbc_ref[pl.ds(pl.multiple_of(jnp.minimum(r0 + L, seq - halo), halo), halo), :]
        halo_s[0:halo, :] = jnp.where(c > 0, prev, 0.0)
        halo_s[halo:halo + L, :] = xbc_ref[rows, :]
        halo_s[halo + L:2 * halo + L, :] = jnp.where(c < nchunks - 1, nxt, 0.0)
        acc = cb_ref[...] + halo_s[pl.ds(halo - SSD_CONV // 2, L), :] * cw_ref[0:1, :]
        for j in range(1, SSD_CONV):
            acc = acc + halo_s[pl.ds(halo - SSD_CONV // 2 + j, L), :] * cw_ref[j:j + 1, :]
        act = _silu(acc)
        xs_s[rows, :] = act[:, 0:SSD_INNER]
        bm_s[rows, :] = act[:, SSD_INNER:SSD_INNER + SSD_BC_W]
        cm_s[rows, :] = act[:, SSD_INNER + SSD_BC_W:XBC_W]
        dt_s[rows, :] = _softplus(dt_ref[rows, :] + dtb_ref[...])
        return carry

    lax.fori_loop(0, nchunks, conv_body, 0)

    def scan(direction):
        tri = jnp.where(lower if direction == 0 else upper, 1.0, 0.0).astype(BF16)
        causal = lower if direction == 0 else upper
        expand = jnp.where(ej == eh + direction * SSD_HEADS, 1.0, 0.0).astype(BF16)
        edge_row = L - 1 if direction == 0 else 0

        def body(i, carry):
            c = i if direction == 0 else nchunks - 1 - i
            rows = chunk_rows(c)
            x_c = xs_s[rows, :]
            b_c = bm_s[rows, :]
            c_c = cm_s[rows, :]
            dt_c = dt_s[rows, :]
            dta = dt_c * a_row
            acs = _dot3_r(tri, dta)
            acs_t = acs.T
            dt_e = _dot3_l(dt_c, expand)
            acs_e = _dot3_l(acs, expand)
            edge = acs_e[edge_row:edge_row + 1, :]
            xd = x_c * dt_e
            xd_b = xd.astype(BF16)
            b_b = b_c.astype(BF16)
            st = st_s[...]
            y = _dot(c_c.astype(BF16), st.astype(BF16)) * jnp.exp(acs_e)
            pairs = []
            for g in range(SSD_GROUPS):
                cg = jnp.where(_lane_mask(c_c.shape, g * SSD_STATE, (g + 1) * SSD_STATE), c_c, 0.0)
                cb = _dot_nt(cg.astype(BF16), b_b)
                for hp in range(2):
                    pair = g * 2 + hp
                    res = []
                    for k in range(2):
                        j = direction * SSD_HEADS + pair * 2 + k
                        diff = acs[:, j:j + 1] - acs_t[j:j + 1, :]
                        dec = jnp.where(causal, jnp.exp(jnp.minimum(diff, 0.0)), 0.0)
                        sc = (cb * dec).astype(BF16)
                        res.append(_dot(sc, xd_b[:, pair * LANES:(pair + 1) * LANES]))
                    pairs.append(jnp.where(_lane_mask(res[0].shape, 0, SSD_HD), res[0], res[1]))
            y = y + jnp.concatenate(pairs, axis=-1)
            xdw = (xd * jnp.exp(edge - acs_e)).astype(BF16)
            s_new = _dot(b_c.T.astype(BF16), xdw) * group_mask
            st_s[...] = st * jnp.exp(edge) + s_new
            if direction == 0:
                o_ref[rows, :] = y + x_c * dexp_ref[...]
            else:
                yt = (o_ref[rows, :] + y) * _silu(z_ref[rows, :])
                ms = jnp.mean(yt * yt, axis=-1, keepdims=True)
                o_ref[rows, :] = yt * lax.rsqrt(ms + EPS) * ng_ref[...]
            return carry

        lax.fori_loop(0, nchunks, body, 0)

    def load_state(ref):
        r = lax.broadcasted_iota(jnp.int32, (SSD_STATE, LANES), 0)
        c = lax.broadcasted_iota(jnp.int32, (SSD_STATE, LANES), 1)
        dup = jnp.where((c & (SSD_STATE - 1)) == r, 1.0, 0.0).astype(BF16)
        return _dot3_l(ref[...], dup).T * group_mask

    def store_state(ref):
        st_t = st_s[...].T
        ref[...] = (st_t + _swap_halves(st_t))[:, 0:SSD_STATE]

    st_s[...] = load_state(sf0_ref) if has_init else jnp.zeros(st_s.shape, F32)
    scan(0)
    store_state(sf_ref)
    st_s[...] = load_state(sb0_ref) if has_init else jnp.zeros(st_s.shape, F32)
    scan(1)
    store_state(sb_ref)


def _ssd(z, xbc, dt, params, init, layer, batch, seq):
    cw, cb, alog, dtb, dexp, ng = params
    t = z.shape[0]
    has_init = init is not None
    full = lambda a: pl.BlockSpec(a.shape, lambda b: (0,) * a.ndim)
    st_spec = pl.BlockSpec((None, SSD_INNER, SSD_STATE), lambda b: (b, 0, 0))
    in_specs = [
        pl.BlockSpec((seq, SSD_INNER), lambda b: (b, 0)),
        pl.BlockSpec((seq, XBC_W), lambda b: (b, 0)),
        pl.BlockSpec((seq, LANES), lambda b: (b, 0)),
        full(cw), full(cb), full(alog), full(dtb), full(dexp), full(ng),
    ]
    args = [z, xbc, dt, cw, cb, alog, dtb, dexp, ng]
    if has_init:
        init_spec = pl.BlockSpec((None, None, SSD_INNER, SSD_STATE), lambda b: (b, layer, 0, 0))
        in_specs += [init_spec, init_spec]
        args += list(init)
    st_shape = jax.ShapeDtypeStruct((batch, SSD_INNER, SSD_STATE), F32)
    return pl.pallas_call(
        functools.partial(_ssd_kernel, has_init, seq),
        grid=(batch,),
        in_specs=in_specs,
        out_specs=[pl.BlockSpec((seq, SSD_INNER), lambda b: (b, 0)), st_spec, st_spec],
        out_shape=[jax.ShapeDtypeStruct((t, SSD_INNER), F32), st_shape, st_shape],
        scratch_shapes=[
            pltpu.VMEM((seq, SSD_INNER), F32), pltpu.VMEM((seq, SSD_BC_W), F32),
            pltpu.VMEM((seq, SSD_BC_W), F32), pltpu.VMEM((seq, LANES), F32),
            pltpu.VMEM((SSD_CHUNK + 16, XBC_W), F32), pltpu.VMEM((LANES, SSD_INNER), F32),
        ],
        compiler_params=pltpu.CompilerParams(
            dimension_semantics=("arbitrary",), vmem_limit_bytes=VMEM_LIMIT),
    )(*args)


def _post_kernel(alpha, d_ff, ff_chunk, x_ref, oa_ref, os_ref, g1_ref, sh2_ref, sc2_ref, g2_ref,
                 wo_ref, wfi_ref, wfo_ref, lng_ref, lnb_ref, y_ref):
    x = x_ref[...]
    wa = oa_ref.shape[1]
    o = _dot(oa_ref[...].astype(BF16), wo_ref[0:wa, :]) + _dot(os_ref[...].astype(BF16), wo_ref[wa:, :])
    x1 = _layer_norm(alpha * x + g1_ref[...] * o, lng_ref[0:1, :], lnb_ref[0:1, :])
    h2 = (x1 * (1.0 + sc2_ref[...]) + sh2_ref[...]).astype(BF16)
    f = jnp.zeros(x.shape, F32)
    for c in range(d_ff // ff_chunk):
        lo = c * ff_chunk
        g = _dot(h2, wfi_ref[:, lo:lo + ff_chunk])
        u = _dot(h2, wfi_ref[:, d_ff + lo:d_ff + lo + ff_chunk])
        f = f + _dot((_silu(g) * u).astype(BF16), wfo_ref[lo:lo + ff_chunk, :])
    y_ref[...] = _layer_norm(alpha * x1 + g2_ref[...] * f, lng_ref[1:2, :], lnb_ref[1:2, :])


def _post(x, oa, os_, mods, layer, row_fn, wo_bf, wfi_bf, wfo_bf, lng, lnb, alpha, tm):
    t, d = x.shape
    d_ff = wfo_bf.shape[0]
    ff_chunk = d_ff // 2
    resident = lambda a: pl.BlockSpec(a.shape, lambda i: (0,) * a.ndim, pipeline_mode=pl.Buffered(1))
    return pl.pallas_call(
        functools.partial(_post_kernel, alpha, d_ff, ff_chunk),
        grid=(t // tm,),
        in_specs=[
            pl.BlockSpec((tm, d), lambda i: (i, 0)),
            pl.BlockSpec((tm, oa.shape[1]), lambda i: (i, 0)),
            pl.BlockSpec((tm, os_.shape[1]), lambda i: (i, 0)),
            _mod_spec(layer, 2, row_fn, d),
            _mod_spec(layer, 3, row_fn, d),
            _mod_spec(layer, 4, row_fn, d),
            _mod_spec(layer, 5, row_fn, d),
            resident(wo_bf), resident(wfi_bf), resident(wfo_bf),
            pl.BlockSpec(lng.shape, lambda i: (0, 0)),
            pl.BlockSpec(lnb.shape, lambda i: (0, 0)),
        ],
        out_specs=pl.BlockSpec((tm, d), lambda i: (i, 0)),
        out_shape=jax.ShapeDtypeStruct((t, d), F32),
        compiler_params=pltpu.CompilerParams(
            dimension_semantics=("arbitrary",), vmem_limit_bytes=VMEM_LIMIT),
    )(x, oa, os_, mods, mods, mods, mods, wo_bf, wfi_bf, wfo_bf, lng, lnb)


def _rope_tables(rows, dim, copies):
    row = jnp.repeat(jnp.arange(rows), GRID_W).astype(F32)
    col = jnp.tile(jnp.arange(GRID_W), rows).astype(F32)
    n_freq = dim // 4
    inv = ROPE_THETA ** (-jnp.arange(n_freq, dtype=F32) / n_freq)
    ang = jnp.concatenate([row[:, None] * inv, col[:, None] * inv], -1)
    cos, sin = jnp.cos(ang), jnp.sin(ang)
    cos_full = jnp.repeat(cos, 2, axis=-1)
    sin_signed = jnp.stack([-sin, sin], axis=-1).reshape(sin.shape[0], dim)
    return jnp.tile(cos_full, (1, copies)), jnp.tile(sin_signed, (1, copies))


def kernel(x_prompt, x_sample, cache_diff_k, cache_diff_v, cache_gqa_k, cache_gqa_v, state_ssd_fwd, state_ssd_bwd, c, c_ctx, w_ada, b_ada, w_in, w_out, diff_lambda, diff_subln_g, qk_norm_g, ssd_conv_w, ssd_conv_b, ssd_A_log, ssd_dt_bias, ssd_D, ssd_norm_g, ln_g, ln_b, w_ffn_in, w_ffn_out):
    batch, seq, d = x_prompt.shape
    dec_batch, dec_seq, _ = x_sample.shape
    depth = w_in.shape[0]
    past = cache_diff_k.shape[2]
    alpha = (2 * depth) ** 0.25
    rows = dec_seq // GRID_W

    n_vec = 1 + dec_batch
    n_pad = -(-n_vec // 8) * 8
    cvec = jnp.concatenate([c_ctx[None, :], c, jnp.zeros((n_pad - n_vec, d), F32)], axis=0)
    mods = _modulation(cvec, w_ada, b_ada).reshape(depth, n_pad, 1, 6 * d)

    cos_d, sin_d = _rope_tables(rows, DIFF_QK, DIFF_W // DIFF_QK)
    cos_g, sin_g = _rope_tables(rows, GQA_HD, GQA_Q_HEADS)
    cosq = jnp.concatenate([cos_d, cos_g], axis=-1)
    sinq = jnp.concatenate([sin_d, sin_g], axis=-1)
    cosk = jnp.concatenate([cos_d, cos_g[:, :GQA_KV_W]], axis=-1)
    sink = jnp.concatenate([sin_d, sin_g[:, :GQA_KV_W]], axis=-1)
    tables = (cosq, sinq, cosk, sink)

    caches = (cache_diff_k.reshape(dec_batch, depth, past, DIFF_W),
              cache_diff_v.reshape(dec_batch, depth, past, DIFF_W),
              cache_gqa_k.reshape(dec_batch, depth, past, GQA_KV_W),
              cache_gqa_v.reshape(dec_batch, depth, past, GQA_KV_W))

    xp = x_prompt.reshape(batch * seq, d)
    xs = x_sample.reshape(dec_batch * dec_seq, d)
    tm_ctx = min(512, batch * seq)
    tm_lat = min(512, dec_seq)
    tq = min(256, dec_seq)
    ctx_row = lambda i: 0
    lat_row = lambda i: 1 + (i * tm_lat) // dec_seq

    new_dk, new_dv, new_gk, new_gv, new_sf, new_sb = [], [], [], [], [], []
    for l in range(depth):
        lam_init = 0.8 - 0.6 * math.exp(-0.3 * l)
        w_in_bf = jnp.pad(w_in[l], ((0, 0), (0, IN_W_PAD - w_in.shape[2]))).astype(BF16)
        wo_bf = w_out[l].astype(BF16)
        wfi_bf = w_ffn_in[l].astype(BF16)
        wfo_bf = w_ffn_out[l].astype(BF16)
        lam_p = diff_lambda[l]
        attn_gain = jnp.concatenate([jnp.tile(diff_subln_g[l], DIFF_HEADS), jnp.ones((GQA_W,), F32)])[None, :]
        gq_t = jnp.tile(qk_norm_g[l, 0], GQA_Q_HEADS)[None, :]
        gk_t = jnp.tile(qk_norm_g[l, 1], GQA_KV_HEADS)[None, :]
        pad_row = lambda v: jnp.pad(v.reshape(1, -1), ((0, 0), (0, LANES - v.size)))
        ssd_params = (ssd_conv_w[l], ssd_conv_b[l][None, :], pad_row(ssd_A_log[l]),
                      pad_row(ssd_dt_bias[l]), jnp.repeat(ssd_D[l], SSD_HD)[None, :],
                      ssd_norm_g[l][None, :])

        qa, kv, z, xbc, dt = _inproj(xp, mods, l, ctx_row, w_in_bf, tm_ctx)
        oa, gkn = _attention(qa, kv, None, l, None, lam_p, attn_gain, gq_t, gk_t, lam_init,
                             batch, seq, seq, True)
        os_, sf, sb = _ssd(z, xbc, dt, ssd_params, None, l, batch, seq)
        xp = _post(xp, oa, os_, mods, l, ctx_row, wo_bf, wfi_bf, wfo_bf, ln_g[l], ln_b[l], alpha, tm_ctx)
        new_dk.append(kv[:, 0:DIFF_W].reshape(batch, seq, DIFF_HEADS, 2 * DIFF_QK))
        new_dv.append(kv[:, DIFF_W:2 * DIFF_W].reshape(batch, seq, DIFF_HEADS, DIFF_V))
        new_gk.append(gkn.reshape(batch, seq, GQA_KV_HEADS, GQA_HD))
        new_gv.append(kv[:, 2 * DIFF_W + GQA_KV_W:].reshape(batch, seq, GQA_KV_HEADS, GQA_HD))
        new_sf.append(sf.reshape(batch, SSD_HEADS, SSD_HD, SSD_STATE))
        new_sb.append(sb.reshape(batch, SSD_HEADS, SSD_HD, SSD_STATE))

        qa, kv, z, xbc, dt = _inproj(xs, mods, l, lat_row, w_in_bf, tm_lat)
        (oa,) = _attention(qa, kv, caches, l, tables, lam_p, attn_gain, gq_t, gk_t, lam_init,
                           dec_batch, dec_seq, tq, False)
        init = (state_ssd_fwd.reshape(dec_batch, depth, SSD_INNER, SSD_STATE),
                state_ssd_bwd.reshape(dec_batch, depth, SSD_INNER, SSD_STATE))
        os_, _, _ = _ssd(z, xbc, dt, ssd_params, init, l, dec_batch, dec_seq)
        xs = _post(xs, oa, os_, mods, l, lat_row, wo_bf, wfi_bf, wfo_bf, ln_g[l], ln_b[l], alpha, tm_lat)

    return (xp.reshape(batch, seq, d), xs.reshape(dec_batch, dec_seq, d),
            jnp.stack(new_dk, 1), jnp.stack(new_dv, 1), jnp.stack(new_gk, 1),
            jnp.stack(new_gv, 1), jnp.stack(new_sf, 1), jnp.stack(new_sb, 1))
```

```python
import functools
import math

import jax
import jax.numpy as jnp
from jax import lax
from jax.experimental import pallas as pl
from jax.experimental.pallas import tpu as pltpu

F32 = jnp.float32
BF16 = jnp.bfloat16

GRID_W = 64
DIFF_HEADS = 4
DIFF_QK = 32
DIFF_V = 64
DIFF_W = DIFF_HEADS * DIFF_V
GQA_HD = 64
GQA_Q_HEADS = 4
GQA_KV_HEADS = 2
GQA_W = GQA_Q_HEADS * GQA_HD
GQA_KV_W = GQA_KV_HEADS * GQA_HD
SSD_HD = 64
SSD_HEADS = 8
SSD_INNER = SSD_HEADS * SSD_HD
SSD_GROUPS = 2
SSD_STATE = 64
SSD_BC_W = SSD_GROUPS * SSD_STATE
SSD_CONV = 5
SSD_CHUNK = 128
XBC_W = SSD_INNER + 2 * SSD_BC_W
DT_W = 2 * SSD_HEADS
ROPE_THETA = 10000.0
EPS = 1e-5
LANES = 128
VMEM_LIMIT = 56 * 1024 * 1024

_C_DQ, _C_DK, _C_DV, _C_GQ, _C_GK, _C_GV, _C_Z, _C_XBC, _C_DT, _C_END = (
    0, 256, 512, 768, 1024, 1152, 1280, 1792, 2560, 2576)
IN_W_PAD = _C_DT + LANES


def _dot(a, b):
    return jnp.dot(a, b, preferred_element_type=F32)


def _dot_nt(a, b):
    return lax.dot_general(a, b, (((1,), (1,)), ((), ())), preferred_element_type=F32)


def _split3(a):
    a1 = a.astype(BF16)
    r1 = a - a1.astype(F32)
    a2 = r1.astype(BF16)
    a3 = (r1 - a2.astype(F32)).astype(BF16)
    return a1, a2, a3


def _dot3_l(a, b_exact):
    a1, a2, a3 = _split3(a)
    return _dot(a1, b_exact) + (_dot(a2, b_exact) + _dot(a3, b_exact))


def _dot3_r(a_exact, b):
    b1, b2, b3 = _split3(b)
    return _dot(a_exact, b1) + (_dot(a_exact, b2) + _dot(a_exact, b3))


def _sigmoid(x):
    return 1.0 / (1.0 + jnp.exp(-x))


def _silu(x):
    return x * _sigmoid(x)


def _layer_norm(x, g, b):
    mu = jnp.mean(x, axis=-1, keepdims=True)
    xc = x - mu
    var = jnp.mean(xc * xc, axis=-1, keepdims=True)
    return xc * lax.rsqrt(var + EPS) * g + b


def _group_avg_matrix(width, group):
    sh = int(math.log2(group))
    r = lax.shift_right_logical(lax.broadcasted_iota(jnp.int32, (width, width), 0), sh)
    c = lax.shift_right_logical(lax.broadcasted_iota(jnp.int32, (width, width), 1), sh)
    return jnp.where(r == c, 1.0 / group, 0.0).astype(BF16)


def _group_mean_sq(x, gmat):
    xx = x * x
    hi = xx.astype(BF16)
    lo = (xx - hi.astype(F32)).astype(BF16)
    return _dot(hi, gmat) + _dot(lo, gmat)


def _rope(x, cos, sin_signed):
    w = x.shape[-1]
    lane = lax.broadcasted_iota(jnp.int32, x.shape, 1)
    nxt = pltpu.roll(x, w - 1, 1)
    prv = pltpu.roll(x, 1, 1)
    partner = jnp.where((lane & 1) == 0, nxt, prv)
    return x * cos + partner * sin_signed


def _lane_mask(shape, lo, hi):
    lane = lax.broadcasted_iota(jnp.int32, shape, 1)
    return (lane >= lo) & (lane < hi)


def _mod_kernel(c_ref, w_ref, b_ref, o_ref):
    a = _silu(c_ref[...])
    a_hi = a.astype(BF16)
    a_lo = (a - a_hi.astype(F32)).astype(BF16)
    w = w_ref[...]
    w_hi = w.astype(BF16)
    w_lo = (w - w_hi.astype(F32)).astype(BF16)
    o_ref[...] = _dot(a_hi, w_hi) + (_dot(a_lo, w_hi) + _dot(a_hi, w_lo)) + b_ref[...]


def _modulation(cvec, w_ada, b_ada):
    depth, d, n = w_ada.shape
    tn = 1536
    rows = cvec.shape[0]
    return pl.pallas_call(
        _mod_kernel,
        grid=(depth, n // tn),
        in_specs=[
            pl.BlockSpec((rows, d), lambda l, j: (0, 0)),
            pl.BlockSpec((None, d, tn), lambda l, j: (l, 0, j)),
            pl.BlockSpec((None, 1, tn), lambda l, j: (l, 0, j)),
        ],
        out_specs=pl.BlockSpec((None, rows, tn), lambda l, j: (l, 0, j)),
        out_shape=jax.ShapeDtypeStruct((depth, rows, n), F32),
        compiler_params=pltpu.CompilerParams(
            dimension_semantics=("arbitrary", "arbitrary"), vmem_limit_bytes=VMEM_LIMIT),
    )(cvec, w_ada, b_ada.reshape(depth, 1, n))


def _inproj_kernel(x_ref, sh_ref, sc_ref, w_ref, qa_ref, kv_ref, z_ref, xbc_ref, dt_ref):
    h = (x_ref[...] * (1.0 + sc_ref[...]) + sh_ref[...]).astype(BF16)

    def mm(lo, hi):
        return _dot(h, w_ref[:, lo:hi])

    qa_ref[:, 0:DIFF_W] = mm(_C_DQ, _C_DK)
    qa_ref[:, DIFF_W:DIFF_W + GQA_W] = mm(_C_GQ, _C_GK)
    kv_ref[:, 0:2 * DIFF_W] = mm(_C_DK, _C_GQ)
    kv_ref[:, 2 * DIFF_W:2 * DIFF_W + 2 * GQA_KV_W] = mm(_C_GK, _C_Z)
    z_ref[...] = mm(_C_Z, _C_XBC)
    xbc_ref[...] = mm(_C_XBC, _C_DT)
    dt_ref[...] = mm(_C_DT, IN_W_PAD)


def _mod_spec(layer, which, row_fn, d):
    return pl.BlockSpec((None, None, 1, d), lambda i: (layer, row_fn(i), 0, which))


def _inproj(x, mods, layer, row_fn, w_in_bf, tm):
    t, d = x.shape
    widths = (DIFF_W + GQA_W, 2 * DIFF_W + 2 * GQA_KV_W, SSD_INNER, XBC_W, LANES)
    return pl.pallas_call(
        _inproj_kernel,
        grid=(t // tm,),
        in_specs=[
            pl.BlockSpec((tm, d), lambda i: (i, 0)),
            _mod_spec(layer, 0, row_fn, d),
            _mod_spec(layer, 1, row_fn, d),
            pl.BlockSpec((d, IN_W_PAD), lambda i: (0, 0)),
        ],
        out_specs=[pl.BlockSpec((tm, w), lambda i: (i, 0)) for w in widths],
        out_shape=[jax.ShapeDtypeStruct((t, w), F32) for w in widths],
        compiler_params=pltpu.CompilerParams(
            dimension_semantics=("arbitrary",), vmem_limit_bytes=VMEM_LIMIT),
    )(x, mods, mods, w_in_bf)


LOG2E = 1.4426950408889634
NEG_BIG = -1e30
N_SCORE_HEADS = 2 * DIFF_HEADS + GQA_Q_HEADS


def _diff_lambda(lam_ref, lam_init):
    lp = lam_ref[...]
    s1 = jnp.sum(lp[0:1, :] * lp[1:2, :], axis=-1, keepdims=True)
    s2 = jnp.sum(lp[2:3, :] * lp[3:4, :], axis=-1, keepdims=True)
    return jnp.exp(s1) - jnp.exp(s2) + lam_init


def _swap_halves(x):
    return pltpu.roll(x, GQA_HD, 1)


def _attn_kernel(cfg, *refs):
    lam_init, seq, past, kb, rope, emit_gk = cfg
    it = iter(refs)
    qa_ref, kv_ref = next(it), next(it)
    if past:
        cdk_ref, cdv_ref, cgk_ref, cgv_ref = next(it), next(it), next(it), next(it)
    if rope:
        cosq_ref, sinq_ref, cosk_ref, sink_ref = next(it), next(it), next(it), next(it)
    lam_ref, gain_ref, gq_ref, gk_ref = next(it), next(it), next(it), next(it)
    o_ref = next(it)
    gkn_ref = next(it) if emit_gk else None
    kd_s, kg_s, vdt_s, vgt_s, wq_s, wg_s, s0_s, s1_s, p_s, m_s, l_s, alpha_s, acc_s = it
    tq = qa_ref.shape[0]
    n_blocks = (seq + past) // kb

    @pl.when(pl.program_id(1) == 0)
    def _prepare_keys():
        gmat = _group_avg_matrix(GQA_KV_W, GQA_HD)
        for i in range(seq // kb):
            rows = slice(i * kb, (i + 1) * kb)
            kv = kv_ref[rows, :]
            dk = kv[:, 0:DIFF_W]
            gk = kv[:, 2 * DIFF_W:2 * DIFF_W + GQA_KV_W]
            gk = gk * lax.rsqrt(_group_mean_sq(gk, gmat) + EPS) * gk_ref[...]
            if emit_gk:
                gkn_ref[rows, :] = gk
            if rope:
                ck = cosk_ref[rows, :]
                sk = sink_ref[rows, :]
                dk = _rope(dk, ck[:, 0:DIFF_W], sk[:, 0:DIFF_W])
                gk = _rope(gk, ck[:, DIFF_W:DIFF_W + GQA_KV_W], sk[:, DIFF_W:DIFF_W + GQA_KV_W])
            kd_s[rows, :] = dk.astype(BF16)
            kg_s[rows, :] = gk.astype(BF16)
            vdt_s[i] = kv[:, DIFF_W:2 * DIFF_W].T.astype(BF16)
            vgt_s[i] = kv[:, 2 * DIFF_W + GQA_KV_W:2 * DIFF_W + 2 * GQA_KV_W].T.astype(BF16)
        for j in range(past // kb):
            src = slice(j * kb, (j + 1) * kb)
            dst = slice(seq + j * kb, seq + (j + 1) * kb)
            kd_s[dst, :] = cdk_ref[src, :].astype(BF16)
            kg_s[dst, :] = cgk_ref[src, :].astype(BF16)
            vdt_s[seq // kb + j] = cdv_ref[src, :].T.astype(BF16)
            vgt_s[seq // kb + j] = cgv_ref[src, :].T.astype(BF16)

    qa = qa_ref[...]
    qd = qa[:, 0:DIFF_W]
    gq = qa[:, DIFF_W:DIFF_W + GQA_W]
    gq = gq * lax.rsqrt(_group_mean_sq(gq, _group_avg_matrix(GQA_W, GQA_HD)) + EPS) * gq_ref[...]
    if rope:
        cq = cosq_ref[...]
        sq = sinq_ref[...]
        qd = _rope(qd, cq[:, 0:DIFF_W], sq[:, 0:DIFF_W])
        gq = _rope(gq, cq[:, DIFF_W:DIFF_W + GQA_W], sq[:, DIFF_W:DIFF_W + GQA_W])
    qd_t = (qd * (DIFF_QK ** -0.5 * LOG2E)).T
    gq_t = (gq * (GQA_HD ** -0.5 * LOG2E)).T
    row = lax.broadcasted_iota(jnp.int32, (DIFF_W, tq), 0)
    for hm in range(2 * DIFF_HEADS):
        lo = hm * DIFF_QK
        wq_s[:, hm * tq:(hm + 1) * tq] = jnp.where((row >= lo) & (row < lo + DIFF_QK), qd_t, 0.0).astype(BF16)
    zeros = jnp.zeros((GQA_HD, tq), F32)
    for h in range(GQA_Q_HEADS):
        piece = gq_t[h * GQA_HD:(h + 1) * GQA_HD, :]
        pair = [piece, zeros] if h // (GQA_Q_HEADS // GQA_KV_HEADS) == 0 else [zeros, piece]
        wg_s[:, h * tq:(h + 1) * tq] = jnp.concatenate(pair, axis=0).astype(BF16)
    m_s[...] = jnp.full(m_s.shape, NEG_BIG, F32)
    l_s[...] = jnp.zeros(l_s.shape, F32)
    acc_s[...] = jnp.zeros(acc_s.shape, F32)
    n_diff = 2 * DIFF_HEADS * tq
    n_all = N_SCORE_HEADS * tq

    def scores(j, s_buf):
        rows = pl.ds(j * kb if isinstance(j, int) else pl.multiple_of(j * kb, kb), kb)
        sd = _dot(kd_s[rows, :], wq_s[...])
        for c in range(n_diff // LANES):
            s_buf[c] = sd[:, c * LANES:(c + 1) * LANES]
        sg = _dot(kg_s[rows, :], wg_s[...])
        for c in range((n_all - n_diff) // LANES):
            s_buf[n_diff // LANES + c] = sg[:, c * LANES:(c + 1) * LANES]

    def accumulate(j, s_buf):
        for c in range(n_all // LANES):
            cols = slice(c * LANES, (c + 1) * LANES)
            s = s_buf[c]
            m_old = m_s[:, cols]
            m_new = jnp.maximum(m_old, jnp.max(s, axis=0, keepdims=True))
            alpha = jnp.exp2(m_old - m_new)
            p = jnp.exp2(s - m_new)
            l_s[:, cols] = alpha * l_s[:, cols] + jnp.sum(p, axis=0, keepdims=True)
            m_s[:, cols] = m_new
            alpha_s[:, cols] = alpha
            p_s[c] = p.astype(BF16)
        v_d = vdt_s[j]
        v_g = vgt_s[j]
        slabs = tq // LANES
        for idx in range(N_SCORE_HEADS):
            cols = slice(idx * tq, (idx + 1) * tq)
            if idx < 2 * DIFF_HEADS:
                vh = idx // 2
                v_t = v_d[vh * DIFF_V:(vh + 1) * DIFF_V, :]
            else:
                vh = (idx - 2 * DIFF_HEADS) // (GQA_Q_HEADS // GQA_KV_HEADS)
                v_t = v_g[vh * GQA_HD:(vh + 1) * GQA_HD, :]
            p = jnp.concatenate([p_s[idx * slabs + k] for k in range(slabs)], axis=1)
            acc_s[idx] = alpha_s[:, cols] * acc_s[idx] + _dot(v_t, p)

    scores(0, s0_s)
    if n_blocks > 1:
        def block_pair(i, carry):
            scores(2 * i + 1, s1_s)
            accumulate(2 * i, s0_s)
            scores(2 * i + 2, s0_s)
            accumulate(2 * i + 1, s1_s)
            return carry

        lax.fori_loop(0, n_blocks // 2 - 1, block_pair, 0)
        scores(n_blocks - 1, s1_s)
        accumulate(n_blocks - 2, s0_s)
        accumulate(n_blocks - 1, s1_s)
    else:
        accumulate(0, s0_s)

    lam = _diff_lambda(lam_ref, lam_init)
    outs = []
    for h in range(DIFF_HEADS):
        c0 = slice(2 * h * tq, (2 * h + 1) * tq)
        c1 = slice((2 * h + 1) * tq, (2 * h + 2) * tq)
        o = acc_s[2 * h] * (1.0 / l_s[:, c0]) - acc_s[2 * h + 1] * (lam / l_s[:, c1])
        ms = jnp.mean(o * o, axis=0, keepdims=True)
        outs.append(o * (lax.rsqrt(ms + EPS) * (1.0 - lam_init)))
    for h in range(GQA_Q_HEADS):
        idx = 2 * DIFF_HEADS + h
        outs.append(acc_s[idx] * (1.0 / l_s[:, idx * tq:(idx + 1) * tq]))
    o_ref[...] = jnp.concatenate(outs, axis=0).T * gain_ref[...]


def _attention(qa, kv, caches, layer, tables, lam_p, gain, gq_t, gk_t, lam_init, batch, seq, tq,
               emit_gk):
    t = qa.shape[0]
    nq = seq // tq
    past = caches[0].shape[2] if caches is not None else 0
    kb = min(256, seq)
    assert seq % kb == 0 and past % kb == 0 and seq % tq == 0
    n_blocks = (seq + past) // kb
    assert n_blocks == 1 or n_blocks % 2 == 0
    n_all = N_SCORE_HEADS * tq
    full = lambda a: pl.BlockSpec(a.shape, lambda b, q: (0,) * a.ndim)
    in_specs = [pl.BlockSpec((tq, qa.shape[1]), lambda b, q: (b * nq + q, 0)),
                pl.BlockSpec((seq, kv.shape[1]), lambda b, q: (b, 0))]
    args = [qa, kv]
    if caches is not None:
        in_specs += [pl.BlockSpec((None, None, past, a.shape[3]), lambda b, q: (b, layer, 0, 0))
                     for a in caches]
        args += list(caches)
    if tables is not None:
        cosq, sinq, cosk, sink = tables
        in_specs += [pl.BlockSpec((tq, cosq.shape[1]), lambda b, q: (q, 0)),
                     pl.BlockSpec((tq, sinq.shape[1]), lambda b, q: (q, 0)),
                     full(cosk), full(sink)]
        args += [cosq, sinq, cosk, sink]
    in_specs += [full(lam_p), full(gain), full(gq_t), full(gk_t)]
    args += [lam_p, gain, gq_t, gk_t]
    out_specs = [pl.BlockSpec((tq, DIFF_W + GQA_W), lambda b, q: (b * nq + q, 0))]
    out_shape = [jax.ShapeDtypeStruct((t, DIFF_W + GQA_W), F32)]
    if emit_gk:
        out_specs.append(pl.BlockSpec((seq, GQA_KV_W), lambda b, q: (b, 0)))
        out_shape.append(jax.ShapeDtypeStruct((t, GQA_KV_W), F32))
    cfg = (lam_init, seq, past, kb, tables is not None, emit_gk)
    return pl.pallas_call(
        functools.partial(_attn_kernel, cfg),
        grid=(batch, nq),
        in_specs=in_specs,
        out_specs=out_specs,
        out_shape=out_shape,
        scratch_shapes=[
            pltpu.VMEM((seq + past, DIFF_W), BF16), pltpu.VMEM((seq + past, GQA_KV_W), BF16),
            pltpu.VMEM((n_blocks, DIFF_W, kb), BF16), pltpu.VMEM((n_blocks, GQA_KV_W, kb), BF16),
            pltpu.VMEM((DIFF_W, 2 * DIFF_HEADS * tq), BF16), pltpu.VMEM((GQA_KV_W, GQA_Q_HEADS * tq), BF16),
            pltpu.VMEM((n_all // LANES, kb, LANES), F32), pltpu.VMEM((n_all // LANES, kb, LANES), F32),
            pltpu.VMEM((n_all // LANES, kb, LANES), BF16),
            pltpu.VMEM((1, n_all), F32), pltpu.VMEM((1, n_all), F32), pltpu.VMEM((1, n_all), F32),
            pltpu.VMEM((N_SCORE_HEADS, GQA_HD, tq), F32),
        ],
        compiler_params=pltpu.CompilerParams(
            dimension_semantics=("arbitrary", "arbitrary"), vmem_limit_bytes=VMEM_LIMIT),
    )(*args)


def _softplus(x):
    return jnp.maximum(x, 0.0) + jnp.log1p(jnp.exp(-jnp.abs(x)))


def _ssd_kernel(has_init, seq, *refs):
    if has_init:
        (z_ref, xbc_ref, dt_ref, cw_ref, cb_ref, alog_ref, dtb_ref, dexp_ref, ng_ref,
         sf0_ref, sb0_ref, o_ref, sf_ref, sb_ref,
         xs_s, bm_s, cm_s, dt_s, halo_s, st_s) = refs
    else:
        (z_ref, xbc_ref, dt_ref, cw_ref, cb_ref, alog_ref, dtb_ref, dexp_ref, ng_ref,
         o_ref, sf_ref, sb_ref,
         xs_s, bm_s, cm_s, dt_s, halo_s, st_s) = refs
        sf0_ref = sb0_ref = None
    L = SSD_CHUNK
    nchunks = seq // L
    halo = 8

    ri = lax.broadcasted_iota(jnp.int32, (L, L), 0)
    ci = lax.broadcasted_iota(jnp.int32, (L, L), 1)
    lower = ri >= ci
    upper = ri <= ci
    lane_row = lax.broadcasted_iota(jnp.int32, (1, LANES), 1)
    a_row = jnp.where(lane_row < DT_W, -jnp.exp(alog_ref[...]), 0.0)
    ej = lax.broadcasted_iota(jnp.int32, (LANES, SSD_INNER), 0)
    eh = lax.shift_right_logical(lax.broadcasted_iota(jnp.int32, (LANES, SSD_INNER), 1), 6)
    group_mask = jnp.where(lax.shift_right_logical(ej, 6) == lax.shift_right_logical(eh, 2), 1.0, 0.0)

    def chunk_rows(c):
        return pl.ds(pl.multiple_of(c * L, L), L)

    def conv_body(c, carry):
        r0 = c * L
        rows = chunk_rows(c)
        prev = xbc_ref[pl.ds(pl.multiple_of(jnp.maximum(r0 - halo, 0), halo), halo), :]
        nxt = xbc_ref[pl.ds(pl.multiple_of(jnp.minimum(r0 + L, seq - halo), halo), halo), :]
        halo_s[0:halo, :] = jnp.where(c > 0, prev, 0.0)
        halo_s[halo:halo + L, :] = xbc_ref[rows, :]
        halo_s[halo + L:2 * halo + L, :] = jnp.where(c < nchunks - 1, nxt, 0.0)
        acc = cb_ref[...] + halo_s[pl.ds(halo - SSD_CONV // 2, L), :] * cw_ref[0:1, :]
        for j in range(1, SSD_CONV):
            acc = acc + halo_s[pl.ds(halo - SSD_CONV // 2 + j, L), :] * cw_ref[j:j + 1, :]
        act = _silu(acc)
        xs_s[rows, :] = act[:, 0:SSD_INNER]
        bm_s[rows, :] = act[:, SSD_INNER:SSD_INNER + SSD_BC_W]
        cm_s[rows, :] = act[:, SSD_INNER + SSD_BC_W:XBC_W]
        dt_s[rows, :] = _softplus(dt_ref[rows, :] + dtb_ref[...])
        return carry

    lax.fori_loop(0, nchunks, conv_body, 0)

    def scan(direction):
        tri = jnp.where(lower if direction == 0 else upper, 1.0, 0.0).astype(BF16)
        causal = lower if direction == 0 else upper
        expand = jnp.where(ej == eh + direction * SSD_HEADS, 1.0, 0.0).astype(BF16)
        edge_row = L - 1 if direction == 0 else 0

        def body(i, carry):
            c = i if direction == 0 else nchunks - 1 - i
            rows = chunk_rows(c)
            x_c = xs_s[rows, :]
            b_c = bm_s[rows, :]
            c_c = cm_s[rows, :]
            dt_c = dt_s[rows, :]
            dta = dt_c * a_row
            acs = _dot3_r(tri, dta)
            acs_t = acs.T
            dt_e = _dot3_l(dt_c, expand)
            acs_e = _dot3_l(acs, expand)
            edge = acs_e[edge_row:edge_row + 1, :]
            xd = x_c * dt_e
            xd_b = xd.astype(BF16)
            b_b = b_c.astype(BF16)
            st = st_s[...]
            y = _dot(c_c.astype(BF16), st.astype(BF16)) * jnp.exp(acs_e)
            pairs = []
            for g in range(SSD_GROUPS):
                cg = jnp.where(_lane_mask(c_c.shape, g * SSD_STATE, (g + 1) * SSD_STATE), c_c, 0.0)
                cb = _dot_nt(cg.astype(BF16), b_b)
                for hp in range(2):
                    pair = g * 2 + hp
                    res = []
                    for k in range(2):
                        j = direction * SSD_HEADS + pair * 2 + k
                        diff = acs[:, j:j + 1] - acs_t[j:j + 1, :]
                        dec = jnp.where(causal, jnp.exp(jnp.minimum(diff, 0.0)), 0.0)
                        sc = (cb * dec).astype(BF16)
                        res.append(_dot(sc, xd_b[:, pair * LANES:(pair + 1) * LANES]))
                    pairs.append(jnp.where(_lane_mask(res[0].shape, 0, SSD_HD), res[0], res[1]))
            y = y + jnp.concatenate(pairs, axis=-1)
            xdw = (xd * jnp.exp(edge - acs_e)).astype(BF16)
            s_new = _dot(b_c.T.astype(BF16), xdw) * group_mask
            st_s[...] = st * jnp.exp(edge) + s_new
            if direction == 0:
                o_ref[rows, :] = y + x_c * dexp_ref[...]
            else:
                yt = (o_ref[rows, :] + y) * _silu(z_ref[rows, :])
                ms = jnp.mean(yt * yt, axis=-1, keepdims=True)
                o_ref[rows, :] = yt * lax.rsqrt(ms + EPS) * ng_ref[...]
            return carry

        lax.fori_loop(0, nchunks, body, 0)

    def load_state(ref):
        r = lax.broadcasted_iota(jnp.int32, (SSD_STATE, LANES), 0)
        c = lax.broadcasted_iota(jnp.int32, (SSD_STATE, LANES), 1)
        dup = jnp.where((c & (SSD_STATE - 1)) == r, 1.0, 0.0).astype(BF16)
        return _dot3_l(ref[...], dup).T * group_mask

    def store_state(ref):
        st_t = st_s[...].T
        ref[...] = (st_t + _swap_halves(st_t))[:, 0:SSD_STATE]

    st_s[...] = load_state(sf0_ref) if has_init else jnp.zeros(st_s.shape, F32)
    scan(0)
    store_state(sf_ref)
    st_s[...] = load_state(sb0_ref) if has_init else jnp.zeros(st_s.shape, F32)
    scan(1)
    store_state(sb_ref)


def _ssd(z, xbc, dt, params, init, layer, batch, seq):
    cw, cb, alog, dtb, dexp, ng = params
    t = z.shape[0]
    has_init = init is not None
    full = lambda a: pl.BlockSpec(a.shape, lambda b: (0,) * a.ndim)
    st_spec = pl.BlockSpec((None, SSD_INNER, SSD_STATE), lambda b: (b, 0, 0))
    in_specs = [
        pl.BlockSpec((seq, SSD_INNER), lambda b: (b, 0)),
        pl.BlockSpec((seq, XBC_W), lambda b: (b, 0)),
        pl.BlockSpec((seq, LANES), lambda b: (b, 0)),
        full(cw), full(cb), full(alog), full(dtb), full(dexp), full(ng),
    ]
    args = [z, xbc, dt, cw, cb, alog, dtb, dexp, ng]
    if has_init:
        init_spec = pl.BlockSpec((None, None, SSD_INNER, SSD_STATE), lambda b: (b, layer, 0, 0))
        in_specs += [init_spec, init_spec]
        args += list(init)
    st_shape = jax.ShapeDtypeStruct((batch, SSD_INNER, SSD_STATE), F32)
    return pl.pallas_call(
        functools.partial(_ssd_kernel, has_init, seq),
        grid=(batch,),
        in_specs=in_specs,
        out_specs=[pl.BlockSpec((seq, SSD_INNER), lambda b: (b, 0)), st_spec, st_spec],
        out_shape=[jax.ShapeDtypeStruct((t, SSD_INNER), F32), st_shape, st_shape],
        scratch_shapes=[
            pltpu.VMEM((seq, SSD_INNER), F32), pltpu.VMEM((seq, SSD_BC_W), F32),
            pltpu.VMEM((seq, SSD_BC_W), F32), pltpu.VMEM((seq, LANES), F32),
            pltpu.VMEM((SSD_CHUNK + 16, XBC_W), F32), pltpu.VMEM((LANES, SSD_INNER), F32),
        ],
        compiler_params=pltpu.CompilerParams(
            dimension_semantics=("arbitrary",), vmem_limit_bytes=VMEM_LIMIT),
    )(*args)


def _post_kernel(alpha, d_ff, ff_chunk, x_ref, oa_ref, os_ref, g1_ref, sh2_ref, sc2_ref, g2_ref,
                 wo_ref, wfi_ref, wfo_ref, lng_ref, lnb_ref, y_ref):
    x = x_ref[...]
    wa = oa_ref.shape[1]
    o = _dot(oa_ref[...].astype(BF16), wo_ref[0:wa, :]) + _dot(os_ref[...].astype(BF16), wo_ref[wa:, :])
    x1 = _layer_norm(alpha * x + g1_ref[...] * o, lng_ref[0:1, :], lnb_ref[0:1, :])
    h2 = (x1 * (1.0 + sc2_ref[...]) + sh2_ref[...]).astype(BF16)
    f = jnp.zeros(x.shape, F32)
    for c in range(d_ff // ff_chunk):
        lo = c * ff_chunk
        g = _dot(h2, wfi_ref[:, lo:lo + ff_chunk])
        u = _dot(h2, wfi_ref[:, d_ff + lo:d_ff + lo + ff_chunk])
        f = f + _dot((_silu(g) * u).astype(BF16), wfo_ref[lo:lo + ff_chunk, :])
    y_ref[...] = _layer_norm(alpha * x1 + g2_ref[...] * f, lng_ref[1:2, :], lnb_ref[1:2, :])


def _post(x, oa, os_, mods, layer, row_fn, wo_bf, wfi_bf, wfo_bf, lng, lnb, alpha, tm):
    t, d = x.shape
    d_ff = wfo_bf.shape[0]
    ff_chunk = d_ff // 2
    resident = lambda a: pl.BlockSpec(a.shape, lambda i: (0,) * a.ndim, pipeline_mode=pl.Buffered(1))
    return pl.pallas_call(
        functools.partial(_post_kernel, alpha, d_ff, ff_chunk),
        grid=(t // tm,),
        in_specs=[
            pl.BlockSpec((tm, d), lambda i: (i, 0)),
            pl.BlockSpec((tm, oa.shape[1]), lambda i: (i, 0)),
            pl.BlockSpec((tm, os_.shape[1]), lambda i: (i, 0)),
            _mod_spec(layer, 2, row_fn, d),
            _mod_spec(layer, 3, row_fn, d),
            _mod_spec(layer, 4, row_fn, d),
            _mod_spec(layer, 5, row_fn, d),
            resident(wo_bf), resident(wfi_bf), resident(wfo_bf),
            pl.BlockSpec(lng.shape, lambda i: (0, 0)),
            pl.BlockSpec(lnb.shape, lambda i: (0, 0)),
        ],
        out_specs=pl.BlockSpec((tm, d), lambda i: (i, 0)),
        out_shape=jax.ShapeDtypeStruct((t, d), F32),
        compiler_params=pltpu.CompilerParams(
            dimension_semantics=("arbitrary",), vmem_limit_bytes=VMEM_LIMIT),
    )(x, oa, os_, mods, mods, mods, mods, wo_bf, wfi_bf, wfo_bf, lng, lnb)


def _rope_tables(rows, dim, copies):
    row = jnp.repeat(jnp.arange(rows), GRID_W).astype(F32)
    col = jnp.tile(jnp.arange(GRID_W), rows).astype(F32)
    n_freq = dim // 4
    inv = ROPE_THETA ** (-jnp.arange(n_freq, dtype=F32) / n_freq)
    ang = jnp.concatenate([row[:, None] * inv, col[:, None] * inv], -1)
    cos, sin = jnp.cos(ang), jnp.sin(ang)
    cos_full = jnp.repeat(cos, 2, axis=-1)
    sin_signed = jnp.stack([-sin, sin], axis=-1).reshape(sin.shape[0], dim)
    return jnp.tile(cos_full, (1, copies)), jnp.tile(sin_signed, (1, copies))


def kernel(x_prompt, x_sample, cache_diff_k, cache_diff_v, cache_gqa_k, cache_gqa_v, state_ssd_fwd, state_ssd_bwd, c, c_ctx, w_ada, b_ada, w_in, w_out, diff_lambda, diff_subln_g, qk_norm_g, ssd_conv_w, ssd_conv_b, ssd_A_log, ssd_dt_bias, ssd_D, ssd_norm_g, ln_g, ln_b, w_ffn_in, w_ffn_out):
    batch, seq, d = x_prompt.shape
    dec_batch, dec_seq, _ = x_sample.shape
    depth = w_in.shape[0]
    past = cache_diff_k.shape[2]
    alpha = (2 * depth) ** 0.25
    rows = dec_seq // GRID_W

    n_vec = 1 + dec_batch
    n_pad = -(-n_vec // 8) * 8
    cvec = jnp.concatenate([c_ctx[None, :], c, jnp.zeros((n_pad - n_vec, d), F32)], axis=0)
    mods = _modulation(cvec, w_ada, b_ada).reshape(depth, n_pad, 1, 6 * d)

    cos_d, sin_d = _rope_tables(rows, DIFF_QK, DIFF_W // DIFF_QK)
    cos_g, sin_g = _rope_tables(rows, GQA_HD, GQA_Q_HEADS)
    cosq = jnp.concatenate([cos_d, cos_g], axis=-1)
    sinq = jnp.concatenate([sin_d, sin_g], axis=-1)
    cosk = jnp.concatenate([cos_d, cos_g[:, :GQA_KV_W]], axis=-1)
    sink = jnp.concatenate([sin_d, sin_g[:, :GQA_KV_W]], axis=-1)
    tables = (cosq, sinq, cosk, sink)

    caches = (cache_diff_k.reshape(dec_batch, depth, past, DIFF_W),
              cache_diff_v.reshape(dec_batch, depth, past, DIFF_W),
              cache_gqa_k.reshape(dec_batch, depth, past, GQA_KV_W),
              cache_gqa_v.reshape(dec_batch, depth, past, GQA_KV_W))

    xp = x_prompt.reshape(batch * seq, d)
    xs = x_sample.reshape(dec_batch * dec_seq, d)
    tm_ctx = min(512, batch * seq)
    tm_lat = min(512, dec_seq)
    tq = min(256, dec_seq)
    ctx_row = lambda i: 0
    lat_row = lambda i: 1 + (i * tm_lat) // dec_seq

    new_dk, new_dv, new_gk, new_gv, new_sf, new_sb = [], [], [], [], [], []
    for l in range(depth):
        lam_init = 0.8 - 0.6 * math.exp(-0.3 * l)
        w_in_bf = jnp.pad(w_in[l], ((0, 0), (0, IN_W_PAD - w_in.shape[2]))).astype(BF16)
        wo_bf = w_out[l].astype(BF16)
        wfi_bf = w_ffn_in[l].astype(BF16)
        wfo_bf = w_ffn_out[l].astype(BF16)
        lam_p = diff_lambda[l]
        attn_gain = jnp.concatenate([jnp.tile(diff_subln_g[l], DIFF_HEADS), jnp.ones((GQA_W,), F32)])[None, :]
        gq_t = jnp.tile(qk_norm_g[l, 0], GQA_Q_HEADS)[None, :]
        gk_t = jnp.tile(qk_norm_g[l, 1], GQA_KV_HEADS)[None, :]
        pad_row = lambda v: jnp.pad(v.reshape(1, -1), ((0, 0), (0, LANES - v.size)))
        ssd_params = (ssd_conv_w[l], ssd_conv_b[l][None, :], pad_row(ssd_A_log[l]),
                      pad_row(ssd_dt_bias[l]), jnp.repeat(ssd_D[l], SSD_HD)[None, :],
                      ssd_norm_g[l][None, :])

        qa, kv, z, xbc, dt = _inproj(xp, mods, l, ctx_row, w_in_bf, tm_ctx)
        oa, gkn = _attention(qa, kv, None, l, None, lam_p, attn_gain, gq_t, gk_t, lam_init,
                             batch, seq, seq, True)
        os_, sf, sb = _ssd(z, xbc, dt, ssd_params, None, l, batch, seq)
        xp = _post(xp, oa, os_, mods, l, ctx_row, wo_bf, wfi_bf, wfo_bf, ln_g[l], ln_b[l], alpha, tm_ctx)
        new_dk.append(kv[:, 0:DIFF_W].reshape(batch, seq, DIFF_HEADS, 2 * DIFF_QK))
        new_dv.append(kv[:, DIFF_W:2 * DIFF_W].reshape(batch, seq, DIFF_HEADS, DIFF_V))
        new_gk.append(gkn.reshape(batch, seq, GQA_KV_HEADS, GQA_HD))
        new_gv.append(kv[:, 2 * DIFF_W + GQA_KV_W:].reshape(batch, seq, GQA_KV_HEADS, GQA_HD))
        new_sf.append(sf.reshape(batch, SSD_HEADS, SSD_HD, SSD_STATE))
        new_sb.append(sb.reshape(batch, SSD_HEADS, SSD_HD, SSD_STATE))

        qa, kv, z, xbc, dt = _inproj(xs, mods, l, lat_row, w_in_bf, tm_lat)
        (oa,) = _attention(qa, kv, caches, l, tables, lam_p, attn_gain, gq_t, gk_t, lam_init,
                           dec_batch, dec_seq, tq, False)
        init = (state_ssd_fwd.reshape(dec_batch, depth, SSD_INNER, SSD_STATE),
                state_ssd_bwd.reshape(dec_batch, depth, SSD_INNER, SSD_STATE))
        os_, _, _ = _ssd(z, xbc, dt, ssd_params, init, l, dec_batch, dec_seq)
        xs = _post(xs, oa, os_, mods, l, lat_row, wo_bf, wfi_bf, wfo_bf, ln_g[l], ln_b[l], alpha, tm_lat)

    return (xp.reshape(batch, seq, d), xs.reshape(dec_batch, dec_seq, d),
            jnp.stack(new_dk, 1), jnp.stack(new_dv, 1), jnp.stack(new_gk, 1),
            jnp.stack(new_gv, 1), jnp.stack(new_sf, 1), jnp.stack(new_sb, 1))
```

```python
import functools
import math

import jax
import jax.numpy as jnp
from jax import lax
from jax.experimental import pallas as pl
from jax.experimental.pallas import tpu as pltpu

F32 = jnp.float32
BF16 = jnp.bfloat16

GRID_W = 64
DIFF_HEADS = 4
DIFF_QK = 32
DIFF_V = 64
DIFF_W = DIFF_HEADS * DIFF_V
GQA_HD = 64
GQA_Q_HEADS = 4
GQA_KV_HEADS = 2
GQA_W = GQA_Q_HEADS * GQA_HD
GQA_KV_W = GQA_KV_HEADS * GQA_HD
SSD_HD = 64
SSD_HEADS = 8
SSD_INNER = SSD_HEADS * SSD_HD
SSD_GROUPS = 2
SSD_STATE = 64
SSD_BC_W = SSD_GROUPS * SSD_STATE
SSD_CONV = 5
SSD_CHUNK = 128
XBC_W = SSD_INNER + 2 * SSD_BC_W
DT_W = 2 * SSD_HEADS
ROPE_THETA = 10000.0
EPS = 1e-5
LANES = 128
VMEM_LIMIT = 56 * 1024 * 1024

_C_DQ, _C_DK, _C_DV, _C_GQ, _C_GK, _C_GV, _C_Z, _C_XBC, _C_DT, _C_END = (
    0, 256, 512, 768, 1024, 1152, 1280, 1792, 2560, 2576)
IN_W_PAD = _C_DT + LANES


def _dot(a, b):
    return jnp.dot(a, b, preferred_element_type=F32)


def _dot_nt(a, b):
    return lax.dot_general(a, b, (((1,), (1,)), ((), ())), preferred_element_type=F32)


def _split3(a):
    a1 = a.astype(BF16)
    r1 = a - a1.astype(F32)
    a2 = r1.astype(BF16)
    a3 = (r1 - a2.astype(F32)).astype(BF16)
    return a1, a2, a3


def _dot3_l(a, b_exact):
    a1, a2, a3 = _split3(a)
    return _dot(a1, b_exact) + (_dot(a2, b_exact) + _dot(a3, b_exact))


def _sigmoid(x):
    return 1.0 / (1.0 + jnp.exp(-x))


def _silu(x):
    return x * _sigmoid(x)


def _layer_norm(x, g, b):
    mu = jnp.mean(x, axis=-1, keepdims=True)
    xc = x - mu
    var = jnp.mean(xc * xc, axis=-1, keepdims=True)
    return xc * lax.rsqrt(var + EPS) * g + b


def _group_avg_matrix(width, group):
    sh = int(math.log2(group))
    r = lax.shift_right_logical(lax.broadcasted_iota(jnp.int32, (width, width), 0), sh)
    c = lax.shift_right_logical(lax.broadcasted_iota(jnp.int32, (width, width), 1), sh)
    return jnp.where(r == c, 1.0 / group, 0.0).astype(BF16)


def _group_mean_sq(x, gmat):
    xx = x * x
    hi = xx.astype(BF16)
    lo = (xx - hi.astype(F32)).astype(BF16)
    return _dot(hi, gmat) + _dot(lo, gmat)


def _rope(x, cos, sin_signed):
    w = x.shape[-1]
    lane = lax.broadcasted_iota(jnp.int32, x.shape, 1)
    nxt = pltpu.roll(x, w - 1, 1)
    prv = pltpu.roll(x, 1, 1)
    partner = jnp.where((lane & 1) == 0, nxt, prv)
    return x * cos + partner * sin_signed


def _lane_mask(shape, lo, hi):
    lane = lax.broadcasted_iota(jnp.int32, shape, 1)
    return (lane >= lo) & (lane < hi)


def _mod_kernel(c_ref, w_ref, b_ref, o_ref):
    a = _silu(c_ref[...])
    a_hi = a.astype(BF16)
    a_lo = (a - a_hi.astype(F32)).astype(BF16)
    w = w_ref[...]
    w_hi = w.astype(BF16)
    w_lo = (w - w_hi.astype(F32)).astype(BF16)
    o_ref[...] = _dot(a_hi, w_hi) + (_dot(a_lo, w_hi) + _dot(a_hi, w_lo)) + b_ref[...]


def _modulation(cvec, w_ada, b_ada):
    depth, d, n = w_ada.shape
    tn = 1536
    rows = cvec.shape[0]
    return pl.pallas_call(
        _mod_kernel,
        grid=(depth, n // tn),
        in_specs=[
            pl.BlockSpec((rows, d), lambda l, j: (0, 0)),
            pl.BlockSpec((None, d, tn), lambda l, j: (l, 0, j)),
            pl.BlockSpec((None, 1, tn), lambda l, j: (l, 0, j)),
        ],
        out_specs=pl.BlockSpec((None, rows, tn), lambda l, j: (l, 0, j)),
        out_shape=jax.ShapeDtypeStruct((depth, rows, n), F32),
        compiler_params=pltpu.CompilerParams(
            dimension_semantics=("arbitrary", "arbitrary"), vmem_limit_bytes=VMEM_LIMIT),
    )(cvec, w_ada, b_ada.reshape(depth, 1, n))


def _inproj_kernel(x_ref, sh_ref, sc_ref, w_ref, qa_ref, dk_ref, dv_ref, gk_ref, gv_ref,
                   z_ref, xbc_ref, dt_ref):
    h = (x_ref[...] * (1.0 + sc_ref[...]) + sh_ref[...]).astype(BF16)

    def mm(lo, hi):
        return _dot(h, w_ref[:, lo:hi])

    qa_ref[:, 0:DIFF_W] = mm(_C_DQ, _C_DK)
    qa_ref[:, DIFF_W:DIFF_W + GQA_W] = mm(_C_GQ, _C_GK)
    dk_ref[...] = mm(_C_DK, _C_DV)
    dv_ref[...] = mm(_C_DV, _C_GQ)
    gk_ref[...] = mm(_C_GK, _C_GV)
    gv_ref[...] = mm(_C_GV, _C_Z)
    z_ref[...] = mm(_C_Z, _C_XBC)
    xbc_ref[...] = mm(_C_XBC, _C_DT)
    dt_ref[...] = mm(_C_DT, IN_W_PAD)


def _mod_spec(layer, which, row_fn, d):
    return pl.BlockSpec((None, None, 1, d), lambda i: (layer, row_fn(i), 0, which))


def _inproj(x, mods, layer, row_fn, w_in_bf, tm):
    t, d = x.shape
    widths = (DIFF_W + GQA_W, DIFF_W, DIFF_W, GQA_KV_W, GQA_KV_W, SSD_INNER, XBC_W, LANES)
    return pl.pallas_call(
        _inproj_kernel,
        grid=(t // tm,),
        in_specs=[
            pl.BlockSpec((tm, d), lambda i: (i, 0)),
            _mod_spec(layer, 0, row_fn, d),
            _mod_spec(layer, 1, row_fn, d),
            pl.BlockSpec((d, IN_W_PAD), lambda i: (0, 0)),
        ],
        out_specs=[pl.BlockSpec((tm, w), lambda i: (i, 0)) for w in widths],
        out_shape=[jax.ShapeDtypeStruct((t, w), F32) for w in widths],
        compiler_params=pltpu.CompilerParams(
            dimension_semantics=("arbitrary",), vmem_limit_bytes=VMEM_LIMIT),
    )(x, mods, mods, w_in_bf)


LOG2E = 1.4426950408889634
NEG_BIG = -1e30
N_SCORE_HEADS = 2 * DIFF_HEADS + GQA_Q_HEADS


def _diff_lambda(lam_ref, lam_init):
    lp = lam_ref[...]
    s1 = jnp.sum(lp[0:1, :] * lp[1:2, :], axis=-1, keepdims=True)
    s2 = jnp.sum(lp[2:3, :] * lp[3:4, :], axis=-1, keepdims=True)
    return jnp.exp(s1) - jnp.exp(s2) + lam_init


def _swap_halves(x):
    return pltpu.roll(x, GQA_HD, 1)


def _attn_kernel(cfg, *refs):
    lam_init, seq, past, kb, rope, emit_gk = cfg
    it = iter(refs)
    qa_ref, dk_ref, dv_ref, gk_in_ref, gv_ref = next(it), next(it), next(it), next(it), next(it)
    if past:
        cdk_ref, cdv_ref, cgk_ref, cgv_ref = next(it), next(it), next(it), next(it)
    if rope:
        cosq_ref, sinq_ref, cosk_ref, sink_ref = next(it), next(it), next(it), next(it)
    lam_ref, gain_ref, gq_ref, gk_ref = next(it), next(it), next(it), next(it)
    o_ref = next(it)
    gkn_ref = next(it) if emit_gk else None
    kd_s, kg_s, vdt_s, vgt_s, wq_s, wg_s, s0_s, s1_s, p_s, m_s, l_s, alpha_s, acc_s = it
    tq = qa_ref.shape[0]
    n_blocks = (seq + past) // kb

    @pl.when(pl.program_id(1) == 0)
    def _prepare_keys():
        gmat = _group_avg_matrix(GQA_KV_W, GQA_HD)
        for i in range(seq // kb):
            rows = slice(i * kb, (i + 1) * kb)
            dk = dk_ref[rows, :]
            gk = gk_in_ref[rows, :]
            gk = gk * lax.rsqrt(_group_mean_sq(gk, gmat) + EPS) * gk_ref[...]
            if emit_gk:
                gkn_ref[rows, :] = gk
            if rope:
                ck = cosk_ref[rows, :]
                sk = sink_ref[rows, :]
                dk = _rope(dk, ck[:, 0:DIFF_W], sk[:, 0:DIFF_W])
                gk = _rope(gk, ck[:, DIFF_W:DIFF_W + GQA_KV_W], sk[:, DIFF_W:DIFF_W + GQA_KV_W])
            kd_s[rows, :] = dk.astype(BF16)
            kg_s[rows, :] = gk.astype(BF16)
            vdt_s[i] = dv_ref[rows, :].T.astype(BF16)
            vgt_s[i] = gv_ref[rows, :].T.astype(BF16)
        for j in range(past // kb):
            src = slice(j * kb, (j + 1) * kb)
            dst = slice(seq + j * kb, seq + (j + 1) * kb)
            kd_s[dst, :] = cdk_ref[src, :].astype(BF16)
            kg_s[dst, :] = cgk_ref[src, :].astype(BF16)
            vdt_s[seq // kb + j] = cdv_ref[src, :].T.astype(BF16)
            vgt_s[seq // kb + j] = cgv_ref[src, :].T.astype(BF16)

    qa = qa_ref[...]
    qd = qa[:, 0:DIFF_W]
    gq = qa[:, DIFF_W:DIFF_W + GQA_W]
    gq = gq * lax.rsqrt(_group_mean_sq(gq, _group_avg_matrix(GQA_W, GQA_HD)) + EPS) * gq_ref[...]
    if rope:
        cq = cosq_ref[...]
        sq = sinq_ref[...]
        qd = _rope(qd, cq[:, 0:DIFF_W], sq[:, 0:DIFF_W])
        gq = _rope(gq, cq[:, DIFF_W:DIFF_W + GQA_W], sq[:, DIFF_W:DIFF_W + GQA_W])
    qd_t = (qd * (DIFF_QK ** -0.5 * LOG2E)).T
    gq_t = (gq * (GQA_HD ** -0.5 * LOG2E)).T
    row = lax.broadcasted_iota(jnp.int32, (DIFF_W, tq), 0)
    for hm in range(2 * DIFF_HEADS):
        lo = hm * DIFF_QK
        wq_s[:, hm * tq:(hm + 1) * tq] = jnp.where((row >= lo) & (row < lo + DIFF_QK), qd_t, 0.0).astype(BF16)
    zeros = jnp.zeros((GQA_HD, tq), F32)
    for h in range(GQA_Q_HEADS):
        piece = gq_t[h * GQA_HD:(h + 1) * GQA_HD, :]
        pair = [piece, zeros] if h // (GQA_Q_HEADS // GQA_KV_HEADS) == 0 else [zeros, piece]
        wg_s[:, h * tq:(h + 1) * tq] = jnp.concatenate(pair, axis=0).astype(BF16)
    m_s[...] = jnp.full(m_s.shape, NEG_BIG, F32)
    l_s[...] = jnp.zeros(l_s.shape, F32)
    acc_s[...] = jnp.zeros(acc_s.shape, F32)
    n_diff = 2 * DIFF_HEADS * tq
    n_all = N_SCORE_HEADS * tq

    def scores(j, s_buf):
        rows = pl.ds(j * kb if isinstance(j, int) else pl.multiple_of(j * kb, kb), kb)
        sd = _dot(kd_s[rows, :], wq_s[...])
        for c in range(n_diff // LANES):
            s_buf[c] = sd[:, c * LANES:(c + 1) * LANES]
        sg = _dot(kg_s[rows, :], wg_s[...])
        for c in range((n_all - n_diff) // LANES):
            s_buf[n_diff // LANES + c] = sg[:, c * LANES:(c + 1) * LANES]

    def accumulate(j, s_buf):
        for c in range(n_all // LANES):
            cols = slice(c * LANES, (c + 1) * LANES)
            s = s_buf[c]
            m_old = m_s[:, cols]
            m_new = jnp.maximum(m_old, jnp.max(s, axis=0, keepdims=True))
            alpha = jnp.exp2(m_old - m_new)
            p = jnp.exp2(s - m_new)
            l_s[:, cols] = alpha * l_s[:, cols] + jnp.sum(p, axis=0, keepdims=True)
            m_s[:, cols] = m_new
            alpha_s[:, cols] = alpha
            p_s[c] = p.astype(BF16)
        v_d = vdt_s[j]
        v_g = vgt_s[j]
        slabs = tq // LANES
        for idx in range(N_SCORE_HEADS):
            cols = slice(idx * tq, (idx + 1) * tq)
            if idx < 2 * DIFF_HEADS:
                vh = idx // 2
                v_t = v_d[vh * DIFF_V:(vh + 1) * DIFF_V, :]
            else:
                vh = (idx - 2 * DIFF_HEADS) // (GQA_Q_HEADS // GQA_KV_HEADS)
                v_t = v_g[vh * GQA_HD:(vh + 1) * GQA_HD, :]
            p = jnp.concatenate([p_s[idx * slabs + k] for k in range(slabs)], axis=1)
            acc_s[idx] = alpha_s[:, cols] * acc_s[idx] + _dot(v_t, p)

    scores(0, s0_s)
    if n_blocks > 1:
        def block_pair(i, carry):
            scores(2 * i + 1, s1_s)
            accumulate(2 * i, s0_s)
            scores(2 * i + 2, s0_s)
            accumulate(2 * i + 1, s1_s)
            return carry

        lax.fori_loop(0, n_blocks // 2 - 1, block_pair, 0)
        scores(n_blocks - 1, s1_s)
        accumulate(n_blocks - 2, s0_s)
        accumulate(n_blocks - 1, s1_s)
    else:
        accumulate(0, s0_s)

    lam = _diff_lambda(lam_ref, lam_init)
    outs = []
    for h in range(DIFF_HEADS):
        c0 = slice(2 * h * tq, (2 * h + 1) * tq)
        c1 = slice((2 * h + 1) * tq, (2 * h + 2) * tq)
        o = acc_s[2 * h] * (1.0 / l_s[:, c0]) - acc_s[2 * h + 1] * (lam / l_s[:, c1])
        ms = jnp.mean(o * o, axis=0, keepdims=True)
        outs.append(o * (lax.rsqrt(ms + EPS) * (1.0 - lam_init)))
    for h in range(GQA_Q_HEADS):
        idx = 2 * DIFF_HEADS + h
        outs.append(acc_s[idx] * (1.0 / l_s[:, idx * tq:(idx + 1) * tq]))
    o_ref[...] = jnp.concatenate(outs, axis=0).T * gain_ref[...]


def _attention(qa, kv, caches, layer, tables, lam_p, gain, gq_t, gk_t, lam_init, batch, seq, tq,
               emit_gk):
    t = qa.shape[0]
    nq = seq // tq
    past = caches[0].shape[2] if caches is not None else 0
    kb = min(256, seq)
    assert seq % kb == 0 and past % kb == 0 and seq % tq == 0
    n_blocks = (seq + past) // kb
    assert n_blocks == 1 or n_blocks % 2 == 0
    n_all = N_SCORE_HEADS * tq
    full = lambda a: pl.BlockSpec(a.shape, lambda b, q: (0,) * a.ndim)
    in_specs = [pl.BlockSpec((tq, qa.shape[1]), lambda b, q: (b * nq + q, 0))]
    in_specs += [pl.BlockSpec((seq, a.shape[1]), lambda b, q: (b, 0)) for a in kv]
    args = [qa, *kv]
    if caches is not None:
        in_specs += [pl.BlockSpec((None, None, past, a.shape[3]), lambda b, q: (b, layer, 0, 0))
                     for a in caches]
        args += list(caches)
    if tables is not None:
        cosq, sinq, cosk, sink = tables
        in_specs += [pl.BlockSpec((tq, cosq.shape[1]), lambda b, q: (q, 0)),
                     pl.BlockSpec((tq, sinq.shape[1]), lambda b, q: (q, 0)),
                     full(cosk), full(sink)]
        args += [cosq, sinq, cosk, sink]
    in_specs += [full(lam_p), full(gain), full(gq_t), full(gk_t)]
    args += [lam_p, gain, gq_t, gk_t]
    out_specs = [pl.BlockSpec((tq, DIFF_W + GQA_W), lambda b, q: (b * nq + q, 0))]
    out_shape = [jax.ShapeDtypeStruct((t, DIFF_W + GQA_W), F32)]
    if emit_gk:
        out_specs.append(pl.BlockSpec((seq, GQA_KV_W), lambda b, q: (b, 0)))
        out_shape.append(jax.ShapeDtypeStruct((t, GQA_KV_W), F32))
    cfg = (lam_init, seq, past, kb, tables is not None, emit_gk)
    return pl.pallas_call(
        functools.partial(_attn_kernel, cfg),
        grid=(batch, nq),
        in_specs=in_specs,
        out_specs=out_specs,
        out_shape=out_shape,
        scratch_shapes=[
            pltpu.VMEM((seq + past, DIFF_W), BF16), pltpu.VMEM((seq + past, GQA_KV_W), BF16),
            pltpu.VMEM((n_blocks, DIFF_W, kb), BF16), pltpu.VMEM((n_blocks, GQA_KV_W, kb), BF16),
            pltpu.VMEM((DIFF_W, 2 * DIFF_HEADS * tq), BF16), pltpu.VMEM((GQA_KV_W, GQA_Q_HEADS * tq), BF16),
            pltpu.VMEM((n_all // LANES, kb, LANES), F32), pltpu.VMEM((n_all // LANES, kb, LANES), F32),
            pltpu.VMEM((n_all // LANES, kb, LANES), BF16),
            pltpu.VMEM((1, n_all), F32), pltpu.VMEM((1, n_all), F32), pltpu.VMEM((1, n_all), F32),
            pltpu.VMEM((N_SCORE_HEADS, GQA_HD, tq), F32),
        ],
        compiler_params=pltpu.CompilerParams(
            dimension_semantics=("arbitrary", "arbitrary"), vmem_limit_bytes=VMEM_LIMIT),
    )(*args)


def _softplus(x):
    return jnp.maximum(x, 0.0) + jnp.log1p(jnp.exp(-jnp.abs(x)))


def _dot2_l(a, b_exact):
    a1 = a.astype(BF16)
    a2 = (a - a1.astype(F32)).astype(BF16)
    return _dot(a1, b_exact) + _dot(a2, b_exact)


def _ssd_kernel(has_init, seq, *refs):
    it = iter(refs)
    z_ref, xbc_ref, dt_ref, cw_ref, cb_ref, alog_ref, dtb_ref, dexp_ref, ng_ref = (
        next(it) for _ in range(9))
    sf0_ref, sb0_ref = (next(it), next(it)) if has_init else (None, None)
    o_ref, sf_ref, sb_ref = next(it), next(it), next(it)
    stf_s, stb_s, inc_s, dec_s, eab_s, cbf_s, exp_s, tri_s, gm_s = it
    L = SSD_CHUNK
    W2 = 2 * SSD_INNER
    nchunks = seq // L
    halo = 8

    ri = lax.broadcasted_iota(jnp.int32, (L, L), 0)
    ci = lax.broadcasted_iota(jnp.int32, (L, L), 1)
    lower = ri >= ci
    upper = ri <= ci
    tri_s[0] = jnp.where(lower, 1.0, 0.0).astype(BF16)
    tri_s[1] = jnp.where(upper, 1.0, 0.0).astype(BF16)
    lane_row = lax.broadcasted_iota(jnp.int32, (1, LANES), 1)
    a_row = jnp.where(lane_row < DT_W, -jnp.exp(alog_ref[...]), 0.0)
    ej = lax.broadcasted_iota(jnp.int32, (LANES, W2), 0)
    eh = lax.shift_right_logical(lax.broadcasted_iota(jnp.int32, (LANES, W2), 1), 6)
    exp_s[...] = jnp.where(ej == eh, 1.0, 0.0).astype(BF16)
    gm_s[...] = jnp.where(lax.shift_right_logical(ej, 6) == (lax.shift_right_logical(eh, 2) & 1), 1.0, 0.0)

    def chunk_rows(c):
        return pl.ds(pl.multiple_of(c * L, L), L)

    def load_state(ref):
        r = lax.broadcasted_iota(jnp.int32, (SSD_STATE, LANES), 0)
        c = lax.broadcasted_iota(jnp.int32, (SSD_STATE, LANES), 1)
        dup = jnp.where((c & (SSD_STATE - 1)) == r, 1.0, 0.0).astype(BF16)
        return _dot3_l(ref[...], dup).T * gm_s[:, 0:SSD_INNER]

    def store_state(st_ref, ref):
        st_t = st_ref[...].T
        ref[...] = (st_t + _swap_halves(st_t))[:, 0:SSD_STATE]

    stf_s[...] = load_state(sf0_ref) if has_init else jnp.zeros(stf_s.shape, F32)
    stb_s[...] = load_state(sb0_ref) if has_init else jnp.zeros(stb_s.shape, F32)

    def forward_pass(c, carry):
        r0 = c * L
        rows = chunk_rows(c)
        prev = xbc_ref[pl.ds(pl.multiple_of(jnp.maximum(r0 - halo, 0), halo), halo), :]
        nxt = xbc_ref[pl.ds(pl.multiple_of(jnp.minimum(r0 + L, seq - halo), halo), halo), :]
        cur = xbc_ref[rows, :]
        win = jnp.concatenate([jnp.where(c > 0, prev, 0.0), cur,
                               jnp.where(c < nchunks - 1, nxt, 0.0)], axis=0)
        acc = cb_ref[...] + cur * cw_ref[SSD_CONV // 2:SSD_CONV // 2 + 1, :]
        for j in range(SSD_CONV):
            if j != SSD_CONV // 2:
                shifted = pltpu.roll(win, (SSD_CONV // 2 - j) % (L + 2 * halo), 0)[halo:halo + L, :]
                acc = acc + shifted * cw_ref[j:j + 1, :]
        act = _silu(acc)
        x_c = act[:, 0:SSD_INNER]
        b_c = act[:, SSD_INNER:SSD_INNER + SSD_BC_W]
        c_c = act[:, SSD_INNER + SSD_BC_W:XBC_W]
        dt_c = _softplus(dt_ref[rows, :] + dtb_ref[...])

        d1, d2, d3 = _split3(dt_c * a_row)
        acs_f = _dot(tri_s[0], d1) + (_dot(tri_s[0], d2) + _dot(tri_s[0], d3))
        acs_b = _dot(tri_s[1], d1) + (_dot(tri_s[1], d2) + _dot(tri_s[1], d3))
        acs = jnp.where(lax.broadcasted_iota(jnp.int32, (L, LANES), 1) < SSD_HEADS, acs_f, acs_b)
        acs_t = acs.T
        expand = exp_s[...]
        dt_e = _dot2_l(dt_c, expand)
        acs_e = _dot3_l(acs, expand)
        edge = jnp.concatenate([acs_e[L - 1:L, 0:SSD_INNER], acs_e[0:1, SSD_INNER:W2]], axis=1)
        eacs = jnp.exp(acs_e)
        cdec = jnp.exp(edge)
        xd = jnp.concatenate([x_c, x_c], axis=1) * dt_e
        xd_b = xd.astype(BF16)
        xdw = (xd * jnp.exp(edge - acs_e)).astype(BF16)
        b_b = b_c.astype(BF16)
        c_b = c_c.astype(BF16)
        s_new = _dot(b_c.T.astype(BF16), xdw) * gm_s[...]

        st_f = stf_s[...]
        y = _dot(c_b, st_f.astype(BF16)) * eacs[:, 0:SSD_INNER]
        stf_s[...] = st_f * cdec[:, 0:SSD_INNER] + s_new[:, 0:SSD_INNER]
        inc_s[c] = s_new[:, SSD_INNER:W2]
        dec_s[c] = cdec[:, SSD_INNER:W2]
        eab_s[rows, :] = eacs[:, SSD_INNER:W2]
        cbf_s[rows, :] = c_b

        for direction, causal in ((0, lower), (1, upper)):
            pairs = []
            for g in range(SSD_GROUPS):
                cg = jnp.where(_lane_mask(c_c.shape, g * SSD_STATE, (g + 1) * SSD_STATE), c_c, 0.0)
                cb = _dot_nt(cg.astype(BF16), b_b)
                for hp in range(2):
                    pair = g * 2 + hp
                    res = []
                    for k in range(2):
                        j = direction * SSD_HEADS + pair * 2 + k
                        diff = acs[:, j:j + 1] - acs_t[j:j + 1, :]
                        dec = jnp.where(causal, jnp.exp(jnp.minimum(diff, 0.0)), 0.0)
                        sc = (cb * dec).astype(BF16)
                        lo = direction * SSD_INNER + pair * LANES
                        res.append(_dot(sc, xd_b[:, lo:lo + LANES]))
                    pairs.append(jnp.where(_lane_mask(res[0].shape, 0, SSD_HD), res[0], res[1]))
            y = y + jnp.concatenate(pairs, axis=-1)
        o_ref[rows, :] = y + x_c * dexp_ref[...]
        return carry

    lax.fori_loop(0, nchunks, forward_pass, 0)
    store_state(stf_s, sf_ref)

    def backward_pass(i, carry):
        c = nchunks - 1 - i
        rows = chunk_rows(c)
        st_b = stb_s[...]
        y = o_ref[rows, :] + _dot(cbf_s[rows, :], st_b.astype(BF16)) * eab_s[rows, :]
        stb_s[...] = st_b * dec_s[c] + inc_s[c]
        yt = y * _silu(z_ref[rows, :])
        ms = jnp.mean(yt * yt, axis=-1, keepdims=True)
        o_ref[rows, :] = yt * lax.rsqrt(ms + EPS) * ng_ref[...]
        return carry

    lax.fori_loop(0, nchunks, backward_pass, 0)
    store_state(stb_s, sb_ref)


def _ssd(z, xbc, dt, params, init, layer, batch, seq):
    cw, cb, alog, dtb, dexp, ng = params
    t = z.shape[0]
    has_init = init is not None
    nchunks = seq // SSD_CHUNK
    full = lambda a: pl.BlockSpec(a.shape, lambda b: (0,) * a.ndim)
    st_spec = pl.BlockSpec((None, SSD_INNER, SSD_STATE), lambda b: (b, 0, 0))
    in_specs = [
        pl.BlockSpec((seq, SSD_INNER), lambda b: (b, 0)),
        pl.BlockSpec((seq, XBC_W), lambda b: (b, 0)),
        pl.BlockSpec((seq, LANES), lambda b: (b, 0)),
        full(cw), full(cb), full(alog), full(dtb), full(dexp), full(ng),
    ]
    args = [z, xbc, dt, cw, cb, alog, dtb, dexp, ng]
    if has_init:
        init_spec = pl.BlockSpec((None, None, SSD_INNER, SSD_STATE), lambda b: (b, layer, 0, 0))
        in_specs += [init_spec, init_spec]
        args += list(init)
    st_shape = jax.ShapeDtypeStruct((batch, SSD_INNER, SSD_STATE), F32)
    return pl.pallas_call(
        functools.partial(_ssd_kernel, has_init, seq),
        grid=(batch,),
        in_specs=in_specs,
        out_specs=[pl.BlockSpec((seq, SSD_INNER), lambda b: (b, 0)), st_spec, st_spec],
        out_shape=[jax.ShapeDtypeStruct((t, SSD_INNER), F32), st_shape, st_shape],
        scratch_shapes=[
            pltpu.VMEM((LANES, SSD_INNER), F32), pltpu.VMEM((LANES, SSD_INNER), F32),
            pltpu.VMEM((nchunks, LANES, SSD_INNER), F32),
            pltpu.VMEM((nchunks, 1, SSD_INNER), F32),
            pltpu.VMEM((seq, SSD_INNER), F32),
            pltpu.VMEM((seq, SSD_BC_W), BF16),
            pltpu.VMEM((LANES, 2 * SSD_INNER), BF16),
            pltpu.VMEM((2, SSD_CHUNK, SSD_CHUNK), BF16),
            pltpu.VMEM((LANES, 2 * SSD_INNER), F32),
        ],
        compiler_params=pltpu.CompilerParams(
            dimension_semantics=("arbitrary",), vmem_limit_bytes=VMEM_LIMIT),
    )(*args)


def _post_kernel(alpha, d_ff, ff_chunk, x_ref, oa_ref, os_ref, g1_ref, sh2_ref, sc2_ref, g2_ref,
                 wo_ref, wfi_ref, wfo_ref, lng_ref, lnb_ref, y_ref):
    x = x_ref[...]
    wa = oa_ref.shape[1]
    o = _dot(oa_ref[...].astype(BF16), wo_ref[0:wa, :]) + _dot(os_ref[...].astype(BF16), wo_ref[wa:, :])
    x1 = _layer_norm(alpha * x + g1_ref[...] * o, lng_ref[0:1, :], lnb_ref[0:1, :])
    h2 = (x1 * (1.0 + sc2_ref[...]) + sh2_ref[...]).astype(BF16)
    f = jnp.zeros(x.shape, F32)
    for c in range(d_ff // ff_chunk):
        lo = c * ff_chunk
        g = _dot(h2, wfi_ref[:, lo:lo + ff_chunk])
        u = _dot(h2, wfi_ref[:, d_ff + lo:d_ff + lo + ff_chunk])
        f = f + _dot((_silu(g) * u).astype(BF16), wfo_ref[lo:lo + ff_chunk, :])
    y_ref[...] = _layer_norm(alpha * x1 + g2_ref[...] * f, lng_ref[1:2, :], lnb_ref[1:2, :])


def _post(x, oa, os_, mods, layer, row_fn, wo_bf, wfi_bf, wfo_bf, lng, lnb, alpha, tm):
    t, d = x.shape
    d_ff = wfo_bf.shape[0]
    ff_chunk = d_ff // 2
    resident = lambda a: pl.BlockSpec(a.shape, lambda i: (0,) * a.ndim, pipeline_mode=pl.Buffered(1))
    return pl.pallas_call(
        functools.partial(_post_kernel, alpha, d_ff, ff_chunk),
        grid=(t // tm,),
        in_specs=[
            pl.BlockSpec((tm, d), lambda i: (i, 0)),
            pl.BlockSpec((tm, oa.shape[1]), lambda i: (i, 0)),
            pl.BlockSpec((tm, os_.shape[1]), lambda i: (i, 0)),
            _mod_spec(layer, 2, row_fn, d),
            _mod_spec(layer, 3, row_fn, d),
            _mod_spec(layer, 4, row_fn, d),
            _mod_spec(layer, 5, row_fn, d),
            resident(wo_bf), resident(wfi_bf), resident(wfo_bf),
            pl.BlockSpec(lng.shape, lambda i: (0, 0)),
            pl.BlockSpec(lnb.shape, lambda i: (0, 0)),
        ],
        out_specs=pl.BlockSpec((tm, d), lambda i: (i, 0)),
        out_shape=jax.ShapeDtypeStruct((t, d), F32),
        compiler_params=pltpu.CompilerParams(
            dimension_semantics=("arbitrary",), vmem_limit_bytes=VMEM_LIMIT),
    )(x, oa, os_, mods, mods, mods, mods, wo_bf, wfi_bf, wfo_bf, lng, lnb)


def _rope_tables(rows, dim, copies):
    row = jnp.repeat(jnp.arange(rows), GRID_W).astype(F32)
    col = jnp.tile(jnp.arange(GRID_W), rows).astype(F32)
    n_freq = dim // 4
    inv = ROPE_THETA ** (-jnp.arange(n_freq, dtype=F32) / n_freq)
    ang = jnp.concatenate([row[:, None] * inv, col[:, None] * inv], -1)
    cos, sin = jnp.cos(ang), jnp.sin(ang)
    cos_full = jnp.repeat(cos, 2, axis=-1)
    sin_signed = jnp.stack([-sin, sin], axis=-1).reshape(sin.shape[0], dim)
    return jnp.tile(cos_full, (1, copies)), jnp.tile(sin_signed, (1, copies))


def kernel(x_prompt, x_sample, cache_diff_k, cache_diff_v, cache_gqa_k, cache_gqa_v, state_ssd_fwd, state_ssd_bwd, c, c_ctx, w_ada, b_ada, w_in, w_out, diff_lambda, diff_subln_g, qk_norm_g, ssd_conv_w, ssd_conv_b, ssd_A_log, ssd_dt_bias, ssd_D, ssd_norm_g, ln_g, ln_b, w_ffn_in, w_ffn_out):
    batch, seq, d = x_prompt.shape
    dec_batch, dec_seq, _ = x_sample.shape
    depth = w_in.shape[0]
    past = cache_diff_k.shape[2]
    alpha = (2 * depth) ** 0.25
    rows = dec_seq // GRID_W

    n_vec = 1 + dec_batch
    n_pad = -(-n_vec // 8) * 8
    cvec = jnp.concatenate([c_ctx[None, :], c, jnp.zeros((n_pad - n_vec, d), F32)], axis=0)
    mods = _modulation(cvec, w_ada, b_ada).reshape(depth, n_pad, 1, 6 * d)

    cos_d, sin_d = _rope_tables(rows, DIFF_QK, DIFF_W // DIFF_QK)
    cos_g, sin_g = _rope_tables(rows, GQA_HD, GQA_Q_HEADS)
    cosq = jnp.concatenate([cos_d, cos_g], axis=-1)
    sinq = jnp.concatenate([sin_d, sin_g], axis=-1)
    cosk = jnp.concatenate([cos_d, cos_g[:, :GQA_KV_W]], axis=-1)
    sink = jnp.concatenate([sin_d, sin_g[:, :GQA_KV_W]], axis=-1)
    tables = (cosq, sinq, cosk, sink)

    caches = (cache_diff_k.reshape(dec_batch, depth, past, DIFF_W),
              cache_diff_v.reshape(dec_batch, depth, past, DIFF_W),
              cache_gqa_k.reshape(dec_batch, depth, past, GQA_KV_W),
              cache_gqa_v.reshape(dec_batch, depth, past, GQA_KV_W))

    xp = x_prompt.reshape(batch * seq, d)
    xs = x_sample.reshape(dec_batch * dec_seq, d)
    tm_ctx = min(512, batch * seq)
    tm_lat = min(512, dec_seq)
    tq = min(256, dec_seq)
    ctx_row = lambda i: 0
    lat_row = lambda i: 1 + (i * tm_lat) // dec_seq

    new_dk, new_dv, new_gk, new_gv, new_sf, new_sb = [], [], [], [], [], []
    for l in range(depth):
        lam_init = 0.8 - 0.6 * math.exp(-0.3 * l)
        w_in_bf = jnp.pad(w_in[l], ((0, 0), (0, IN_W_PAD - w_in.shape[2]))).astype(BF16)
        wo_bf = w_out[l].astype(BF16)
        wfi_bf = w_ffn_in[l].astype(BF16)
        wfo_bf = w_ffn_out[l].astype(BF16)
        lam_p = diff_lambda[l]
        attn_gain = jnp.concatenate([jnp.tile(diff_subln_g[l], DIFF_HEADS), jnp.ones((GQA_W,), F32)])[None, :]
        gq_t = jnp.tile(qk_norm_g[l, 0], GQA_Q_HEADS)[None, :]
        gk_t = jnp.tile(qk_norm_g[l, 1], GQA_KV_HEADS)[None, :]
        pad_row = lambda v: jnp.pad(v.reshape(1, -1), ((0, 0), (0, LANES - v.size)))
        ssd_params = (ssd_conv_w[l], ssd_conv_b[l][None, :], pad_row(ssd_A_log[l]),
                      pad_row(ssd_dt_bias[l]), jnp.repeat(ssd_D[l], SSD_HD)[None, :],
                      ssd_norm_g[l][None, :])

        qa, dk, dv, gk, gv, z, xbc, dt = _inproj(xp, mods, l, ctx_row, w_in_bf, tm_ctx)
        oa, gkn = _attention(qa, (dk, dv, gk, gv), None, l, None, lam_p, attn_gain, gq_t, gk_t,
                             lam_init, batch, seq, seq, True)
        os_, sf, sb = _ssd(z, xbc, dt, ssd_params, None, l, batch, seq)
        xp = _post(xp, oa, os_, mods, l, ctx_row, wo_bf, wfi_bf, wfo_bf, ln_g[l], ln_b[l], alpha, tm_ctx)
        new_dk.append(dk.reshape(batch, seq, DIFF_HEADS, 2 * DIFF_QK))
        new_dv.append(dv.reshape(batch, seq, DIFF_HEADS, DIFF_V))
        new_gk.append(gkn.reshape(batch, seq, GQA_KV_HEADS, GQA_HD))
        new_gv.append(gv.reshape(batch, seq, GQA_KV_HEADS, GQA_HD))
        new_sf.append(sf.reshape(batch, SSD_HEADS, SSD_HD, SSD_STATE))
        new_sb.append(sb.reshape(batch, SSD_HEADS, SSD_HD, SSD_STATE))

        qa, dk, dv, gk, gv, z, xbc, dt = _inproj(xs, mods, l, lat_row, w_in_bf, tm_lat)
        (oa,) = _attention(qa, (dk, dv, gk, gv), caches, l, tables, lam_p, attn_gain, gq_t, gk_t,
                           lam_init, dec_batch, dec_seq, tq, False)
        init = (state_ssd_fwd.reshape(dec_batch, depth, SSD_INNER, SSD_STATE),
                state_ssd_bwd.reshape(dec_batch, depth, SSD_INNER, SSD_STATE))
        os_, _, _ = _ssd(z, xbc, dt, ssd_params, init, l, dec_batch, dec_seq)
        xs = _post(xs, oa, os_, mods, l, lat_row, wo_bf, wfi_bf, wfo_bf, ln_g[l], ln_b[l], alpha, tm_lat)

    return (xp.reshape(batch, seq, d), xs.reshape(dec_batch, dec_seq, d),
            jnp.stack(new_dk, 1), jnp.stack(new_dv, 1), jnp.stack(new_gk, 1),
            jnp.stack(new_gv, 1), jnp.stack(new_sf, 1), jnp.stack(new_sb, 1))
```

```python
import functools
import math

import jax
import jax.numpy as jnp
from jax import lax
from jax.experimental import pallas as pl
from jax.experimental.pallas import tpu as pltpu

F32 = jnp.float32
BF16 = jnp.bfloat16

GRID_W = 64
DIFF_HEADS = 4
DIFF_QK = 32
DIFF_V = 64
DIFF_W = DIFF_HEADS * DIFF_V
GQA_HD = 64
GQA_Q_HEADS = 4
GQA_KV_HEADS = 2
GQA_W = GQA_Q_HEADS * GQA_HD
GQA_KV_W = GQA_KV_HEADS * GQA_HD
SSD_HD = 64
SSD_HEADS = 8
SSD_INNER = SSD_HEADS * SSD_HD
SSD_GROUPS = 2
SSD_STATE = 64
SSD_BC_W = SSD_GROUPS * SSD_STATE
SSD_CONV = 5
SSD_CHUNK = 128
XBC_W = SSD_INNER + 2 * SSD_BC_W
DT_W = 2 * SSD_HEADS
ROPE_THETA = 10000.0
EPS = 1e-5
LANES = 128
VMEM_LIMIT = 56 * 1024 * 1024

_C_DQ, _C_DK, _C_DV, _C_GQ, _C_GK, _C_GV, _C_Z, _C_XBC, _C_DT, _C_END = (
    0, 256, 512, 768, 1024, 1152, 1280, 1792, 2560, 2576)
IN_W_PAD = _C_DT + LANES


def _dot(a, b):
    return jnp.dot(a, b, preferred_element_type=F32)


def _dot_nt(a, b):
    return lax.dot_general(a, b, (((1,), (1,)), ((), ())), preferred_element_type=F32)


def _split3(a):
    a1 = a.astype(BF16)
    r1 = a - a1.astype(F32)
    a2 = r1.astype(BF16)
    a3 = (r1 - a2.astype(F32)).astype(BF16)
    return a1, a2, a3


def _dot3_l(a, b_exact):
    a1, a2, a3 = _split3(a)
    return _dot(a1, b_exact) + (_dot(a2, b_exact) + _dot(a3, b_exact))


def _sigmoid(x):
    return 1.0 / (1.0 + jnp.exp(-x))


def _silu(x):
    return x * _sigmoid(x)


def _layer_norm(x, g, b):
    mu = jnp.mean(x, axis=-1, keepdims=True)
    xc = x - mu
    var = jnp.mean(xc * xc, axis=-1, keepdims=True)
    return xc * lax.rsqrt(var + EPS) * g + b


def _group_avg_matrix(width, group):
    sh = int(math.log2(group))
    r = lax.shift_right_logical(lax.broadcasted_iota(jnp.int32, (width, width), 0), sh)
    c = lax.shift_right_logical(lax.broadcasted_iota(jnp.int32, (width, width), 1), sh)
    return jnp.where(r == c, 1.0 / group, 0.0).astype(BF16)


def _group_mean_sq(x, gmat):
    xx = x * x
    hi = xx.astype(BF16)
    lo = (xx - hi.astype(F32)).astype(BF16)
    return _dot(hi, gmat) + _dot(lo, gmat)


def _rope(x, cos, sin_signed):
    w = x.shape[-1]
    lane = lax.broadcasted_iota(jnp.int32, x.shape, 1)
    nxt = pltpu.roll(x, w - 1, 1)
    prv = pltpu.roll(x, 1, 1)
    partner = jnp.where((lane & 1) == 0, nxt, prv)
    return x * cos + partner * sin_signed


def _lane_mask(shape, lo, hi):
    lane = lax.broadcasted_iota(jnp.int32, shape, 1)
    return (lane >= lo) & (lane < hi)


def _mod_kernel(c_ref, w_ref, b_ref, o_ref):
    a = _silu(c_ref[...])
    a_hi = a.astype(BF16)
    a_lo = (a - a_hi.astype(F32)).astype(BF16)
    w = w_ref[...]
    w_hi = w.astype(BF16)
    w_lo = (w - w_hi.astype(F32)).astype(BF16)
    o_ref[...] = _dot(a_hi, w_hi) + (_dot(a_lo, w_hi) + _dot(a_hi, w_lo)) + b_ref[...]


def _modulation(cvec, w_ada, b_ada):
    depth, d, n = w_ada.shape
    tn = 1536
    rows = cvec.shape[0]
    return pl.pallas_call(
        _mod_kernel,
        grid=(depth, n // tn),
        in_specs=[
            pl.BlockSpec((rows, d), lambda l, j: (0, 0)),
            pl.BlockSpec((None, d, tn), lambda l, j: (l, 0, j)),
            pl.BlockSpec((None, 1, tn), lambda l, j: (l, 0, j)),
        ],
        out_specs=pl.BlockSpec((None, rows, tn), lambda l, j: (l, 0, j)),
        out_shape=jax.ShapeDtypeStruct((depth, rows, n), F32),
        compiler_params=pltpu.CompilerParams(
            dimension_semantics=("arbitrary", "arbitrary"), vmem_limit_bytes=VMEM_LIMIT),
    )(cvec, w_ada, b_ada.reshape(depth, 1, n))


def _inproj_kernel(feature_major_cache, n_prev, *refs):
    x_ref, sh_ref, sc_ref, w_ref = refs[:4]
    outs = refs[4 + n_prev:]
    h = (x_ref[...] * (1.0 + sc_ref[...]) + sh_ref[...]).astype(BF16)

    def mm(lo, hi):
        return _dot(h, w_ref[:, lo:hi])

    if feature_major_cache:
        (qa_ref, dk_ref, gk_ref, z_ref, xbc_ref, dt_ref, dkt_ref, dvt_ref, gvt_ref,
         dv_s, gv_s) = outs
        dk_ref[...] = mm(_C_DK, _C_DV)
        dkt_ref[...] = dk_ref[...].T
        dv_s[...] = mm(_C_DV, _C_GQ)
        dvt_ref[...] = dv_s[...].T
        gv_s[...] = mm(_C_GV, _C_Z)
        gvt_ref[...] = gv_s[...].T
    else:
        qa_ref, dk_ref, dv_ref, gk_ref, gv_ref, z_ref, xbc_ref, dt_ref = outs
        dk_ref[...] = mm(_C_DK, _C_DV)
        dv_ref[...] = mm(_C_DV, _C_GQ)
        gv_ref[...] = mm(_C_GV, _C_Z)
    qa_ref[:, 0:DIFF_W] = mm(_C_DQ, _C_DK)
    qa_ref[:, DIFF_W:DIFF_W + GQA_W] = mm(_C_GQ, _C_GK)
    gk_ref[...] = mm(_C_GK, _C_GV)
    z_ref[...] = mm(_C_Z, _C_XBC)
    xbc_ref[...] = mm(_C_XBC, _C_DT)
    dt_ref[...] = mm(_C_DT, IN_W_PAD)


def _mod_spec(layer, which, row_fn, d):
    return pl.BlockSpec((None, None, 1, d), lambda i: (layer, row_fn(i), 0, which))


def _inproj(x, mods, layer, row_fn, w_in_bf, tm, cache_prev=None, depth=None):
    t, d = x.shape
    feature_major = cache_prev is not None
    token_spec = lambda w: pl.BlockSpec((tm, w), lambda i: (i, 0))
    token_shape = lambda w: jax.ShapeDtypeStruct((t, w), F32)
    if feature_major:
        widths = (DIFF_W + GQA_W, DIFF_W, GQA_KV_W, SSD_INNER, XBC_W, LANES)
        cache_w = (DIFF_W, DIFF_W, GQA_KV_W)
        out_specs = [token_spec(w) for w in widths] + [
            pl.BlockSpec((None, w, tm), lambda i: (layer, 0, i)) for w in cache_w]
        out_shape = [token_shape(w) for w in widths] + [
            jax.ShapeDtypeStruct((depth, w, t), F32) for w in cache_w]
        aliases = {4 + k: len(widths) + k for k in range(len(cache_prev))}
    else:
        widths = (DIFF_W + GQA_W, DIFF_W, DIFF_W, GQA_KV_W, GQA_KV_W, SSD_INNER, XBC_W, LANES)
        out_specs = [token_spec(w) for w in widths]
        out_shape = [token_shape(w) for w in widths]
        cache_prev, aliases = (), {}
    return pl.pallas_call(
        functools.partial(_inproj_kernel, feature_major, len(cache_prev)),
        grid=(t // tm,),
        in_specs=[
            pl.BlockSpec((tm, d), lambda i: (i, 0)),
            _mod_spec(layer, 0, row_fn, d),
            _mod_spec(layer, 1, row_fn, d),
            pl.BlockSpec((None, d, IN_W_PAD), lambda i: (layer, 0, 0)),
        ] + [pl.BlockSpec(memory_space=pl.ANY)] * len(cache_prev),
        out_specs=out_specs,
        out_shape=out_shape,
        input_output_aliases=aliases,
        scratch_shapes=([pltpu.VMEM((tm, DIFF_W), F32), pltpu.VMEM((tm, GQA_KV_W), F32)]
                        if feature_major else []),
        compiler_params=pltpu.CompilerParams(
            dimension_semantics=("arbitrary",), vmem_limit_bytes=VMEM_LIMIT),
    )(x, mods, mods, w_in_bf, *cache_prev)


LOG2E = 1.4426950408889634
NEG_BIG = -1e30
N_SCORE_HEADS = 2 * DIFF_HEADS + GQA_Q_HEADS


def _diff_lambda(lam_ref, lam_init):
    lp = lam_ref[...]
    s1 = jnp.sum(lp[0:1, :] * lp[1:2, :], axis=-1, keepdims=True)
    s2 = jnp.sum(lp[2:3, :] * lp[3:4, :], axis=-1, keepdims=True)
    return jnp.exp(s1) - jnp.exp(s2) + lam_init


def _swap_halves(x):
    return pltpu.roll(x, GQA_HD, 1)


def _attn_kernel(cfg, *refs):
    lam_init, seq, past, kb, rope, feature_major, n_prev = cfg
    it = iter(refs)
    qa_ref, dk_ref, dv_ref, gk_in_ref, gv_ref = next(it), next(it), next(it), next(it), next(it)
    for _ in range(n_prev):
        next(it)
    if past:
        cdk_ref, cdv_ref, cgk_ref, cgv_ref = next(it), next(it), next(it), next(it)
    if rope:
        cosq_ref, sinq_ref, cosk_ref, sink_ref = next(it), next(it), next(it), next(it)
    lam_ref, gain_ref, gq_ref, gk_ref = next(it), next(it), next(it), next(it)
    o_ref = next(it)
    gkn_ref = next(it) if feature_major else None
    kd_s, kg_s, vdt_s, vgt_s, wq_s, wg_s, s0_s, s1_s, p_s, m_s, l_s, alpha_s, acc_s = it
    tq = qa_ref.shape[0]
    n_blocks = (seq + past) // kb

    @pl.when(pl.program_id(1) == 0)
    def _prepare_keys():
        gmat = _group_avg_matrix(GQA_KV_W, GQA_HD)
        for i in range(seq // kb):
            rows = slice(i * kb, (i + 1) * kb)
            dk = dk_ref[rows, :]
            gk = gk_in_ref[rows, :]
            gk = gk * lax.rsqrt(_group_mean_sq(gk, gmat) + EPS) * gk_ref[...]
            if feature_major:
                gkn_ref[:, rows] = gk.T
            if rope:
                ck = cosk_ref[rows, :]
                sk = sink_ref[rows, :]
                dk = _rope(dk, ck[:, 0:DIFF_W], sk[:, 0:DIFF_W])
                gk = _rope(gk, ck[:, DIFF_W:DIFF_W + GQA_KV_W], sk[:, DIFF_W:DIFF_W + GQA_KV_W])
            kd_s[rows, :] = dk.astype(BF16)
            kg_s[rows, :] = gk.astype(BF16)
            if feature_major:
                vdt_s[i] = dv_ref[:, rows].astype(BF16)
                vgt_s[i] = gv_ref[:, rows].astype(BF16)
            else:
                vdt_s[i] = dv_ref[rows, :].T.astype(BF16)
                vgt_s[i] = gv_ref[rows, :].T.astype(BF16)
        for j in range(past // kb):
            src = slice(j * kb, (j + 1) * kb)
            dst = slice(seq + j * kb, seq + (j + 1) * kb)
            kd_s[dst, :] = cdk_ref[src, :].astype(BF16)
            kg_s[dst, :] = cgk_ref[src, :].astype(BF16)
            vdt_s[seq // kb + j] = cdv_ref[src, :].T.astype(BF16)
            vgt_s[seq // kb + j] = cgv_ref[src, :].T.astype(BF16)

    qa = qa_ref[...]
    qd = qa[:, 0:DIFF_W]
    gq = qa[:, DIFF_W:DIFF_W + GQA_W]
    gq = gq * lax.rsqrt(_group_mean_sq(gq, _group_avg_matrix(GQA_W, GQA_HD)) + EPS) * gq_ref[...]
    if rope:
        cq = cosq_ref[...]
        sq = sinq_ref[...]
        qd = _rope(qd, cq[:, 0:DIFF_W], sq[:, 0:DIFF_W])
        gq = _rope(gq, cq[:, DIFF_W:DIFF_W + GQA_W], sq[:, DIFF_W:DIFF_W + GQA_W])
    qd_t = (qd * (DIFF_QK ** -0.5 * LOG2E)).T
    gq_t = (gq * (GQA_HD ** -0.5 * LOG2E)).T
    row = lax.broadcasted_iota(jnp.int32, (DIFF_W, tq), 0)
    for hm in range(2 * DIFF_HEADS):
        lo = hm * DIFF_QK
        wq_s[:, hm * tq:(hm + 1) * tq] = jnp.where((row >= lo) & (row < lo + DIFF_QK), qd_t, 0.0).astype(BF16)
    zeros = jnp.zeros((GQA_HD, tq), F32)
    for h in range(GQA_Q_HEADS):
        piece = gq_t[h * GQA_HD:(h + 1) * GQA_HD, :]
        pair = [piece, zeros] if h // (GQA_Q_HEADS // GQA_KV_HEADS) == 0 else [zeros, piece]
        wg_s[:, h * tq:(h + 1) * tq] = jnp.concatenate(pair, axis=0).astype(BF16)
    m_s[...] = jnp.full(m_s.shape, NEG_BIG, F32)
    l_s[...] = jnp.zeros(l_s.shape, F32)
    acc_s[...] = jnp.zeros(acc_s.shape, F32)
    n_diff = 2 * DIFF_HEADS * tq
    n_all = N_SCORE_HEADS * tq

    def scores(j, s_buf):
        rows = pl.ds(j * kb if isinstance(j, int) else pl.multiple_of(j * kb, kb), kb)
        sd = _dot(kd_s[rows, :], wq_s[...])
        for c in range(n_diff // LANES):
            s_buf[c] = sd[:, c * LANES:(c + 1) * LANES]
        sg = _dot(kg_s[rows, :], wg_s[...])
        for c in range((n_all - n_diff) // LANES):
            s_buf[n_diff // LANES + c] = sg[:, c * LANES:(c + 1) * LANES]

    def accumulate(j, s_buf):
        for c in range(n_all // LANES):
            cols = slice(c * LANES, (c + 1) * LANES)
            s = s_buf[c]
            m_old = m_s[:, cols]
            m_new = jnp.maximum(m_old, jnp.max(s, axis=0, keepdims=True))
            alpha = jnp.exp2(m_old - m_new)
            p = jnp.exp2(s - m_new)
            l_s[:, cols] = alpha * l_s[:, cols] + jnp.sum(p, axis=0, keepdims=True)
            m_s[:, cols] = m_new
            alpha_s[:, cols] = alpha
            p_s[c] = p.astype(BF16)
        v_d = vdt_s[j]
        v_g = vgt_s[j]
        slabs = tq // LANES
        for idx in range(N_SCORE_HEADS):
            cols = slice(idx * tq, (idx + 1) * tq)
            if idx < 2 * DIFF_HEADS:
                vh = idx // 2
                v_t = v_d[vh * DIFF_V:(vh + 1) * DIFF_V, :]
            else:
                vh = (idx - 2 * DIFF_HEADS) // (GQA_Q_HEADS // GQA_KV_HEADS)
                v_t = v_g[vh * GQA_HD:(vh + 1) * GQA_HD, :]
            p = jnp.concatenate([p_s[idx * slabs + k] for k in range(slabs)], axis=1)
            acc_s[idx] = alpha_s[:, cols] * acc_s[idx] + _dot(v_t, p)

    scores(0, s0_s)
    if n_blocks > 1:
        def block_pair(i, carry):
            scores(2 * i + 1, s1_s)
            accumulate(2 * i, s0_s)
            scores(2 * i + 2, s0_s)
            accumulate(2 * i + 1, s1_s)
            return carry

        lax.fori_loop(0, n_blocks // 2 - 1, block_pair, 0)
        scores(n_blocks - 1, s1_s)
        accumulate(n_blocks - 2, s0_s)
        accumulate(n_blocks - 1, s1_s)
    else:
        accumulate(0, s0_s)

    lam = _diff_lambda(lam_ref, lam_init)
    outs = []
    for h in range(DIFF_HEADS):
        c0 = slice(2 * h * tq, (2 * h + 1) * tq)
        c1 = slice((2 * h + 1) * tq, (2 * h + 2) * tq)
        o = acc_s[2 * h] * (1.0 / l_s[:, c0]) - acc_s[2 * h + 1] * (lam / l_s[:, c1])
        ms = jnp.mean(o * o, axis=0, keepdims=True)
        outs.append(o * (lax.rsqrt(ms + EPS) * (1.0 - lam_init)))
    for h in range(GQA_Q_HEADS):
        idx = 2 * DIFF_HEADS + h
        outs.append(acc_s[idx] * (1.0 / l_s[:, idx * tq:(idx + 1) * tq]))
    o_ref[...] = jnp.concatenate(outs, axis=0).T * gain_ref[...]


def _attention(qa, kv, caches, layer, tables, lam_p, gain, gq_t, gk_t, lam_init, batch, seq, tq,
               gkn_prev=None):
    t = qa.shape[0]
    depth = kv[1].shape[0]
    nq = seq // tq
    feature_major = gkn_prev is not None
    past = caches[0].shape[2] if caches is not None else 0
    kb = min(256, seq)
    assert seq % kb == 0 and past % kb == 0 and seq % tq == 0
    n_blocks = (seq + past) // kb
    assert n_blocks == 1 or n_blocks % 2 == 0
    n_all = N_SCORE_HEADS * tq
    full = lambda a: pl.BlockSpec(a.shape, lambda b, q: (0,) * a.ndim)
    token_major = lambda a: pl.BlockSpec((seq, a.shape[1]), lambda b, q: (b, 0))
    by_feature = lambda a: pl.BlockSpec((None, a.shape[1], seq), lambda b, q: (layer, 0, b))
    dk, dv, gk, gv = kv
    in_specs = [pl.BlockSpec((tq, qa.shape[1]), lambda b, q: (b * nq + q, 0)),
                token_major(dk), by_feature(dv) if feature_major else token_major(dv),
                token_major(gk), by_feature(gv) if feature_major else token_major(gv)]
    args = [qa, dk, dv, gk, gv]
    aliases = {}
    if feature_major:
        assert nq == 1
        in_specs += [pl.BlockSpec(memory_space=pl.ANY)] * len(gkn_prev)
        args += list(gkn_prev)
        aliases = {5 + k: 1 + k for k in range(len(gkn_prev))}
    if caches is not None:
        in_specs += [pl.BlockSpec((None, None, past, a.shape[3]), lambda b, q: (b, layer, 0, 0))
                     for a in caches]
        args += list(caches)
    if tables is not None:
        cosq, sinq, cosk, sink = tables
        in_specs += [pl.BlockSpec((tq, cosq.shape[1]), lambda b, q: (q, 0)),
                     pl.BlockSpec((tq, sinq.shape[1]), lambda b, q: (q, 0)),
                     full(cosk), full(sink)]
        args += [cosq, sinq, cosk, sink]
    in_specs += [full(lam_p), full(gain), full(gq_t), full(gk_t)]
    args += [lam_p, gain, gq_t, gk_t]
    out_specs = [pl.BlockSpec((tq, DIFF_W + GQA_W), lambda b, q: (b * nq + q, 0))]
    out_shape = [jax.ShapeDtypeStruct((t, DIFF_W + GQA_W), F32)]
    if feature_major:
        out_specs.append(pl.BlockSpec((None, GQA_KV_W, seq), lambda b, q: (layer, 0, b)))
        out_shape.append(jax.ShapeDtypeStruct((depth, GQA_KV_W, t), F32))
    cfg = (lam_init, seq, past, kb, tables is not None, feature_major,
           len(gkn_prev) if feature_major else 0)
    return pl.pallas_call(
        functools.partial(_attn_kernel, cfg),
        grid=(batch, nq),
        in_specs=in_specs,
        out_specs=out_specs,
        out_shape=out_shape,
        input_output_aliases=aliases,
        scratch_shapes=[
            pltpu.VMEM((seq + past, DIFF_W), BF16), pltpu.VMEM((seq + past, GQA_KV_W), BF16),
            pltpu.VMEM((n_blocks, DIFF_W, kb), BF16), pltpu.VMEM((n_blocks, GQA_KV_W, kb), BF16),
            pltpu.VMEM((DIFF_W, 2 * DIFF_HEADS * tq), BF16), pltpu.VMEM((GQA_KV_W, GQA_Q_HEADS * tq), BF16),
            pltpu.VMEM((n_all // LANES, kb, LANES), F32), pltpu.VMEM((n_all // LANES, kb, LANES), F32),
            pltpu.VMEM((n_all // LANES, kb, LANES), BF16),
            pltpu.VMEM((1, n_all), F32), pltpu.VMEM((1, n_all), F32), pltpu.VMEM((1, n_all), F32),
            pltpu.VMEM((N_SCORE_HEADS, GQA_HD, tq), F32),
        ],
        compiler_params=pltpu.CompilerParams(
            dimension_semantics=("arbitrary", "arbitrary"), vmem_limit_bytes=VMEM_LIMIT),
    )(*args)


def _softplus(x):
    return jnp.maximum(x, 0.0) + jnp.log1p(jnp.exp(-jnp.abs(x)))


def _dot2_l(a, b_exact):
    a1 = a.astype(BF16)
    a2 = (a - a1.astype(F32)).astype(BF16)
    return _dot(a1, b_exact) + _dot(a2, b_exact)


def _ssd_kernel(has_init, n_prev, seq, *refs):
    it = iter(refs)
    z_ref, xbc_ref, dt_ref, cw_ref, cb_ref, alog_ref, dtb_ref, dexp_ref, ng_ref = (
        next(it) for _ in range(9))
    sf0_ref, sb0_ref = (next(it), next(it)) if has_init else (None, None)
    for _ in range(n_prev):
        next(it)
    o_ref, sf_ref, sb_ref = next(it), next(it), next(it)
    stf_s, stb_s, inc_s, dec_s, eab_s, cbf_s, exp_s, tri_s, gm_s = it
    L = SSD_CHUNK
    W2 = 2 * SSD_INNER
    nchunks = seq // L
    halo = 8

    ri = lax.broadcasted_iota(jnp.int32, (L, L), 0)
    ci = lax.broadcasted_iota(jnp.int32, (L, L), 1)
    lower = ri >= ci
    upper = ri <= ci
    tri_s[0] = jnp.where(lower, 1.0, 0.0).astype(BF16)
    tri_s[1] = jnp.where(upper, 1.0, 0.0).astype(BF16)
    lane_row = lax.broadcasted_iota(jnp.int32, (1, LANES), 1)
    a_row = jnp.where(lane_row < DT_W, -jnp.exp(alog_ref[...]), 0.0)
    ej = lax.broadcasted_iota(jnp.int32, (LANES, W2), 0)
    eh = lax.shift_right_logical(lax.broadcasted_iota(jnp.int32, (LANES, W2), 1), 6)
    exp_s[...] = jnp.where(ej == eh, 1.0, 0.0).astype(BF16)
    gm_s[...] = jnp.where(lax.shift_right_logical(ej, 6) == (lax.shift_right_logical(eh, 2) & 1), 1.0, 0.0)

    def chunk_rows(c):
        return pl.ds(pl.multiple_of(c * L, L), L)

    def load_state(ref):
        r = lax.broadcasted_iota(jnp.int32, (SSD_STATE, LANES), 0)
        c = lax.broadcasted_iota(jnp.int32, (SSD_STATE, LANES), 1)
        dup = jnp.where((c & (SSD_STATE - 1)) == r, 1.0, 0.0).astype(BF16)
        return _dot3_l(ref[...], dup).T * gm_s[:, 0:SSD_INNER]

    def store_state(st_ref, ref):
        st_t = st_ref[...].T
        ref[...] = (st_t + _swap_halves(st_t))[:, 0:SSD_STATE]

    stf_s[...] = load_state(sf0_ref) if has_init else jnp.zeros(stf_s.shape, F32)
    stb_s[...] = load_state(sb0_ref) if has_init else jnp.zeros(stb_s.shape, F32)

    def forward_pass(c, carry):
        r0 = c * L
        rows = chunk_rows(c)
        prev = xbc_ref[pl.ds(pl.multiple_of(jnp.maximum(r0 - halo, 0), halo), halo), :]
        nxt = xbc_ref[pl.ds(pl.multiple_of(jnp.minimum(r0 + L, seq - halo), halo), halo), :]
        cur = xbc_ref[rows, :]
        win = jnp.concatenate([jnp.where(c > 0, prev, 0.0), cur,
                               jnp.where(c < nchunks - 1, nxt, 0.0)], axis=0)
        acc = cb_ref[...] + cur * cw_ref[SSD_CONV // 2:SSD_CONV // 2 + 1, :]
        for j in range(SSD_CONV):
            if j != SSD_CONV // 2:
                shifted = pltpu.roll(win, (SSD_CONV // 2 - j) % (L + 2 * halo), 0)[halo:halo + L, :]
                acc = acc + shifted * cw_ref[j:j + 1, :]
        act = _silu(acc)
        x_c = act[:, 0:SSD_INNER]
        b_c = act[:, SSD_INNER:SSD_INNER + SSD_BC_W]
        c_c = act[:, SSD_INNER + SSD_BC_W:XBC_W]
        dt_c = _softplus(dt_ref[rows, :] + dtb_ref[...])

        d1, d2, d3 = _split3(dt_c * a_row)
        acs_f = _dot(tri_s[0], d1) + (_dot(tri_s[0], d2) + _dot(tri_s[0], d3))
        acs_b = _dot(tri_s[1], d1) + (_dot(tri_s[1], d2) + _dot(tri_s[1], d3))
        acs = jnp.where(lax.broadcasted_iota(jnp.int32, (L, LANES), 1) < SSD_HEADS, acs_f, acs_b)
        acs_t = acs.T
        expand = exp_s[...]
        dt_e = _dot2_l(dt_c, expand)
        acs_e = _dot3_l(acs, expand)
        edge = jnp.concatenate([acs_e[L - 1:L, 0:SSD_INNER], acs_e[0:1, SSD_INNER:W2]], axis=1)
        eacs = jnp.exp(acs_e)
        cdec = jnp.exp(edge)
        xd = jnp.concatenate([x_c, x_c], axis=1) * dt_e
        xd_b = xd.astype(BF16)
        xdw = (xd * jnp.exp(edge - acs_e)).astype(BF16)
        b_b = b_c.astype(BF16)
        c_b = c_c.astype(BF16)
        s_new = _dot(b_c.T.astype(BF16), xdw) * gm_s[...]

        st_f = stf_s[...]
        y = _dot(c_b, st_f.astype(BF16)) * eacs[:, 0:SSD_INNER]
        stf_s[...] = st_f * cdec[:, 0:SSD_INNER] + s_new[:, 0:SSD_INNER]
        inc_s[c] = s_new[:, SSD_INNER:W2]
        dec_s[c] = cdec[:, SSD_INNER:W2]
        eab_s[rows, :] = eacs[:, SSD_INNER:W2]
        cbf_s[rows, :] = c_b

        for direction, causal in ((0, lower), (1, upper)):
            pairs = []
            for g in range(SSD_GROUPS):
                cg = jnp.where(_lane_mask(c_c.shape, g * SSD_STATE, (g + 1) * SSD_STATE), c_c, 0.0)
                cb = _dot_nt(cg.astype(BF16), b_b)
                for hp in range(2):
                    pair = g * 2 + hp
                    res = []
                    for k in range(2):
                        j = direction * SSD_HEADS + pair * 2 + k
                        diff = acs[:, j:j + 1] - acs_t[j:j + 1, :]
                        dec = jnp.where(causal, jnp.exp(jnp.minimum(diff, 0.0)), 0.0)
                        sc = (cb * dec).astype(BF16)
                        lo = direction * SSD_INNER + pair * LANES
                        res.append(_dot(sc, xd_b[:, lo:lo + LANES]))
                    pairs.append(jnp.where(_lane_mask(res[0].shape, 0, SSD_HD), res[0], res[1]))
            y = y + jnp.concatenate(pairs, axis=-1)
        o_ref[rows, :] = y + x_c * dexp_ref[...]
        return carry

    lax.fori_loop(0, nchunks, forward_pass, 0)
    store_state(stf_s, sf_ref)

    def backward_pass(i, carry):
        c = nchunks - 1 - i
        rows = chunk_rows(c)
        st_b = stb_s[...]
        y = o_ref[rows, :] + _dot(cbf_s[rows, :], st_b.astype(BF16)) * eab_s[rows, :]
        stb_s[...] = st_b * dec_s[c] + inc_s[c]
        yt = y * _silu(z_ref[rows, :])
        ms = jnp.mean(yt * yt, axis=-1, keepdims=True)
        o_ref[rows, :] = yt * lax.rsqrt(ms + EPS) * ng_ref[...]
        return carry

    lax.fori_loop(0, nchunks, backward_pass, 0)
    store_state(stb_s, sb_ref)


def _ssd(z, xbc, dt, params, init, layer, batch, seq, state_prev=(), depth=1):
    cw, cb, alog, dtb, dexp, ng = params
    t = z.shape[0]
    has_init = init is not None
    nchunks = seq // SSD_CHUNK
    full = lambda a: pl.BlockSpec(a.shape, lambda b: (0,) * a.ndim)
    out_layer = layer if depth > 1 else 0
    st_spec = pl.BlockSpec((None, None, SSD_INNER, SSD_STATE), lambda b: (b, out_layer, 0, 0))
    in_specs = [
        pl.BlockSpec((seq, SSD_INNER), lambda b: (b, 0)),
        pl.BlockSpec((seq, XBC_W), lambda b: (b, 0)),
        pl.BlockSpec((seq, LANES), lambda b: (b, 0)),
        full(cw), full(cb), full(alog), full(dtb), full(dexp), full(ng),
    ]
    args = [z, xbc, dt, cw, cb, alog, dtb, dexp, ng]
    if has_init:
        init_spec = pl.BlockSpec((None, None, SSD_INNER, SSD_STATE), lambda b: (b, layer, 0, 0))
        in_specs += [init_spec, init_spec]
        args += list(init)
    aliases = {len(args) + k: 1 + k for k in range(len(state_prev))}
    in_specs += [pl.BlockSpec(memory_space=pl.ANY)] * len(state_prev)
    args += list(state_prev)
    st_shape = jax.ShapeDtypeStruct((batch, depth, SSD_INNER, SSD_STATE), F32)
    return pl.pallas_call(
        functools.partial(_ssd_kernel, has_init, len(state_prev), seq),
        grid=(batch,),
        in_specs=in_specs,
        out_specs=[pl.BlockSpec((seq, SSD_INNER), lambda b: (b, 0)), st_spec, st_spec],
        out_shape=[jax.ShapeDtypeStruct((t, SSD_INNER), F32), st_shape, st_shape],
        input_output_aliases=aliases,
        scratch_shapes=[
            pltpu.VMEM((LANES, SSD_INNER), F32), pltpu.VMEM((LANES, SSD_INNER), F32),
            pltpu.VMEM((nchunks, LANES, SSD_INNER), F32),
            pltpu.VMEM((nchunks, 1, SSD_INNER), F32),
            pltpu.VMEM((seq, SSD_INNER), F32),
            pltpu.VMEM((seq, SSD_BC_W), BF16),
            pltpu.VMEM((LANES, 2 * SSD_INNER), BF16),
            pltpu.VMEM((2, SSD_CHUNK, SSD_CHUNK), BF16),
            pltpu.VMEM((LANES, 2 * SSD_INNER), F32),
        ],
        compiler_params=pltpu.CompilerParams(
            dimension_semantics=("arbitrary",), vmem_limit_bytes=VMEM_LIMIT),
    )(*args)


def _post_kernel(alpha, d_ff, ff_chunk, x_ref, oa_ref, os_ref, g1_ref, sh2_ref, sc2_ref, g2_ref,
                 wo_ref, wfi_ref, wfo_ref, lng_ref, lnb_ref, y_ref):
    x = x_ref[...]
    wa = oa_ref.shape[1]
    o = _dot(oa_ref[...].astype(BF16), wo_ref[0:wa, :]) + _dot(os_ref[...].astype(BF16), wo_ref[wa:, :])
    x1 = _layer_norm(alpha * x + g1_ref[...] * o, lng_ref[0:1, :], lnb_ref[0:1, :])
    h2 = (x1 * (1.0 + sc2_ref[...]) + sh2_ref[...]).astype(BF16)
    f = jnp.zeros(x.shape, F32)
    for c in range(d_ff // ff_chunk):
        lo = c * ff_chunk
        g = _dot(h2, wfi_ref[:, lo:lo + ff_chunk])
        u = _dot(h2, wfi_ref[:, d_ff + lo:d_ff + lo + ff_chunk])
        f = f + _dot((_silu(g) * u).astype(BF16), wfo_ref[lo:lo + ff_chunk, :])
    y_ref[...] = _layer_norm(alpha * x1 + g2_ref[...] * f, lng_ref[1:2, :], lnb_ref[1:2, :])


def _post(x, oa, os_, mods, layer, row_fn, wo_bf, wfi_bf, wfo_bf, lng, lnb, alpha, tm):
    t, d = x.shape
    d_ff = wfo_bf.shape[1]
    ff_chunk = d_ff // 2
    per_layer = lambda a, **kw: pl.BlockSpec((None,) + a.shape[1:], lambda i: (layer, 0, 0), **kw)
    resident = lambda a: per_layer(a, pipeline_mode=pl.Buffered(1))
    return pl.pallas_call(
        functools.partial(_post_kernel, alpha, d_ff, ff_chunk),
        grid=(t // tm,),
        in_specs=[
            pl.BlockSpec((tm, d), lambda i: (i, 0)),
            pl.BlockSpec((tm, oa.shape[1]), lambda i: (i, 0)),
            pl.BlockSpec((tm, os_.shape[1]), lambda i: (i, 0)),
            _mod_spec(layer, 2, row_fn, d),
            _mod_spec(layer, 3, row_fn, d),
            _mod_spec(layer, 4, row_fn, d),
            _mod_spec(layer, 5, row_fn, d),
            resident(wo_bf), resident(wfi_bf), resident(wfo_bf),
            per_layer(lng), per_layer(lnb),
        ],
        out_specs=pl.BlockSpec((tm, d), lambda i: (i, 0)),
        out_shape=jax.ShapeDtypeStruct((t, d), F32),
        compiler_params=pltpu.CompilerParams(
            dimension_semantics=("arbitrary",), vmem_limit_bytes=VMEM_LIMIT),
    )(x, oa, os_, mods, mods, mods, mods, wo_bf, wfi_bf, wfo_bf, lng, lnb)


def _rope_tables(rows, dim, copies):
    row = jnp.repeat(jnp.arange(rows), GRID_W).astype(F32)
    col = jnp.tile(jnp.arange(GRID_W), rows).astype(F32)
    n_freq = dim // 4
    inv = ROPE_THETA ** (-jnp.arange(n_freq, dtype=F32) / n_freq)
    ang = jnp.concatenate([row[:, None] * inv, col[:, None] * inv], -1)
    cos, sin = jnp.cos(ang), jnp.sin(ang)
    cos_full = jnp.repeat(cos, 2, axis=-1)
    sin_signed = jnp.stack([-sin, sin], axis=-1).reshape(sin.shape[0], dim)
    return jnp.tile(cos_full, (1, copies)), jnp.tile(sin_signed, (1, copies))


def kernel(x_prompt, x_sample, cache_diff_k, cache_diff_v, cache_gqa_k, cache_gqa_v, state_ssd_fwd, state_ssd_bwd, c, c_ctx, w_ada, b_ada, w_in, w_out, diff_lambda, diff_subln_g, qk_norm_g, ssd_conv_w, ssd_conv_b, ssd_A_log, ssd_dt_bias, ssd_D, ssd_norm_g, ln_g, ln_b, w_ffn_in, w_ffn_out):
    batch, seq, d = x_prompt.shape
    dec_batch, dec_seq, _ = x_sample.shape
    depth = w_in.shape[0]
    past = cache_diff_k.shape[2]
    alpha = (2 * depth) ** 0.25
    rows = dec_seq // GRID_W

    n_vec = 1 + dec_batch
    n_pad = -(-n_vec // 8) * 8
    cvec = jnp.concatenate([c_ctx[None, :], c, jnp.zeros((n_pad - n_vec, d), F32)], axis=0)
    mods = _modulation(cvec, w_ada, b_ada).reshape(depth, n_pad, 1, 6 * d)

    cos_d, sin_d = _rope_tables(rows, DIFF_QK, DIFF_W // DIFF_QK)
    cos_g, sin_g = _rope_tables(rows, GQA_HD, GQA_Q_HEADS)
    cosq = jnp.concatenate([cos_d, cos_g], axis=-1)
    sinq = jnp.concatenate([sin_d, sin_g], axis=-1)
    cosk = jnp.concatenate([cos_d, cos_g[:, :GQA_KV_W]], axis=-1)
    sink = jnp.concatenate([sin_d, sin_g[:, :GQA_KV_W]], axis=-1)
    tables = (cosq, sinq, cosk, sink)

    caches = (cache_diff_k.reshape(dec_batch, depth, past, DIFF_W),
              cache_diff_v.reshape(dec_batch, depth, past, DIFF_W),
              cache_gqa_k.reshape(dec_batch, depth, past, GQA_KV_W),
              cache_gqa_v.reshape(dec_batch, depth, past, GQA_KV_W))

    xp = x_prompt.reshape(batch * seq, d)
    xs = x_sample.reshape(dec_batch * dec_seq, d)
    tm_ctx = min(512, batch * seq)
    tm_lat = min(512, dec_seq)
    tq = min(256, dec_seq)
    ctx_row = lambda i: 0
    lat_row = lambda i: 1 + (i * tm_lat) // dec_seq

    w_in_bf = jnp.pad(w_in, ((0, 0), (0, 0), (0, IN_W_PAD - w_in.shape[2]))).astype(BF16)
    wo_bf = w_out.astype(BF16)
    wfi_bf = w_ffn_in.astype(BF16)
    wfo_bf = w_ffn_out.astype(BF16)
    init = (state_ssd_fwd.reshape(dec_batch, depth, SSD_INNER, SSD_STATE),
            state_ssd_bwd.reshape(dec_batch, depth, SSD_INNER, SSD_STATE))

    kv_cache = ()
    gk_cache = ()
    ssd_states = ()
    for l in range(depth):
        lam_init = 0.8 - 0.6 * math.exp(-0.3 * l)
        lam_p = diff_lambda[l]
        attn_gain = jnp.concatenate([jnp.tile(diff_subln_g[l], DIFF_HEADS), jnp.ones((GQA_W,), F32)])[None, :]
        gq_t = jnp.tile(qk_norm_g[l, 0], GQA_Q_HEADS)[None, :]
        gk_t = jnp.tile(qk_norm_g[l, 1], GQA_KV_HEADS)[None, :]
        pad_row = lambda v: jnp.pad(v.reshape(1, -1), ((0, 0), (0, LANES - v.size)))
        ssd_params = (ssd_conv_w[l], ssd_conv_b[l][None, :], pad_row(ssd_A_log[l]),
                      pad_row(ssd_dt_bias[l]), jnp.repeat(ssd_D[l], SSD_HD)[None, :],
                      ssd_norm_g[l][None, :])

        qa, dk, gk, z, xbc, dt, *kv_cache = _inproj(xp, mods, l, ctx_row, w_in_bf, tm_ctx,
                                                    cache_prev=tuple(kv_cache), depth=depth)
        oa, gkn = _attention(qa, (dk, kv_cache[1], gk, kv_cache[2]), None, l, None, lam_p, attn_gain,
                             gq_t, gk_t, lam_init, batch, seq, seq, gkn_prev=gk_cache)
        gk_cache = (gkn,)
        os_, *ssd_states = _ssd(z, xbc, dt, ssd_params, None, l, batch, seq,
                                state_prev=tuple(ssd_states), depth=depth)
        xp = _post(xp, oa, os_, mods, l, ctx_row, wo_bf, wfi_bf, wfo_bf, ln_g, ln_b, alpha, tm_ctx)

        qa, dk, dv, gk, gv, z, xbc, dt = _inproj(xs, mods, l, lat_row, w_in_bf, tm_lat)
        (oa,) = _attention(qa, (dk, dv, gk, gv), caches, l, tables, lam_p, attn_gain, gq_t, gk_t,
                           lam_init, dec_batch, dec_seq, tq)
        os_, _, _ = _ssd(z, xbc, dt, ssd_params, init, l, dec_batch, dec_seq)
        xs = _post(xs, oa, os_, mods, l, lat_row, wo_bf, wfi_bf, wfo_bf, ln_g, ln_b, alpha, tm_lat)

    def token_major(a, heads, width):
        return jnp.transpose(a.reshape(depth, heads, width, batch, seq), (3, 0, 4, 1, 2))

    state = lambda a: a.reshape(batch, depth, SSD_HEADS, SSD_HD, SSD_STATE)
    return (xp.reshape(batch, seq, d), xs.reshape(dec_batch, dec_seq, d),
            token_major(kv_cache[0], DIFF_HEADS, 2 * DIFF_QK), token_major(kv_cache[1], DIFF_HEADS, DIFF_V),
            token_major(gk_cache[0], GQA_KV_HEADS, GQA_HD), token_major(kv_cache[2], GQA_KV_HEADS, GQA_HD),
            state(ssd_states[0]), state(ssd_states[1]))
```

```python
import functools
import math

import jax
import jax.numpy as jnp
from jax import lax
from jax.experimental import pallas as pl
from jax.experimental.pallas import tpu as pltpu

F32 = jnp.float32
BF16 = jnp.bfloat16

GRID_W = 64
DIFF_HEADS = 4
DIFF_QK = 32
DIFF_V = 64
DIFF_W = DIFF_HEADS * DIFF_V
GQA_HD = 64
GQA_Q_HEADS = 4
GQA_KV_HEADS = 2
GQA_W = GQA_Q_HEADS * GQA_HD
GQA_KV_W = GQA_KV_HEADS * GQA_HD
SSD_HD = 64
SSD_HEADS = 8
SSD_INNER = SSD_HEADS * SSD_HD
SSD_GROUPS = 2
SSD_STATE = 64
SSD_BC_W = SSD_GROUPS * SSD_STATE
SSD_CONV = 5
SSD_CHUNK = 128
XBC_W = SSD_INNER + 2 * SSD_BC_W
DT_W = 2 * SSD_HEADS
ROPE_THETA = 10000.0
EPS = 1e-5
LANES = 128
VMEM_LIMIT = 56 * 1024 * 1024

_C_DQ, _C_DK, _C_DV, _C_GQ, _C_GK, _C_GV, _C_Z, _C_XBC, _C_DT, _C_END = (
    0, 256, 512, 768, 1024, 1152, 1280, 1792, 2560, 2576)
IN_W_PAD = _C_DT + LANES


def _dot(a, b):
    return jnp.dot(a, b, preferred_element_type=F32)


def _dot_nt(a, b):
    return lax.dot_general(a, b, (((1,), (1,)), ((), ())), preferred_element_type=F32)


def _split3(a):
    a1 = a.astype(BF16)
    r1 = a - a1.astype(F32)
    a2 = r1.astype(BF16)
    a3 = (r1 - a2.astype(F32)).astype(BF16)
    return a1, a2, a3


def _dot3_l(a, b_exact):
    a1, a2, a3 = _split3(a)
    return _dot(a1, b_exact) + (_dot(a2, b_exact) + _dot(a3, b_exact))


def _sigmoid(x):
    return 1.0 / (1.0 + jnp.exp(-x))


def _silu(x):
    return x * _sigmoid(x)


def _layer_norm(x, g, b):
    mu = jnp.mean(x, axis=-1, keepdims=True)
    xc = x - mu
    var = jnp.mean(xc * xc, axis=-1, keepdims=True)
    return xc * lax.rsqrt(var + EPS) * g + b


def _group_avg_matrix(width, group):
    sh = int(math.log2(group))
    r = lax.shift_right_logical(lax.broadcasted_iota(jnp.int32, (width, width), 0), sh)
    c = lax.shift_right_logical(lax.broadcasted_iota(jnp.int32, (width, width), 1), sh)
    return jnp.where(r == c, 1.0 / group, 0.0).astype(BF16)


def _group_mean_sq(x, gmat):
    xx = x * x
    hi = xx.astype(BF16)
    lo = (xx - hi.astype(F32)).astype(BF16)
    return _dot(hi, gmat) + _dot(lo, gmat)


def _rope(x, cos, sin_signed):
    w = x.shape[-1]
    lane = lax.broadcasted_iota(jnp.int32, x.shape, 1)
    nxt = pltpu.roll(x, w - 1, 1)
    prv = pltpu.roll(x, 1, 1)
    partner = jnp.where((lane & 1) == 0, nxt, prv)
    return x * cos + partner * sin_signed


def _lane_mask(shape, lo, hi):
    lane = lax.broadcasted_iota(jnp.int32, shape, 1)
    return (lane >= lo) & (lane < hi)


def _mod_kernel(c_ref, w_ref, b_ref, o_ref):
    a = _silu(c_ref[...])
    a_hi = a.astype(BF16)
    a_lo = (a - a_hi.astype(F32)).astype(BF16)
    w = w_ref[...]
    w_hi = w.astype(BF16)
    w_lo = (w - w_hi.astype(F32)).astype(BF16)
    o_ref[...] = _dot(a_hi, w_hi) + (_dot(a_lo, w_hi) + _dot(a_hi, w_lo)) + b_ref[...]


def _modulation(cvec, w_ada, b_ada):
    depth, d, n = w_ada.shape
    tn = 1536
    rows = cvec.shape[0]
    return pl.pallas_call(
        _mod_kernel,
        grid=(depth, n // tn),
        in_specs=[
            pl.BlockSpec((rows, d), lambda l, j: (0, 0)),
            pl.BlockSpec((None, d, tn), lambda l, j: (l, 0, j)),
            pl.BlockSpec((None, 1, tn), lambda l, j: (l, 0, j)),
        ],
        out_specs=pl.BlockSpec((None, rows, tn), lambda l, j: (l, 0, j)),
        out_shape=jax.ShapeDtypeStruct((depth, rows, n), F32),
        compiler_params=pltpu.CompilerParams(
            dimension_semantics=("arbitrary", "arbitrary"), vmem_limit_bytes=VMEM_LIMIT),
    )(cvec, w_ada, b_ada.reshape(depth, 1, n))


def _inproj_kernel(feature_major_cache, n_prev, *refs):
    x_ref, sh_ref, sc_ref, w_ref = refs[:4]
    outs = refs[4 + n_prev:]
    h = (x_ref[...] * (1.0 + sc_ref[...]) + sh_ref[...]).astype(BF16)

    def mm(lo, hi):
        return _dot(h, w_ref[:, lo:hi])

    if feature_major_cache:
        (qa_ref, dk_ref, gk_ref, z_ref, xbc_ref, dt_ref, dkt_ref, dvt_ref, gvt_ref,
         dv_s, gv_s) = outs
        dk_ref[...] = mm(_C_DK, _C_DV)
        dv_s[...] = mm(_C_DV, _C_GQ)
        gv_s[...] = mm(_C_GV, _C_Z)
        seq = dkt_ref.shape[-1]
        for j in range(dkt_ref.shape[0]):
            rows = slice(j * seq, (j + 1) * seq)
            dkt_ref[j] = dk_ref[rows, :].T
            dvt_ref[j] = dv_s[rows, :].T
            gvt_ref[j] = gv_s[rows, :].T
    else:
        qa_ref, dk_ref, dv_ref, gk_ref, gv_ref, z_ref, xbc_ref, dt_ref = outs
        dk_ref[...] = mm(_C_DK, _C_DV)
        dv_ref[...] = mm(_C_DV, _C_GQ)
        gv_ref[...] = mm(_C_GV, _C_Z)
    qa_ref[:, 0:DIFF_W] = mm(_C_DQ, _C_DK)
    qa_ref[:, DIFF_W:DIFF_W + GQA_W] = mm(_C_GQ, _C_GK)
    gk_ref[...] = mm(_C_GK, _C_GV)
    z_ref[...] = mm(_C_Z, _C_XBC)
    xbc_ref[...] = mm(_C_XBC, _C_DT)
    dt_ref[...] = mm(_C_DT, IN_W_PAD)


def _mod_spec(layer, which, row_fn, d):
    return pl.BlockSpec((None, None, 1, d), lambda i: (layer, row_fn(i), 0, which))


def _inproj(x, mods, layer, row_fn, w_in_bf, tm, cache_prev=None, depth=None, seq=None):
    t, d = x.shape
    feature_major = cache_prev is not None
    token_spec = lambda w: pl.BlockSpec((tm, w), lambda i: (i, 0))
    token_shape = lambda w: jax.ShapeDtypeStruct((t, w), F32)
    if feature_major:
        widths = (DIFF_W + GQA_W, DIFF_W, GQA_KV_W, SSD_INNER, XBC_W, LANES)
        cache_w = (DIFF_W, DIFF_W, GQA_KV_W)
        assert tm % seq == 0
        out_specs = [token_spec(w) for w in widths] + [
            pl.BlockSpec((tm // seq, None, w, seq), lambda i: (i, layer, 0, 0)) for w in cache_w]
        out_shape = [token_shape(w) for w in widths] + [
            jax.ShapeDtypeStruct((t // seq, depth, w, seq), F32) for w in cache_w]
        aliases = {4 + k: len(widths) + k for k in range(len(cache_prev))}
    else:
        widths = (DIFF_W + GQA_W, DIFF_W, DIFF_W, GQA_KV_W, GQA_KV_W, SSD_INNER, XBC_W, LANES)
        out_specs = [token_spec(w) for w in widths]
        out_shape = [token_shape(w) for w in widths]
        cache_prev, aliases = (), {}
    return pl.pallas_call(
        functools.partial(_inproj_kernel, feature_major, len(cache_prev)),
        grid=(t // tm,),
        in_specs=[
            pl.BlockSpec((tm, d), lambda i: (i, 0)),
            _mod_spec(layer, 0, row_fn, d),
            _mod_spec(layer, 1, row_fn, d),
            pl.BlockSpec((None, d, IN_W_PAD), lambda i: (layer, 0, 0)),
        ] + [pl.BlockSpec(memory_space=pl.ANY)] * len(cache_prev),
        out_specs=out_specs,
        out_shape=out_shape,
        input_output_aliases=aliases,
        scratch_shapes=([pltpu.VMEM((tm, DIFF_W), F32), pltpu.VMEM((tm, GQA_KV_W), F32)]
                        if feature_major else []),
        compiler_params=pltpu.CompilerParams(
            dimension_semantics=("arbitrary",), vmem_limit_bytes=VMEM_LIMIT),
    )(x, mods, mods, w_in_bf, *cache_prev)


LOG2E = 1.4426950408889634
NEG_BIG = -1e30
N_SCORE_HEADS = 2 * DIFF_HEADS + GQA_Q_HEADS


def _diff_lambda(lam_ref, lam_init):
    lp = lam_ref[...]
    s1 = jnp.sum(lp[0:1, :] * lp[1:2, :], axis=-1, keepdims=True)
    s2 = jnp.sum(lp[2:3, :] * lp[3:4, :], axis=-1, keepdims=True)
    return jnp.exp(s1) - jnp.exp(s2) + lam_init


def _swap_halves(x):
    return pltpu.roll(x, GQA_HD, 1)


def _attn_kernel(cfg, *refs):
    lam_init, seq, past, kb, rope, feature_major, n_prev = cfg
    it = iter(refs)
    qa_ref, dk_ref, dv_ref, gk_in_ref, gv_ref = next(it), next(it), next(it), next(it), next(it)
    for _ in range(n_prev):
        next(it)
    if past:
        cdk_ref, cdv_ref, cgk_ref, cgv_ref = next(it), next(it), next(it), next(it)
    if rope:
        cosq_ref, sinq_ref, cosk_ref, sink_ref = next(it), next(it), next(it), next(it)
    lam_ref, gain_ref, gq_ref, gk_ref = next(it), next(it), next(it), next(it)
    o_ref = next(it)
    gkn_ref = next(it) if feature_major else None
    kd_s, kg_s, vdt_s, vgt_s, wq_s, wg_s, s0_s, s1_s, p_s, m_s, l_s, alpha_s, acc_s = it
    tq = qa_ref.shape[0]
    n_blocks = (seq + past) // kb

    @pl.when(pl.program_id(1) == 0)
    def _prepare_keys():
        gmat = _group_avg_matrix(GQA_KV_W, GQA_HD)
        for i in range(seq // kb):
            rows = slice(i * kb, (i + 1) * kb)
            dk = dk_ref[rows, :]
            gk = gk_in_ref[rows, :]
            gk = gk * lax.rsqrt(_group_mean_sq(gk, gmat) + EPS) * gk_ref[...]
            if feature_major:
                gkn_ref[:, rows] = gk.T
            if rope:
                ck = cosk_ref[rows, :]
                sk = sink_ref[rows, :]
                dk = _rope(dk, ck[:, 0:DIFF_W], sk[:, 0:DIFF_W])
                gk = _rope(gk, ck[:, DIFF_W:DIFF_W + GQA_KV_W], sk[:, DIFF_W:DIFF_W + GQA_KV_W])
            kd_s[rows, :] = dk.astype(BF16)
            kg_s[rows, :] = gk.astype(BF16)
            if feature_major:
                vdt_s[i] = dv_ref[:, rows].astype(BF16)
                vgt_s[i] = gv_ref[:, rows].astype(BF16)
            else:
                vdt_s[i] = dv_ref[rows, :].T.astype(BF16)
                vgt_s[i] = gv_ref[rows, :].T.astype(BF16)
        for j in range(past // kb):
            src = slice(j * kb, (j + 1) * kb)
            dst = slice(seq + j * kb, seq + (j + 1) * kb)
            kd_s[dst, :] = cdk_ref[:, src].T.astype(BF16)
            kg_s[dst, :] = cgk_ref[:, src].T.astype(BF16)
            vdt_s[seq // kb + j] = cdv_ref[:, src].astype(BF16)
            vgt_s[seq // kb + j] = cgv_ref[:, src].astype(BF16)

    qa = qa_ref[...]
    qd = qa[:, 0:DIFF_W]
    gq = qa[:, DIFF_W:DIFF_W + GQA_W]
    gq = gq * lax.rsqrt(_group_mean_sq(gq, _group_avg_matrix(GQA_W, GQA_HD)) + EPS) * gq_ref[...]
    if rope:
        cq = cosq_ref[...]
        sq = sinq_ref[...]
        qd = _rope(qd, cq[:, 0:DIFF_W], sq[:, 0:DIFF_W])
        gq = _rope(gq, cq[:, DIFF_W:DIFF_W + GQA_W], sq[:, DIFF_W:DIFF_W + GQA_W])
    qd_t = (qd * (DIFF_QK ** -0.5 * LOG2E)).T
    gq_t = (gq * (GQA_HD ** -0.5 * LOG2E)).T
    row = lax.broadcasted_iota(jnp.int32, (DIFF_W, tq), 0)
    for hm in range(2 * DIFF_HEADS):
        lo = hm * DIFF_QK
        wq_s[:, hm * tq:(hm + 1) * tq] = jnp.where((row >= lo) & (row < lo + DIFF_QK), qd_t, 0.0).astype(BF16)
    zeros = jnp.zeros((GQA_HD, tq), F32)
    for h in range(GQA_Q_HEADS):
        piece = gq_t[h * GQA_HD:(h + 1) * GQA_HD, :]
        pair = [piece, zeros] if h // (GQA_Q_HEADS // GQA_KV_HEADS) == 0 else [zeros, piece]
        wg_s[:, h * tq:(h + 1) * tq] = jnp.concatenate(pair, axis=0).astype(BF16)
    m_s[...] = jnp.full(m_s.shape, NEG_BIG, F32)
    l_s[...] = jnp.zeros(l_s.shape, F32)
    acc_s[...] = jnp.zeros(acc_s.shape, F32)
    n_diff = 2 * DIFF_HEADS * tq
    n_all = N_SCORE_HEADS * tq

    def scores(j, s_buf):
        rows = pl.ds(j * kb if isinstance(j, int) else pl.multiple_of(j * kb, kb), kb)
        sd = _dot(kd_s[rows, :], wq_s[...])
        for c in range(n_diff // LANES):
            s_buf[c] = sd[:, c * LANES:(c + 1) * LANES]
        sg = _dot(kg_s[rows, :], wg_s[...])
        for c in range((n_all - n_diff) // LANES):
            s_buf[n_diff // LANES + c] = sg[:, c * LANES:(c + 1) * LANES]

    def accumulate(j, s_buf):
        for c in range(n_all // LANES):
            cols = slice(c * LANES, (c + 1) * LANES)
            s = s_buf[c]
            m_old = m_s[:, cols]
            m_new = jnp.maximum(m_old, jnp.max(s, axis=0, keepdims=True))
            alpha = jnp.exp2(m_old - m_new)
            p = jnp.exp2(s - m_new)
            l_s[:, cols] = alpha * l_s[:, cols] + jnp.sum(p, axis=0, keepdims=True)
            m_s[:, cols] = m_new
            alpha_s[:, cols] = alpha
            p_s[c] = p.astype(BF16)
        v_d = vdt_s[j]
        v_g = vgt_s[j]
        slabs = tq // LANES
        for idx in range(N_SCORE_HEADS):
            cols = slice(idx * tq, (idx + 1) * tq)
            if idx < 2 * DIFF_HEADS:
                vh = idx // 2
                v_t = v_d[vh * DIFF_V:(vh + 1) * DIFF_V, :]
            else:
                vh = (idx - 2 * DIFF_HEADS) // (GQA_Q_HEADS // GQA_KV_HEADS)
                v_t = v_g[vh * GQA_HD:(vh + 1) * GQA_HD, :]
            p = jnp.concatenate([p_s[idx * slabs + k] for k in range(slabs)], axis=1)
            acc_s[idx] = alpha_s[:, cols] * acc_s[idx] + _dot(v_t, p)

    scores(0, s0_s)
    if n_blocks > 1:
        def block_pair(i, carry):
            scores(2 * i + 1, s1_s)
            accumulate(2 * i, s0_s)
            scores(2 * i + 2, s0_s)
            accumulate(2 * i + 1, s1_s)
            return carry

        lax.fori_loop(0, n_blocks // 2 - 1, block_pair, 0)
        scores(n_blocks - 1, s1_s)
        accumulate(n_blocks - 2, s0_s)
        accumulate(n_blocks - 1, s1_s)
    else:
        accumulate(0, s0_s)

    lam = _diff_lambda(lam_ref, lam_init)
    outs = []
    for h in range(DIFF_HEADS):
        c0 = slice(2 * h * tq, (2 * h + 1) * tq)
        c1 = slice((2 * h + 1) * tq, (2 * h + 2) * tq)
        o = acc_s[2 * h] * (1.0 / l_s[:, c0]) - acc_s[2 * h + 1] * (lam / l_s[:, c1])
        ms = jnp.mean(o * o, axis=0, keepdims=True)
        outs.append(o * (lax.rsqrt(ms + EPS) * (1.0 - lam_init)))
    for h in range(GQA_Q_HEADS):
        idx = 2 * DIFF_HEADS + h
        outs.append(acc_s[idx] * (1.0 / l_s[:, idx * tq:(idx + 1) * tq]))
    o_ref[...] = jnp.concatenate(outs, axis=0).T * gain_ref[...]


def _attention(qa, kv, caches, layer, tables, lam_p, gain, gq_t, gk_t, lam_init, batch, seq, tq,
               gkn_prev=None):
    t = qa.shape[0]
    nq = seq // tq
    feature_major = gkn_prev is not None
    past = caches[0].shape[3] if caches is not None else 0
    kb = min(256, seq)
    assert seq % kb == 0 and past % kb == 0 and seq % tq == 0
    n_blocks = (seq + past) // kb
    assert n_blocks == 1 or n_blocks % 2 == 0
    n_all = N_SCORE_HEADS * tq
    full = lambda a: pl.BlockSpec(a.shape, lambda b, q: (0,) * a.ndim)
    token_major = lambda a: pl.BlockSpec((seq, a.shape[1]), lambda b, q: (b, 0))
    by_feature = lambda a: pl.BlockSpec((None, None, a.shape[2], seq), lambda b, q: (b, layer, 0, 0))
    dk, dv, gk, gv = kv
    in_specs = [pl.BlockSpec((tq, qa.shape[1]), lambda b, q: (b * nq + q, 0)),
                token_major(dk), by_feature(dv) if feature_major else token_major(dv),
                token_major(gk), by_feature(gv) if feature_major else token_major(gv)]
    args = [qa, dk, dv, gk, gv]
    aliases = {}
    if feature_major:
        assert nq == 1
        in_specs += [pl.BlockSpec(memory_space=pl.ANY)] * len(gkn_prev)
        args += list(gkn_prev)
        aliases = {5 + k: 1 + k for k in range(len(gkn_prev))}
    if caches is not None:
        in_specs += [pl.BlockSpec((None, None, a.shape[2], past), lambda b, q: (b, layer, 0, 0))
                     for a in caches]
        args += list(caches)
    if tables is not None:
        cosq, sinq, cosk, sink = tables
        in_specs += [pl.BlockSpec((tq, cosq.shape[1]), lambda b, q: (q, 0)),
                     pl.BlockSpec((tq, sinq.shape[1]), lambda b, q: (q, 0)),
                     full(cosk), full(sink)]
        args += [cosq, sinq, cosk, sink]
    in_specs += [full(lam_p), full(gain), full(gq_t), full(gk_t)]
    args += [lam_p, gain, gq_t, gk_t]
    out_specs = [pl.BlockSpec((tq, DIFF_W + GQA_W), lambda b, q: (b * nq + q, 0))]
    out_shape = [jax.ShapeDtypeStruct((t, DIFF_W + GQA_W), F32)]
    if feature_major:
        out_specs.append(pl.BlockSpec((None, None, GQA_KV_W, seq), lambda b, q: (b, layer, 0, 0)))
        out_shape.append(jax.ShapeDtypeStruct((batch, dv.shape[1], GQA_KV_W, seq), F32))
    cfg = (lam_init, seq, past, kb, tables is not None, feature_major,
           len(gkn_prev) if feature_major else 0)
    return pl.pallas_call(
        functools.partial(_attn_kernel, cfg),
        grid=(batch, nq),
        in_specs=in_specs,
        out_specs=out_specs,
        out_shape=out_shape,
        input_output_aliases=aliases,
        scratch_shapes=[
            pltpu.VMEM((seq + past, DIFF_W), BF16), pltpu.VMEM((seq + past, GQA_KV_W), BF16),
            pltpu.VMEM((n_blocks, DIFF_W, kb), BF16), pltpu.VMEM((n_blocks, GQA_KV_W, kb), BF16),
            pltpu.VMEM((DIFF_W, 2 * DIFF_HEADS * tq), BF16), pltpu.VMEM((GQA_KV_W, GQA_Q_HEADS * tq), BF16),
            pltpu.VMEM((n_all // LANES, kb, LANES), F32), pltpu.VMEM((n_all // LANES, kb, LANES), F32),
            pltpu.VMEM((n_all // LANES, kb, LANES), BF16),
            pltpu.VMEM((1, n_all), F32), pltpu.VMEM((1, n_all), F32), pltpu.VMEM((1, n_all), F32),
            pltpu.VMEM((N_SCORE_HEADS, GQA_HD, tq), F32),
        ],
        compiler_params=pltpu.CompilerParams(
            dimension_semantics=("arbitrary", "arbitrary"), vmem_limit_bytes=VMEM_LIMIT),
    )(*args)


def _softplus(x):
    return jnp.maximum(x, 0.0) + jnp.log1p(jnp.exp(-jnp.abs(x)))


def _dot2_l(a, b_exact):
    a1 = a.astype(BF16)
    a2 = (a - a1.astype(F32)).astype(BF16)
    return _dot(a1, b_exact) + _dot(a2, b_exact)


def _ssd_kernel(has_init, n_prev, seq, *refs):
    it = iter(refs)
    z_ref, xbc_ref, dt_ref, cw_ref, cb_ref, alog_ref, dtb_ref, dexp_ref, ng_ref = (
        next(it) for _ in range(9))
    sf0_ref, sb0_ref = (next(it), next(it)) if has_init else (None, None)
    for _ in range(n_prev):
        next(it)
    o_ref, sf_ref, sb_ref = next(it), next(it), next(it)
    stf_s, stb_s, inc_s, dec_s, eab_s, cbf_s, exp_s, tri_s, gm_s = it
    L = SSD_CHUNK
    W2 = 2 * SSD_INNER
    nchunks = seq // L
    halo = 8

    ri = lax.broadcasted_iota(jnp.int32, (L, L), 0)
    ci = lax.broadcasted_iota(jnp.int32, (L, L), 1)
    lower = ri >= ci
    upper = ri <= ci
    tri_s[0] = jnp.where(lower, 1.0, 0.0).astype(BF16)
    tri_s[1] = jnp.where(upper, 1.0, 0.0).astype(BF16)
    lane_row = lax.broadcasted_iota(jnp.int32, (1, LANES), 1)
    a_row = jnp.where(lane_row < DT_W, -jnp.exp(alog_ref[...]), 0.0)
    ej = lax.broadcasted_iota(jnp.int32, (LANES, W2), 0)
    eh = lax.shift_right_logical(lax.broadcasted_iota(jnp.int32, (LANES, W2), 1), 6)
    exp_s[...] = jnp.where(ej == eh, 1.0, 0.0).astype(BF16)
    gm_s[...] = jnp.where(lax.shift_right_logical(ej, 6) == (lax.shift_right_logical(eh, 2) & 1), 1.0, 0.0)

    def chunk_rows(c):
        return pl.ds(pl.multiple_of(c * L, L), L)

    def load_state(ref):
        r = lax.broadcasted_iota(jnp.int32, (SSD_STATE, LANES), 0)
        c = lax.broadcasted_iota(jnp.int32, (SSD_STATE, LANES), 1)
        dup = jnp.where((c & (SSD_STATE - 1)) == r, 1.0, 0.0).astype(BF16)
        return _dot3_l(ref[...], dup).T * gm_s[:, 0:SSD_INNER]

    def store_state(st_ref, ref):
        st_t = st_ref[...].T
        ref[...] = (st_t + _swap_halves(st_t))[:, 0:SSD_STATE]

    stf_s[...] = load_state(sf0_ref) if has_init else jnp.zeros(stf_s.shape, F32)
    stb_s[...] = load_state(sb0_ref) if has_init else jnp.zeros(stb_s.shape, F32)

    def forward_pass(c, carry):
        r0 = c * L
        rows = chunk_rows(c)
        prev = xbc_ref[pl.ds(pl.multiple_of(jnp.maximum(r0 - halo, 0), halo), halo), :]
        nxt = xbc_ref[pl.ds(pl.multiple_of(jnp.minimum(r0 + L, seq - halo), halo), halo), :]
        cur = xbc_ref[rows, :]
        win = jnp.concatenate([jnp.where(c > 0, prev, 0.0), cur,
                               jnp.where(c < nchunks - 1, nxt, 0.0)], axis=0)
        acc = cb_ref[...] + cur * cw_ref[SSD_CONV // 2:SSD_CONV // 2 + 1, :]
        for j in range(SSD_CONV):
            if j != SSD_CONV // 2:
                shifted = pltpu.roll(win, (SSD_CONV // 2 - j) % (L + 2 * halo), 0)[halo:halo + L, :]
                acc = acc + shifted * cw_ref[j:j + 1, :]
        act = _silu(acc)
        x_c = act[:, 0:SSD_INNER]
        b_c = act[:, SSD_INNER:SSD_INNER + SSD_BC_W]
        c_c = act[:, SSD_INNER + SSD_BC_W:XBC_W]
        dt_c = _softplus(dt_ref[rows, :] + dtb_ref[...])

        d1, d2, d3 = _split3(dt_c * a_row)
        acs_f = _dot(tri_s[0], d1) + (_dot(tri_s[0], d2) + _dot(tri_s[0], d3))
        acs_b = _dot(tri_s[1], d1) + (_dot(tri_s[1], d2) + _dot(tri_s[1], d3))
        acs = jnp.where(lax.broadcasted_iota(jnp.int32, (L, LANES), 1) < SSD_HEADS, acs_f, acs_b)
        acs_t = acs.T
        expand = exp_s[...]
        dt_e = _dot2_l(dt_c, expand)
        acs_e = _dot3_l(acs, expand)
        edge = jnp.concatenate([acs_e[L - 1:L, 0:SSD_INNER], acs_e[0:1, SSD_INNER:W2]], axis=1)
        eacs = jnp.exp(acs_e)
        cdec = jnp.exp(edge)
        xd = jnp.concatenate([x_c, x_c], axis=1) * dt_e
        xd_b = xd.astype(BF16)
        xdw = (xd * jnp.exp(edge - acs_e)).astype(BF16)
        b_b = b_c.astype(BF16)
        c_b = c_c.astype(BF16)
        s_new = _dot(b_c.T.astype(BF16), xdw) * gm_s[...]

        st_f = stf_s[...]
        y = _dot(c_b, st_f.astype(BF16)) * eacs[:, 0:SSD_INNER]
        stf_s[...] = st_f * cdec[:, 0:SSD_INNER] + s_new[:, 0:SSD_INNER]
        inc_s[c] = s_new[:, SSD_INNER:W2]
        dec_s[c] = cdec[:, SSD_INNER:W2]
        eab_s[rows, :] = eacs[:, SSD_INNER:W2]
        cbf_s[rows, :] = c_b

        for direction, causal in ((0, lower), (1, upper)):
            pairs = []
            for g in range(SSD_GROUPS):
                cg = jnp.where(_lane_mask(c_c.shape, g * SSD_STATE, (g + 1) * SSD_STATE), c_c, 0.0)
                cb = _dot_nt(cg.astype(BF16), b_b)
                for hp in range(2):
                    pair = g * 2 + hp
                    res = []
                    for k in range(2):
                        j = direction * SSD_HEADS + pair * 2 + k
                        diff = acs[:, j:j + 1] - acs_t[j:j + 1, :]
                        dec = jnp.where(causal, jnp.exp(jnp.minimum(diff, 0.0)), 0.0)
                        sc = (cb * dec).astype(BF16)
                        lo = direction * SSD_INNER + pair * LANES
                        res.append(_dot(sc, xd_b[:, lo:lo + LANES]))
                    pairs.append(jnp.where(_lane_mask(res[0].shape, 0, SSD_HD), res[0], res[1]))
            y = y + jnp.concatenate(pairs, axis=-1)
        o_ref[rows, :] = y + x_c * dexp_ref[...]
        return carry

    lax.fori_loop(0, nchunks, forward_pass, 0)
    store_state(stf_s, sf_ref)

    def backward_pass(i, carry):
        c = nchunks - 1 - i
        rows = chunk_rows(c)
        st_b = stb_s[...]
        y = o_ref[rows, :] + _dot(cbf_s[rows, :], st_b.astype(BF16)) * eab_s[rows, :]
        stb_s[...] = st_b * dec_s[c] + inc_s[c]
        yt = y * _silu(z_ref[rows, :])
        ms = jnp.mean(yt * yt, axis=-1, keepdims=True)
        o_ref[rows, :] = yt * lax.rsqrt(ms + EPS) * ng_ref[...]
        return carry

    lax.fori_loop(0, nchunks, backward_pass, 0)
    store_state(stb_s, sb_ref)


def _ssd(z, xbc, dt, params, init, layer, batch, seq, state_prev=(), depth=1):
    cw, cb, alog, dtb, dexp, ng = params
    t = z.shape[0]
    has_init = init is not None
    nchunks = seq // SSD_CHUNK
    full = lambda a: pl.BlockSpec(a.shape, lambda b: (0,) * a.ndim)
    out_layer = layer if depth > 1 else 0
    st_spec = pl.BlockSpec((None, None, SSD_INNER, SSD_STATE), lambda b: (b, out_layer, 0, 0))
    in_specs = [
        pl.BlockSpec((seq, SSD_INNER), lambda b: (b, 0)),
        pl.BlockSpec((seq, XBC_W), lambda b: (b, 0)),
        pl.BlockSpec((seq, LANES), lambda b: (b, 0)),
        full(cw), full(cb), full(alog), full(dtb), full(dexp), full(ng),
    ]
    args = [z, xbc, dt, cw, cb, alog, dtb, dexp, ng]
    if has_init:
        init_spec = pl.BlockSpec((None, None, SSD_INNER, SSD_STATE), lambda b: (b, layer, 0, 0))
        in_specs += [init_spec, init_spec]
        args += list(init)
    aliases = {len(args) + k: 1 + k for k in range(len(state_prev))}
    in_specs += [pl.BlockSpec(memory_space=pl.ANY)] * len(state_prev)
    args += list(state_prev)
    st_shape = jax.ShapeDtypeStruct((batch, depth, SSD_INNER, SSD_STATE), F32)
    return pl.pallas_call(
        functools.partial(_ssd_kernel, has_init, len(state_prev), seq),
        grid=(batch,),
        in_specs=in_specs,
        out_specs=[pl.BlockSpec((seq, SSD_INNER), lambda b: (b, 0)), st_spec, st_spec],
        out_shape=[jax.ShapeDtypeStruct((t, SSD_INNER), F32), st_shape, st_shape],
        input_output_aliases=aliases,
        scratch_shapes=[
            pltpu.VMEM((LANES, SSD_INNER), F32), pltpu.VMEM((LANES, SSD_INNER), F32),
            pltpu.VMEM((nchunks, LANES, SSD_INNER), F32),
            pltpu.VMEM((nchunks, 1, SSD_INNER), F32),
            pltpu.VMEM((seq, SSD_INNER), F32),
            pltpu.VMEM((seq, SSD_BC_W), BF16),
            pltpu.VMEM((LANES, 2 * SSD_INNER), BF16),
            pltpu.VMEM((2, SSD_CHUNK, SSD_CHUNK), BF16),
            pltpu.VMEM((LANES, 2 * SSD_INNER), F32),
        ],
        compiler_params=pltpu.CompilerParams(
            dimension_semantics=("arbitrary",), vmem_limit_bytes=VMEM_LIMIT),
    )(*args)


def _post_kernel(alpha, d_ff, ff_chunk, x_ref, oa_ref, os_ref, g1_ref, sh2_ref, sc2_ref, g2_ref,
                 wo_ref, wfi_ref, wfo_ref, lng_ref, lnb_ref, y_ref):
    x = x_ref[...]
    wa = oa_ref.shape[1]
    o = _dot(oa_ref[...].astype(BF16), wo_ref[0:wa, :]) + _dot(os_ref[...].astype(BF16), wo_ref[wa:, :])
    x1 = _layer_norm(alpha * x + g1_ref[...] * o, lng_ref[0:1, :], lnb_ref[0:1, :])
    h2 = (x1 * (1.0 + sc2_ref[...]) + sh2_ref[...]).astype(BF16)
    f = jnp.zeros(x.shape, F32)
    for c in range(d_ff // ff_chunk):
        lo = c * ff_chunk
        g = _dot(h2, wfi_ref[:, lo:lo + ff_chunk])
        u = _dot(h2, wfi_ref[:, d_ff + lo:d_ff + lo + ff_chunk])
        f = f + _dot((_silu(g) * u).astype(BF16), wfo_ref[lo:lo + ff_chunk, :])
    y_ref[...] = _layer_norm(alpha * x1 + g2_ref[...] * f, lng_ref[1:2, :], lnb_ref[1:2, :])


def _post(x, oa, os_, mods, layer, row_fn, wo_bf, wfi_bf, wfo_bf, lng, lnb, alpha, tm):
    t, d = x.shape
    d_ff = wfo_bf.shape[1]
    ff_chunk = d_ff // 2
    per_layer = lambda a, **kw: pl.BlockSpec((None,) + a.shape[1:], lambda i: (layer, 0, 0), **kw)
    resident = lambda a: per_layer(a, pipeline_mode=pl.Buffered(1))
    return pl.pallas_call(
        functools.partial(_post_kernel, alpha, d_ff, ff_chunk),
        grid=(t // tm,),
        in_specs=[
            pl.BlockSpec((tm, d), lambda i: (i, 0)),
            pl.BlockSpec((tm, oa.shape[1]), lambda i: (i, 0)),
            pl.BlockSpec((tm, os_.shape[1]), lambda i: (i, 0)),
            _mod_spec(layer, 2, row_fn, d),
            _mod_spec(layer, 3, row_fn, d),
            _mod_spec(layer, 4, row_fn, d),
            _mod_spec(layer, 5, row_fn, d),
            resident(wo_bf), resident(wfi_bf), resident(wfo_bf),
            per_layer(lng), per_layer(lnb),
        ],
        out_specs=pl.BlockSpec((tm, d), lambda i: (i, 0)),
        out_shape=jax.ShapeDtypeStruct((t, d), F32),
        compiler_params=pltpu.CompilerParams(
            dimension_semantics=("arbitrary",), vmem_limit_bytes=VMEM_LIMIT),
    )(x, oa, os_, mods, mods, mods, mods, wo_bf, wfi_bf, wfo_bf, lng, lnb)


def _rope_tables(rows, dim, copies):
    row = jnp.repeat(jnp.arange(rows), GRID_W).astype(F32)
    col = jnp.tile(jnp.arange(GRID_W), rows).astype(F32)
    n_freq = dim // 4
    inv = ROPE_THETA ** (-jnp.arange(n_freq, dtype=F32) / n_freq)
    ang = jnp.concatenate([row[:, None] * inv, col[:, None] * inv], -1)
    cos, sin = jnp.cos(ang), jnp.sin(ang)
    cos_full = jnp.repeat(cos, 2, axis=-1)
    sin_signed = jnp.stack([-sin, sin], axis=-1).reshape(sin.shape[0], dim)
    return jnp.tile(cos_full, (1, copies)), jnp.tile(sin_signed, (1, copies))


def kernel(x_prompt, x_sample, cache_diff_k, cache_diff_v, cache_gqa_k, cache_gqa_v, state_ssd_fwd, state_ssd_bwd, c, c_ctx, w_ada, b_ada, w_in, w_out, diff_lambda, diff_subln_g, qk_norm_g, ssd_conv_w, ssd_conv_b, ssd_A_log, ssd_dt_bias, ssd_D, ssd_norm_g, ln_g, ln_b, w_ffn_in, w_ffn_out):
    batch, seq, d = x_prompt.shape
    dec_batch, dec_seq, _ = x_sample.shape
    depth = w_in.shape[0]
    past = cache_diff_k.shape[2]
    alpha = (2 * depth) ** 0.25
    rows = dec_seq // GRID_W

    n_vec = 1 + dec_batch
    n_pad = -(-n_vec // 8) * 8
    cvec = jnp.concatenate([c_ctx[None, :], c, jnp.zeros((n_pad - n_vec, d), F32)], axis=0)
    mods = _modulation(cvec, w_ada, b_ada).reshape(depth, n_pad, 1, 6 * d)

    cos_d, sin_d = _rope_tables(rows, DIFF_QK, DIFF_W // DIFF_QK)
    cos_g, sin_g = _rope_tables(rows, GQA_HD, GQA_Q_HEADS)
    cosq = jnp.concatenate([cos_d, cos_g], axis=-1)
    sinq = jnp.concatenate([sin_d, sin_g], axis=-1)
    cosk = jnp.concatenate([cos_d, cos_g[:, :GQA_KV_W]], axis=-1)
    sink = jnp.concatenate([sin_d, sin_g[:, :GQA_KV_W]], axis=-1)
    tables = (cosq, sinq, cosk, sink)

    feature_major = lambda a, w: jnp.transpose(a.reshape(dec_batch, depth, past, w), (0, 1, 3, 2))
    caches = (feature_major(cache_diff_k, DIFF_W), feature_major(cache_diff_v, DIFF_W),
              feature_major(cache_gqa_k, GQA_KV_W), feature_major(cache_gqa_v, GQA_KV_W))

    xp = x_prompt.reshape(batch * seq, d)
    xs = x_sample.reshape(dec_batch * dec_seq, d)
    tm_ctx = min(512, batch * seq)
    tm_lat = min(512, dec_seq)
    tq = min(256, dec_seq)
    ctx_row = lambda i: 0
    lat_row = lambda i: 1 + (i * tm_lat) // dec_seq

    w_in_bf = jnp.pad(w_in, ((0, 0), (0, 0), (0, IN_W_PAD - w_in.shape[2]))).astype(BF16)
    wo_bf = w_out.astype(BF16)
    wfi_bf = w_ffn_in.astype(BF16)
    wfo_bf = w_ffn_out.astype(BF16)
    init = (state_ssd_fwd.reshape(dec_batch, depth, SSD_INNER, SSD_STATE),
            state_ssd_bwd.reshape(dec_batch, depth, SSD_INNER, SSD_STATE))

    kv_cache = ()
    gk_cache = ()
    ssd_states = ()
    for l in range(depth):
        lam_init = 0.8 - 0.6 * math.exp(-0.3 * l)
        lam_p = diff_lambda[l]
        attn_gain = jnp.concatenate([jnp.tile(diff_subln_g[l], DIFF_HEADS), jnp.ones((GQA_W,), F32)])[None, :]
        gq_t = jnp.tile(qk_norm_g[l, 0], GQA_Q_HEADS)[None, :]
        gk_t = jnp.tile(qk_norm_g[l, 1], GQA_KV_HEADS)[None, :]
        pad_row = lambda v: jnp.pad(v.reshape(1, -1), ((0, 0), (0, LANES - v.size)))
        ssd_params = (ssd_conv_w[l], ssd_conv_b[l][None, :], pad_row(ssd_A_log[l]),
                      pad_row(ssd_dt_bias[l]), jnp.repeat(ssd_D[l], SSD_HD)[None, :],
                      ssd_norm_g[l][None, :])

        qa, dk, gk, z, xbc, dt, *kv_cache = _inproj(xp, mods, l, ctx_row, w_in_bf, tm_ctx,
                                                    cache_prev=tuple(kv_cache), depth=depth, seq=seq)
        oa, gkn = _attention(qa, (dk, kv_cache[1], gk, kv_cache[2]), None, l, None, lam_p, attn_gain,
                             gq_t, gk_t, lam_init, batch, seq, seq, gkn_prev=gk_cache)
        gk_cache = (gkn,)
        os_, *ssd_states = _ssd(z, xbc, dt, ssd_params, None, l, batch, seq,
                                state_prev=tuple(ssd_states), depth=depth)
        xp = _post(xp, oa, os_, mods, l, ctx_row, wo_bf, wfi_bf, wfo_bf, ln_g, ln_b, alpha, tm_ctx)

        qa, dk, dv, gk, gv, z, xbc, dt = _inproj(xs, mods, l, lat_row, w_in_bf, tm_lat)
        (oa,) = _attention(qa, (dk, dv, gk, gv), caches, l, tables, lam_p, attn_gain, gq_t, gk_t,
                           lam_init, dec_batch, dec_seq, tq)
        os_, _, _ = _ssd(z, xbc, dt, ssd_params, init, l, dec_batch, dec_seq)
        xs = _post(xs, oa, os_, mods, l, lat_row, wo_bf, wfi_bf, wfo_bf, ln_g, ln_b, alpha, tm_lat)

    def token_major(a, heads, width):
        return jnp.transpose(a.reshape(batch, depth, heads, width, seq), (0, 1, 4, 2, 3))

    state = lambda a: a.reshape(batch, depth, SSD_HEADS, SSD_HD, SSD_STATE)
    return (xp.reshape(batch, seq, d), xs.reshape(dec_batch, dec_seq, d),
            token_major(kv_cache[0], DIFF_HEADS, 2 * DIFF_QK), token_major(kv_cache[1], DIFF_HEADS, DIFF_V),
            token_major(gk_cache[0], GQA_KV_HEADS, GQA_HD), token_major(kv_cache[2], GQA_KV_HEADS, GQA_HD),
            state(ssd_states[0]), state(ssd_states[1]))
```

```python
import functools
import math

import jax
import jax.numpy as jnp
from jax import lax
from jax.experimental import pallas as pl
from jax.experimental.pallas import tpu as pltpu

F32 = jnp.float32
BF16 = jnp.bfloat16

GRID_W = 64
DIFF_HEADS = 4
DIFF_QK = 32
DIFF_V = 64
DIFF_W = DIFF_HEADS * DIFF_V
GQA_HD = 64
GQA_Q_HEADS = 4
GQA_KV_HEADS = 2
GQA_W = GQA_Q_HEADS * GQA_HD
GQA_KV_W = GQA_KV_HEADS * GQA_HD
SSD_HD = 64
SSD_HEADS = 8
SSD_INNER = SSD_HEADS * SSD_HD
SSD_GROUPS = 2
SSD_STATE = 64
SSD_BC_W = SSD_GROUPS * SSD_STATE
SSD_CONV = 5
SSD_CHUNK = 128
XBC_W = SSD_INNER + 2 * SSD_BC_W
DT_W = 2 * SSD_HEADS
ROPE_THETA = 10000.0
EPS = 1e-5
LANES = 128
VMEM_LIMIT = 56 * 1024 * 1024

_C_DQ, _C_DK, _C_DV, _C_GQ, _C_GK, _C_GV, _C_Z, _C_XBC, _C_DT, _C_END = (
    0, 256, 512, 768, 1024, 1152, 1280, 1792, 2560, 2576)
IN_W_PAD = _C_DT + LANES


def _dot(a, b):
    return jnp.dot(a, b, preferred_element_type=F32)


def _dot_nt(a, b):
    return lax.dot_general(a, b, (((1,), (1,)), ((), ())), preferred_element_type=F32)


def _split3(a):
    a1 = a.astype(BF16)
    r1 = a - a1.astype(F32)
    a2 = r1.astype(BF16)
    a3 = (r1 - a2.astype(F32)).astype(BF16)
    return a1, a2, a3


def _dot3_l(a, b_exact):
    a1, a2, a3 = _split3(a)
    return _dot(a1, b_exact) + (_dot(a2, b_exact) + _dot(a3, b_exact))


def _sigmoid(x):
    return 1.0 / (1.0 + jnp.exp(-x))


def _silu(x):
    return x * _sigmoid(x)


def _layer_norm(x, g, b):
    mu = jnp.mean(x, axis=-1, keepdims=True)
    xc = x - mu
    var = jnp.mean(xc * xc, axis=-1, keepdims=True)
    return xc * lax.rsqrt(var + EPS) * g + b


def _group_avg_matrix(width, group):
    sh = int(math.log2(group))
    r = lax.shift_right_logical(lax.broadcasted_iota(jnp.int32, (width, width), 0), sh)
    c = lax.shift_right_logical(lax.broadcasted_iota(jnp.int32, (width, width), 1), sh)
    return jnp.where(r == c, 1.0 / group, 0.0).astype(BF16)


def _group_mean_sq(x, gmat):
    xx = x * x
    hi = xx.astype(BF16)
    lo = (xx - hi.astype(F32)).astype(BF16)
    return _dot(hi, gmat) + _dot(lo, gmat)


def _rope(x, cos, sin_signed):
    w = x.shape[-1]
    lane = lax.broadcasted_iota(jnp.int32, x.shape, 1)
    nxt = pltpu.roll(x, w - 1, 1)
    prv = pltpu.roll(x, 1, 1)
    partner = jnp.where((lane & 1) == 0, nxt, prv)
    return x * cos + partner * sin_signed


def _lane_mask(shape, lo, hi):
    lane = lax.broadcasted_iota(jnp.int32, shape, 1)
    return (lane >= lo) & (lane < hi)


def _mod_kernel(c_ref, w_ref, b_ref, o_ref):
    a = _silu(c_ref[...])
    a_hi = a.astype(BF16)
    a_lo = (a - a_hi.astype(F32)).astype(BF16)
    w = w_ref[...]
    w_hi = w.astype(BF16)
    w_lo = (w - w_hi.astype(F32)).astype(BF16)
    o_ref[...] = _dot(a_hi, w_hi) + (_dot(a_lo, w_hi) + _dot(a_hi, w_lo)) + b_ref[...]


def _modulation(cvec, w_ada, b_ada):
    depth, d, n = w_ada.shape
    tn = 1536
    rows = cvec.shape[0]
    return pl.pallas_call(
        _mod_kernel,
        grid=(depth, n // tn),
        in_specs=[
            pl.BlockSpec((rows, d), lambda l, j: (0, 0)),
            pl.BlockSpec((None, d, tn), lambda l, j: (l, 0, j)),
            pl.BlockSpec((None, 1, tn), lambda l, j: (l, 0, j)),
        ],
        out_specs=pl.BlockSpec((None, rows, tn), lambda l, j: (l, 0, j)),
        out_shape=jax.ShapeDtypeStruct((depth, rows, n), F32),
        compiler_params=pltpu.CompilerParams(
            dimension_semantics=("arbitrary", "arbitrary"), vmem_limit_bytes=VMEM_LIMIT),
    )(cvec, w_ada, b_ada.reshape(depth, 1, n))


def _inproj_kernel(feature_major_cache, n_prev, *refs):
    x_ref, sh_ref, sc_ref, w_ref = refs[:4]
    outs = refs[4 + n_prev:]
    h = (x_ref[...] * (1.0 + sc_ref[...]) + sh_ref[...]).astype(BF16)

    def mm(lo, hi):
        return _dot(h, w_ref[:, lo:hi])

    if feature_major_cache:
        (qa_ref, dk_ref, gk_ref, z_ref, xbc_ref, dt_ref, dkt_ref, dvt_ref, gvt_ref,
         dv_s, gv_s) = outs
        dk_ref[...] = mm(_C_DK, _C_DV)
        dv_s[...] = mm(_C_DV, _C_GQ)
        gv_s[...] = mm(_C_GV, _C_Z)
        seq = dkt_ref.shape[-1]
        for j in range(dkt_ref.shape[0]):
            rows = slice(j * seq, (j + 1) * seq)
            dkt_ref[j] = dk_ref[rows, :].T
            dvt_ref[j] = dv_s[rows, :].T
            gvt_ref[j] = gv_s[rows, :].T
    else:
        qa_ref, dk_ref, dv_ref, gk_ref, gv_ref, z_ref, xbc_ref, dt_ref = outs
        dk_ref[...] = mm(_C_DK, _C_DV)
        dv_ref[...] = mm(_C_DV, _C_GQ)
        gv_ref[...] = mm(_C_GV, _C_Z)
    qa_ref[:, 0:DIFF_W] = mm(_C_DQ, _C_DK)
    qa_ref[:, DIFF_W:DIFF_W + GQA_W] = mm(_C_GQ, _C_GK)
    gk_ref[...] = mm(_C_GK, _C_GV)
    z_ref[...] = mm(_C_Z, _C_XBC)
    xbc_ref[...] = mm(_C_XBC, _C_DT)
    dt_ref[...] = mm(_C_DT, IN_W_PAD)


def _mod_spec(layer, which, row_fn, d):
    return pl.BlockSpec((None, None, 1, d), lambda i: (layer, row_fn(i), 0, which))


def _inproj(x, mods, layer, row_fn, w_in_bf, tm, cache_prev=None, depth=None, seq=None):
    t, d = x.shape
    feature_major = cache_prev is not None
    token_spec = lambda w: pl.BlockSpec((tm, w), lambda i: (i, 0))
    token_shape = lambda w: jax.ShapeDtypeStruct((t, w), F32)
    if feature_major:
        widths = (DIFF_W + GQA_W, DIFF_W, GQA_KV_W, SSD_INNER, XBC_W, LANES)
        cache_w = (DIFF_W, DIFF_W, GQA_KV_W)
        assert tm % seq == 0
        out_specs = [token_spec(w) for w in widths] + [
            pl.BlockSpec((tm // seq, None, w, seq), lambda i: (i, layer, 0, 0)) for w in cache_w]
        out_shape = [token_shape(w) for w in widths] + [
            jax.ShapeDtypeStruct((t // seq, depth, w, seq), F32) for w in cache_w]
        aliases = {4 + k: len(widths) + k for k in range(len(cache_prev))}
    else:
        widths = (DIFF_W + GQA_W, DIFF_W, DIFF_W, GQA_KV_W, GQA_KV_W, SSD_INNER, XBC_W, LANES)
        out_specs = [token_spec(w) for w in widths]
        out_shape = [token_shape(w) for w in widths]
        cache_prev, aliases = (), {}
    return pl.pallas_call(
        functools.partial(_inproj_kernel, feature_major, len(cache_prev)),
        grid=(t // tm,),
        in_specs=[
            pl.BlockSpec((tm, d), lambda i: (i, 0)),
            _mod_spec(layer, 0, row_fn, d),
            _mod_spec(layer, 1, row_fn, d),
            pl.BlockSpec((None, d, IN_W_PAD), lambda i: (layer, 0, 0)),
        ] + [pl.BlockSpec(memory_space=pl.ANY)] * len(cache_prev),
        out_specs=out_specs,
        out_shape=out_shape,
        input_output_aliases=aliases,
        scratch_shapes=([pltpu.VMEM((tm, DIFF_W), F32), pltpu.VMEM((tm, GQA_KV_W), F32)]
                        if feature_major else []),
        compiler_params=pltpu.CompilerParams(
            dimension_semantics=("arbitrary",), vmem_limit_bytes=VMEM_LIMIT),
    )(x, mods, mods, w_in_bf, *cache_prev)


LOG2E = 1.4426950408889634
NEG_BIG = -1e30
N_SCORE_HEADS = 2 * DIFF_HEADS + GQA_Q_HEADS


def _diff_lambda(lam_ref, lam_init):
    lp = lam_ref[...]
    s1 = jnp.sum(lp[0:1, :] * lp[1:2, :], axis=-1, keepdims=True)
    s2 = jnp.sum(lp[2:3, :] * lp[3:4, :], axis=-1, keepdims=True)
    return jnp.exp(s1) - jnp.exp(s2) + lam_init


def _swap_halves(x):
    return pltpu.roll(x, GQA_HD, 1)


def _attn_kernel(cfg, *refs):
    lam_init, seq, past, kb, rope, feature_major, n_prev = cfg
    it = iter(refs)
    qa_ref, dk_ref, dv_ref, gk_in_ref, gv_ref = next(it), next(it), next(it), next(it), next(it)
    for _ in range(n_prev):
        next(it)
    if past:
        cdk_ref, cdv_ref, cgk_ref, cgv_ref = next(it), next(it), next(it), next(it)
    if rope:
        cosq_ref, sinq_ref, cosk_ref, sink_ref = next(it), next(it), next(it), next(it)
    lam_ref, gain_ref, gq_ref, gk_ref = next(it), next(it), next(it), next(it)
    o_ref = next(it)
    gkn_ref = next(it) if feature_major else None
    kd_s, kg_s, vdt_s, vgt_s, wq_s, wg_s, s0_s, s1_s, p_s, m_s, l_s, alpha_s, acc_s = it
    tq = qa_ref.shape[0]
    n_blocks = (seq + past) // kb

    @pl.when(pl.program_id(1) == 0)
    def _prepare_keys():
        gmat = _group_avg_matrix(GQA_KV_W, GQA_HD)
        for i in range(seq // kb):
            rows = slice(i * kb, (i + 1) * kb)
            dk = dk_ref[rows, :]
            gk = gk_in_ref[rows, :]
            gk = gk * lax.rsqrt(_group_mean_sq(gk, gmat) + EPS) * gk_ref[...]
            if feature_major:
                gkn_ref[:, rows] = gk.T
            if rope:
                ck = cosk_ref[rows, :]
                sk = sink_ref[rows, :]
                dk = _rope(dk, ck[:, 0:DIFF_W], sk[:, 0:DIFF_W])
                gk = _rope(gk, ck[:, DIFF_W:DIFF_W + GQA_KV_W], sk[:, DIFF_W:DIFF_W + GQA_KV_W])
            kd_s[rows, :] = dk.astype(BF16)
            kg_s[rows, :] = gk.astype(BF16)
            if feature_major:
                vdt_s[i] = dv_ref[:, rows].astype(BF16)
                vgt_s[i] = gv_ref[:, rows].astype(BF16)
            else:
                vdt_s[i] = dv_ref[rows, :].T.astype(BF16)
                vgt_s[i] = gv_ref[rows, :].T.astype(BF16)
        for j in range(past // kb):
            src = slice(j * kb, (j + 1) * kb)
            dst = slice(seq + j * kb, seq + (j + 1) * kb)
            kd_s[dst, :] = cdk_ref[:, src].T.astype(BF16)
            kg_s[dst, :] = cgk_ref[:, src].T.astype(BF16)
            vdt_s[seq // kb + j] = cdv_ref[:, src].astype(BF16)
            vgt_s[seq // kb + j] = cgv_ref[:, src].astype(BF16)

    qa = qa_ref[...]
    qd = qa[:, 0:DIFF_W]
    gq = qa[:, DIFF_W:DIFF_W + GQA_W]
    gq = gq * lax.rsqrt(_group_mean_sq(gq, _group_avg_matrix(GQA_W, GQA_HD)) + EPS) * gq_ref[...]
    if rope:
        cq = cosq_ref[...]
        sq = sinq_ref[...]
        qd = _rope(qd, cq[:, 0:DIFF_W], sq[:, 0:DIFF_W])
        gq = _rope(gq, cq[:, DIFF_W:DIFF_W + GQA_W], sq[:, DIFF_W:DIFF_W + GQA_W])
    qd_t = (qd * (DIFF_QK ** -0.5 * LOG2E)).T
    gq_t = (gq * (GQA_HD ** -0.5 * LOG2E)).T
    row = lax.broadcasted_iota(jnp.int32, (DIFF_W, tq), 0)
    for hm in range(2 * DIFF_HEADS):
        lo = hm * DIFF_QK
        wq_s[:, hm * tq:(hm + 1) * tq] = jnp.where((row >= lo) & (row < lo + DIFF_QK), qd_t, 0.0).astype(BF16)
    zeros = jnp.zeros((GQA_HD, tq), F32)
    for h in range(GQA_Q_HEADS):
        piece = gq_t[h * GQA_HD:(h + 1) * GQA_HD, :]
        pair = [piece, zeros] if h // (GQA_Q_HEADS // GQA_KV_HEADS) == 0 else [zeros, piece]
        wg_s[:, h * tq:(h + 1) * tq] = jnp.concatenate(pair, axis=0).astype(BF16)
    m_s[...] = jnp.full(m_s.shape, NEG_BIG, F32)
    l_s[...] = jnp.zeros(l_s.shape, F32)
    acc_s[...] = jnp.zeros(acc_s.shape, F32)
    n_diff = 2 * DIFF_HEADS * tq
    n_all = N_SCORE_HEADS * tq

    def scores(j, s_buf):
        rows = pl.ds(j * kb if isinstance(j, int) else pl.multiple_of(j * kb, kb), kb)
        sd = _dot(kd_s[rows, :], wq_s[...])
        for c in range(n_diff // LANES):
            s_buf[c] = sd[:, c * LANES:(c + 1) * LANES]
        sg = _dot(kg_s[rows, :], wg_s[...])
        for c in range((n_all - n_diff) // LANES):
            s_buf[n_diff // LANES + c] = sg[:, c * LANES:(c + 1) * LANES]

    def accumulate(j, s_buf):
        for c in range(n_all // LANES):
            cols = slice(c * LANES, (c + 1) * LANES)
            s = s_buf[c]
            m_old = m_s[:, cols]
            m_new = jnp.maximum(m_old, jnp.max(s, axis=0, keepdims=True))
            alpha = jnp.exp2(m_old - m_new)
            p = jnp.exp2(s - m_new)
            l_s[:, cols] = alpha * l_s[:, cols] + jnp.sum(p, axis=0, keepdims=True)
            m_s[:, cols] = m_new
            alpha_s[:, cols] = alpha
            p_s[c] = p.astype(BF16)
        v_d = vdt_s[j]
        v_g = vgt_s[j]
        slabs = tq // LANES
        for idx in range(N_SCORE_HEADS):
            cols = slice(idx * tq, (idx + 1) * tq)
            if idx < 2 * DIFF_HEADS:
                vh = idx // 2
                v_t = v_d[vh * DIFF_V:(vh + 1) * DIFF_V, :]
            else:
                vh = (idx - 2 * DIFF_HEADS) // (GQA_Q_HEADS // GQA_KV_HEADS)
                v_t = v_g[vh * GQA_HD:(vh + 1) * GQA_HD, :]
            p = jnp.concatenate([p_s[idx * slabs + k] for k in range(slabs)], axis=1)
            acc_s[idx] = alpha_s[:, cols] * acc_s[idx] + _dot(v_t, p)

    scores(0, s0_s)
    if n_blocks > 1:
        def block_pair(i, carry):
            scores(2 * i + 1, s1_s)
            accumulate(2 * i, s0_s)
            scores(2 * i + 2, s0_s)
            accumulate(2 * i + 1, s1_s)
            return carry

        lax.fori_loop(0, n_blocks // 2 - 1, block_pair, 0)
        scores(n_blocks - 1, s1_s)
        accumulate(n_blocks - 2, s0_s)
        accumulate(n_blocks - 1, s1_s)
    else:
        accumulate(0, s0_s)

    lam = _diff_lambda(lam_ref, lam_init)
    outs = []
    for h in range(DIFF_HEADS):
        c0 = slice(2 * h * tq, (2 * h + 1) * tq)
        c1 = slice((2 * h + 1) * tq, (2 * h + 2) * tq)
        o = acc_s[2 * h] * (1.0 / l_s[:, c0]) - acc_s[2 * h + 1] * (lam / l_s[:, c1])
        ms = jnp.mean(o * o, axis=0, keepdims=True)
        outs.append(o * (lax.rsqrt(ms + EPS) * (1.0 - lam_init)))
    for h in range(GQA_Q_HEADS):
        idx = 2 * DIFF_HEADS + h
        outs.append(acc_s[idx] * (1.0 / l_s[:, idx * tq:(idx + 1) * tq]))
    o_ref[...] = jnp.concatenate(outs, axis=0).T * gain_ref[...]


def _attention(qa, kv, caches, layer, tables, lam_p, gain, gq_t, gk_t, lam_init, batch, seq, tq,
               gkn_prev=None):
    t = qa.shape[0]
    nq = seq // tq
    feature_major = gkn_prev is not None
    past = caches[0].shape[3] if caches is not None else 0
    kb = min(256, seq)
    assert seq % kb == 0 and past % kb == 0 and seq % tq == 0
    n_blocks = (seq + past) // kb
    assert n_blocks == 1 or n_blocks % 2 == 0
    n_all = N_SCORE_HEADS * tq
    full = lambda a: pl.BlockSpec(a.shape, lambda b, q: (0,) * a.ndim)
    token_major = lambda a: pl.BlockSpec((seq, a.shape[1]), lambda b, q: (b, 0))
    by_feature = lambda a: pl.BlockSpec((None, None, a.shape[2], seq), lambda b, q: (b, layer, 0, 0))
    dk, dv, gk, gv = kv
    in_specs = [pl.BlockSpec((tq, qa.shape[1]), lambda b, q: (b * nq + q, 0)),
                token_major(dk), by_feature(dv) if feature_major else token_major(dv),
                token_major(gk), by_feature(gv) if feature_major else token_major(gv)]
    args = [qa, dk, dv, gk, gv]
    aliases = {}
    if feature_major:
        assert nq == 1
        in_specs += [pl.BlockSpec(memory_space=pl.ANY)] * len(gkn_prev)
        args += list(gkn_prev)
        aliases = {5 + k: 1 + k for k in range(len(gkn_prev))}
    if caches is not None:
        in_specs += [pl.BlockSpec((None, None, a.shape[2], past), lambda b, q: (b, layer, 0, 0))
                     for a in caches]
        args += list(caches)
    if tables is not None:
        cosq, sinq, cosk, sink = tables
        in_specs += [pl.BlockSpec((tq, cosq.shape[1]), lambda b, q: (q, 0)),
                     pl.BlockSpec((tq, sinq.shape[1]), lambda b, q: (q, 0)),
                     full(cosk), full(sink)]
        args += [cosq, sinq, cosk, sink]
    in_specs += [full(lam_p), full(gain), full(gq_t), full(gk_t)]
    args += [lam_p, gain, gq_t, gk_t]
    out_specs = [pl.BlockSpec((tq, DIFF_W + GQA_W), lambda b, q: (b * nq + q, 0))]
    out_shape = [jax.ShapeDtypeStruct((t, DIFF_W + GQA_W), F32)]
    if feature_major:
        out_specs.append(pl.BlockSpec((None, None, GQA_KV_W, seq), lambda b, q: (b, layer, 0, 0)))
        out_shape.append(jax.ShapeDtypeStruct((batch, dv.shape[1], GQA_KV_W, seq), F32))
    cfg = (lam_init, seq, past, kb, tables is not None, feature_major,
           len(gkn_prev) if feature_major else 0)
    return pl.pallas_call(
        functools.partial(_attn_kernel, cfg),
        grid=(batch, nq),
        in_specs=in_specs,
        out_specs=out_specs,
        out_shape=out_shape,
        input_output_aliases=aliases,
        scratch_shapes=[
            pltpu.VMEM((seq + past, DIFF_W), BF16), pltpu.VMEM((seq + past, GQA_KV_W), BF16),
            pltpu.VMEM((n_blocks, DIFF_W, kb), BF16), pltpu.VMEM((n_blocks, GQA_KV_W, kb), BF16),
            pltpu.VMEM((DIFF_W, 2 * DIFF_HEADS * tq), BF16), pltpu.VMEM((GQA_KV_W, GQA_Q_HEADS * tq), BF16),
            pltpu.VMEM((n_all // LANES, kb, LANES), F32), pltpu.VMEM((n_all // LANES, kb, LANES), F32),
            pltpu.VMEM((n_all // LANES, kb, LANES), BF16),
            pltpu.VMEM((1, n_all), F32), pltpu.VMEM((1, n_all), F32), pltpu.VMEM((1, n_all), F32),
            pltpu.VMEM((N_SCORE_HEADS, GQA_HD, tq), F32),
        ],
        compiler_params=pltpu.CompilerParams(
            dimension_semantics=("arbitrary", "arbitrary"), vmem_limit_bytes=VMEM_LIMIT),
    )(*args)


def _softplus(x):
    return jnp.maximum(x, 0.0) + jnp.log1p(jnp.exp(-jnp.abs(x)))


def _dot2_l(a, b_exact):
    a1 = a.astype(BF16)
    a2 = (a - a1.astype(F32)).astype(BF16)
    return _dot(a1, b_exact) + _dot(a2, b_exact)


def _ssd_kernel(has_init, n_prev, seq, *refs):
    it = iter(refs)
    z_ref, xbc_ref, dt_ref, cw_ref, cb_ref, alog_ref, dtb_ref, dexp_ref, ng_ref = (
        next(it) for _ in range(9))
    sf0_ref, sb0_ref = (next(it), next(it)) if has_init else (None, None)
    for _ in range(n_prev):
        next(it)
    o_ref, sf_ref, sb_ref = next(it), next(it), next(it)
    stf_s, stb_s, inc_s, dec_s, eab_s, cbf_s, exp_s, tri_s, gm_s = it
    L = SSD_CHUNK
    W2 = 2 * SSD_INNER
    nchunks = seq // L
    halo = 8

    ri = lax.broadcasted_iota(jnp.int32, (L, L), 0)
    ci = lax.broadcasted_iota(jnp.int32, (L, L), 1)
    lower = ri >= ci
    upper = ri <= ci
    tri_s[0] = jnp.where(lower, 1.0, 0.0).astype(BF16)
    tri_s[1] = jnp.where(upper, 1.0, 0.0).astype(BF16)
    lane_row = lax.broadcasted_iota(jnp.int32, (1, LANES), 1)
    a_row = jnp.where(lane_row < DT_W, -jnp.exp(alog_ref[...]), 0.0)
    ej = lax.broadcasted_iota(jnp.int32, (LANES, W2), 0)
    eh = lax.shift_right_logical(lax.broadcasted_iota(jnp.int32, (LANES, W2), 1), 6)
    exp_s[...] = jnp.where(ej == eh, 1.0, 0.0).astype(BF16)
    gm_s[...] = jnp.where(lax.shift_right_logical(ej, 6) == (lax.shift_right_logical(eh, 2) & 1), 1.0, 0.0)

    def chunk_rows(c):
        return pl.ds(pl.multiple_of(c * L, L), L)

    def load_state(ref):
        r = lax.broadcasted_iota(jnp.int32, (SSD_STATE, LANES), 0)
        c = lax.broadcasted_iota(jnp.int32, (SSD_STATE, LANES), 1)
        dup = jnp.where((c & (SSD_STATE - 1)) == r, 1.0, 0.0).astype(BF16)
        return _dot3_l(ref[...], dup).T * gm_s[:, 0:SSD_INNER]

    def store_state(st_ref, ref):
        st_t = st_ref[...].T
        ref[...] = (st_t + _swap_halves(st_t))[:, 0:SSD_STATE]

    stf_s[...] = load_state(sf0_ref) if has_init else jnp.zeros(stf_s.shape, F32)
    stb_s[...] = load_state(sb0_ref) if has_init else jnp.zeros(stb_s.shape, F32)

    def forward_pass(c, carry):
        r0 = c * L
        rows = chunk_rows(c)
        prev = xbc_ref[pl.ds(pl.multiple_of(jnp.maximum(r0 - halo, 0), halo), halo), :]
        nxt = xbc_ref[pl.ds(pl.multiple_of(jnp.minimum(r0 + L, seq - halo), halo), halo), :]
        cur = xbc_ref[rows, :]
        win = jnp.concatenate([jnp.where(c > 0, prev, 0.0), cur,
                               jnp.where(c < nchunks - 1, nxt, 0.0)], axis=0)
        acc = cb_ref[...] + cur * cw_ref[SSD_CONV // 2:SSD_CONV // 2 + 1, :]
        for j in range(SSD_CONV):
            if j != SSD_CONV // 2:
                shifted = pltpu.roll(win, (SSD_CONV // 2 - j) % (L + 2 * halo), 0)[halo:halo + L, :]
                acc = acc + shifted * cw_ref[j:j + 1, :]
        act = _silu(acc)
        x_c = act[:, 0:SSD_INNER]
        b_c = act[:, SSD_INNER:SSD_INNER + SSD_BC_W]
        c_c = act[:, SSD_INNER + SSD_BC_W:XBC_W]
        dt_c = _softplus(dt_ref[rows, :] + dtb_ref[...])

        d1, d2, d3 = _split3(dt_c * a_row)
        acs_f = _dot(tri_s[0], d1) + (_dot(tri_s[0], d2) + _dot(tri_s[0], d3))
        acs_b = _dot(tri_s[1], d1) + (_dot(tri_s[1], d2) + _dot(tri_s[1], d3))
        acs = jnp.where(lax.broadcasted_iota(jnp.int32, (L, LANES), 1) < SSD_HEADS, acs_f, acs_b)
        acs_t = acs.T
        expand = exp_s[...]
        dt_e = _dot2_l(dt_c, expand)
        acs_e = _dot3_l(acs, expand)
        edge = jnp.concatenate([acs_e[L - 1:L, 0:SSD_INNER], acs_e[0:1, SSD_INNER:W2]], axis=1)
        eacs = jnp.exp(acs_e)
        cdec = jnp.exp(edge)
        xd = jnp.concatenate([x_c, x_c], axis=1) * dt_e
        xd_b = xd.astype(BF16)
        xdw = (xd * jnp.exp(edge - acs_e)).astype(BF16)
        b_b = b_c.astype(BF16)
        c_b = c_c.astype(BF16)
        s_new = _dot(b_c.T.astype(BF16), xdw) * gm_s[...]

        st_f = stf_s[...]
        y = _dot(c_b, st_f.astype(BF16)) * eacs[:, 0:SSD_INNER]
        stf_s[...] = st_f * cdec[:, 0:SSD_INNER] + s_new[:, 0:SSD_INNER]
        inc_s[c] = s_new[:, SSD_INNER:W2]
        dec_s[c] = cdec[:, SSD_INNER:W2]
        eab_s[rows, :] = eacs[:, SSD_INNER:W2]
        cbf_s[rows, :] = c_b

        for direction, causal in ((0, lower), (1, upper)):
            pairs = []
            for g in range(SSD_GROUPS):
                cg = jnp.where(_lane_mask(c_c.shape, g * SSD_STATE, (g + 1) * SSD_STATE), c_c, 0.0)
                cb = _dot_nt(cg.astype(BF16), b_b)
                for hp in range(2):
                    pair = g * 2 + hp
                    res = []
                    for k in range(2):
                        j = direction * SSD_HEADS + pair * 2 + k
                        diff = acs[:, j:j + 1] - acs_t[j:j + 1, :]
                        dec = jnp.where(causal, jnp.exp(jnp.minimum(diff, 0.0)), 0.0)
                        sc = (cb * dec).astype(BF16)
                        lo = direction * SSD_INNER + pair * LANES
                        res.append(_dot(sc, xd_b[:, lo:lo + LANES]))
                    pairs.append(jnp.where(_lane_mask(res[0].shape, 0, SSD_HD), res[0], res[1]))
            y = y + jnp.concatenate(pairs, axis=-1)
        o_ref[rows, :] = y + x_c * dexp_ref[...]
        return carry

    lax.fori_loop(0, nchunks, forward_pass, 0)
    store_state(stf_s, sf_ref)

    def backward_pass(i, carry):
        c = nchunks - 1 - i
        rows = chunk_rows(c)
        st_b = stb_s[...]
        y = o_ref[rows, :] + _dot(cbf_s[rows, :], st_b.astype(BF16)) * eab_s[rows, :]
        stb_s[...] = st_b * dec_s[c] + inc_s[c]
        yt = y * _silu(z_ref[rows, :])
        ms = jnp.mean(yt * yt, axis=-1, keepdims=True)
        o_ref[rows, :] = yt * lax.rsqrt(ms + EPS) * ng_ref[...]
        return carry

    lax.fori_loop(0, nchunks, backward_pass, 0)
    store_state(stb_s, sb_ref)


def _ssd(z, xbc, dt, params, init, layer, batch, seq, state_prev=(), depth=1):
    cw, cb, alog, dtb, dexp, ng = params
    t = z.shape[0]
    has_init = init is not None
    nchunks = seq // SSD_CHUNK
    full = lambda a: pl.BlockSpec(a.shape, lambda b: (0,) * a.ndim)
    out_layer = layer if depth > 1 else 0
    st_spec = pl.BlockSpec((None, None, SSD_INNER, SSD_STATE), lambda b: (b, out_layer, 0, 0))
    in_specs = [
        pl.BlockSpec((seq, SSD_INNER), lambda b: (b, 0)),
        pl.BlockSpec((seq, XBC_W), lambda b: (b, 0)),
        pl.BlockSpec((seq, LANES), lambda b: (b, 0)),
        full(cw), full(cb), full(alog), full(dtb), full(dexp), full(ng),
    ]
    args = [z, xbc, dt, cw, cb, alog, dtb, dexp, ng]
    if has_init:
        init_spec = pl.BlockSpec((None, None, SSD_INNER, SSD_STATE), lambda b: (b, layer, 0, 0))
        in_specs += [init_spec, init_spec]
        args += list(init)
    aliases = {len(args) + k: 1 + k for k in range(len(state_prev))}
    in_specs += [pl.BlockSpec(memory_space=pl.ANY)] * len(state_prev)
    args += list(state_prev)
    st_shape = jax.ShapeDtypeStruct((batch, depth, SSD_INNER, SSD_STATE), F32)
    return pl.pallas_call(
        functools.partial(_ssd_kernel, has_init, len(state_prev), seq),
        grid=(batch,),
        in_specs=in_specs,
        out_specs=[pl.BlockSpec((seq, SSD_INNER), lambda b: (b, 0)), st_spec, st_spec],
        out_shape=[jax.ShapeDtypeStruct((t, SSD_INNER), F32), st_shape, st_shape],
        input_output_aliases=aliases,
        scratch_shapes=[
            pltpu.VMEM((LANES, SSD_INNER), F32), pltpu.VMEM((LANES, SSD_INNER), F32),
            pltpu.VMEM((nchunks, LANES, SSD_INNER), F32),
            pltpu.VMEM((nchunks, 1, SSD_INNER), F32),
            pltpu.VMEM((seq, SSD_INNER), F32),
            pltpu.VMEM((seq, SSD_BC_W), BF16),
            pltpu.VMEM((LANES, 2 * SSD_INNER), BF16),
            pltpu.VMEM((2, SSD_CHUNK, SSD_CHUNK), BF16),
            pltpu.VMEM((LANES, 2 * SSD_INNER), F32),
        ],
        compiler_params=pltpu.CompilerParams(
            dimension_semantics=("arbitrary",), vmem_limit_bytes=VMEM_LIMIT),
    )(*args)


def _post_kernel(alpha, d_ff, ff_chunk, x_ref, oa_ref, os_ref, g1_ref, sh2_ref, sc2_ref, g2_ref,
                 wo_ref, wfi_ref, wfo_ref, lng_ref, lnb_ref, y_ref):
    x = x_ref[...]
    wa = oa_ref.shape[1]
    o = _dot(oa_ref[...].astype(BF16), wo_ref[0:wa, :]) + _dot(os_ref[...].astype(BF16), wo_ref[wa:, :])
    x1 = _layer_norm(alpha * x + g1_ref[...] * o, lng_ref[0:1, :], lnb_ref[0:1, :])
    h2 = (x1 * (1.0 + sc2_ref[...]) + sh2_ref[...]).astype(BF16)
    f = jnp.zeros(x.shape, F32)
    for c in range(d_ff // ff_chunk):
        lo = c * ff_chunk
        g = _dot(h2, wfi_ref[:, lo:lo + ff_chunk])
        u = _dot(h2, wfi_ref[:, d_ff + lo:d_ff + lo + ff_chunk])
        f = f + _dot((_silu(g) * u).astype(BF16), wfo_ref[lo:lo + ff_chunk, :])
    y_ref[...] = _layer_norm(alpha * x1 + g2_ref[...] * f, lng_ref[1:2, :], lnb_ref[1:2, :])


def _post(x, oa, os_, mods, layer, row_fn, wo_bf, wfi_bf, wfo_bf, lng, lnb, alpha, tm):
    t, d = x.shape
    d_ff = wfo_bf.shape[1]
    ff_chunk = d_ff // 2
    per_layer = lambda a, **kw: pl.BlockSpec((None,) + a.shape[1:], lambda i: (layer, 0, 0), **kw)
    resident = lambda a: per_layer(a, pipeline_mode=pl.Buffered(1))
    return pl.pallas_call(
        functools.partial(_post_kernel, alpha, d_ff, ff_chunk),
        grid=(t // tm,),
        in_specs=[
            pl.BlockSpec((tm, d), lambda i: (i, 0)),
            pl.BlockSpec((tm, oa.shape[1]), lambda i: (i, 0)),
            pl.BlockSpec((tm, os_.shape[1]), lambda i: (i, 0)),
            _mod_spec(layer, 2, row_fn, d),
            _mod_spec(layer, 3, row_fn, d),
            _mod_spec(layer, 4, row_fn, d),
            _mod_spec(layer, 5, row_fn, d),
            resident(wo_bf), resident(wfi_bf), resident(wfo_bf),
            per_layer(lng), per_layer(lnb),
        ],
        out_specs=pl.BlockSpec((tm, d), lambda i: (i, 0)),
        out_shape=jax.ShapeDtypeStruct((t, d), F32),
        compiler_params=pltpu.CompilerParams(
            dimension_semantics=("arbitrary",), vmem_limit_bytes=VMEM_LIMIT),
    )(x, oa, os_, mods, mods, mods, mods, wo_bf, wfi_bf, wfo_bf, lng, lnb)


def _rope_tables(rows, dim, copies):
    row = jnp.repeat(jnp.arange(rows), GRID_W).astype(F32)
    col = jnp.tile(jnp.arange(GRID_W), rows).astype(F32)
    n_freq = dim // 4
    inv = ROPE_THETA ** (-jnp.arange(n_freq, dtype=F32) / n_freq)
    ang = jnp.concatenate([row[:, None] * inv, col[:, None] * inv], -1)
    cos, sin = jnp.cos(ang), jnp.sin(ang)
    cos_full = jnp.repeat(cos, 2, axis=-1)
    sin_signed = jnp.stack([-sin, sin], axis=-1).reshape(sin.shape[0], dim)
    return jnp.tile(cos_full, (1, copies)), jnp.tile(sin_signed, (1, copies))


def kernel(x_prompt, x_sample, cache_diff_k, cache_diff_v, cache_gqa_k, cache_gqa_v, state_ssd_fwd, state_ssd_bwd, c, c_ctx, w_ada, b_ada, w_in, w_out, diff_lambda, diff_subln_g, qk_norm_g, ssd_conv_w, ssd_conv_b, ssd_A_log, ssd_dt_bias, ssd_D, ssd_norm_g, ln_g, ln_b, w_ffn_in, w_ffn_out):
    batch, seq, d = x_prompt.shape
    dec_batch, dec_seq, _ = x_sample.shape
    depth = w_in.shape[0]
    past = cache_diff_k.shape[2]
    alpha = (2 * depth) ** 0.25
    rows = dec_seq // GRID_W

    n_vec = 1 + dec_batch
    n_pad = -(-n_vec // 8) * 8
    cvec = jnp.concatenate([c_ctx[None, :], c, jnp.zeros((n_pad - n_vec, d), F32)], axis=0)
    mods = _modulation(cvec, w_ada, b_ada).reshape(depth, n_pad, 1, 6 * d)

    cos_d, sin_d = _rope_tables(rows, DIFF_QK, DIFF_W // DIFF_QK)
    cos_g, sin_g = _rope_tables(rows, GQA_HD, GQA_Q_HEADS)
    cosq = jnp.concatenate([cos_d, cos_g], axis=-1)
    sinq = jnp.concatenate([sin_d, sin_g], axis=-1)
    cosk = jnp.concatenate([cos_d, cos_g[:, :GQA_KV_W]], axis=-1)
    sink = jnp.concatenate([sin_d, sin_g[:, :GQA_KV_W]], axis=-1)
    tables = (cosq, sinq, cosk, sink)

    feature_major = lambda a, w: jnp.transpose(a.reshape(dec_batch, depth, past, w), (0, 1, 3, 2))
    caches = (feature_major(cache_diff_k, DIFF_W), feature_major(cache_diff_v, DIFF_W),
              feature_major(cache_gqa_k, GQA_KV_W), feature_major(cache_gqa_v, GQA_KV_W))

    xp = x_prompt.reshape(batch * seq, d)
    xs = x_sample.reshape(dec_batch * dec_seq, d)
    tm_ctx = min(512, batch * seq)
    tm_lat = min(512, dec_seq)
    tq = min(256, dec_seq)
    ctx_row = lambda i: 0
    lat_row = lambda i: 1 + (i * tm_lat) // dec_seq

    w_in_bf = jnp.pad(w_in, ((0, 0), (0, 0), (0, IN_W_PAD - w_in.shape[2]))).astype(BF16)
    wo_bf = w_out.astype(BF16)
    wfi_bf = w_ffn_in.astype(BF16)
    wfo_bf = w_ffn_out.astype(BF16)
    init = (state_ssd_fwd.reshape(dec_batch, depth, SSD_INNER, SSD_STATE),
            state_ssd_bwd.reshape(dec_batch, depth, SSD_INNER, SSD_STATE))

    cache = lambda w: jnp.zeros((batch, depth, w, seq), F32)
    kv_cache = (cache(DIFF_W), cache(DIFF_W), cache(GQA_KV_W))
    gk_cache = (cache(GQA_KV_W),)
    ssd_states = tuple(jnp.zeros((batch, depth, SSD_INNER, SSD_STATE), F32) for _ in range(2))
    for l in range(depth):
        lam_init = 0.8 - 0.6 * math.exp(-0.3 * l)
        lam_p = diff_lambda[l]
        attn_gain = jnp.concatenate([jnp.tile(diff_subln_g[l], DIFF_HEADS), jnp.ones((GQA_W,), F32)])[None, :]
        gq_t = jnp.tile(qk_norm_g[l, 0], GQA_Q_HEADS)[None, :]
        gk_t = jnp.tile(qk_norm_g[l, 1], GQA_KV_HEADS)[None, :]
        pad_row = lambda v: jnp.pad(v.reshape(1, -1), ((0, 0), (0, LANES - v.size)))
        ssd_params = (ssd_conv_w[l], ssd_conv_b[l][None, :], pad_row(ssd_A_log[l]),
                      pad_row(ssd_dt_bias[l]), jnp.repeat(ssd_D[l], SSD_HD)[None, :],
                      ssd_norm_g[l][None, :])

        qa, dk, gk, z, xbc, dt, *kv_cache = _inproj(xp, mods, l, ctx_row, w_in_bf, tm_ctx,
                                                    cache_prev=tuple(kv_cache), depth=depth, seq=seq)
        oa, gkn = _attention(qa, (dk, kv_cache[1], gk, kv_cache[2]), None, l, None, lam_p, attn_gain,
                             gq_t, gk_t, lam_init, batch, seq, seq, gkn_prev=gk_cache)
        gk_cache = (gkn,)
        os_, *ssd_states = _ssd(z, xbc, dt, ssd_params, None, l, batch, seq,
                                state_prev=tuple(ssd_states), depth=depth)
        xp = _post(xp, oa, os_, mods, l, ctx_row, wo_bf, wfi_bf, wfo_bf, ln_g, ln_b, alpha, tm_ctx)

        qa, dk, dv, gk, gv, z, xbc, dt = _inproj(xs, mods, l, lat_row, w_in_bf, tm_lat)
        (oa,) = _attention(qa, (dk, dv, gk, gv), caches, l, tables, lam_p, attn_gain, gq_t, gk_t,
                           lam_init, dec_batch, dec_seq, tq)
        os_, _, _ = _ssd(z, xbc, dt, ssd_params, init, l, dec_batch, dec_seq)
        xs = _post(xs, oa, os_, mods, l, lat_row, wo_bf, wfi_bf, wfo_bf, ln_g, ln_b, alpha, tm_lat)

    def token_major(a, heads, width):
        return jnp.transpose(a.reshape(batch, depth, heads, width, seq), (0, 1, 4, 2, 3))

    state = lambda a: a.reshape(batch, depth, SSD_HEADS, SSD_HD, SSD_STATE)
    return (xp.reshape(batch, seq, d), xs.reshape(dec_batch, dec_seq, d),
            token_major(kv_cache[0], DIFF_HEADS, 2 * DIFF_QK), token_major(kv_cache[1], DIFF_HEADS, DIFF_V),
            token_major(gk_cache[0], GQA_KV_HEADS, GQA_HD), token_major(kv_cache[2], GQA_KV_HEADS, GQA_HD),
            state(ssd_states[0]), state(ssd_states[1]))
```

```python
import functools
import math

import jax
import jax.numpy as jnp
from jax import lax
from jax.experimental import pallas as pl
from jax.experimental.pallas import tpu as pltpu

F32 = jnp.float32
BF16 = jnp.bfloat16

GRID_W = 64
DIFF_HEADS = 4
DIFF_QK = 32
DIFF_V = 64
DIFF_W = DIFF_HEADS * DIFF_V
GQA_HD = 64
GQA_Q_HEADS = 4
GQA_KV_HEADS = 2
GQA_W = GQA_Q_HEADS * GQA_HD
GQA_KV_W = GQA_KV_HEADS * GQA_HD
SSD_HD = 64
SSD_HEADS = 8
SSD_INNER = SSD_HEADS * SSD_HD
SSD_GROUPS = 2
SSD_STATE = 64
SSD_BC_W = SSD_GROUPS * SSD_STATE
SSD_CONV = 5
SSD_CHUNK = 128
XBC_W = SSD_INNER + 2 * SSD_BC_W
DT_W = 2 * SSD_HEADS
ROPE_THETA = 10000.0
EPS = 1e-5
LANES = 128
VMEM_LIMIT = 56 * 1024 * 1024

_C_DQ, _C_DK, _C_DV, _C_GQ, _C_GK, _C_GV, _C_Z, _C_XBC, _C_DT, _C_END = (
    0, 256, 512, 768, 1024, 1152, 1280, 1792, 2560, 2576)
IN_W_PAD = _C_DT + LANES


def _dot(a, b):
    return jnp.dot(a, b, preferred_element_type=F32)


def _dot_nt(a, b):
    return lax.dot_general(a, b, (((1,), (1,)), ((), ())), preferred_element_type=F32)


def _split3(a):
    a1 = a.astype(BF16)
    r1 = a - a1.astype(F32)
    a2 = r1.astype(BF16)
    a3 = (r1 - a2.astype(F32)).astype(BF16)
    return a1, a2, a3


def _dot3_l(a, b_exact):
    a1, a2, a3 = _split3(a)
    return _dot(a1, b_exact) + (_dot(a2, b_exact) + _dot(a3, b_exact))


def _sigmoid(x):
    return 1.0 / (1.0 + jnp.exp(-x))


def _silu(x):
    return x * _sigmoid(x)


def _layer_norm(x, g, b):
    mu = jnp.mean(x, axis=-1, keepdims=True)
    xc = x - mu
    var = jnp.mean(xc * xc, axis=-1, keepdims=True)
    return xc * lax.rsqrt(var + EPS) * g + b


def _group_avg_matrix(width, group):
    sh = int(math.log2(group))
    r = lax.shift_right_logical(lax.broadcasted_iota(jnp.int32, (width, width), 0), sh)
    c = lax.shift_right_logical(lax.broadcasted_iota(jnp.int32, (width, width), 1), sh)
    return jnp.where(r == c, 1.0 / group, 0.0).astype(BF16)


def _group_mean_sq(x, gmat):
    xx = x * x
    hi = xx.astype(BF16)
    lo = (xx - hi.astype(F32)).astype(BF16)
    return _dot(hi, gmat) + _dot(lo, gmat)


def _rope(x, cos, sin_signed):
    w = x.shape[-1]
    lane = lax.broadcasted_iota(jnp.int32, x.shape, 1)
    nxt = pltpu.roll(x, w - 1, 1)
    prv = pltpu.roll(x, 1, 1)
    partner = jnp.where((lane & 1) == 0, nxt, prv)
    return x * cos + partner * sin_signed


def _lane_mask(shape, lo, hi):
    lane = lax.broadcasted_iota(jnp.int32, shape, 1)
    return (lane >= lo) & (lane < hi)


def _mod_kernel(c_ref, w_ref, b_ref, o_ref):
    a = _silu(c_ref[...])
    a_hi = a.astype(BF16)
    a_lo = (a - a_hi.astype(F32)).astype(BF16)
    w = w_ref[...]
    w_hi = w.astype(BF16)
    w_lo = (w - w_hi.astype(F32)).astype(BF16)
    o_ref[...] = _dot(a_hi, w_hi) + (_dot(a_lo, w_hi) + _dot(a_hi, w_lo)) + b_ref[...]


def _modulation(cvec, w_ada, b_ada):
    depth, d, n = w_ada.shape
    tn = 1536
    rows = cvec.shape[0]
    return pl.pallas_call(
        _mod_kernel,
        grid=(depth, n // tn),
        in_specs=[
            pl.BlockSpec((rows, d), lambda l, j: (0, 0)),
            pl.BlockSpec((None, d, tn), lambda l, j: (l, 0, j)),
            pl.BlockSpec((None, 1, tn), lambda l, j: (l, 0, j)),
        ],
        out_specs=pl.BlockSpec((None, rows, tn), lambda l, j: (l, 0, j)),
        out_shape=jax.ShapeDtypeStruct((depth, rows, n), F32),
        compiler_params=pltpu.CompilerParams(
            dimension_semantics=("arbitrary", "arbitrary"), vmem_limit_bytes=VMEM_LIMIT),
    )(cvec, w_ada, b_ada.reshape(depth, 1, n))


def _inproj_kernel(feature_major_cache, n_prev, *refs):
    x_ref, sh_ref, sc_ref, w_ref = refs[:4]
    outs = refs[4 + n_prev:]
    h = (x_ref[...] * (1.0 + sc_ref[...]) + sh_ref[...]).astype(BF16)

    def mm(lo, hi):
        return _dot(h, w_ref[:, lo:hi])

    if feature_major_cache:
        (qa_ref, dk_ref, gk_ref, z_ref, xbc_ref, dt_ref, dkt_ref, dvt_ref, gvt_ref,
         dv_s, gv_s) = outs
        dk_ref[...] = mm(_C_DK, _C_DV)
        dv_s[...] = mm(_C_DV, _C_GQ)
        gv_s[...] = mm(_C_GV, _C_Z)
        seq = dkt_ref.shape[-1]
        for j in range(dkt_ref.shape[0]):
            rows = slice(j * seq, (j + 1) * seq)
            dkt_ref[j] = dk_ref[rows, :].T
            dvt_ref[j] = dv_s[rows, :].T
            gvt_ref[j] = gv_s[rows, :].T
    else:
        qa_ref, dk_ref, dv_ref, gk_ref, gv_ref, z_ref, xbc_ref, dt_ref = outs
        dk_ref[...] = mm(_C_DK, _C_DV)
        dv_ref[...] = mm(_C_DV, _C_GQ)
        gv_ref[...] = mm(_C_GV, _C_Z)
    qa_ref[:, 0:DIFF_W] = mm(_C_DQ, _C_DK)
    qa_ref[:, DIFF_W:DIFF_W + GQA_W] = mm(_C_GQ, _C_GK)
    gk_ref[...] = mm(_C_GK, _C_GV)
    z_ref[...] = mm(_C_Z, _C_XBC)
    xbc_ref[...] = mm(_C_XBC, _C_DT)
    dt_ref[...] = mm(_C_DT, IN_W_PAD)


def _mod_spec(layer, which, row_fn, d):
    return pl.BlockSpec((None, None, 1, d), lambda i: (layer, row_fn(i), 0, which))


def _inproj(x, mods, layer, row_fn, w_in_bf, tm, cache_prev=None, depth=None, seq=None):
    t, d = x.shape
    feature_major = cache_prev is not None
    token_spec = lambda w: pl.BlockSpec((tm, w), lambda i: (i, 0))
    token_shape = lambda w: jax.ShapeDtypeStruct((t, w), F32)
    if feature_major:
        widths = (DIFF_W + GQA_W, DIFF_W, GQA_KV_W, SSD_INNER, XBC_W, LANES)
        cache_w = (DIFF_W, DIFF_W, GQA_KV_W)
        assert tm % seq == 0
        out_specs = [token_spec(w) for w in widths] + [
            pl.BlockSpec((tm // seq, None, w, seq), lambda i: (i, layer, 0, 0)) for w in cache_w]
        out_shape = [token_shape(w) for w in widths] + [
            jax.ShapeDtypeStruct((t // seq, depth, w, seq), F32) for w in cache_w]
        aliases = {4 + k: len(widths) + k for k in range(len(cache_prev))}
    else:
        widths = (DIFF_W + GQA_W, DIFF_W, DIFF_W, GQA_KV_W, GQA_KV_W, SSD_INNER, XBC_W, LANES)
        out_specs = [token_spec(w) for w in widths]
        out_shape = [token_shape(w) for w in widths]
        cache_prev, aliases = (), {}
    return pl.pallas_call(
        functools.partial(_inproj_kernel, feature_major, len(cache_prev)),
        grid=(t // tm,),
        in_specs=[
            pl.BlockSpec((tm, d), lambda i: (i, 0)),
            _mod_spec(layer, 0, row_fn, d),
            _mod_spec(layer, 1, row_fn, d),
            pl.BlockSpec((None, d, IN_W_PAD), lambda i: (layer, 0, 0)),
        ] + [pl.BlockSpec(memory_space=pl.ANY)] * len(cache_prev),
        out_specs=out_specs,
        out_shape=out_shape,
        input_output_aliases=aliases,
        scratch_shapes=([pltpu.VMEM((tm, DIFF_W), F32), pltpu.VMEM((tm, GQA_KV_W), F32)]
                        if feature_major else []),
        compiler_params=pltpu.CompilerParams(
            dimension_semantics=("arbitrary",), vmem_limit_bytes=VMEM_LIMIT),
    )(x, mods, mods, w_in_bf, *cache_prev)


LOG2E = 1.4426950408889634
NEG_BIG = -1e30
N_SCORE_HEADS = 2 * DIFF_HEADS + GQA_Q_HEADS


def _diff_lambda(lam_ref, lam_init):
    lp = lam_ref[...]
    s1 = jnp.sum(lp[0:1, :] * lp[1:2, :], axis=-1, keepdims=True)
    s2 = jnp.sum(lp[2:3, :] * lp[3:4, :], axis=-1, keepdims=True)
    return jnp.exp(s1) - jnp.exp(s2) + lam_init


def _swap_halves(x):
    return pltpu.roll(x, GQA_HD, 1)


def _attn_kernel(cfg, *refs):
    lam_init, seq, past, kb, rope, feature_major, n_prev = cfg
    it = iter(refs)
    qa_ref, dk_ref, dv_ref, gk_in_ref, gv_ref = next(it), next(it), next(it), next(it), next(it)
    for _ in range(n_prev):
        next(it)
    if past:
        cdk_ref, cdv_ref, cgk_ref, cgv_ref = next(it), next(it), next(it), next(it)
    if rope:
        cosq_ref, sinq_ref, cosk_ref, sink_ref = next(it), next(it), next(it), next(it)
    lam_ref, gain_ref, gq_ref, gk_ref = next(it), next(it), next(it), next(it)
    o_ref = next(it)
    gkn_ref = next(it) if feature_major else None
    kd_s, kg_s, vdt_s, vgt_s, wq_s, wg_s, s0_s, s1_s, m_s, l_s, acc_s = it
    tq = qa_ref.shape[0]
    n_blocks = (seq + past) // kb

    @pl.when(pl.program_id(1) == 0)
    def _prepare_keys():
        gmat = _group_avg_matrix(GQA_KV_W, GQA_HD)
        for i in range(seq // kb):
            rows = slice(i * kb, (i + 1) * kb)
            dk = dk_ref[rows, :]
            gk = gk_in_ref[rows, :]
            gk = gk * lax.rsqrt(_group_mean_sq(gk, gmat) + EPS) * gk_ref[...]
            if feature_major:
                gkn_ref[:, rows] = gk.T
            if rope:
                ck = cosk_ref[rows, :]
                sk = sink_ref[rows, :]
                dk = _rope(dk, ck[:, 0:DIFF_W], sk[:, 0:DIFF_W])
                gk = _rope(gk, ck[:, DIFF_W:DIFF_W + GQA_KV_W], sk[:, DIFF_W:DIFF_W + GQA_KV_W])
            kd_s[rows, :] = dk.astype(BF16)
            kg_s[rows, :] = gk.astype(BF16)
            if feature_major:
                vdt_s[i] = dv_ref[:, rows].astype(BF16)
                vgt_s[i] = gv_ref[:, rows].astype(BF16)
            else:
                vdt_s[i] = dv_ref[rows, :].T.astype(BF16)
                vgt_s[i] = gv_ref[rows, :].T.astype(BF16)
        for j in range(past // kb):
            src = slice(j * kb, (j + 1) * kb)
            dst = slice(seq + j * kb, seq + (j + 1) * kb)
            kd_s[dst, :] = cdk_ref[:, src].T.astype(BF16)
            kg_s[dst, :] = cgk_ref[:, src].T.astype(BF16)
            vdt_s[seq // kb + j] = cdv_ref[:, src].astype(BF16)
            vgt_s[seq // kb + j] = cgv_ref[:, src].astype(BF16)

    qa = qa_ref[...]
    qd = qa[:, 0:DIFF_W]
    gq = qa[:, DIFF_W:DIFF_W + GQA_W]
    gq = gq * lax.rsqrt(_group_mean_sq(gq, _group_avg_matrix(GQA_W, GQA_HD)) + EPS) * gq_ref[...]
    if rope:
        cq = cosq_ref[...]
        sq = sinq_ref[...]
        qd = _rope(qd, cq[:, 0:DIFF_W], sq[:, 0:DIFF_W])
        gq = _rope(gq, cq[:, DIFF_W:DIFF_W + GQA_W], sq[:, DIFF_W:DIFF_W + GQA_W])
    qd_t = (qd * (DIFF_QK ** -0.5 * LOG2E)).T
    gq_t = (gq * (GQA_HD ** -0.5 * LOG2E)).T
    row = lax.broadcasted_iota(jnp.int32, (DIFF_W, tq), 0)
    for hm in range(2 * DIFF_HEADS):
        lo = hm * DIFF_QK
        wq_s[:, hm * tq:(hm + 1) * tq] = jnp.where((row >= lo) & (row < lo + DIFF_QK), qd_t, 0.0).astype(BF16)
    zeros = jnp.zeros((GQA_HD, tq), F32)
    for h in range(GQA_Q_HEADS):
        piece = gq_t[h * GQA_HD:(h + 1) * GQA_HD, :]
        pair = [piece, zeros] if h // (GQA_Q_HEADS // GQA_KV_HEADS) == 0 else [zeros, piece]
        wg_s[:, h * tq:(h + 1) * tq] = jnp.concatenate(pair, axis=0).astype(BF16)
    m_s[...] = jnp.full(m_s.shape, NEG_BIG, F32)
    l_s[...] = jnp.zeros(l_s.shape, F32)
    acc_s[...] = jnp.zeros(acc_s.shape, F32)
    n_diff = 2 * DIFF_HEADS * tq
    n_all = N_SCORE_HEADS * tq

    slabs = tq // LANES

    def key_rows(j):
        return pl.ds(j * kb if isinstance(j, int) else pl.multiple_of(j * kb, kb), kb)

    def head_scores(idx, k_d, k_g, s_buf):
        if idx < 2 * DIFF_HEADS:
            s = _dot(k_d, wq_s[:, idx * tq:(idx + 1) * tq])
        else:
            h = idx - 2 * DIFF_HEADS
            s = _dot(k_g, wg_s[:, h * tq:(h + 1) * tq])
        for k in range(slabs):
            s_buf[idx * slabs + k] = s[:, k * LANES:(k + 1) * LANES]

    def head_update(idx, v_d, v_g, s_buf):
        ps, alphas = [], []
        for k in range(slabs):
            c = idx * slabs + k
            cols = slice(c * LANES, (c + 1) * LANES)
            s = s_buf[c]
            m_old = m_s[:, cols]
            m_new = jnp.maximum(m_old, jnp.max(s, axis=0, keepdims=True))
            alpha = jnp.exp2(m_old - m_new)
            p = jnp.exp2(s - m_new)
            l_s[:, cols] = alpha * l_s[:, cols] + jnp.sum(p, axis=0, keepdims=True)
            m_s[:, cols] = m_new
            ps.append(p.astype(BF16))
            alphas.append(alpha)
        if idx < 2 * DIFF_HEADS:
            vh = idx // 2
            v_t = v_d[vh * DIFF_V:(vh + 1) * DIFF_V, :]
        else:
            vh = (idx - 2 * DIFF_HEADS) // (GQA_Q_HEADS // GQA_KV_HEADS)
            v_t = v_g[vh * GQA_HD:(vh + 1) * GQA_HD, :]
        acc_s[idx] = (jnp.concatenate(alphas, axis=1) * acc_s[idx]
                      + _dot(v_t, jnp.concatenate(ps, axis=1)))

    def key_block(j, s_cur, j_next, s_next):
        v_d = vdt_s[j]
        v_g = vgt_s[j]
        if j_next is not None:
            k_d = kd_s[key_rows(j_next), :]
            k_g = kg_s[key_rows(j_next), :]
        for idx in range(N_SCORE_HEADS):
            if j_next is not None:
                head_scores(idx, k_d, k_g, s_next)
            head_update(idx, v_d, v_g, s_cur)

    for idx in range(N_SCORE_HEADS):
        head_scores(idx, kd_s[key_rows(0), :], kg_s[key_rows(0), :], s0_s)
    if n_blocks > 1:
        def block_pair(i, carry):
            key_block(2 * i, s0_s, 2 * i + 1, s1_s)
            key_block(2 * i + 1, s1_s, 2 * i + 2, s0_s)
            return carry

        lax.fori_loop(0, n_blocks // 2 - 1, block_pair, 0)
        key_block(n_blocks - 2, s0_s, n_blocks - 1, s1_s)
        key_block(n_blocks - 1, s1_s, None, None)
    else:
        key_block(0, s0_s, None, None)

    lam = _diff_lambda(lam_ref, lam_init)
    outs = []
    for h in range(DIFF_HEADS):
        c0 = slice(2 * h * tq, (2 * h + 1) * tq)
        c1 = slice((2 * h + 1) * tq, (2 * h + 2) * tq)
        o = acc_s[2 * h] * (1.0 / l_s[:, c0]) - acc_s[2 * h + 1] * (lam / l_s[:, c1])
        ms = jnp.mean(o * o, axis=0, keepdims=True)
        outs.append(o * (lax.rsqrt(ms + EPS) * (1.0 - lam_init)))
    for h in range(GQA_Q_HEADS):
        idx = 2 * DIFF_HEADS + h
        outs.append(acc_s[idx] * (1.0 / l_s[:, idx * tq:(idx + 1) * tq]))
    o_ref[...] = jnp.concatenate(outs, axis=0).T * gain_ref[...]


def _attention(qa, kv, caches, layer, tables, lam_p, gain, gq_t, gk_t, lam_init, batch, seq, tq,
               gkn_prev=None):
    t = qa.shape[0]
    nq = seq // tq
    feature_major = gkn_prev is not None
    past = caches[0].shape[3] if caches is not None else 0
    kb = min(256, seq)
    assert seq % kb == 0 and past % kb == 0 and seq % tq == 0
    n_blocks = (seq + past) // kb
    assert n_blocks == 1 or n_blocks % 2 == 0
    n_all = N_SCORE_HEADS * tq
    full = lambda a: pl.BlockSpec(a.shape, lambda b, q: (0,) * a.ndim)
    token_major = lambda a: pl.BlockSpec((seq, a.shape[1]), lambda b, q: (b, 0))
    by_feature = lambda a: pl.BlockSpec((None, None, a.shape[2], seq), lambda b, q: (b, layer, 0, 0))
    dk, dv, gk, gv = kv
    in_specs = [pl.BlockSpec((tq, qa.shape[1]), lambda b, q: (b * nq + q, 0)),
                token_major(dk), by_feature(dv) if feature_major else token_major(dv),
                token_major(gk), by_feature(gv) if feature_major else token_major(gv)]
    args = [qa, dk, dv, gk, gv]
    aliases = {}
    if feature_major:
        assert nq == 1
        in_specs += [pl.BlockSpec(memory_space=pl.ANY)] * len(gkn_prev)
        args += list(gkn_prev)
        aliases = {5 + k: 1 + k for k in range(len(gkn_prev))}
    if caches is not None:
        in_specs += [pl.BlockSpec((None, None, a.shape[2], past), lambda b, q: (b, layer, 0, 0))
                     for a in caches]
        args += list(caches)
    if tables is not None:
        cosq, sinq, cosk, sink = tables
        in_specs += [pl.BlockSpec((tq, cosq.shape[1]), lambda b, q: (q, 0)),
                     pl.BlockSpec((tq, sinq.shape[1]), lambda b, q: (q, 0)),
                     full(cosk), full(sink)]
        args += [cosq, sinq, cosk, sink]
    in_specs += [full(lam_p), full(gain), full(gq_t), full(gk_t)]
    args += [lam_p, gain, gq_t, gk_t]
    out_specs = [pl.BlockSpec((tq, DIFF_W + GQA_W), lambda b, q: (b * nq + q, 0))]
    out_shape = [jax.ShapeDtypeStruct((t, DIFF_W + GQA_W), F32)]
    if feature_major:
        out_specs.append(pl.BlockSpec((None, None, GQA_KV_W, seq), lambda b, q: (b, layer, 0, 0)))
        out_shape.append(jax.ShapeDtypeStruct((batch, dv.shape[1], GQA_KV_W, seq), F32))
    cfg = (lam_init, seq, past, kb, tables is not None, feature_major,
           len(gkn_prev) if feature_major else 0)
    return pl.pallas_call(
        functools.partial(_attn_kernel, cfg),
        grid=(batch, nq),
        in_specs=in_specs,
        out_specs=out_specs,
        out_shape=out_shape,
        input_output_aliases=aliases,
        scratch_shapes=[
            pltpu.VMEM((seq + past, DIFF_W), BF16), pltpu.VMEM((seq + past, GQA_KV_W), BF16),
            pltpu.VMEM((n_blocks, DIFF_W, kb), BF16), pltpu.VMEM((n_blocks, GQA_KV_W, kb), BF16),
            pltpu.VMEM((DIFF_W, 2 * DIFF_HEADS * tq), BF16), pltpu.VMEM((GQA_KV_W, GQA_Q_HEADS * tq), BF16),
            pltpu.VMEM((n_all // LANES, kb, LANES), F32), pltpu.VMEM((n_all // LANES, kb, LANES), F32),
            pltpu.VMEM((1, n_all), F32), pltpu.VMEM((1, n_all), F32),
            pltpu.VMEM((N_SCORE_HEADS, GQA_HD, tq), F32),
        ],
        compiler_params=pltpu.CompilerParams(
            dimension_semantics=("arbitrary", "arbitrary"), vmem_limit_bytes=VMEM_LIMIT),
    )(*args)


def _softplus(x):
    return jnp.maximum(x, 0.0) + jnp.log1p(jnp.exp(-jnp.abs(x)))


def _dot2_l(a, b_exact):
    a1 = a.astype(BF16)
    a2 = (a - a1.astype(F32)).astype(BF16)
    return _dot(a1, b_exact) + _dot(a2, b_exact)


def _ssd_kernel(has_init, n_prev, seq, *refs):
    it = iter(refs)
    z_ref, xbc_ref, dt_ref, cw_ref, cb_ref, alog_ref, dtb_ref, dexp_ref, ng_ref = (
        next(it) for _ in range(9))
    sf0_ref, sb0_ref = (next(it), next(it)) if has_init else (None, None)
    for _ in range(n_prev):
        next(it)
    o_ref, sf_ref, sb_ref = next(it), next(it), next(it)
    stf_s, stb_s, inc_s, dec_s, eab_s, cbf_s, exp_s, tri_s, gm_s = it
    L = SSD_CHUNK
    W2 = 2 * SSD_INNER
    nchunks = seq // L
    halo = 8

    ri = lax.broadcasted_iota(jnp.int32, (L, L), 0)
    ci = lax.broadcasted_iota(jnp.int32, (L, L), 1)
    lower = ri >= ci
    upper = ri <= ci
    tri_s[0] = jnp.where(lower, 1.0, 0.0).astype(BF16)
    tri_s[1] = jnp.where(upper, 1.0, 0.0).astype(BF16)
    lane_row = lax.broadcasted_iota(jnp.int32, (1, LANES), 1)
    a_row = jnp.where(lane_row < DT_W, -jnp.exp(alog_ref[...]), 0.0)
    ej = lax.broadcasted_iota(jnp.int32, (LANES, W2), 0)
    eh = lax.shift_right_logical(lax.broadcasted_iota(jnp.int32, (LANES, W2), 1), 6)
    exp_s[...] = jnp.where(ej == eh, 1.0, 0.0).astype(BF16)
    gm_s[...] = jnp.where(lax.shift_right_logical(ej, 6) == (lax.shift_right_logical(eh, 2) & 1), 1.0, 0.0)

    def chunk_rows(c):
        return pl.ds(pl.multiple_of(c * L, L), L)

    def load_state(ref):
        r = lax.broadcasted_iota(jnp.int32, (SSD_STATE, LANES), 0)
        c = lax.broadcasted_iota(jnp.int32, (SSD_STATE, LANES), 1)
        dup = jnp.where((c & (SSD_STATE - 1)) == r, 1.0, 0.0).astype(BF16)
        return _dot3_l(ref[...], dup).T * gm_s[:, 0:SSD_INNER]

    def store_state(st_ref, ref):
        st_t = st_ref[...].T
        ref[...] = (st_t + _swap_halves(st_t))[:, 0:SSD_STATE]

    stf_s[...] = load_state(sf0_ref) if has_init else jnp.zeros(stf_s.shape, F32)
    stb_s[...] = load_state(sb0_ref) if has_init else jnp.zeros(stb_s.shape, F32)

    def forward_pass(c, carry):
        r0 = c * L
        rows = chunk_rows(c)
        prev = xbc_ref[pl.ds(pl.multiple_of(jnp.maximum(r0 - halo, 0), halo), halo), :]
        nxt = xbc_ref[pl.ds(pl.multiple_of(jnp.minimum(r0 + L, seq - halo), halo), halo), :]
        cur = xbc_ref[rows, :]
        win = jnp.concatenate([jnp.where(c > 0, prev, 0.0), cur,
                               jnp.where(c < nchunks - 1, nxt, 0.0)], axis=0)
        acc = cb_ref[...] + cur * cw_ref[SSD_CONV // 2:SSD_CONV // 2 + 1, :]
        for j in range(SSD_CONV):
            if j != SSD_CONV // 2:
                shifted = pltpu.roll(win, (SSD_CONV // 2 - j) % (L + 2 * halo), 0)[halo:halo + L, :]
                acc = acc + shifted * cw_ref[j:j + 1, :]
        act = _silu(acc)
        x_c = act[:, 0:SSD_INNER]
        b_c = act[:, SSD_INNER:SSD_INNER + SSD_BC_W]
        c_c = act[:, SSD_INNER + SSD_BC_W:XBC_W]
        dt_c = _softplus(dt_ref[rows, :] + dtb_ref[...])

        d1, d2, d3 = _split3(dt_c * a_row)
        acs_f = _dot(tri_s[0], d1) + (_dot(tri_s[0], d2) + _dot(tri_s[0], d3))
        acs_b = _dot(tri_s[1], d1) + (_dot(tri_s[1], d2) + _dot(tri_s[1], d3))
        acs = jnp.where(lax.broadcasted_iota(jnp.int32, (L, LANES), 1) < SSD_HEADS, acs_f, acs_b)
        acs_t = acs.T
        expand = exp_s[...]
        dt_e = _dot2_l(dt_c, expand)
        acs_e = _dot3_l(acs, expand)
        edge = jnp.concatenate([acs_e[L - 1:L, 0:SSD_INNER], acs_e[0:1, SSD_INNER:W2]], axis=1)
        eacs = jnp.exp(acs_e)
        cdec = jnp.exp(edge)
        xd = jnp.concatenate([x_c, x_c], axis=1) * dt_e
        xd_b = xd.astype(BF16)
        xdw = (xd * jnp.exp(edge - acs_e)).astype(BF16)
        b_b = b_c.astype(BF16)
        c_b = c_c.astype(BF16)
        s_new = _dot(b_c.T.astype(BF16), xdw) * gm_s[...]

        st_f = stf_s[...]
        y = _dot(c_b, st_f.astype(BF16)) * eacs[:, 0:SSD_INNER]
        stf_s[...] = st_f * cdec[:, 0:SSD_INNER] + s_new[:, 0:SSD_INNER]
        inc_s[c] = s_new[:, SSD_INNER:W2]
        dec_s[c] = cdec[:, SSD_INNER:W2]
        eab_s[rows, :] = eacs[:, SSD_INNER:W2]
        cbf_s[rows, :] = c_b

        for direction, causal in ((0, lower), (1, upper)):
            pairs = []
            for g in range(SSD_GROUPS):
                cg = jnp.where(_lane_mask(c_c.shape, g * SSD_STATE, (g + 1) * SSD_STATE), c_c, 0.0)
                cb = _dot_nt(cg.astype(BF16), b_b)
                for hp in range(2):
                    pair = g * 2 + hp
                    res = []
                    for k in range(2):
                        j = direction * SSD_HEADS + pair * 2 + k
                        diff = acs[:, j:j + 1] - acs_t[j:j + 1, :]
                        dec = jnp.where(causal, jnp.exp(jnp.minimum(diff, 0.0)), 0.0)
                        sc = (cb * dec).astype(BF16)
                        lo = direction * SSD_INNER + pair * LANES
                        res.append(_dot(sc, xd_b[:, lo:lo + LANES]))
                    pairs.append(jnp.where(_lane_mask(res[0].shape, 0, SSD_HD), res[0], res[1]))
            y = y + jnp.concatenate(pairs, axis=-1)
        o_ref[rows, :] = y + x_c * dexp_ref[...]
        return carry

    lax.fori_loop(0, nchunks, forward_pass, 0)
    store_state(stf_s, sf_ref)

    def backward_pass(i, carry):
        c = nchunks - 1 - i
        rows = chunk_rows(c)
        st_b = stb_s[...]
        y = o_ref[rows, :] + _dot(cbf_s[rows, :], st_b.astype(BF16)) * eab_s[rows, :]
        stb_s[...] = st_b * dec_s[c] + inc_s[c]
        yt = y * _silu(z_ref[rows, :])
        ms = jnp.mean(yt * yt, axis=-1, keepdims=True)
        o_ref[rows, :] = yt * lax.rsqrt(ms + EPS) * ng_ref[...]
        return carry

    lax.fori_loop(0, nchunks, backward_pass, 0)
    store_state(stb_s, sb_ref)


def _ssd(z, xbc, dt, params, init, layer, batch, seq, state_prev=(), depth=1):
    cw, cb, alog, dtb, dexp, ng = params
    t = z.shape[0]
    has_init = init is not None
    nchunks = seq // SSD_CHUNK
    full = lambda a: pl.BlockSpec(a.shape, lambda b: (0,) * a.ndim)
    out_layer = layer if depth > 1 else 0
    st_spec = pl.BlockSpec((None, None, SSD_INNER, SSD_STATE), lambda b: (b, out_layer, 0, 0))
    in_specs = [
        pl.BlockSpec((seq, SSD_INNER), lambda b: (b, 0)),
        pl.BlockSpec((seq, XBC_W), lambda b: (b, 0)),
        pl.BlockSpec((seq, LANES), lambda b: (b, 0)),
        full(cw), full(cb), full(alog), full(dtb), full(dexp), full(ng),
    ]
    args = [z, xbc, dt, cw, cb, alog, dtb, dexp, ng]
    if has_init:
        init_spec = pl.BlockSpec((None, None, SSD_INNER, SSD_STATE), lambda b: (b, layer, 0, 0))
        in_specs += [init_spec, init_spec]
        args += list(init)
    aliases = {len(args) + k: 1 + k for k in range(len(state_prev))}
    in_specs += [pl.BlockSpec(memory_space=pl.ANY)] * len(state_prev)
    args += list(state_prev)
    st_shape = jax.ShapeDtypeStruct((batch, depth, SSD_INNER, SSD_STATE), F32)
    return pl.pallas_call(
        functools.partial(_ssd_kernel, has_init, len(state_prev), seq),
        grid=(batch,),
        in_specs=in_specs,
        out_specs=[pl.BlockSpec((seq, SSD_INNER), lambda b: (b, 0)), st_spec, st_spec],
        out_shape=[jax.ShapeDtypeStruct((t, SSD_INNER), F32), st_shape, st_shape],
        input_output_aliases=aliases,
        scratch_shapes=[
            pltpu.VMEM((LANES, SSD_INNER), F32), pltpu.VMEM((LANES, SSD_INNER), F32),
            pltpu.VMEM((nchunks, LANES, SSD_INNER), F32),
            pltpu.VMEM((nchunks, 1, SSD_INNER), F32),
            pltpu.VMEM((seq, SSD_INNER), F32),
            pltpu.VMEM((seq, SSD_BC_W), BF16),
            pltpu.VMEM((LANES, 2 * SSD_INNER), BF16),
            pltpu.VMEM((2, SSD_CHUNK, SSD_CHUNK), BF16),
            pltpu.VMEM((LANES, 2 * SSD_INNER), F32),
        ],
        compiler_params=pltpu.CompilerParams(
            dimension_semantics=("arbitrary",), vmem_limit_bytes=VMEM_LIMIT),
    )(*args)


def _post_kernel(alpha, d_ff, ff_chunk, x_ref, oa_ref, os_ref, g1_ref, sh2_ref, sc2_ref, g2_ref,
                 wo_ref, wfi_ref, wfo_ref, lng_ref, lnb_ref, y_ref):
    x = x_ref[...]
    wa = oa_ref.shape[1]
    o = _dot(oa_ref[...].astype(BF16), wo_ref[0:wa, :]) + _dot(os_ref[...].astype(BF16), wo_ref[wa:, :])
    x1 = _layer_norm(alpha * x + g1_ref[...] * o, lng_ref[0:1, :], lnb_ref[0:1, :])
    h2 = (x1 * (1.0 + sc2_ref[...]) + sh2_ref[...]).astype(BF16)
    f = jnp.zeros(x.shape, F32)
    for c in range(d_ff // ff_chunk):
        lo = c * ff_chunk
        g = _dot(h2, wfi_ref[:, lo:lo + ff_chunk])
        u = _dot(h2, wfi_ref[:, d_ff + lo:d_ff + lo + ff_chunk])
        f = f + _dot((_silu(g) * u).astype(BF16), wfo_ref[lo:lo + ff_chunk, :])
    y_ref[...] = _layer_norm(alpha * x1 + g2_ref[...] * f, lng_ref[1:2, :], lnb_ref[1:2, :])


def _post(x, oa, os_, mods, layer, row_fn, wo_bf, wfi_bf, wfo_bf, lng, lnb, alpha, tm):
    t, d = x.shape
    d_ff = wfo_bf.shape[1]
    ff_chunk = d_ff // 2
    per_layer = lambda a, **kw: pl.BlockSpec((None,) + a.shape[1:], lambda i: (layer, 0, 0), **kw)
    resident = lambda a: per_layer(a, pipeline_mode=pl.Buffered(1))
    return pl.pallas_call(
        functools.partial(_post_kernel, alpha, d_ff, ff_chunk),
        grid=(t // tm,),
        in_specs=[
            pl.BlockSpec((tm, d), lambda i: (i, 0)),
            pl.BlockSpec((tm, oa.shape[1]), lambda i: (i, 0)),
            pl.BlockSpec((tm, os_.shape[1]), lambda i: (i, 0)),
            _mod_spec(layer, 2, row_fn, d),
            _mod_spec(layer, 3, row_fn, d),
            _mod_spec(layer, 4, row_fn, d),
            _mod_spec(layer, 5, row_fn, d),
            resident(wo_bf), resident(wfi_bf), resident(wfo_bf),
            per_layer(lng), per_layer(lnb),
        ],
        out_specs=pl.BlockSpec((tm, d), lambda i: (i, 0)),
        out_shape=jax.ShapeDtypeStruct((t, d), F32),
        compiler_params=pltpu.CompilerParams(
            dimension_semantics=("arbitrary",), vmem_limit_bytes=VMEM_LIMIT),
    )(x, oa, os_, mods, mods, mods, mods, wo_bf, wfi_bf, wfo_bf, lng, lnb)


def _rope_tables(rows, dim, copies):
    row = jnp.repeat(jnp.arange(rows), GRID_W).astype(F32)
    col = jnp.tile(jnp.arange(GRID_W), rows).astype(F32)
    n_freq = dim // 4
    inv = ROPE_THETA ** (-jnp.arange(n_freq, dtype=F32) / n_freq)
    ang = jnp.concatenate([row[:, None] * inv, col[:, None] * inv], -1)
    cos, sin = jnp.cos(ang), jnp.sin(ang)
    cos_full = jnp.repeat(cos, 2, axis=-1)
    sin_signed = jnp.stack([-sin, sin], axis=-1).reshape(sin.shape[0], dim)
    return jnp.tile(cos_full, (1, copies)), jnp.tile(sin_signed, (1, copies))


def kernel(x_prompt, x_sample, cache_diff_k, cache_diff_v, cache_gqa_k, cache_gqa_v, state_ssd_fwd, state_ssd_bwd, c, c_ctx, w_ada, b_ada, w_in, w_out, diff_lambda, diff_subln_g, qk_norm_g, ssd_conv_w, ssd_conv_b, ssd_A_log, ssd_dt_bias, ssd_D, ssd_norm_g, ln_g, ln_b, w_ffn_in, w_ffn_out):
    batch, seq, d = x_prompt.shape
    dec_batch, dec_seq, _ = x_sample.shape
    depth = w_in.shape[0]
    past = cache_diff_k.shape[2]
    alpha = (2 * depth) ** 0.25
    rows = dec_seq // GRID_W

    n_vec = 1 + dec_batch
    n_pad = -(-n_vec // 8) * 8
    cvec = jnp.concatenate([c_ctx[None, :], c, jnp.zeros((n_pad - n_vec, d), F32)], axis=0)
    mods = _modulation(cvec, w_ada, b_ada).reshape(depth, n_pad, 1, 6 * d)

    cos_d, sin_d = _rope_tables(rows, DIFF_QK, DIFF_W // DIFF_QK)
    cos_g, sin_g = _rope_tables(rows, GQA_HD, GQA_Q_HEADS)
    cosq = jnp.concatenate([cos_d, cos_g], axis=-1)
    sinq = jnp.concatenate([sin_d, sin_g], axis=-1)
    cosk = jnp.concatenate([cos_d, cos_g[:, :GQA_KV_W]], axis=-1)
    sink = jnp.concatenate([sin_d, sin_g[:, :GQA_KV_W]], axis=-1)
    tables = (cosq, sinq, cosk, sink)

    feature_major = lambda a, w: jnp.transpose(a.reshape(dec_batch, depth, past, w), (0, 1, 3, 2))
    caches = (feature_major(cache_diff_k, DIFF_W), feature_major(cache_diff_v, DIFF_W),
              feature_major(cache_gqa_k, GQA_KV_W), feature_major(cache_gqa_v, GQA_KV_W))

    xp = x_prompt.reshape(batch * seq, d)
    xs = x_sample.reshape(dec_batch * dec_seq, d)
    tm_ctx = min(512, batch * seq)
    tm_lat = min(512, dec_seq)
    tq = min(256, dec_seq)
    ctx_row = lambda i: 0
    lat_row = lambda i: 1 + (i * tm_lat) // dec_seq

    w_in_bf = jnp.pad(w_in, ((0, 0), (0, 0), (0, IN_W_PAD - w_in.shape[2]))).astype(BF16)
    wo_bf = w_out.astype(BF16)
    wfi_bf = w_ffn_in.astype(BF16)
    wfo_bf = w_ffn_out.astype(BF16)
    init = (state_ssd_fwd.reshape(dec_batch, depth, SSD_INNER, SSD_STATE),
            state_ssd_bwd.reshape(dec_batch, depth, SSD_INNER, SSD_STATE))

    cache = lambda w: jnp.zeros((batch, depth, w, seq), F32)
    kv_cache = (cache(DIFF_W), cache(DIFF_W), cache(GQA_KV_W))
    gk_cache = (cache(GQA_KV_W),)
    ssd_states = tuple(jnp.zeros((batch, depth, SSD_INNER, SSD_STATE), F32) for _ in range(2))
    for l in range(depth):
        lam_init = 0.8 - 0.6 * math.exp(-0.3 * l)
        lam_p = diff_lambda[l]
        attn_gain = jnp.concatenate([jnp.tile(diff_subln_g[l], DIFF_HEADS), jnp.ones((GQA_W,), F32)])[None, :]
        gq_t = jnp.tile(qk_norm_g[l, 0], GQA_Q_HEADS)[None, :]
        gk_t = jnp.tile(qk_norm_g[l, 1], GQA_KV_HEADS)[None, :]
        pad_row = lambda v: jnp.pad(v.reshape(1, -1), ((0, 0), (0, LANES - v.size)))
        ssd_params = (ssd_conv_w[l], ssd_conv_b[l][None, :], pad_row(ssd_A_log[l]),
                      pad_row(ssd_dt_bias[l]), jnp.repeat(ssd_D[l], SSD_HD)[None, :],
                      ssd_norm_g[l][None, :])

        qa, dk, gk, z, xbc, dt, *kv_cache = _inproj(xp, mods, l, ctx_row, w_in_bf, tm_ctx,
                                                    cache_prev=tuple(kv_cache), depth=depth, seq=seq)
        oa, gkn = _attention(qa, (dk, kv_cache[1], gk, kv_cache[2]), None, l, None, lam_p, attn_gain,
                             gq_t, gk_t, lam_init, batch, seq, seq, gkn_prev=gk_cache)
        gk_cache = (gkn,)
        os_, *ssd_states = _ssd(z, xbc, dt, ssd_params, None, l, batch, seq,
                                state_prev=tuple(ssd_states), depth=depth)
        xp = _post(xp, oa, os_, mods, l, ctx_row, wo_bf, wfi_bf, wfo_bf, ln_g, ln_b, alpha, tm_ctx)

        qa, dk, dv, gk, gv, z, xbc, dt = _inproj(xs, mods, l, lat_row, w_in_bf, tm_lat)
        (oa,) = _attention(qa, (dk, dv, gk, gv), caches, l, tables, lam_p, attn_gain, gq_t, gk_t,
                           lam_init, dec_batch, dec_seq, tq)
        os_, _, _ = _ssd(z, xbc, dt, ssd_params, init, l, dec_batch, dec_seq)
        xs = _post(xs, oa, os_, mods, l, lat_row, wo_bf, wfi_bf, wfo_bf, ln_g, ln_b, alpha, tm_lat)

    def token_major(a, heads, width):
        return jnp.transpose(a.reshape(batch, depth, heads, width, seq), (0, 1, 4, 2, 3))

    state = lambda a: a.reshape(batch, depth, SSD_HEADS, SSD_HD, SSD_STATE)
    return (xp.reshape(batch, seq, d), xs.reshape(dec_batch, dec_seq, d),
            token_major(kv_cache[0], DIFF_HEADS, 2 * DIFF_QK), token_major(kv_cache[1], DIFF_HEADS, DIFF_V),
            token_major(gk_cache[0], GQA_KV_HEADS, GQA_HD), token_major(kv_cache[2], GQA_KV_HEADS, GQA_HD),
            state(ssd_states[0]), state(ssd_states[1]))
```

```python
import functools
import math

import jax
import jax.numpy as jnp
from jax import lax
from jax.experimental import pallas as pl
from jax.experimental.pallas import tpu as pltpu

F32 = jnp.float32
BF16 = jnp.bfloat16

GRID_W = 64
DIFF_HEADS = 4
DIFF_QK = 32
DIFF_V = 64
DIFF_W = DIFF_HEADS * DIFF_V
GQA_HD = 64
GQA_Q_HEADS = 4
GQA_KV_HEADS = 2
GQA_W = GQA_Q_HEADS * GQA_HD
GQA_KV_W = GQA_KV_HEADS * GQA_HD
SSD_HD = 64
SSD_HEADS = 8
SSD_INNER = SSD_HEADS * SSD_HD
SSD_GROUPS = 2
SSD_STATE = 64
SSD_BC_W = SSD_GROUPS * SSD_STATE
SSD_CONV = 5
SSD_CHUNK = 128
XBC_W = SSD_INNER + 2 * SSD_BC_W
DT_W = 2 * SSD_HEADS
ROPE_THETA = 10000.0
EPS = 1e-5
LANES = 128
VMEM_LIMIT = 56 * 1024 * 1024

_C_DQ, _C_DK, _C_DV, _C_GQ, _C_GK, _C_GV, _C_Z, _C_XBC, _C_DT, _C_END = (
    0, 256, 512, 768, 1024, 1152, 1280, 1792, 2560, 2576)
IN_W_PAD = _C_DT + LANES


def _dot(a, b):
    return jnp.dot(a, b, preferred_element_type=F32)


def _dot_nt(a, b):
    return lax.dot_general(a, b, (((1,), (1,)), ((), ())), preferred_element_type=F32)


def _split3(a):
    a1 = a.astype(BF16)
    r1 = a - a1.astype(F32)
    a2 = r1.astype(BF16)
    a3 = (r1 - a2.astype(F32)).astype(BF16)
    return a1, a2, a3


def _dot3_l(a, b_exact):
    a1, a2, a3 = _split3(a)
    return _dot(a1, b_exact) + (_dot(a2, b_exact) + _dot(a3, b_exact))


def _sigmoid(x):
    return 1.0 / (1.0 + jnp.exp(-x))


def _silu(x):
    return x * _sigmoid(x)


def _layer_norm(x, g, b):
    mu = jnp.mean(x, axis=-1, keepdims=True)
    xc = x - mu
    var = jnp.mean(xc * xc, axis=-1, keepdims=True)
    return xc * lax.rsqrt(var + EPS) * g + b


def _group_avg_matrix(width, group):
    sh = int(math.log2(group))
    r = lax.shift_right_logical(lax.broadcasted_iota(jnp.int32, (width, width), 0), sh)
    c = lax.shift_right_logical(lax.broadcasted_iota(jnp.int32, (width, width), 1), sh)
    return jnp.where(r == c, 1.0 / group, 0.0).astype(BF16)


def _group_mean_sq(x, gmat):
    xx = x * x
    hi = xx.astype(BF16)
    lo = (xx - hi.astype(F32)).astype(BF16)
    return _dot(hi, gmat) + _dot(lo, gmat)


def _rope(x, cos, sin_signed):
    w = x.shape[-1]
    lane = lax.broadcasted_iota(jnp.int32, x.shape, 1)
    nxt = pltpu.roll(x, w - 1, 1)
    prv = pltpu.roll(x, 1, 1)
    partner = jnp.where((lane & 1) == 0, nxt, prv)
    return x * cos + partner * sin_signed


def _lane_mask(shape, lo, hi):
    lane = lax.broadcasted_iota(jnp.int32, shape, 1)
    return (lane >= lo) & (lane < hi)


def _mod_kernel(c_ref, w_ref, b_ref, o_ref):
    a = _silu(c_ref[...])
    a_hi = a.astype(BF16)
    a_lo = (a - a_hi.astype(F32)).astype(BF16)
    w = w_ref[...]
    w_hi = w.astype(BF16)
    w_lo = (w - w_hi.astype(F32)).astype(BF16)
    o_ref[...] = _dot(a_hi, w_hi) + (_dot(a_lo, w_hi) + _dot(a_hi, w_lo)) + b_ref[...]


def _modulation(cvec, w_ada, b_ada):
    depth, d, n = w_ada.shape
    tn = 1536
    rows = cvec.shape[0]
    return pl.pallas_call(
        _mod_kernel,
        grid=(depth, n // tn),
        in_specs=[
            pl.BlockSpec((rows, d), lambda l, j: (0, 0)),
            pl.BlockSpec((None, d, tn), lambda l, j: (l, 0, j)),
            pl.BlockSpec((None, 1, tn), lambda l, j: (l, 0, j)),
        ],
        out_specs=pl.BlockSpec((None, rows, tn), lambda l, j: (l, 0, j)),
        out_shape=jax.ShapeDtypeStruct((depth, rows, n), F32),
        compiler_params=pltpu.CompilerParams(
            dimension_semantics=("arbitrary", "arbitrary"), vmem_limit_bytes=VMEM_LIMIT),
    )(cvec, w_ada, b_ada.reshape(depth, 1, n))


def _inproj_kernel(feature_major_cache, n_prev, *refs):
    x_ref, sh_ref, sc_ref, w_ref = refs[:4]
    outs = refs[4 + n_prev:]
    h = (x_ref[...] * (1.0 + sc_ref[...]) + sh_ref[...]).astype(BF16)

    def mm(lo, hi):
        return _dot(h, w_ref[:, lo:hi])

    if feature_major_cache:
        (qa_ref, dk_ref, gk_ref, z_ref, xbc_ref, dt_ref, dkt_ref, dvt_ref, gvt_ref,
         dv_s, gv_s) = outs
        dk_ref[...] = mm(_C_DK, _C_DV)
        dv_s[...] = mm(_C_DV, _C_GQ)
        gv_s[...] = mm(_C_GV, _C_Z)
        seq = dkt_ref.shape[-1]
        for j in range(dkt_ref.shape[0]):
            rows = slice(j * seq, (j + 1) * seq)
            dkt_ref[j] = dk_ref[rows, :].T
            dvt_ref[j] = dv_s[rows, :].T
            gvt_ref[j] = gv_s[rows, :].T
    else:
        qa_ref, dk_ref, dv_ref, gk_ref, gv_ref, z_ref, xbc_ref, dt_ref = outs
        dk_ref[...] = mm(_C_DK, _C_DV)
        dv_ref[...] = mm(_C_DV, _C_GQ)
        gv_ref[...] = mm(_C_GV, _C_Z)
    qa_ref[:, 0:DIFF_W] = mm(_C_DQ, _C_DK)
    qa_ref[:, DIFF_W:DIFF_W + GQA_W] = mm(_C_GQ, _C_GK)
    gk_ref[...] = mm(_C_GK, _C_GV)
    z_ref[...] = mm(_C_Z, _C_XBC)
    xbc_ref[...] = mm(_C_XBC, _C_DT)
    dt_ref[...] = mm(_C_DT, IN_W_PAD)


def _mod_spec(layer, which, row_fn, d):
    return pl.BlockSpec((None, None, 1, d), lambda i: (layer, row_fn(i), 0, which))


def _inproj(x, mods, layer, row_fn, w_in_bf, tm, cache_prev=None, depth=None, seq=None):
    t, d = x.shape
    feature_major = cache_prev is not None
    token_spec = lambda w: pl.BlockSpec((tm, w), lambda i: (i, 0))
    token_shape = lambda w: jax.ShapeDtypeStruct((t, w), F32)
    if feature_major:
        widths = (DIFF_W + GQA_W, DIFF_W, GQA_KV_W, SSD_INNER, XBC_W, LANES)
        cache_w = (DIFF_W, DIFF_W, GQA_KV_W)
        assert tm % seq == 0
        out_specs = [token_spec(w) for w in widths] + [
            pl.BlockSpec((tm // seq, None, w, seq), lambda i: (i, layer, 0, 0)) for w in cache_w]
        out_shape = [token_shape(w) for w in widths] + [
            jax.ShapeDtypeStruct((t // seq, depth, w, seq), F32) for w in cache_w]
        aliases = {4 + k: len(widths) + k for k in range(len(cache_prev))}
    else:
        widths = (DIFF_W + GQA_W, DIFF_W, DIFF_W, GQA_KV_W, GQA_KV_W, SSD_INNER, XBC_W, LANES)
        out_specs = [token_spec(w) for w in widths]
        out_shape = [token_shape(w) for w in widths]
        cache_prev, aliases = (), {}
    return pl.pallas_call(
        functools.partial(_inproj_kernel, feature_major, len(cache_prev)),
        grid=(t // tm,),
        in_specs=[
            pl.BlockSpec((tm, d), lambda i: (i, 0)),
            _mod_spec(layer, 0, row_fn, d),
            _mod_spec(layer, 1, row_fn, d),
            pl.BlockSpec((None, d, IN_W_PAD), lambda i: (layer, 0, 0)),
        ] + [pl.BlockSpec(memory_space=pl.ANY)] * len(cache_prev),
        out_specs=out_specs,
        out_shape=out_shape,
        input_output_aliases=aliases,
        scratch_shapes=([pltpu.VMEM((tm, DIFF_W), F32), pltpu.VMEM((tm, GQA_KV_W), F32)]
                        if feature_major else []),
        compiler_params=pltpu.CompilerParams(
            dimension_semantics=("arbitrary",), vmem_limit_bytes=VMEM_LIMIT),
    )(x, mods, mods, w_in_bf, *cache_prev)


LOG2E = 1.4426950408889634
NEG_BIG = -1e30
N_SCORE_HEADS = 2 * DIFF_HEADS + GQA_Q_HEADS
ONES_ROWS = 16


def _diff_lambda(lam_ref, lam_init):
    lp = lam_ref[...]
    s1 = jnp.sum(lp[0:1, :] * lp[1:2, :], axis=-1, keepdims=True)
    s2 = jnp.sum(lp[2:3, :] * lp[3:4, :], axis=-1, keepdims=True)
    return jnp.exp(s1) - jnp.exp(s2) + lam_init


def _swap_halves(x):
    return pltpu.roll(x, GQA_HD, 1)


def _attn_kernel(cfg, *refs):
    lam_init, seq, past, kb, rope, feature_major, n_prev = cfg
    it = iter(refs)
    qa_ref, dk_ref, dv_ref, gk_in_ref, gv_ref = next(it), next(it), next(it), next(it), next(it)
    for _ in range(n_prev):
        next(it)
    if past:
        cdk_ref, cdv_ref, cgk_ref, cgv_ref = next(it), next(it), next(it), next(it)
    if rope:
        cosq_ref, sinq_ref, cosk_ref, sink_ref = next(it), next(it), next(it), next(it)
    lam_ref, gain_ref, gq_ref, gk_ref = next(it), next(it), next(it), next(it)
    o_ref = next(it)
    gkn_ref = next(it) if feature_major else None
    kd_s, kg_s, vdt_s, vgt_s, wq_s, wg_s, s0_s, s1_s, m_s, l_s, acc_s = it
    tq = qa_ref.shape[0]
    n_blocks = (seq + past) // kb

    @pl.when(pl.program_id(1) == 0)
    def _prepare_keys():
        gmat = _group_avg_matrix(GQA_KV_W, GQA_HD)
        for i in range(seq // kb):
            rows = slice(i * kb, (i + 1) * kb)
            dk = dk_ref[rows, :]
            gk = gk_in_ref[rows, :]
            gk = gk * lax.rsqrt(_group_mean_sq(gk, gmat) + EPS) * gk_ref[...]
            if feature_major:
                gkn_ref[:, rows] = gk.T
            if rope:
                ck = cosk_ref[rows, :]
                sk = sink_ref[rows, :]
                dk = _rope(dk, ck[:, 0:DIFF_W], sk[:, 0:DIFF_W])
                gk = _rope(gk, ck[:, DIFF_W:DIFF_W + GQA_KV_W], sk[:, DIFF_W:DIFF_W + GQA_KV_W])
            kd_s[rows, :] = dk.astype(BF16)
            kg_s[rows, :] = gk.astype(BF16)
            if feature_major:
                vdt_s[i] = dv_ref[:, rows].astype(BF16)
                vgt_s[i] = gv_ref[:, rows].astype(BF16)
            else:
                vdt_s[i] = dv_ref[rows, :].T.astype(BF16)
                vgt_s[i] = gv_ref[rows, :].T.astype(BF16)
        for j in range(past // kb):
            src = slice(j * kb, (j + 1) * kb)
            dst = slice(seq + j * kb, seq + (j + 1) * kb)
            kd_s[dst, :] = cdk_ref[:, src].T.astype(BF16)
            kg_s[dst, :] = cgk_ref[:, src].T.astype(BF16)
            vdt_s[seq // kb + j] = cdv_ref[:, src].astype(BF16)
            vgt_s[seq // kb + j] = cgv_ref[:, src].astype(BF16)

    qa = qa_ref[...]
    qd = qa[:, 0:DIFF_W]
    gq = qa[:, DIFF_W:DIFF_W + GQA_W]
    gq = gq * lax.rsqrt(_group_mean_sq(gq, _group_avg_matrix(GQA_W, GQA_HD)) + EPS) * gq_ref[...]
    if rope:
        cq = cosq_ref[...]
        sq = sinq_ref[...]
        qd = _rope(qd, cq[:, 0:DIFF_W], sq[:, 0:DIFF_W])
        gq = _rope(gq, cq[:, DIFF_W:DIFF_W + GQA_W], sq[:, DIFF_W:DIFF_W + GQA_W])
    qd_t = (qd * (DIFF_QK ** -0.5 * LOG2E)).T
    gq_t = (gq * (GQA_HD ** -0.5 * LOG2E)).T
    row = lax.broadcasted_iota(jnp.int32, (DIFF_W, tq), 0)
    for hm in range(2 * DIFF_HEADS):
        lo = hm * DIFF_QK
        wq_s[:, hm * tq:(hm + 1) * tq] = jnp.where((row >= lo) & (row < lo + DIFF_QK), qd_t, 0.0).astype(BF16)
    zeros = jnp.zeros((GQA_HD, tq), F32)
    for h in range(GQA_Q_HEADS):
        piece = gq_t[h * GQA_HD:(h + 1) * GQA_HD, :]
        pair = [piece, zeros] if h // (GQA_Q_HEADS // GQA_KV_HEADS) == 0 else [zeros, piece]
        wg_s[:, h * tq:(h + 1) * tq] = jnp.concatenate(pair, axis=0).astype(BF16)
    m_s[...] = jnp.full(m_s.shape, NEG_BIG, F32)
    l_s[...] = jnp.zeros(l_s.shape, F32)
    acc_s[...] = jnp.zeros(acc_s.shape, F32)
    n_diff = 2 * DIFF_HEADS * tq
    n_all = N_SCORE_HEADS * tq

    slabs = tq // LANES

    def key_rows(j):
        return pl.ds(j * kb if isinstance(j, int) else pl.multiple_of(j * kb, kb), kb)

    def head_scores(idx, k_d, k_g, s_buf):
        if idx < 2 * DIFF_HEADS:
            s = _dot(k_d, wq_s[:, idx * tq:(idx + 1) * tq])
        else:
            h = idx - 2 * DIFF_HEADS
            s = _dot(k_g, wg_s[:, h * tq:(h + 1) * tq])
        for k in range(slabs):
            s_buf[idx * slabs + k] = s[:, k * LANES:(k + 1) * LANES]

    def head_update(idx, v_d, v_g, s_buf):
        ps, alphas = [], []
        for k in range(slabs):
            c = idx * slabs + k
            cols = slice(c * LANES, (c + 1) * LANES)
            s = s_buf[c]
            m_old = m_s[:, cols]
            m_new = jnp.maximum(m_old, jnp.max(s, axis=0, keepdims=True))
            alphas.append(jnp.exp2(m_old - m_new))
            ps.append(jnp.exp2(s - m_new).astype(BF16))
            m_s[:, cols] = m_new
        if idx < 2 * DIFF_HEADS:
            vh = idx // 2
            v_t = v_d[vh * DIFF_V:(vh + 1) * DIFF_V, :]
        else:
            vh = (idx - 2 * DIFF_HEADS) // (GQA_Q_HEADS // GQA_KV_HEADS)
            v_t = v_g[vh * GQA_HD:(vh + 1) * GQA_HD, :]
        v_ext = jnp.concatenate([v_t, jnp.ones((ONES_ROWS, kb), BF16)], axis=0)
        alpha = jnp.concatenate(alphas, axis=1)
        pv = _dot(v_ext, jnp.concatenate(ps, axis=1))
        cols = slice(idx * tq, (idx + 1) * tq)
        acc_s[idx] = alpha * acc_s[idx] + pv[0:GQA_HD, :]
        l_s[:, cols] = alpha * l_s[:, cols] + pv[GQA_HD:GQA_HD + 1, :]

    def key_block(j, s_cur, j_next, s_next):
        v_d = vdt_s[j]
        v_g = vgt_s[j]
        if j_next is not None:
            k_d = kd_s[key_rows(j_next), :]
            k_g = kg_s[key_rows(j_next), :]
        for idx in range(N_SCORE_HEADS):
            if j_next is not None:
                head_scores(idx, k_d, k_g, s_next)
            head_update(idx, v_d, v_g, s_cur)

    for idx in range(N_SCORE_HEADS):
        head_scores(idx, kd_s[key_rows(0), :], kg_s[key_rows(0), :], s0_s)
    if n_blocks > 1:
        def block_pair(i, carry):
            key_block(2 * i, s0_s, 2 * i + 1, s1_s)
            key_block(2 * i + 1, s1_s, 2 * i + 2, s0_s)
            return carry

        lax.fori_loop(0, n_blocks // 2 - 1, block_pair, 0)
        key_block(n_blocks - 2, s0_s, n_blocks - 1, s1_s)
        key_block(n_blocks - 1, s1_s, None, None)
    else:
        key_block(0, s0_s, None, None)

    lam = _diff_lambda(lam_ref, lam_init)
    outs = []
    for h in range(DIFF_HEADS):
        c0 = slice(2 * h * tq, (2 * h + 1) * tq)
        c1 = slice((2 * h + 1) * tq, (2 * h + 2) * tq)
        o = acc_s[2 * h] * (1.0 / l_s[:, c0]) - acc_s[2 * h + 1] * (lam / l_s[:, c1])
        ms = jnp.mean(o * o, axis=0, keepdims=True)
        outs.append(o * (lax.rsqrt(ms + EPS) * (1.0 - lam_init)))
    for h in range(GQA_Q_HEADS):
        idx = 2 * DIFF_HEADS + h
        outs.append(acc_s[idx] * (1.0 / l_s[:, idx * tq:(idx + 1) * tq]))
    o_ref[...] = jnp.concatenate(outs, axis=0).T * gain_ref[...]


def _attention(qa, kv, caches, layer, tables, lam_p, gain, gq_t, gk_t, lam_init, batch, seq, tq,
               gkn_prev=None):
    t = qa.shape[0]
    nq = seq // tq
    feature_major = gkn_prev is not None
    past = caches[0].shape[3] if caches is not None else 0
    kb = min(256, seq)
    assert seq % kb == 0 and past % kb == 0 and seq % tq == 0
    n_blocks = (seq + past) // kb
    assert n_blocks == 1 or n_blocks % 2 == 0
    n_all = N_SCORE_HEADS * tq
    full = lambda a: pl.BlockSpec(a.shape, lambda b, q: (0,) * a.ndim)
    token_major = lambda a: pl.BlockSpec((seq, a.shape[1]), lambda b, q: (b, 0))
    by_feature = lambda a: pl.BlockSpec((None, None, a.shape[2], seq), lambda b, q: (b, layer, 0, 0))
    dk, dv, gk, gv = kv
    in_specs = [pl.BlockSpec((tq, qa.shape[1]), lambda b, q: (b * nq + q, 0)),
                token_major(dk), by_feature(dv) if feature_major else token_major(dv),
                token_major(gk), by_feature(gv) if feature_major else token_major(gv)]
    args = [qa, dk, dv, gk, gv]
    aliases = {}
    if feature_major:
        assert nq == 1
        in_specs += [pl.BlockSpec(memory_space=pl.ANY)] * len(gkn_prev)
        args += list(gkn_prev)
        aliases = {5 + k: 1 + k for k in range(len(gkn_prev))}
    if caches is not None:
        in_specs += [pl.BlockSpec((None, None, a.shape[2], past), lambda b, q: (b, layer, 0, 0))
                     for a in caches]
        args += list(caches)
    if tables is not None:
        cosq, sinq, cosk, sink = tables
        in_specs += [pl.BlockSpec((tq, cosq.shape[1]), lambda b, q: (q, 0)),
                     pl.BlockSpec((tq, sinq.shape[1]), lambda b, q: (q, 0)),
                     full(cosk), full(sink)]
        args += [cosq, sinq, cosk, sink]
    in_specs += [full(lam_p), full(gain), full(gq_t), full(gk_t)]
    args += [lam_p, gain, gq_t, gk_t]
    out_specs = [pl.BlockSpec((tq, DIFF_W + GQA_W), lambda b, q: (b * nq + q, 0))]
    out_shape = [jax.ShapeDtypeStruct((t, DIFF_W + GQA_W), F32)]
    if feature_major:
        out_specs.append(pl.BlockSpec((None, None, GQA_KV_W, seq), lambda b, q: (b, layer, 0, 0)))
        out_shape.append(jax.ShapeDtypeStruct((batch, dv.shape[1], GQA_KV_W, seq), F32))
    cfg = (lam_init, seq, past, kb, tables is not None, feature_major,
           len(gkn_prev) if feature_major else 0)
    return pl.pallas_call(
        functools.partial(_attn_kernel, cfg),
        grid=(batch, nq),
        in_specs=in_specs,
        out_specs=out_specs,
        out_shape=out_shape,
        input_output_aliases=aliases,
        scratch_shapes=[
            pltpu.VMEM((seq + past, DIFF_W), BF16), pltpu.VMEM((seq + past, GQA_KV_W), BF16),
            pltpu.VMEM((n_blocks, DIFF_W, kb), BF16), pltpu.VMEM((n_blocks, GQA_KV_W, kb), BF16),
            pltpu.VMEM((DIFF_W, 2 * DIFF_HEADS * tq), BF16), pltpu.VMEM((GQA_KV_W, GQA_Q_HEADS * tq), BF16),
            pltpu.VMEM((n_all // LANES, kb, LANES), F32), pltpu.VMEM((n_all // LANES, kb, LANES), F32),
            pltpu.VMEM((1, n_all), F32), pltpu.VMEM((1, n_all), F32),
            pltpu.VMEM((N_SCORE_HEADS, GQA_HD, tq), F32),
        ],
        compiler_params=pltpu.CompilerParams(
            dimension_semantics=("arbitrary", "arbitrary"), vmem_limit_bytes=VMEM_LIMIT),
    )(*args)


def _softplus(x):
    return jnp.maximum(x, 0.0) + jnp.log1p(jnp.exp(-jnp.abs(x)))


def _dot2_l(a, b_exact):
    a1 = a.astype(BF16)
    a2 = (a - a1.astype(F32)).astype(BF16)
    return _dot(a1, b_exact) + _dot(a2, b_exact)


def _ssd_kernel(has_init, n_prev, seq, *refs):
    it = iter(refs)
    z_ref, xbc_ref, dt_ref, cw_ref, cb_ref, alog_ref, dtb_ref, dexp_ref, ng_ref = (
        next(it) for _ in range(9))
    sf0_ref, sb0_ref = (next(it), next(it)) if has_init else (None, None)
    for _ in range(n_prev):
        next(it)
    o_ref, sf_ref, sb_ref = next(it), next(it), next(it)
    stf_s, stb_s, inc_s, dec_s, eab_s, cbf_s, exp_s, tri_s, gm_s = it
    L = SSD_CHUNK
    W2 = 2 * SSD_INNER
    nchunks = seq // L
    halo = 8

    ri = lax.broadcasted_iota(jnp.int32, (L, L), 0)
    ci = lax.broadcasted_iota(jnp.int32, (L, L), 1)
    lower = ri >= ci
    upper = ri <= ci
    tri_s[0] = jnp.where(lower, 1.0, 0.0).astype(BF16)
    tri_s[1] = jnp.where(upper, 1.0, 0.0).astype(BF16)
    lane_row = lax.broadcasted_iota(jnp.int32, (1, LANES), 1)
    a_row = jnp.where(lane_row < DT_W, -jnp.exp(alog_ref[...]), 0.0)
    ej = lax.broadcasted_iota(jnp.int32, (LANES, W2), 0)
    eh = lax.shift_right_logical(lax.broadcasted_iota(jnp.int32, (LANES, W2), 1), 6)
    exp_s[...] = jnp.where(ej == eh, 1.0, 0.0).astype(BF16)
    gm_s[...] = jnp.where(lax.shift_right_logical(ej, 6) == (lax.shift_right_logical(eh, 2) & 1), 1.0, 0.0)

    def chunk_rows(c):
        return pl.ds(pl.multiple_of(c * L, L), L)

    def load_state(ref):
        r = lax.broadcasted_iota(jnp.int32, (SSD_STATE, LANES), 0)
        c = lax.broadcasted_iota(jnp.int32, (SSD_STATE, LANES), 1)
        dup = jnp.where((c & (SSD_STATE - 1)) == r, 1.0, 0.0).astype(BF16)
        return _dot3_l(ref[...], dup).T * gm_s[:, 0:SSD_INNER]

    def store_state(st_ref, ref):
        st_t = st_ref[...].T
        ref[...] = (st_t + _swap_halves(st_t))[:, 0:SSD_STATE]

    stf_s[...] = load_state(sf0_ref) if has_init else jnp.zeros(stf_s.shape, F32)
    stb_s[...] = load_state(sb0_ref) if has_init else jnp.zeros(stb_s.shape, F32)

    def forward_pass(c, carry):
        r0 = c * L
        rows = chunk_rows(c)
        dt_c = _softplus(dt_ref[rows, :] + dtb_ref[...])

        d1, d2, d3 = _split3(dt_c * a_row)
        acs_f = _dot(tri_s[0], d1) + (_dot(tri_s[0], d2) + _dot(tri_s[0], d3))
        acs_b = _dot(tri_s[1], d1) + (_dot(tri_s[1], d2) + _dot(tri_s[1], d3))
        acs = jnp.where(lax.broadcasted_iota(jnp.int32, (L, LANES), 1) < SSD_HEADS, acs_f, acs_b)
        acs_t = acs.T
        expand = exp_s[...]
        dt_e = _dot2_l(dt_c, expand)
        acs_e = _dot3_l(acs, expand)
        edge = jnp.concatenate([acs_e[L - 1:L, 0:SSD_INNER], acs_e[0:1, SSD_INNER:W2]], axis=1)
        eacs = jnp.exp(acs_e)
        cdec = jnp.exp(edge)

        prev = xbc_ref[pl.ds(pl.multiple_of(jnp.maximum(r0 - halo, 0), halo), halo), :]
        nxt = xbc_ref[pl.ds(pl.multiple_of(jnp.minimum(r0 + L, seq - halo), halo), halo), :]
        cur = xbc_ref[rows, :]
        win = jnp.concatenate([jnp.where(c > 0, prev, 0.0), cur,
                               jnp.where(c < nchunks - 1, nxt, 0.0)], axis=0)
        acc = cb_ref[...] + cur * cw_ref[SSD_CONV // 2:SSD_CONV // 2 + 1, :]
        for j in range(SSD_CONV):
            if j != SSD_CONV // 2:
                shifted = pltpu.roll(win, (SSD_CONV // 2 - j) % (L + 2 * halo), 0)[halo:halo + L, :]
                acc = acc + shifted * cw_ref[j:j + 1, :]
        act = _silu(acc)
        x_c = act[:, 0:SSD_INNER]
        b_c = act[:, SSD_INNER:SSD_INNER + SSD_BC_W]
        c_c = act[:, SSD_INNER + SSD_BC_W:XBC_W]
        xd = jnp.concatenate([x_c, x_c], axis=1) * dt_e
        xd_b = xd.astype(BF16)
        xdw = (xd * jnp.exp(edge - acs_e)).astype(BF16)
        b_b = b_c.astype(BF16)
        c_b = c_c.astype(BF16)
        s_new = _dot(b_c.T.astype(BF16), xdw) * gm_s[...]

        st_f = stf_s[...]
        y = _dot(c_b, st_f.astype(BF16)) * eacs[:, 0:SSD_INNER]
        stf_s[...] = st_f * cdec[:, 0:SSD_INNER] + s_new[:, 0:SSD_INNER]
        inc_s[c] = s_new[:, SSD_INNER:W2]
        dec_s[c] = cdec[:, SSD_INNER:W2]
        eab_s[rows, :] = eacs[:, SSD_INNER:W2]
        cbf_s[rows, :] = c_b

        cbs = []
        for g in range(SSD_GROUPS):
            cg = jnp.where(_lane_mask(c_c.shape, g * SSD_STATE, (g + 1) * SSD_STATE), c_c, 0.0)
            cbs.append(_dot_nt(cg.astype(BF16), b_b))
        for direction, causal in ((0, lower), (1, upper)):
            pairs = []
            for g in range(SSD_GROUPS):
                cb = cbs[g]
                for hp in range(2):
                    pair = g * 2 + hp
                    res = []
                    for k in range(2):
                        j = direction * SSD_HEADS + pair * 2 + k
                        diff = acs[:, j:j + 1] - acs_t[j:j + 1, :]
                        dec = jnp.where(causal, jnp.exp(jnp.minimum(diff, 0.0)), 0.0)
                        sc = (cb * dec).astype(BF16)
                        lo = direction * SSD_INNER + pair * LANES
                        res.append(_dot(sc, xd_b[:, lo:lo + LANES]))
                    pairs.append(jnp.where(_lane_mask(res[0].shape, 0, SSD_HD), res[0], res[1]))
            y = y + jnp.concatenate(pairs, axis=-1)
        o_ref[rows, :] = y + x_c * dexp_ref[...]
        return carry

    lax.fori_loop(0, nchunks, forward_pass, 0)
    store_state(stf_s, sf_ref)

    def backward_pass(i, carry):
        c = nchunks - 1 - i
        rows = chunk_rows(c)
        st_b = stb_s[...]
        y = o_ref[rows, :] + _dot(cbf_s[rows, :], st_b.astype(BF16)) * eab_s[rows, :]
        stb_s[...] = st_b * dec_s[c] + inc_s[c]
        yt = y * _silu(z_ref[rows, :])
        ms = jnp.mean(yt * yt, axis=-1, keepdims=True)
        o_ref[rows, :] = yt * lax.rsqrt(ms + EPS) * ng_ref[...]
        return carry

    lax.fori_loop(0, nchunks, backward_pass, 0, unroll=2)
    store_state(stb_s, sb_ref)


def _ssd(z, xbc, dt, params, init, layer, batch, seq, state_prev=(), depth=1):
    cw, cb, alog, dtb, dexp, ng = params
    t = z.shape[0]
    has_init = init is not None
    nchunks = seq // SSD_CHUNK
    full = lambda a: pl.BlockSpec(a.shape, lambda b: (0,) * a.ndim)
    out_layer = layer if depth > 1 else 0
    st_spec = pl.BlockSpec((None, None, SSD_INNER, SSD_STATE), lambda b: (b, out_layer, 0, 0))
    in_specs = [
        pl.BlockSpec((seq, SSD_INNER), lambda b: (b, 0)),
        pl.BlockSpec((seq, XBC_W), lambda b: (b, 0)),
        pl.BlockSpec((seq, LANES), lambda b: (b, 0)),
        full(cw), full(cb), full(alog), full(dtb), full(dexp), full(ng),
    ]
    args = [z, xbc, dt, cw, cb, alog, dtb, dexp, ng]
    if has_init:
        init_spec = pl.BlockSpec((None, None, SSD_INNER, SSD_STATE), lambda b: (b, layer, 0, 0))
        in_specs += [init_spec, init_spec]
        args += list(init)
    aliases = {len(args) + k: 1 + k for k in range(len(state_prev))}
    in_specs += [pl.BlockSpec(memory_space=pl.ANY)] * len(state_prev)
    args += list(state_prev)
    st_shape = jax.ShapeDtypeStruct((batch, depth, SSD_INNER, SSD_STATE), F32)
    return pl.pallas_call(
        functools.partial(_ssd_kernel, has_init, len(state_prev), seq),
        grid=(batch,),
        in_specs=in_specs,
        out_specs=[pl.BlockSpec((seq, SSD_INNER), lambda b: (b, 0)), st_spec, st_spec],
        out_shape=[jax.ShapeDtypeStruct((t, SSD_INNER), F32), st_shape, st_shape],
        input_output_aliases=aliases,
        scratch_shapes=[
            pltpu.VMEM((LANES, SSD_INNER), F32), pltpu.VMEM((LANES, SSD_INNER), F32),
            pltpu.VMEM((nchunks, LANES, SSD_INNER), F32),
            pltpu.VMEM((nchunks, 1, SSD_INNER), F32),
            pltpu.VMEM((seq, SSD_INNER), F32),
            pltpu.VMEM((seq, SSD_BC_W), BF16),
            pltpu.VMEM((LANES, 2 * SSD_INNER), BF16),
            pltpu.VMEM((2, SSD_CHUNK, SSD_CHUNK), BF16),
            pltpu.VMEM((LANES, 2 * SSD_INNER), F32),
        ],
        compiler_params=pltpu.CompilerParams(
            dimension_semantics=("arbitrary",), vmem_limit_bytes=VMEM_LIMIT),
    )(*args)


def _post_kernel(alpha, d_ff, ff_chunk, x_ref, oa_ref, os_ref, g1_ref, sh2_ref, sc2_ref, g2_ref,
                 wo_ref, wfi_ref, wfo_ref, lng_ref, lnb_ref, y_ref):
    x = x_ref[...]
    wa = oa_ref.shape[1]
    o = _dot(oa_ref[...].astype(BF16), wo_ref[0:wa, :]) + _dot(os_ref[...].astype(BF16), wo_ref[wa:, :])
    x1 = _layer_norm(alpha * x + g1_ref[...] * o, lng_ref[0:1, :], lnb_ref[0:1, :])
    h2 = (x1 * (1.0 + sc2_ref[...]) + sh2_ref[...]).astype(BF16)
    f = jnp.zeros(x.shape, F32)
    for c in range(d_ff // ff_chunk):
        lo = c * ff_chunk
        g = _dot(h2, wfi_ref[:, lo:lo + ff_chunk])
        u = _dot(h2, wfi_ref[:, d_ff + lo:d_ff + lo + ff_chunk])
        f = f + _dot((_silu(g) * u).astype(BF16), wfo_ref[lo:lo + ff_chunk, :])
    y_ref[...] = _layer_norm(alpha * x1 + g2_ref[...] * f, lng_ref[1:2, :], lnb_ref[1:2, :])


def _post(x, oa, os_, mods, layer, row_fn, wo_bf, wfi_bf, wfo_bf, lng, lnb, alpha, tm):
    t, d = x.shape
    d_ff = wfo_bf.shape[1]
    ff_chunk = d_ff // 2
    per_layer = lambda a, **kw: pl.BlockSpec((None,) + a.shape[1:], lambda i: (layer, 0, 0), **kw)
    resident = lambda a: per_layer(a, pipeline_mode=pl.Buffered(1))
    return pl.pallas_call(
        functools.partial(_post_kernel, alpha, d_ff, ff_chunk),
        grid=(t // tm,),
        in_specs=[
            pl.BlockSpec((tm, d), lambda i: (i, 0)),
            pl.BlockSpec((tm, oa.shape[1]), lambda i: (i, 0)),
            pl.BlockSpec((tm, os_.shape[1]), lambda i: (i, 0)),
            _mod_spec(layer, 2, row_fn, d),
            _mod_spec(layer, 3, row_fn, d),
            _mod_spec(layer, 4, row_fn, d),
            _mod_spec(layer, 5, row_fn, d),
            resident(wo_bf), resident(wfi_bf), resident(wfo_bf),
            per_layer(lng), per_layer(lnb),
        ],
        out_specs=pl.BlockSpec((tm, d), lambda i: (i, 0)),
        out_shape=jax.ShapeDtypeStruct((t, d), F32),
        compiler_params=pltpu.CompilerParams(
            dimension_semantics=("arbitrary",), vmem_limit_bytes=VMEM_LIMIT),
    )(x, oa, os_, mods, mods, mods, mods, wo_bf, wfi_bf, wfo_bf, lng, lnb)


def _rope_tables(rows, dim, copies):
    row = jnp.repeat(jnp.arange(rows), GRID_W).astype(F32)
    col = jnp.tile(jnp.arange(GRID_W), rows).astype(F32)
    n_freq = dim // 4
    inv = ROPE_THETA ** (-jnp.arange(n_freq, dtype=F32) / n_freq)
    ang = jnp.concatenate([row[:, None] * inv, col[:, None] * inv], -1)
    cos, sin = jnp.cos(ang), jnp.sin(ang)
    cos_full = jnp.repeat(cos, 2, axis=-1)
    sin_signed = jnp.stack([-sin, sin], axis=-1).reshape(sin.shape[0], dim)
    return jnp.tile(cos_full, (1, copies)), jnp.tile(sin_signed, (1, copies))


def kernel(x_prompt, x_sample, cache_diff_k, cache_diff_v, cache_gqa_k, cache_gqa_v, state_ssd_fwd, state_ssd_bwd, c, c_ctx, w_ada, b_ada, w_in, w_out, diff_lambda, diff_subln_g, qk_norm_g, ssd_conv_w, ssd_conv_b, ssd_A_log, ssd_dt_bias, ssd_D, ssd_norm_g, ln_g, ln_b, w_ffn_in, w_ffn_out):
    batch, seq, d = x_prompt.shape
    dec_batch, dec_seq, _ = x_sample.shape
    depth = w_in.shape[0]
    past = cache_diff_k.shape[2]
    alpha = (2 * depth) ** 0.25
    rows = dec_seq // GRID_W

    n_vec = 1 + dec_batch
    n_pad = -(-n_vec // 8) * 8
    cvec = jnp.concatenate([c_ctx[None, :], c, jnp.zeros((n_pad - n_vec, d), F32)], axis=0)
    mods = _modulation(cvec, w_ada, b_ada).reshape(depth, n_pad, 1, 6 * d)

    cos_d, sin_d = _rope_tables(rows, DIFF_QK, DIFF_W // DIFF_QK)
    cos_g, sin_g = _rope_tables(rows, GQA_HD, GQA_Q_HEADS)
    cosq = jnp.concatenate([cos_d, cos_g], axis=-1)
    sinq = jnp.concatenate([sin_d, sin_g], axis=-1)
    cosk = jnp.concatenate([cos_d, cos_g[:, :GQA_KV_W]], axis=-1)
    sink = jnp.concatenate([sin_d, sin_g[:, :GQA_KV_W]], axis=-1)
    tables = (cosq, sinq, cosk, sink)

    feature_major = lambda a, w: jnp.transpose(a.reshape(dec_batch, depth, past, w), (0, 1, 3, 2))
    caches = (feature_major(cache_diff_k, DIFF_W), feature_major(cache_diff_v, DIFF_W),
              feature_major(cache_gqa_k, GQA_KV_W), feature_major(cache_gqa_v, GQA_KV_W))

    xp = x_prompt.reshape(batch * seq, d)
    xs = x_sample.reshape(dec_batch * dec_seq, d)
    tm_ctx = min(512, batch * seq)
    tm_lat = min(512, dec_seq)
    tq = min(256, dec_seq)
    ctx_row = lambda i: 0
    lat_row = lambda i: 1 + (i * tm_lat) // dec_seq

    w_in_bf = jnp.pad(w_in, ((0, 0), (0, 0), (0, IN_W_PAD - w_in.shape[2]))).astype(BF16)
    wo_bf = w_out.astype(BF16)
    wfi_bf = w_ffn_in.astype(BF16)
    wfo_bf = w_ffn_out.astype(BF16)
    init = (state_ssd_fwd.reshape(dec_batch, depth, SSD_INNER, SSD_STATE),
            state_ssd_bwd.reshape(dec_batch, depth, SSD_INNER, SSD_STATE))

    cache = lambda w: jnp.zeros((batch, depth, w, seq), F32)
    kv_cache = (cache(DIFF_W), cache(DIFF_W), cache(GQA_KV_W))
    gk_cache = (cache(GQA_KV_W),)
    ssd_states = tuple(jnp.zeros((batch, depth, SSD_INNER, SSD_STATE), F32) for _ in range(2))
    for l in range(depth):
        lam_init = 0.8 - 0.6 * math.exp(-0.3 * l)
        lam_p = diff_lambda[l]
        attn_gain = jnp.concatenate([jnp.tile(diff_subln_g[l], DIFF_HEADS), jnp.ones((GQA_W,), F32)])[None, :]
        gq_t = jnp.tile(qk_norm_g[l, 0], GQA_Q_HEADS)[None, :]
        gk_t = jnp.tile(qk_norm_g[l, 1], GQA_KV_HEADS)[None, :]
        pad_row = lambda v: jnp.pad(v.reshape(1, -1), ((0, 0), (0, LANES - v.size)))
        ssd_params = (ssd_conv_w[l], ssd_conv_b[l][None, :], pad_row(ssd_A_log[l]),
                      pad_row(ssd_dt_bias[l]), jnp.repeat(ssd_D[l], SSD_HD)[None, :],
                      ssd_norm_g[l][None, :])

        qa, dk, gk, z, xbc, dt, *kv_cache = _inproj(xp, mods, l, ctx_row, w_in_bf, tm_ctx,
                                                    cache_prev=tuple(kv_cache), depth=depth, seq=seq)
        oa, gkn = _attention(qa, (dk, kv_cache[1], gk, kv_cache[2]), None, l, None, lam_p, attn_gain,
                             gq_t, gk_t, lam_init, batch, seq, seq, gkn_prev=gk_cache)
        gk_cache = (gkn,)
        os_, *ssd_states = _ssd(z, xbc, dt, ssd_params, None, l, batch, seq,
                                state_prev=tuple(ssd_states), depth=depth)
        xp = _post(xp, oa, os_, mods, l, ctx_row, wo_bf, wfi_bf, wfo_bf, ln_g, ln_b, alpha, tm_ctx)

        qa, dk, dv, gk, gv, z, xbc, dt = _inproj(xs, mods, l, lat_row, w_in_bf, tm_lat)
        (oa,) = _attention(qa, (dk, dv, gk, gv), caches, l, tables, lam_p, attn_gain, gq_t, gk_t,
                           lam_init, dec_batch, dec_seq, tq)
        os_, _, _ = _ssd(z, xbc, dt, ssd_params, init, l, dec_batch, dec_seq)
        xs = _post(xs, oa, os_, mods, l, lat_row, wo_bf, wfi_bf, wfo_bf, ln_g, ln_b, alpha, tm_lat)

    def token_major(a, heads, width):
        return jnp.transpose(a.reshape(batch, depth, heads, width, seq), (0, 1, 4, 2, 3))

    state = lambda a: a.reshape(batch, depth, SSD_HEADS, SSD_HD, SSD_STATE)
    return (xp.reshape(batch, seq, d), xs.reshape(dec_batch, dec_seq, d),
            token_major(kv_cache[0], DIFF_HEADS, 2 * DIFF_QK), token_major(kv_cache[1], DIFF_HEADS, DIFF_V),
            token_major(gk_cache[0], GQA_KV_HEADS, GQA_HD), token_major(kv_cache[2], GQA_KV_HEADS, GQA_HD),
            state(ssd_states[0]), state(ssd_states[1]))
```

```python
import functools
import math

import jax
import jax.numpy as jnp
from jax import lax
from jax.experimental import pallas as pl
from jax.experimental.pallas import tpu as pltpu

F32 = jnp.float32
BF16 = jnp.bfloat16

GRID_W = 64
DIFF_HEADS = 4
DIFF_QK = 32
DIFF_V = 64
DIFF_W = DIFF_HEADS * DIFF_V
GQA_HD = 64
GQA_Q_HEADS = 4
GQA_KV_HEADS = 2
GQA_W = GQA_Q_HEADS * GQA_HD
GQA_KV_W = GQA_KV_HEADS * GQA_HD
SSD_HD = 64
SSD_HEADS = 8
SSD_INNER = SSD_HEADS * SSD_HD
SSD_GROUPS = 2
SSD_STATE = 64
SSD_BC_W = SSD_GROUPS * SSD_STATE
SSD_CONV = 5
SSD_CHUNK = 128
XBC_W = SSD_INNER + 2 * SSD_BC_W
DT_W = 2 * SSD_HEADS
ROPE_THETA = 10000.0
EPS = 1e-5
LANES = 128
MXU_TILE = 256
VMEM_LIMIT = 56 * 1024 * 1024

_C_DQ, _C_DK, _C_DV, _C_GQ, _C_GK, _C_GV, _C_Z, _C_XBC, _C_DT, _C_END = (
    0, 256, 512, 768, 1024, 1152, 1280, 1792, 2560, 2576)
IN_W_PAD = _C_DT + LANES


def _dot(a, b):
    return jnp.dot(a, b, preferred_element_type=F32)


def _dot_nt(a, b):
    return lax.dot_general(a, b, (((1,), (1,)), ((), ())), preferred_element_type=F32)


def _split3(a):
    a1 = a.astype(BF16)
    r1 = a - a1.astype(F32)
    a2 = r1.astype(BF16)
    a3 = (r1 - a2.astype(F32)).astype(BF16)
    return a1, a2, a3


def _dot3_l(a, b_exact):
    a1, a2, a3 = _split3(a)
    return _dot(a1, b_exact) + (_dot(a2, b_exact) + _dot(a3, b_exact))


def _sigmoid(x):
    return 1.0 / (1.0 + jnp.exp(-x))


def _silu(x):
    return x * _sigmoid(x)


def _layer_norm(x, g, b):
    mu = jnp.mean(x, axis=-1, keepdims=True)
    xc = x - mu
    var = jnp.mean(xc * xc, axis=-1, keepdims=True)
    return xc * lax.rsqrt(var + EPS) * g + b


def _group_avg_matrix(width, group):
    sh = int(math.log2(group))
    r = lax.shift_right_logical(lax.broadcasted_iota(jnp.int32, (width, width), 0), sh)
    c = lax.shift_right_logical(lax.broadcasted_iota(jnp.int32, (width, width), 1), sh)
    return jnp.where(r == c, 1.0 / group, 0.0).astype(BF16)


def _group_mean_sq(x, gmat):
    xx = x * x
    hi = xx.astype(BF16)
    lo = (xx - hi.astype(F32)).astype(BF16)
    return _dot(hi, gmat) + _dot(lo, gmat)


def _rope(x, cos, sin_signed):
    w = x.shape[-1]
    lane = lax.broadcasted_iota(jnp.int32, x.shape, 1)
    nxt = pltpu.roll(x, w - 1, 1)
    prv = pltpu.roll(x, 1, 1)
    partner = jnp.where((lane & 1) == 0, nxt, prv)
    return x * cos + partner * sin_signed


def _lane_mask(shape, lo, hi):
    lane = lax.broadcasted_iota(jnp.int32, shape, 1)
    return (lane >= lo) & (lane < hi)


def _mod_kernel(c_ref, w_ref, b_ref, o_ref):
    a = _silu(c_ref[...])
    a_hi = a.astype(BF16)
    a_lo = (a - a_hi.astype(F32)).astype(BF16)
    w = w_ref[...]
    w_hi = w.astype(BF16)
    w_lo = (w - w_hi.astype(F32)).astype(BF16)
    o_ref[...] = _dot(a_hi, w_hi) + (_dot(a_lo, w_hi) + _dot(a_hi, w_lo)) + b_ref[...]


def _modulation(cvec, w_ada, b_ada):
    depth, d, n = w_ada.shape
    tn = 1536
    rows = cvec.shape[0]
    return pl.pallas_call(
        _mod_kernel,
        grid=(depth, n // tn),
        in_specs=[
            pl.BlockSpec((rows, d), lambda l, j: (0, 0)),
            pl.BlockSpec((None, d, tn), lambda l, j: (l, 0, j)),
            pl.BlockSpec((None, 1, tn), lambda l, j: (l, 0, j)),
        ],
        out_specs=pl.BlockSpec((None, rows, tn), lambda l, j: (l, 0, j)),
        out_shape=jax.ShapeDtypeStruct((depth, rows, n), F32),
        compiler_params=pltpu.CompilerParams(
            dimension_semantics=("arbitrary", "arbitrary"), vmem_limit_bytes=VMEM_LIMIT),
    )(cvec, w_ada, b_ada.reshape(depth, 1, n))


def _inproj_kernel(feature_major_cache, n_prev, *refs):
    x_ref, sh_ref, sc_ref, w_ref = refs[:4]
    outs = refs[4 + n_prev:]
    h = (x_ref[...] * (1.0 + sc_ref[...]) + sh_ref[...]).astype(BF16)

    def mm(lo, hi):
        return _dot(h, w_ref[:, lo:hi])

    gkv = mm(_C_GK, _C_Z)
    if feature_major_cache:
        (qa_ref, dk_ref, gk_ref, z_ref, xbc_ref, dt_ref, dkt_ref, dvt_ref, gvt_ref,
         dv_s, gv_s) = outs
        dk_ref[...] = mm(_C_DK, _C_DV)
        dv_s[...] = mm(_C_DV, _C_GQ)
        gv_s[...] = gkv[:, GQA_KV_W:]
        seq = dkt_ref.shape[-1]
        for j in range(dkt_ref.shape[0]):
            rows = slice(j * seq, (j + 1) * seq)
            dkt_ref[j] = dk_ref[rows, :].T
            dvt_ref[j] = dv_s[rows, :].T
            gvt_ref[j] = gv_s[rows, :].T
    else:
        qa_ref, dk_ref, dv_ref, gk_ref, gv_ref, z_ref, xbc_ref, dt_ref = outs
        dk_ref[...] = mm(_C_DK, _C_DV)
        dv_ref[...] = mm(_C_DV, _C_GQ)
        gv_ref[...] = gkv[:, GQA_KV_W:]
    qa_ref[:, 0:DIFF_W] = mm(_C_DQ, _C_DK)
    qa_ref[:, DIFF_W:DIFF_W + GQA_W] = mm(_C_GQ, _C_GK)
    gk_ref[...] = gkv[:, 0:GQA_KV_W]
    z_ref[...] = mm(_C_Z, _C_XBC)
    xbc_ref[...] = mm(_C_XBC, _C_DT)
    dt_ref[...] = mm(_C_DT, IN_W_PAD)


def _mod_spec(layer, which, row_fn, d):
    return pl.BlockSpec((None, None, 1, d), lambda i: (layer, row_fn(i), 0, which))


def _inproj(x, mods, layer, row_fn, w_in_bf, tm, cache_prev=None, depth=None, seq=None):
    t, d = x.shape
    feature_major = cache_prev is not None
    token_spec = lambda w: pl.BlockSpec((tm, w), lambda i: (i, 0))
    token_shape = lambda w: jax.ShapeDtypeStruct((t, w), F32)
    if feature_major:
        widths = (DIFF_W + GQA_W, DIFF_W, GQA_KV_W, SSD_INNER, XBC_W, LANES)
        cache_w = (DIFF_W, DIFF_W, GQA_KV_W)
        assert tm % seq == 0
        out_specs = [token_spec(w) for w in widths] + [
            pl.BlockSpec((tm // seq, None, w, seq), lambda i: (i, layer, 0, 0)) for w in cache_w]
        out_shape = [token_shape(w) for w in widths] + [
            jax.ShapeDtypeStruct((t // seq, depth, w, seq), F32) for w in cache_w]
        aliases = {4 + k: len(widths) + k for k in range(len(cache_prev))}
    else:
        widths = (DIFF_W + GQA_W, DIFF_W, DIFF_W, GQA_KV_W, GQA_KV_W, SSD_INNER, XBC_W, LANES)
        out_specs = [token_spec(w) for w in widths]
        out_shape = [token_shape(w) for w in widths]
        cache_prev, aliases = (), {}
    return pl.pallas_call(
        functools.partial(_inproj_kernel, feature_major, len(cache_prev)),
        grid=(t // tm,),
        in_specs=[
            pl.BlockSpec((tm, d), lambda i: (i, 0)),
            _mod_spec(layer, 0, row_fn, d),
            _mod_spec(layer, 1, row_fn, d),
            pl.BlockSpec((None, d, IN_W_PAD), lambda i: (layer, 0, 0)),
        ] + [pl.BlockSpec(memory_space=pl.ANY)] * len(cache_prev),
        out_specs=out_specs,
        out_shape=out_shape,
        input_output_aliases=aliases,
        scratch_shapes=([pltpu.VMEM((tm, DIFF_W), F32), pltpu.VMEM((tm, GQA_KV_W), F32)]
                        if feature_major else []),
        compiler_params=pltpu.CompilerParams(
            dimension_semantics=("arbitrary",), vmem_limit_bytes=VMEM_LIMIT),
    )(x, mods, mods, w_in_bf, *cache_prev)


LOG2E = 1.4426950408889634
NEG_BIG = -1e30
N_SCORE_HEADS = 2 * DIFF_HEADS + GQA_Q_HEADS
ONES_ROWS = 16


def _diff_lambda(lam_ref, lam_init):
    lp = lam_ref[...]
    s1 = jnp.sum(lp[0:1, :] * lp[1:2, :], axis=-1, keepdims=True)
    s2 = jnp.sum(lp[2:3, :] * lp[3:4, :], axis=-1, keepdims=True)
    return jnp.exp(s1) - jnp.exp(s2) + lam_init


def _swap_halves(x):
    return pltpu.roll(x, GQA_HD, 1)


def _attn_kernel(cfg, *refs):
    lam_init, seq, past, kb, rope, feature_major, n_prev = cfg
    it = iter(refs)
    qa_ref, dk_ref, dv_ref, gk_in_ref, gv_ref = next(it), next(it), next(it), next(it), next(it)
    for _ in range(n_prev):
        next(it)
    if past:
        cdk_ref, cdv_ref, cgk_ref, cgv_ref = next(it), next(it), next(it), next(it)
    if rope:
        cosq_ref, sinq_ref, cosk_ref, sink_ref = next(it), next(it), next(it), next(it)
    lam_ref, gain_ref, gq_ref, gk_ref = next(it), next(it), next(it), next(it)
    o_ref = next(it)
    gkn_ref = next(it) if feature_major else None
    kd_s, kg_s, vdt_s, vgt_s, wq_s, wg_s, s0_s, s1_s, m_s, l_s, acc_s = it
    tq = qa_ref.shape[0]
    n_blocks = (seq + past) // kb

    @pl.when(pl.program_id(1) == 0)
    def _prepare_keys():
        gmat = _group_avg_matrix(GQA_KV_W, GQA_HD)
        for i in range(seq // kb):
            rows = slice(i * kb, (i + 1) * kb)
            dk = dk_ref[rows, :]
            gk = gk_in_ref[rows, :]
            gk = gk * lax.rsqrt(_group_mean_sq(gk, gmat) + EPS) * gk_ref[...]
            if feature_major:
                gkn_ref[:, rows] = gk.T
            if rope:
                ck = cosk_ref[rows, :]
                sk = sink_ref[rows, :]
                dk = _rope(dk, ck[:, 0:DIFF_W], sk[:, 0:DIFF_W])
                gk = _rope(gk, ck[:, DIFF_W:DIFF_W + GQA_KV_W], sk[:, DIFF_W:DIFF_W + GQA_KV_W])
            kd_s[rows, :] = dk.astype(BF16)
            kg_s[rows, :] = gk.astype(BF16)
            if feature_major:
                vdt_s[i] = dv_ref[:, rows].astype(BF16)
                vgt_s[i] = gv_ref[:, rows].astype(BF16)
            else:
                vdt_s[i] = dv_ref[rows, :].T.astype(BF16)
                vgt_s[i] = gv_ref[rows, :].T.astype(BF16)
        for j in range(past // kb):
            src = slice(j * kb, (j + 1) * kb)
            dst = slice(seq + j * kb, seq + (j + 1) * kb)
            kd_s[dst, :] = cdk_ref[:, src].T.astype(BF16)
            kg_s[dst, :] = cgk_ref[:, src].T.astype(BF16)
            vdt_s[seq // kb + j] = cdv_ref[:, src].astype(BF16)
            vgt_s[seq // kb + j] = cgv_ref[:, src].astype(BF16)

    qa = qa_ref[...]
    qd = qa[:, 0:DIFF_W]
    gq = qa[:, DIFF_W:DIFF_W + GQA_W]
    gq = gq * lax.rsqrt(_group_mean_sq(gq, _group_avg_matrix(GQA_W, GQA_HD)) + EPS) * gq_ref[...]
    if rope:
        cq = cosq_ref[...]
        sq = sinq_ref[...]
        qd = _rope(qd, cq[:, 0:DIFF_W], sq[:, 0:DIFF_W])
        gq = _rope(gq, cq[:, DIFF_W:DIFF_W + GQA_W], sq[:, DIFF_W:DIFF_W + GQA_W])
    qd_t = (qd * (DIFF_QK ** -0.5 * LOG2E)).T
    gq_t = (gq * (GQA_HD ** -0.5 * LOG2E)).T
    row = lax.broadcasted_iota(jnp.int32, (DIFF_W, tq), 0)
    for hm in range(2 * DIFF_HEADS):
        lo = hm * DIFF_QK
        wq_s[:, hm * tq:(hm + 1) * tq] = jnp.where((row >= lo) & (row < lo + DIFF_QK), qd_t, 0.0).astype(BF16)
    zeros = jnp.zeros((GQA_HD, tq), F32)
    for h in range(GQA_Q_HEADS):
        piece = gq_t[h * GQA_HD:(h + 1) * GQA_HD, :]
        pair = [piece, zeros] if h // (GQA_Q_HEADS // GQA_KV_HEADS) == 0 else [zeros, piece]
        wg_s[:, h * tq:(h + 1) * tq] = jnp.concatenate(pair, axis=0).astype(BF16)
    m_s[...] = jnp.full(m_s.shape, NEG_BIG, F32)
    l_s[...] = jnp.zeros(l_s.shape, F32)
    acc_s[...] = jnp.zeros(acc_s.shape, F32)
    n_diff = 2 * DIFF_HEADS * tq
    n_all = N_SCORE_HEADS * tq

    slabs = tq // LANES

    def key_rows(j):
        return pl.ds(j * kb if isinstance(j, int) else pl.multiple_of(j * kb, kb), kb)

    def head_scores(idx, k_d, k_g, s_buf):
        if idx < 2 * DIFF_HEADS:
            s = _dot(k_d, wq_s[:, idx * tq:(idx + 1) * tq])
        else:
            h = idx - 2 * DIFF_HEADS
            s = _dot(k_g, wg_s[:, h * tq:(h + 1) * tq])
        for k in range(slabs):
            s_buf[idx * slabs + k] = s[:, k * LANES:(k + 1) * LANES]

    def head_update(idx, v_d, v_g, s_buf):
        ps, alphas = [], []
        for k in range(slabs):
            c = idx * slabs + k
            cols = slice(c * LANES, (c + 1) * LANES)
            s = s_buf[c]
            m_old = m_s[:, cols]
            m_new = jnp.maximum(m_old, jnp.max(s, axis=0, keepdims=True))
            alphas.append(jnp.exp2(m_old - m_new))
            ps.append(jnp.exp2(s - m_new).astype(BF16))
            m_s[:, cols] = m_new
        if idx < 2 * DIFF_HEADS:
            vh = idx // 2
            v_t = v_d[vh * DIFF_V:(vh + 1) * DIFF_V, :]
        else:
            vh = (idx - 2 * DIFF_HEADS) // (GQA_Q_HEADS // GQA_KV_HEADS)
            v_t = v_g[vh * GQA_HD:(vh + 1) * GQA_HD, :]
        v_ext = jnp.concatenate([v_t, jnp.ones((ONES_ROWS, kb), BF16)], axis=0)
        alpha = jnp.concatenate(alphas, axis=1)
        pv = _dot(v_ext, jnp.concatenate(ps, axis=1))
        cols = slice(idx * tq, (idx + 1) * tq)
        acc_s[idx] = alpha * acc_s[idx] + pv[0:GQA_HD, :]
        l_s[:, cols] = alpha * l_s[:, cols] + pv[GQA_HD:GQA_HD + 1, :]

    def key_block(j, s_cur, j_next, s_next):
        v_d = vdt_s[j]
        v_g = vgt_s[j]
        if j_next is not None:
            k_d = kd_s[key_rows(j_next), :]
            k_g = kg_s[key_rows(j_next), :]
        for idx in range(N_SCORE_HEADS):
            if j_next is not None:
                head_scores(idx, k_d, k_g, s_next)
            head_update(idx, v_d, v_g, s_cur)

    for idx in range(N_SCORE_HEADS):
        head_scores(idx, kd_s[key_rows(0), :], kg_s[key_rows(0), :], s0_s)
    if n_blocks > 1:
        def block_pair(i, carry):
            key_block(2 * i, s0_s, 2 * i + 1, s1_s)
            key_block(2 * i + 1, s1_s, 2 * i + 2, s0_s)
            return carry

        lax.fori_loop(0, n_blocks // 2 - 1, block_pair, 0)
        key_block(n_blocks - 2, s0_s, n_blocks - 1, s1_s)
        key_block(n_blocks - 1, s1_s, None, None)
    else:
        key_block(0, s0_s, None, None)

    lam = _diff_lambda(lam_ref, lam_init)
    outs = []
    for h in range(DIFF_HEADS):
        c0 = slice(2 * h * tq, (2 * h + 1) * tq)
        c1 = slice((2 * h + 1) * tq, (2 * h + 2) * tq)
        o = acc_s[2 * h] * (1.0 / l_s[:, c0]) - acc_s[2 * h + 1] * (lam / l_s[:, c1])
        ms = jnp.mean(o * o, axis=0, keepdims=True)
        outs.append(o * (lax.rsqrt(ms + EPS) * (1.0 - lam_init)))
    for h in range(GQA_Q_HEADS):
        idx = 2 * DIFF_HEADS + h
        outs.append(acc_s[idx] * (1.0 / l_s[:, idx * tq:(idx + 1) * tq]))
    o_ref[...] = jnp.concatenate(outs, axis=0).T * gain_ref[...]


def _attention(qa, kv, caches, layer, tables, lam_p, gain, gq_t, gk_t, lam_init, batch, seq, tq,
               gkn_prev=None):
    t = qa.shape[0]
    nq = seq // tq
    feature_major = gkn_prev is not None
    past = caches[0].shape[3] if caches is not None else 0
    kb = min(256, seq)
    assert seq % kb == 0 and past % kb == 0 and seq % tq == 0
    n_blocks = (seq + past) // kb
    assert n_blocks == 1 or n_blocks % 2 == 0
    n_all = N_SCORE_HEADS * tq
    full = lambda a: pl.BlockSpec(a.shape, lambda b, q: (0,) * a.ndim)
    token_major = lambda a: pl.BlockSpec((seq, a.shape[1]), lambda b, q: (b, 0))
    by_feature = lambda a: pl.BlockSpec((None, None, a.shape[2], seq), lambda b, q: (b, layer, 0, 0))
    dk, dv, gk, gv = kv
    in_specs = [pl.BlockSpec((tq, qa.shape[1]), lambda b, q: (b * nq + q, 0)),
                token_major(dk), by_feature(dv) if feature_major else token_major(dv),
                token_major(gk), by_feature(gv) if feature_major else token_major(gv)]
    args = [qa, dk, dv, gk, gv]
    aliases = {}
    if feature_major:
        assert nq == 1
        in_specs += [pl.BlockSpec(memory_space=pl.ANY)] * len(gkn_prev)
        args += list(gkn_prev)
        aliases = {5 + k: 1 + k for k in range(len(gkn_prev))}
    if caches is not None:
        in_specs += [pl.BlockSpec((None, None, a.shape[2], past), lambda b, q: (b, layer, 0, 0))
                     for a in caches]
        args += list(caches)
    if tables is not None:
        cosq, sinq, cosk, sink = tables
        in_specs += [pl.BlockSpec((tq, cosq.shape[1]), lambda b, q: (q, 0)),
                     pl.BlockSpec((tq, sinq.shape[1]), lambda b, q: (q, 0)),
                     full(cosk), full(sink)]
        args += [cosq, sinq, cosk, sink]
    in_specs += [full(lam_p), full(gain), full(gq_t), full(gk_t)]
    args += [lam_p, gain, gq_t, gk_t]
    out_specs = [pl.BlockSpec((tq, DIFF_W + GQA_W), lambda b, q: (b * nq + q, 0))]
    out_shape = [jax.ShapeDtypeStruct((t, DIFF_W + GQA_W), F32)]
    if feature_major:
        out_specs.append(pl.BlockSpec((None, None, GQA_KV_W, seq), lambda b, q: (b, layer, 0, 0)))
        out_shape.append(jax.ShapeDtypeStruct((batch, dv.shape[1], GQA_KV_W, seq), F32))
    cfg = (lam_init, seq, past, kb, tables is not None, feature_major,
           len(gkn_prev) if feature_major else 0)
    return pl.pallas_call(
        functools.partial(_attn_kernel, cfg),
        grid=(batch, nq),
        in_specs=in_specs,
        out_specs=out_specs,
        out_shape=out_shape,
        input_output_aliases=aliases,
        scratch_shapes=[
            pltpu.VMEM((seq + past, DIFF_W), BF16), pltpu.VMEM((seq + past, GQA_KV_W), BF16),
            pltpu.VMEM((n_blocks, DIFF_W, kb), BF16), pltpu.VMEM((n_blocks, GQA_KV_W, kb), BF16),
            pltpu.VMEM((DIFF_W, 2 * DIFF_HEADS * tq), BF16), pltpu.VMEM((GQA_KV_W, GQA_Q_HEADS * tq), BF16),
            pltpu.VMEM((n_all // LANES, kb, LANES), F32), pltpu.VMEM((n_all // LANES, kb, LANES), F32),
            pltpu.VMEM((1, n_all), F32), pltpu.VMEM((1, n_all), F32),
            pltpu.VMEM((N_SCORE_HEADS, GQA_HD, tq), F32),
        ],
        compiler_params=pltpu.CompilerParams(
            dimension_semantics=("arbitrary", "arbitrary"), vmem_limit_bytes=VMEM_LIMIT),
    )(*args)


def _softplus(x):
    return jnp.maximum(x, 0.0) + jnp.log1p(jnp.exp(-jnp.abs(x)))


def _dot2_l(a, b_exact):
    a1 = a.astype(BF16)
    a2 = (a - a1.astype(F32)).astype(BF16)
    return _dot(a1, b_exact) + _dot(a2, b_exact)


def _ssd_kernel(has_init, n_prev, seq, *refs):
    it = iter(refs)
    z_ref, xbc_ref, dt_ref, cw_ref, cb_ref, alog_ref, dtb_ref, dexp_ref, ng_ref = (
        next(it) for _ in range(9))
    sf0_ref, sb0_ref = (next(it), next(it)) if has_init else (None, None)
    for _ in range(n_prev):
        next(it)
    o_ref, sf_ref, sb_ref = next(it), next(it), next(it)
    stf_s, stb_s, inc_s, dec_s, eab_s, cbf_s, exp_s, tri_s, gm_s = it
    L = SSD_CHUNK
    W2 = 2 * SSD_INNER
    nchunks = seq // L
    halo = 8

    ri = lax.broadcasted_iota(jnp.int32, (L, L), 0)
    ci = lax.broadcasted_iota(jnp.int32, (L, L), 1)
    lower = ri >= ci
    upper = ri <= ci
    tri_s[0] = jnp.where(lower, 1.0, 0.0).astype(BF16)
    tri_s[1] = jnp.where(upper, 1.0, 0.0).astype(BF16)
    lane_row = lax.broadcasted_iota(jnp.int32, (1, LANES), 1)
    a_row = jnp.where(lane_row < DT_W, -jnp.exp(alog_ref[...]), 0.0)
    ej = lax.broadcasted_iota(jnp.int32, (LANES, W2), 0)
    eh = lax.shift_right_logical(lax.broadcasted_iota(jnp.int32, (LANES, W2), 1), 6)
    exp_s[...] = jnp.where(ej == eh, 1.0, 0.0).astype(BF16)
    gm_s[...] = jnp.where(lax.shift_right_logical(ej, 6) == (lax.shift_right_logical(eh, 2) & 1), 1.0, 0.0)

    def chunk_rows(c):
        return pl.ds(pl.multiple_of(c * L, L), L)

    def load_state(ref):
        r = lax.broadcasted_iota(jnp.int32, (SSD_STATE, LANES), 0)
        c = lax.broadcasted_iota(jnp.int32, (SSD_STATE, LANES), 1)
        dup = jnp.where((c & (SSD_STATE - 1)) == r, 1.0, 0.0).astype(BF16)
        return _dot3_l(ref[...], dup).T * gm_s[:, 0:SSD_INNER]

    def store_state(st_ref, ref):
        st_t = st_ref[...].T
        ref[...] = (st_t + _swap_halves(st_t))[:, 0:SSD_STATE]

    stf_s[...] = load_state(sf0_ref) if has_init else jnp.zeros(stf_s.shape, F32)
    stb_s[...] = load_state(sb0_ref) if has_init else jnp.zeros(stb_s.shape, F32)

    def forward_pass(c, carry):
        r0 = c * L
        rows = chunk_rows(c)
        dt_c = _softplus(dt_ref[rows, :] + dtb_ref[...])

        d1, d2, d3 = _split3(dt_c * a_row)
        acs_f = _dot(tri_s[0], d1) + (_dot(tri_s[0], d2) + _dot(tri_s[0], d3))
        acs_b = _dot(tri_s[1], d1) + (_dot(tri_s[1], d2) + _dot(tri_s[1], d3))
        acs = jnp.where(lax.broadcasted_iota(jnp.int32, (L, LANES), 1) < SSD_HEADS, acs_f, acs_b)
        acs_t = acs.T
        expand = exp_s[...]
        dt_e = _dot2_l(dt_c, expand)
        acs_e = _dot3_l(acs, expand)
        edge = jnp.concatenate([acs_e[L - 1:L, 0:SSD_INNER], acs_e[0:1, SSD_INNER:W2]], axis=1)
        eacs = jnp.exp(acs_e)
        cdec = jnp.exp(edge)

        prev = xbc_ref[pl.ds(pl.multiple_of(jnp.maximum(r0 - halo, 0), halo), halo), :]
        nxt = xbc_ref[pl.ds(pl.multiple_of(jnp.minimum(r0 + L, seq - halo), halo), halo), :]
        cur = xbc_ref[rows, :]
        win = jnp.concatenate([jnp.where(c > 0, prev, 0.0), cur,
                               jnp.where(c < nchunks - 1, nxt, 0.0)], axis=0)
        acc = cb_ref[...] + cur * cw_ref[SSD_CONV // 2:SSD_CONV // 2 + 1, :]
        for j in range(SSD_CONV):
            if j != SSD_CONV // 2:
                shifted = pltpu.roll(win, (SSD_CONV // 2 - j) % (L + 2 * halo), 0)[halo:halo + L, :]
                acc = acc + shifted * cw_ref[j:j + 1, :]
        act = _silu(acc)
        x_c = act[:, 0:SSD_INNER]
        b_c = act[:, SSD_INNER:SSD_INNER + SSD_BC_W]
        c_c = act[:, SSD_INNER + SSD_BC_W:XBC_W]
        xd = jnp.concatenate([x_c, x_c], axis=1) * dt_e
        xd_b = xd.astype(BF16)
        xdw = (xd * jnp.exp(edge - acs_e)).astype(BF16)
        b_b = b_c.astype(BF16)
        c_b = c_c.astype(BF16)
        s_new = _dot(b_c.T.astype(BF16), xdw) * gm_s[...]

        st_f = stf_s[...]
        y = _dot(c_b, st_f.astype(BF16)) * eacs[:, 0:SSD_INNER]
        stf_s[...] = st_f * cdec[:, 0:SSD_INNER] + s_new[:, 0:SSD_INNER]
        inc_s[c] = s_new[:, SSD_INNER:W2]
        dec_s[c] = cdec[:, SSD_INNER:W2]
        eab_s[rows, :] = eacs[:, SSD_INNER:W2]
        cbf_s[rows, :] = c_b

        cbs = []
        for g in range(SSD_GROUPS):
            cg = jnp.where(_lane_mask(c_c.shape, g * SSD_STATE, (g + 1) * SSD_STATE), c_c, 0.0)
            cbs.append(_dot_nt(cg.astype(BF16), b_b))
        for direction, causal in ((0, lower), (1, upper)):
            pairs = []
            for g in range(SSD_GROUPS):
                cb = cbs[g]
                for hp in range(2):
                    pair = g * 2 + hp
                    res = []
                    for k in range(2):
                        j = direction * SSD_HEADS + pair * 2 + k
                        diff = acs[:, j:j + 1] - acs_t[j:j + 1, :]
                        dec = jnp.where(causal, jnp.exp(jnp.minimum(diff, 0.0)), 0.0)
                        sc = (cb * dec).astype(BF16)
                        lo = direction * SSD_INNER + pair * LANES
                        res.append(_dot(sc, xd_b[:, lo:lo + LANES]))
                    pairs.append(jnp.where(_lane_mask(res[0].shape, 0, SSD_HD), res[0], res[1]))
            y = y + jnp.concatenate(pairs, axis=-1)
        o_ref[rows, :] = y + x_c * dexp_ref[...]
        return carry

    lax.fori_loop(0, nchunks, forward_pass, 0)
    store_state(stf_s, sf_ref)

    def backward_pass(i, carry):
        c = nchunks - 1 - i
        rows = chunk_rows(c)
        st_b = stb_s[...]
        y = o_ref[rows, :] + _dot(cbf_s[rows, :], st_b.astype(BF16)) * eab_s[rows, :]
        stb_s[...] = st_b * dec_s[c] + inc_s[c]
        yt = y * _silu(z_ref[rows, :])
        ms = jnp.mean(yt * yt, axis=-1, keepdims=True)
        o_ref[rows, :] = yt * lax.rsqrt(ms + EPS) * ng_ref[...]
        return carry

    lax.fori_loop(0, nchunks, backward_pass, 0, unroll=2)
    store_state(stb_s, sb_ref)


def _ssd(z, xbc, dt, params, init, layer, batch, seq, state_prev=(), depth=1):
    cw, cb, alog, dtb, dexp, ng = params
    t = z.shape[0]
    has_init = init is not None
    nchunks = seq // SSD_CHUNK
    full = lambda a: pl.BlockSpec(a.shape, lambda b: (0,) * a.ndim)
    out_layer = layer if depth > 1 else 0
    st_spec = pl.BlockSpec((None, None, SSD_INNER, SSD_STATE), lambda b: (b, out_layer, 0, 0))
    in_specs = [
        pl.BlockSpec((seq, SSD_INNER), lambda b: (b, 0)),
        pl.BlockSpec((seq, XBC_W), lambda b: (b, 0)),
        pl.BlockSpec((seq, LANES), lambda b: (b, 0)),
        full(cw), full(cb), full(alog), full(dtb), full(dexp), full(ng),
    ]
    args = [z, xbc, dt, cw, cb, alog, dtb, dexp, ng]
    if has_init:
        init_spec = pl.BlockSpec((None, None, SSD_INNER, SSD_STATE), lambda b: (b, layer, 0, 0))
        in_specs += [init_spec, init_spec]
        args += list(init)
    aliases = {len(args) + k: 1 + k for k in range(len(state_prev))}
    in_specs += [pl.BlockSpec(memory_space=pl.ANY)] * len(state_prev)
    args += list(state_prev)
    st_shape = jax.ShapeDtypeStruct((batch, depth, SSD_INNER, SSD_STATE), F32)
    return pl.pallas_call(
        functools.partial(_ssd_kernel, has_init, len(state_prev), seq),
        grid=(batch,),
        in_specs=in_specs,
        out_specs=[pl.BlockSpec((seq, SSD_INNER), lambda b: (b, 0)), st_spec, st_spec],
        out_shape=[jax.ShapeDtypeStruct((t, SSD_INNER), F32), st_shape, st_shape],
        input_output_aliases=aliases,
        scratch_shapes=[
            pltpu.VMEM((LANES, SSD_INNER), F32), pltpu.VMEM((LANES, SSD_INNER), F32),
            pltpu.VMEM((nchunks, LANES, SSD_INNER), F32),
            pltpu.VMEM((nchunks, 1, SSD_INNER), F32),
            pltpu.VMEM((seq, SSD_INNER), F32),
            pltpu.VMEM((seq, SSD_BC_W), BF16),
            pltpu.VMEM((LANES, 2 * SSD_INNER), BF16),
            pltpu.VMEM((2, SSD_CHUNK, SSD_CHUNK), BF16),
            pltpu.VMEM((LANES, 2 * SSD_INNER), F32),
        ],
        compiler_params=pltpu.CompilerParams(
            dimension_semantics=("arbitrary",), vmem_limit_bytes=VMEM_LIMIT),
    )(*args)


def _post_kernel(alpha, d_ff, ff_chunk, x_ref, oa_ref, os_ref, g1_ref, sh2_ref, sc2_ref, g2_ref,
                 wo_ref, wfi_ref, wfo_ref, lng_ref, lnb_ref, y_ref):
    x = x_ref[...]
    wa = oa_ref.shape[1]
    o = _dot(oa_ref[...].astype(BF16), wo_ref[0:wa, :]) + _dot(os_ref[...].astype(BF16), wo_ref[wa:, :])
    x1 = _layer_norm(alpha * x + g1_ref[...] * o, lng_ref[0:1, :], lnb_ref[0:1, :])
    h2 = (x1 * (1.0 + sc2_ref[...]) + sh2_ref[...]).astype(BF16)
    bounds = list(range(0, d_ff, ff_chunk)) + [d_ff]

    def up(c):
        lo, hi = bounds[c], bounds[c + 1]
        return _dot(h2, wfi_ref[:, lo:hi]), _dot(h2, wfi_ref[:, d_ff + lo:d_ff + hi])

    f = jnp.zeros(x.shape, F32)
    g, u = up(0)
    for c in range(len(bounds) - 1):
        nxt = up(c + 1) if c + 2 < len(bounds) else None
        f = f + _dot((_silu(g) * u).astype(BF16), wfo_ref[bounds[c]:bounds[c + 1], :])
        if nxt is not None:
            g, u = nxt
    y_ref[...] = _layer_norm(alpha * x1 + g2_ref[...] * f, lng_ref[1:2, :], lnb_ref[1:2, :])


def _post(x, oa, os_, mods, layer, row_fn, wo_bf, wfi_bf, wfo_bf, lng, lnb, alpha, tm):
    t, d = x.shape
    d_ff = wfo_bf.shape[1]
    ff_chunk = 3 * MXU_TILE
    assert d_ff % MXU_TILE == 0
    per_layer = lambda a, **kw: pl.BlockSpec((None,) + a.shape[1:], lambda i: (layer, 0, 0), **kw)
    resident = lambda a: per_layer(a, pipeline_mode=pl.Buffered(1))
    return pl.pallas_call(
        functools.partial(_post_kernel, alpha, d_ff, ff_chunk),
        grid=(t // tm,),
        in_specs=[
            pl.BlockSpec((tm, d), lambda i: (i, 0)),
            pl.BlockSpec((tm, oa.shape[1]), lambda i: (i, 0)),
            pl.BlockSpec((tm, os_.shape[1]), lambda i: (i, 0)),
            _mod_spec(layer, 2, row_fn, d),
            _mod_spec(layer, 3, row_fn, d),
            _mod_spec(layer, 4, row_fn, d),
            _mod_spec(layer, 5, row_fn, d),
            resident(wo_bf), resident(wfi_bf), resident(wfo_bf),
            per_layer(lng), per_layer(lnb),
        ],
        out_specs=pl.BlockSpec((tm, d), lambda i: (i, 0)),
        out_shape=jax.ShapeDtypeStruct((t, d), F32),
        compiler_params=pltpu.CompilerParams(
            dimension_semantics=("arbitrary",), vmem_limit_bytes=VMEM_LIMIT),
    )(x, oa, os_, mods, mods, mods, mods, wo_bf, wfi_bf, wfo_bf, lng, lnb)


def _rope_tables(rows, dim, copies):
    row = jnp.repeat(jnp.arange(rows), GRID_W).astype(F32)
    col = jnp.tile(jnp.arange(GRID_W), rows).astype(F32)
    n_freq = dim // 4
    inv = ROPE_THETA ** (-jnp.arange(n_freq, dtype=F32) / n_freq)
    ang = jnp.concatenate([row[:, None] * inv, col[:, None] * inv], -1)
    cos, sin = jnp.cos(ang), jnp.sin(ang)
    cos_full = jnp.repeat(cos, 2, axis=-1)
    sin_signed = jnp.stack([-sin, sin], axis=-1).reshape(sin.shape[0], dim)
    return jnp.tile(cos_full, (1, copies)), jnp.tile(sin_signed, (1, copies))


def kernel(x_prompt, x_sample, cache_diff_k, cache_diff_v, cache_gqa_k, cache_gqa_v, state_ssd_fwd, state_ssd_bwd, c, c_ctx, w_ada, b_ada, w_in, w_out, diff_lambda, diff_subln_g, qk_norm_g, ssd_conv_w, ssd_conv_b, ssd_A_log, ssd_dt_bias, ssd_D, ssd_norm_g, ln_g, ln_b, w_ffn_in, w_ffn_out):
    batch, seq, d = x_prompt.shape
    dec_batch, dec_seq, _ = x_sample.shape
    depth = w_in.shape[0]
    past = cache_diff_k.shape[2]
    alpha = (2 * depth) ** 0.25
    rows = dec_seq // GRID_W

    n_vec = 1 + dec_batch
    n_pad = -(-n_vec // 8) * 8
    cvec = jnp.concatenate([c_ctx[None, :], c, jnp.zeros((n_pad - n_vec, d), F32)], axis=0)
    mods = _modulation(cvec, w_ada, b_ada).reshape(depth, n_pad, 1, 6 * d)

    cos_d, sin_d = _rope_tables(rows, DIFF_QK, DIFF_W // DIFF_QK)
    cos_g, sin_g = _rope_tables(rows, GQA_HD, GQA_Q_HEADS)
    cosq = jnp.concatenate([cos_d, cos_g], axis=-1)
    sinq = jnp.concatenate([sin_d, sin_g], axis=-1)
    cosk = jnp.concatenate([cos_d, cos_g[:, :GQA_KV_W]], axis=-1)
    sink = jnp.concatenate([sin_d, sin_g[:, :GQA_KV_W]], axis=-1)
    tables = (cosq, sinq, cosk, sink)

    feature_major = lambda a, w: jnp.transpose(a.reshape(dec_batch, depth, past, w), (0, 1, 3, 2))
    caches = (feature_major(cache_diff_k, DIFF_W), feature_major(cache_diff_v, DIFF_W),
              feature_major(cache_gqa_k, GQA_KV_W), feature_major(cache_gqa_v, GQA_KV_W))

    xp = x_prompt.reshape(batch * seq, d)
    xs = x_sample.reshape(dec_batch * dec_seq, d)
    tm_ctx = min(512, batch * seq)
    tm_lat = min(512, dec_seq)
    tq = min(256, dec_seq)
    ctx_row = lambda i: 0
    lat_row = lambda i: 1 + (i * tm_lat) // dec_seq

    w_in_bf = jnp.pad(w_in, ((0, 0), (0, 0), (0, IN_W_PAD - w_in.shape[2]))).astype(BF16)
    wo_bf = w_out.astype(BF16)
    wfi_bf = w_ffn_in.astype(BF16)
    wfo_bf = w_ffn_out.astype(BF16)
    init = (state_ssd_fwd.reshape(dec_batch, depth, SSD_INNER, SSD_STATE),
            state_ssd_bwd.reshape(dec_batch, depth, SSD_INNER, SSD_STATE))

    cache = lambda w: jnp.zeros((batch, depth, w, seq), F32)
    kv_cache = (cache(DIFF_W), cache(DIFF_W), cache(GQA_KV_W))
    gk_cache = (cache(GQA_KV_W),)
    ssd_states = tuple(jnp.zeros((batch, depth, SSD_INNER, SSD_STATE), F32) for _ in range(2))
    for l in range(depth):
        lam_init = 0.8 - 0.6 * math.exp(-0.3 * l)
        lam_p = diff_lambda[l]
        attn_gain = jnp.concatenate([jnp.tile(diff_subln_g[l], DIFF_HEADS), jnp.ones((GQA_W,), F32)])[None, :]
        gq_t = jnp.tile(qk_norm_g[l, 0], GQA_Q_HEADS)[None, :]
        gk_t = jnp.tile(qk_norm_g[l, 1], GQA_KV_HEADS)[None, :]
        pad_row = lambda v: jnp.pad(v.reshape(1, -1), ((0, 0), (0, LANES - v.size)))
        ssd_params = (ssd_conv_w[l], ssd_conv_b[l][None, :], pad_row(ssd_A_log[l]),
                      pad_row(ssd_dt_bias[l]), jnp.repeat(ssd_D[l], SSD_HD)[None, :],
                      ssd_norm_g[l][None, :])

        qa, dk, gk, z, xbc, dt, *kv_cache = _inproj(xp, mods, l, ctx_row, w_in_bf, tm_ctx,
                                                    cache_prev=tuple(kv_cache), depth=depth, seq=seq)
        oa, gkn = _attention(qa, (dk, kv_cache[1], gk, kv_cache[2]), None, l, None, lam_p, attn_gain,
                             gq_t, gk_t, lam_init, batch, seq, seq, gkn_prev=gk_cache)
        gk_cache = (gkn,)
        os_, *ssd_states = _ssd(z, xbc, dt, ssd_params, None, l, batch, seq,
                                state_prev=tuple(ssd_states), depth=depth)
        xp = _post(xp, oa, os_, mods, l, ctx_row, wo_bf, wfi_bf, wfo_bf, ln_g, ln_b, alpha, tm_ctx)

        qa, dk, dv, gk, gv, z, xbc, dt = _inproj(xs, mods, l, lat_row, w_in_bf, tm_lat)
        (oa,) = _attention(qa, (dk, dv, gk, gv), caches, l, tables, lam_p, attn_gain, gq_t, gk_t,
                           lam_init, dec_batch, dec_seq, tq)
        os_, _, _ = _ssd(z, xbc, dt, ssd_params, init, l, dec_batch, dec_seq)
        xs = _post(xs, oa, os_, mods, l, lat_row, wo_bf, wfi_bf, wfo_bf, ln_g, ln_b, alpha, tm_lat)

    def token_major(a, heads, width):
        return jnp.transpose(a.reshape(batch, depth, heads, width, seq), (0, 1, 4, 2, 3))

    state = lambda a: a.reshape(batch, depth, SSD_HEADS, SSD_HD, SSD_STATE)
    return (xp.reshape(batch, seq, d), xs.reshape(dec_batch, dec_seq, d),
            token_major(kv_cache[0], DIFF_HEADS, 2 * DIFF_QK), token_major(kv_cache[1], DIFF_HEADS, DIFF_V),
            token_major(gk_cache[0], GQA_KV_HEADS, GQA_HD), token_major(kv_cache[2], GQA_KV_HEADS, GQA_HD),
            state(ssd_states[0]), state(ssd_states[1]))
```

```python
import functools
import math

import jax
import jax.numpy as jnp
from jax import lax
from jax.experimental import pallas as pl
from jax.experimental.pallas import tpu as pltpu

F32 = jnp.float32
BF16 = jnp.bfloat16

GRID_W = 64
DIFF_HEADS = 4
DIFF_QK = 32
DIFF_V = 64
DIFF_W = DIFF_HEADS * DIFF_V
GQA_HD = 64
GQA_Q_HEADS = 4
GQA_KV_HEADS = 2
GQA_W = GQA_Q_HEADS * GQA_HD
GQA_KV_W = GQA_KV_HEADS * GQA_HD
SSD_HD = 64
SSD_HEADS = 8
SSD_INNER = SSD_HEADS * SSD_HD
SSD_GROUPS = 2
SSD_STATE = 64
SSD_BC_W = SSD_GROUPS * SSD_STATE
SSD_CONV = 5
SSD_CHUNK = 128
XBC_W = SSD_INNER + 2 * SSD_BC_W
DT_W = 2 * SSD_HEADS
ROPE_THETA = 10000.0
EPS = 1e-5
LANES = 128
MXU_TILE = 256
VMEM_LIMIT = 56 * 1024 * 1024

_C_DQ, _C_DK, _C_DV, _C_GQ, _C_GK, _C_GV, _C_Z, _C_XBC, _C_DT, _C_END = (
    0, 256, 512, 768, 1024, 1152, 1280, 1792, 2560, 2576)
IN_W_PAD = _C_DT + LANES


def _dot(a, b):
    return jnp.dot(a, b, preferred_element_type=F32)


def _dot_nt(a, b):
    return lax.dot_general(a, b, (((1,), (1,)), ((), ())), preferred_element_type=F32)


def _split3(a):
    a1 = a.astype(BF16)
    r1 = a - a1.astype(F32)
    a2 = r1.astype(BF16)
    a3 = (r1 - a2.astype(F32)).astype(BF16)
    return a1, a2, a3


def _dot3_l(a, b_exact):
    a1, a2, a3 = _split3(a)
    return _dot(a1, b_exact) + (_dot(a2, b_exact) + _dot(a3, b_exact))


def _sigmoid(x):
    return 1.0 / (1.0 + jnp.exp(-x))


def _silu(x):
    return x * _sigmoid(x)


def _layer_norm(x, g, b):
    mu = jnp.mean(x, axis=-1, keepdims=True)
    xc = x - mu
    var = jnp.mean(xc * xc, axis=-1, keepdims=True)
    return xc * lax.rsqrt(var + EPS) * g + b


def _group_avg_matrix(width, group):
    sh = int(math.log2(group))
    r = lax.shift_right_logical(lax.broadcasted_iota(jnp.int32, (width, width), 0), sh)
    c = lax.shift_right_logical(lax.broadcasted_iota(jnp.int32, (width, width), 1), sh)
    return jnp.where(r == c, 1.0 / group, 0.0).astype(BF16)


def _group_mean_sq(x, gmat):
    xx = x * x
    hi = xx.astype(BF16)
    lo = (xx - hi.astype(F32)).astype(BF16)
    return _dot(hi, gmat) + _dot(lo, gmat)


def _rope(x, cos, sin_signed):
    w = x.shape[-1]
    lane = lax.broadcasted_iota(jnp.int32, x.shape, 1)
    nxt = pltpu.roll(x, w - 1, 1)
    prv = pltpu.roll(x, 1, 1)
    partner = jnp.where((lane & 1) == 0, nxt, prv)
    return x * cos + partner * sin_signed


def _lane_mask(shape, lo, hi):
    lane = lax.broadcasted_iota(jnp.int32, shape, 1)
    return (lane >= lo) & (lane < hi)


def _mod_kernel(c_ref, w_ref, b_ref, o_ref):
    a = _silu(c_ref[...])
    a_hi = a.astype(BF16)
    a_lo = (a - a_hi.astype(F32)).astype(BF16)
    w = w_ref[...]
    w_hi = w.astype(BF16)
    w_lo = (w - w_hi.astype(F32)).astype(BF16)
    o_ref[...] = _dot(a_hi, w_hi) + (_dot(a_lo, w_hi) + _dot(a_hi, w_lo)) + b_ref[...]


def _modulation(cvec, w_ada, b_ada):
    depth, d, n = w_ada.shape
    tn = 1536
    rows = cvec.shape[0]
    return pl.pallas_call(
        _mod_kernel,
        grid=(depth, n // tn),
        in_specs=[
            pl.BlockSpec((rows, d), lambda l, j: (0, 0)),
            pl.BlockSpec((None, d, tn), lambda l, j: (l, 0, j)),
            pl.BlockSpec((None, 1, tn), lambda l, j: (l, 0, j)),
        ],
        out_specs=pl.BlockSpec((None, rows, tn), lambda l, j: (l, 0, j)),
        out_shape=jax.ShapeDtypeStruct((depth, rows, n), F32),
        compiler_params=pltpu.CompilerParams(
            dimension_semantics=("arbitrary", "arbitrary"), vmem_limit_bytes=VMEM_LIMIT),
    )(cvec, w_ada, b_ada.reshape(depth, 1, n))


def _inproj_kernel(feature_major_cache, n_prev, *refs):
    x_ref, sh_ref, sc_ref, w_ref = refs[:4]
    outs = refs[4 + n_prev:]
    h = (x_ref[...] * (1.0 + sc_ref[...]) + sh_ref[...]).astype(BF16)

    def mm(lo, hi):
        return _dot(h, w_ref[:, lo:hi])

    gkv = mm(_C_GK, _C_Z)
    if feature_major_cache:
        (qa_ref, dk_ref, gk_ref, z_ref, xbc_ref, dt_ref, dkt_ref, dvt_ref, gvt_ref,
         dv_s, gv_s) = outs
        dk_ref[...] = mm(_C_DK, _C_DV)
        dv_s[...] = mm(_C_DV, _C_GQ)
        gv_s[...] = gkv[:, GQA_KV_W:]
        seq = dkt_ref.shape[-1]
        for j in range(dkt_ref.shape[0]):
            rows = slice(j * seq, (j + 1) * seq)
            dkt_ref[j] = dk_ref[rows, :].T
            dvt_ref[j] = dv_s[rows, :].T
            gvt_ref[j] = gv_s[rows, :].T
    else:
        qa_ref, dk_ref, dv_ref, gk_ref, gv_ref, z_ref, xbc_ref, dt_ref = outs
        dk_ref[...] = mm(_C_DK, _C_DV)
        dv_ref[...] = mm(_C_DV, _C_GQ)
        gv_ref[...] = gkv[:, GQA_KV_W:]
    qa_ref[:, 0:DIFF_W] = mm(_C_DQ, _C_DK)
    qa_ref[:, DIFF_W:DIFF_W + GQA_W] = mm(_C_GQ, _C_GK)
    gk_ref[...] = gkv[:, 0:GQA_KV_W]
    z_ref[...] = mm(_C_Z, _C_XBC)
    xbc_ref[...] = mm(_C_XBC, _C_DT)
    dt_ref[...] = mm(_C_DT, IN_W_PAD)


def _mod_spec(layer, which, row_fn, d):
    return pl.BlockSpec((None, None, 1, d), lambda i: (layer, row_fn(i), 0, which))


def _inproj(x, mods, layer, row_fn, w_in_bf, tm, cache_prev=None, depth=None, seq=None):
    t, d = x.shape
    feature_major = cache_prev is not None
    token_spec = lambda w: pl.BlockSpec((tm, w), lambda i: (i, 0))
    token_shape = lambda w: jax.ShapeDtypeStruct((t, w), F32)
    if feature_major:
        widths = (DIFF_W + GQA_W, DIFF_W, GQA_KV_W, SSD_INNER, XBC_W, LANES)
        cache_w = (DIFF_W, DIFF_W, GQA_KV_W)
        assert tm % seq == 0
        out_specs = [token_spec(w) for w in widths] + [
            pl.BlockSpec((tm // seq, None, w, seq), lambda i: (i, layer, 0, 0)) for w in cache_w]
        out_shape = [token_shape(w) for w in widths] + [
            jax.ShapeDtypeStruct((t // seq, depth, w, seq), F32) for w in cache_w]
        aliases = {4 + k: len(widths) + k for k in range(len(cache_prev))}
    else:
        widths = (DIFF_W + GQA_W, DIFF_W, DIFF_W, GQA_KV_W, GQA_KV_W, SSD_INNER, XBC_W, LANES)
        out_specs = [token_spec(w) for w in widths]
        out_shape = [token_shape(w) for w in widths]
        cache_prev, aliases = (), {}
    return pl.pallas_call(
        functools.partial(_inproj_kernel, feature_major, len(cache_prev)),
        grid=(t // tm,),
        in_specs=[
            pl.BlockSpec((tm, d), lambda i: (i, 0)),
            _mod_spec(layer, 0, row_fn, d),
            _mod_spec(layer, 1, row_fn, d),
            pl.BlockSpec((None, d, IN_W_PAD), lambda i: (layer, 0, 0)),
        ] + [pl.BlockSpec(memory_space=pl.ANY)] * len(cache_prev),
        out_specs=out_specs,
        out_shape=out_shape,
        input_output_aliases=aliases,
        scratch_shapes=([pltpu.VMEM((tm, DIFF_W), F32), pltpu.VMEM((tm, GQA_KV_W), F32)]
                        if feature_major else []),
        compiler_params=pltpu.CompilerParams(
            dimension_semantics=("arbitrary",), vmem_limit_bytes=VMEM_LIMIT),
    )(x, mods, mods, w_in_bf, *cache_prev)


LOG2E = 1.4426950408889634
NEG_BIG = -1e30
N_SCORE_HEADS = 2 * DIFF_HEADS + GQA_Q_HEADS
ONES_ROWS = 16


def _diff_lambda(lam_ref, lam_init):
    lp = lam_ref[...]
    s1 = jnp.sum(lp[0:1, :] * lp[1:2, :], axis=-1, keepdims=True)
    s2 = jnp.sum(lp[2:3, :] * lp[3:4, :], axis=-1, keepdims=True)
    return jnp.exp(s1) - jnp.exp(s2) + lam_init


def _swap_halves(x):
    return pltpu.roll(x, GQA_HD, 1)


def _attn_kernel(cfg, *refs):
    lam_init, seq, past, kb, rope, feature_major, n_prev = cfg
    it = iter(refs)
    qa_ref, dk_ref, dv_ref, gk_in_ref, gv_ref = next(it), next(it), next(it), next(it), next(it)
    for _ in range(n_prev):
        next(it)
    if past:
        cdk_ref, cdv_ref, cgk_ref, cgv_ref = next(it), next(it), next(it), next(it)
    if rope:
        cosq_ref, sinq_ref, cosk_ref, sink_ref = next(it), next(it), next(it), next(it)
    lam_ref, gain_ref, gq_ref, gk_ref = next(it), next(it), next(it), next(it)
    o_ref = next(it)
    gkn_ref = next(it) if feature_major else None
    kd_s, kg_s, vdt_s, vgt_s, wq_s, wg_s, s0_s, s1_s, m_s, l_s, acc_s = it
    tq = qa_ref.shape[0]
    n_blocks = (seq + past) // kb

    @pl.when(pl.program_id(1) == 0)
    def _prepare_keys():
        gmat = _group_avg_matrix(GQA_KV_W, GQA_HD)
        for i in range(seq // kb):
            rows = slice(i * kb, (i + 1) * kb)
            dk = dk_ref[rows, :]
            gk = gk_in_ref[rows, :]
            gk = gk * lax.rsqrt(_group_mean_sq(gk, gmat) + EPS) * gk_ref[...]
            if feature_major:
                gkn_ref[:, rows] = gk.T
            if rope:
                ck = cosk_ref[rows, :]
                sk = sink_ref[rows, :]
                dk = _rope(dk, ck[:, 0:DIFF_W], sk[:, 0:DIFF_W])
                gk = _rope(gk, ck[:, DIFF_W:DIFF_W + GQA_KV_W], sk[:, DIFF_W:DIFF_W + GQA_KV_W])
            kd_s[rows, :] = dk.astype(BF16)
            kg_s[rows, :] = gk.astype(BF16)
            if feature_major:
                vdt_s[i] = dv_ref[:, rows].astype(BF16)
                vgt_s[i] = gv_ref[:, rows].astype(BF16)
            else:
                vdt_s[i] = dv_ref[rows, :].T.astype(BF16)
                vgt_s[i] = gv_ref[rows, :].T.astype(BF16)
        for j in range(past // kb):
            src = slice(j * kb, (j + 1) * kb)
            dst = slice(seq + j * kb, seq + (j + 1) * kb)
            kd_s[dst, :] = cdk_ref[:, src].T.astype(BF16)
            kg_s[dst, :] = cgk_ref[:, src].T.astype(BF16)
            vdt_s[seq // kb + j] = cdv_ref[:, src].astype(BF16)
            vgt_s[seq // kb + j] = cgv_ref[:, src].astype(BF16)

    qa = qa_ref[...]
    qd = qa[:, 0:DIFF_W]
    gq = qa[:, DIFF_W:DIFF_W + GQA_W]
    gq = gq * lax.rsqrt(_group_mean_sq(gq, _group_avg_matrix(GQA_W, GQA_HD)) + EPS) * gq_ref[...]
    if rope:
        cq = cosq_ref[...]
        sq = sinq_ref[...]
        qd = _rope(qd, cq[:, 0:DIFF_W], sq[:, 0:DIFF_W])
        gq = _rope(gq, cq[:, DIFF_W:DIFF_W + GQA_W], sq[:, DIFF_W:DIFF_W + GQA_W])
    qd_t = (qd * (DIFF_QK ** -0.5 * LOG2E)).T
    gq_t = (gq * (GQA_HD ** -0.5 * LOG2E)).T
    row = lax.broadcasted_iota(jnp.int32, (DIFF_W, tq), 0)
    for hm in range(2 * DIFF_HEADS):
        lo = hm * DIFF_QK
        wq_s[:, hm * tq:(hm + 1) * tq] = jnp.where((row >= lo) & (row < lo + DIFF_QK), qd_t, 0.0).astype(BF16)
    zeros = jnp.zeros((GQA_HD, tq), F32)
    for h in range(GQA_Q_HEADS):
        piece = gq_t[h * GQA_HD:(h + 1) * GQA_HD, :]
        pair = [piece, zeros] if h // (GQA_Q_HEADS // GQA_KV_HEADS) == 0 else [zeros, piece]
        wg_s[:, h * tq:(h + 1) * tq] = jnp.concatenate(pair, axis=0).astype(BF16)
    m_s[...] = jnp.full(m_s.shape, NEG_BIG, F32)
    l_s[...] = jnp.zeros(l_s.shape, F32)
    acc_s[...] = jnp.zeros(acc_s.shape, F32)
    n_diff = 2 * DIFF_HEADS * tq
    n_all = N_SCORE_HEADS * tq

    slabs = tq // LANES

    def key_rows(j):
        return pl.ds(j * kb if isinstance(j, int) else pl.multiple_of(j * kb, kb), kb)

    def head_scores(idx, k_d, k_g, s_buf):
        if idx < 2 * DIFF_HEADS:
            s = _dot(k_d, wq_s[:, idx * tq:(idx + 1) * tq])
        else:
            h = idx - 2 * DIFF_HEADS
            s = _dot(k_g, wg_s[:, h * tq:(h + 1) * tq])
        for k in range(slabs):
            s_buf[idx * slabs + k] = s[:, k * LANES:(k + 1) * LANES]

    def head_update(idx, v_d, v_g, s_buf):
        ps, alphas = [], []
        for k in range(slabs):
            c = idx * slabs + k
            cols = slice(c * LANES, (c + 1) * LANES)
            s = s_buf[c]
            m_old = m_s[:, cols]
            m_new = jnp.maximum(m_old, jnp.max(s, axis=0, keepdims=True))
            alphas.append(jnp.exp2(m_old - m_new))
            ps.append(jnp.exp2(s - m_new).astype(BF16))
            m_s[:, cols] = m_new
        if idx < 2 * DIFF_HEADS:
            vh = idx // 2
            v_t = v_d[vh * DIFF_V:(vh + 1) * DIFF_V, :]
        else:
            vh = (idx - 2 * DIFF_HEADS) // (GQA_Q_HEADS // GQA_KV_HEADS)
            v_t = v_g[vh * GQA_HD:(vh + 1) * GQA_HD, :]
        v_ext = jnp.concatenate([v_t, jnp.ones((ONES_ROWS, kb), BF16)], axis=0)
        alpha = jnp.concatenate(alphas, axis=1)
        pv = _dot(v_ext, jnp.concatenate(ps, axis=1))
        cols = slice(idx * tq, (idx + 1) * tq)
        acc_s[idx] = alpha * acc_s[idx] + pv[0:GQA_HD, :]
        l_s[:, cols] = alpha * l_s[:, cols] + pv[GQA_HD:GQA_HD + 1, :]

    def key_block(j, s_cur, j_next, s_next):
        v_d = vdt_s[j]
        v_g = vgt_s[j]
        if j_next is not None:
            k_d = kd_s[key_rows(j_next), :]
            k_g = kg_s[key_rows(j_next), :]
        for idx in range(N_SCORE_HEADS):
            if j_next is not None:
                head_scores(idx, k_d, k_g, s_next)
            head_update(idx, v_d, v_g, s_cur)

    for idx in range(N_SCORE_HEADS):
        head_scores(idx, kd_s[key_rows(0), :], kg_s[key_rows(0), :], s0_s)
    if n_blocks > 1:
        def block_pair(i, carry):
            key_block(2 * i, s0_s, 2 * i + 1, s1_s)
            key_block(2 * i + 1, s1_s, 2 * i + 2, s0_s)
            return carry

        lax.fori_loop(0, n_blocks // 2 - 1, block_pair, 0)
        key_block(n_blocks - 2, s0_s, n_blocks - 1, s1_s)
        key_block(n_blocks - 1, s1_s, None, None)
    else:
        key_block(0, s0_s, None, None)

    lam = _diff_lambda(lam_ref, lam_init)
    outs = []
    for h in range(DIFF_HEADS):
        c0 = slice(2 * h * tq, (2 * h + 1) * tq)
        c1 = slice((2 * h + 1) * tq, (2 * h + 2) * tq)
        o = acc_s[2 * h] * (1.0 / l_s[:, c0]) - acc_s[2 * h + 1] * (lam / l_s[:, c1])
        ms = jnp.mean(o * o, axis=0, keepdims=True)
        outs.append(o * (lax.rsqrt(ms + EPS) * (1.0 - lam_init)))
    for h in range(GQA_Q_HEADS):
        idx = 2 * DIFF_HEADS + h
        outs.append(acc_s[idx] * (1.0 / l_s[:, idx * tq:(idx + 1) * tq]))
    o_ref[...] = jnp.concatenate(outs, axis=0).T * gain_ref[...]


def _attention(qa, kv, caches, layer, tables, lam_p, gain, gq_t, gk_t, lam_init, batch, seq, tq,
               gkn_prev=None):
    t = qa.shape[0]
    nq = seq // tq
    feature_major = gkn_prev is not None
    past = caches[0].shape[3] if caches is not None else 0
    kb = min(256, seq)
    assert seq % kb == 0 and past % kb == 0 and seq % tq == 0
    n_blocks = (seq + past) // kb
    assert n_blocks == 1 or n_blocks % 2 == 0
    n_all = N_SCORE_HEADS * tq
    full = lambda a: pl.BlockSpec(a.shape, lambda b, q: (0,) * a.ndim)
    token_major = lambda a: pl.BlockSpec((seq, a.shape[1]), lambda b, q: (b, 0))
    by_feature = lambda a: pl.BlockSpec((None, None, a.shape[2], seq), lambda b, q: (b, layer, 0, 0))
    dk, dv, gk, gv = kv
    in_specs = [pl.BlockSpec((tq, qa.shape[1]), lambda b, q: (b * nq + q, 0)),
                token_major(dk), by_feature(dv) if feature_major else token_major(dv),
                token_major(gk), by_feature(gv) if feature_major else token_major(gv)]
    args = [qa, dk, dv, gk, gv]
    aliases = {}
    if feature_major:
        assert nq == 1
        in_specs += [pl.BlockSpec(memory_space=pl.ANY)] * len(gkn_prev)
        args += list(gkn_prev)
        aliases = {5 + k: 1 + k for k in range(len(gkn_prev))}
    if caches is not None:
        in_specs += [pl.BlockSpec((None, None, a.shape[2], past), lambda b, q: (b, layer, 0, 0))
                     for a in caches]
        args += list(caches)
    if tables is not None:
        cosq, sinq, cosk, sink = tables
        in_specs += [pl.BlockSpec((tq, cosq.shape[1]), lambda b, q: (q, 0)),
                     pl.BlockSpec((tq, sinq.shape[1]), lambda b, q: (q, 0)),
                     full(cosk), full(sink)]
        args += [cosq, sinq, cosk, sink]
    in_specs += [full(lam_p), full(gain), full(gq_t), full(gk_t)]
    args += [lam_p, gain, gq_t, gk_t]
    out_specs = [pl.BlockSpec((tq, DIFF_W + GQA_W), lambda b, q: (b * nq + q, 0))]
    out_shape = [jax.ShapeDtypeStruct((t, DIFF_W + GQA_W), F32)]
    if feature_major:
        out_specs.append(pl.BlockSpec((None, None, GQA_KV_W, seq), lambda b, q: (b, layer, 0, 0)))
        out_shape.append(jax.ShapeDtypeStruct((batch, dv.shape[1], GQA_KV_W, seq), F32))
    cfg = (lam_init, seq, past, kb, tables is not None, feature_major,
           len(gkn_prev) if feature_major else 0)
    return pl.pallas_call(
        functools.partial(_attn_kernel, cfg),
        grid=(batch, nq),
        in_specs=in_specs,
        out_specs=out_specs,
        out_shape=out_shape,
        input_output_aliases=aliases,
        scratch_shapes=[
            pltpu.VMEM((seq + past, DIFF_W), BF16), pltpu.VMEM((seq + past, GQA_KV_W), BF16),
            pltpu.VMEM((n_blocks, DIFF_W, kb), BF16), pltpu.VMEM((n_blocks, GQA_KV_W, kb), BF16),
            pltpu.VMEM((DIFF_W, 2 * DIFF_HEADS * tq), BF16), pltpu.VMEM((GQA_KV_W, GQA_Q_HEADS * tq), BF16),
            pltpu.VMEM((n_all // LANES, kb, LANES), F32), pltpu.VMEM((n_all // LANES, kb, LANES), F32),
            pltpu.VMEM((1, n_all), F32), pltpu.VMEM((1, n_all), F32),
            pltpu.VMEM((N_SCORE_HEADS, GQA_HD, tq), F32),
        ],
        compiler_params=pltpu.CompilerParams(
            dimension_semantics=("arbitrary", "arbitrary"), vmem_limit_bytes=VMEM_LIMIT),
    )(*args)


def _softplus(x):
    return jnp.maximum(x, 0.0) + jnp.log1p(jnp.exp(-jnp.abs(x)))


def _dot2_l(a, b_exact):
    a1 = a.astype(BF16)
    a2 = (a - a1.astype(F32)).astype(BF16)
    return _dot(a1, b_exact) + _dot(a2, b_exact)


def _ssd_kernel(has_init, n_prev, seq, *refs):
    it = iter(refs)
    z_ref, xbc_ref, dt_ref, cw_ref, cb_ref, alog_ref, dtb_ref, dexp_ref, ng_ref = (
        next(it) for _ in range(9))
    sf0_ref, sb0_ref = (next(it), next(it)) if has_init else (None, None)
    for _ in range(n_prev):
        next(it)
    o_ref, sf_ref, sb_ref = next(it), next(it), next(it)
    stf_s, stb_s, inc_s, dec_s, eab_s, cbf_s, exp_s, tri_s, gm_s = it
    L = SSD_CHUNK
    W2 = 2 * SSD_INNER
    nchunks = seq // L
    halo = 8

    ri = lax.broadcasted_iota(jnp.int32, (L, L), 0)
    ci = lax.broadcasted_iota(jnp.int32, (L, L), 1)
    lower = ri >= ci
    upper = ri <= ci
    tri_s[0] = jnp.where(lower, 1.0, 0.0).astype(BF16)
    tri_s[1] = jnp.where(upper, 1.0, 0.0).astype(BF16)
    lane_row = lax.broadcasted_iota(jnp.int32, (1, LANES), 1)
    a_row = jnp.where(lane_row < DT_W, -jnp.exp(alog_ref[...]), 0.0)
    ej = lax.broadcasted_iota(jnp.int32, (LANES, W2), 0)
    eh = lax.shift_right_logical(lax.broadcasted_iota(jnp.int32, (LANES, W2), 1), 6)
    exp_s[...] = jnp.where(ej == eh, 1.0, 0.0).astype(BF16)
    gm_s[...] = jnp.where(lax.shift_right_logical(ej, 6) == (lax.shift_right_logical(eh, 2) & 1), 1.0, 0.0)

    def chunk_rows(c):
        return pl.ds(pl.multiple_of(c * L, L), L)

    def load_state(ref):
        r = lax.broadcasted_iota(jnp.int32, (SSD_STATE, LANES), 0)
        c = lax.broadcasted_iota(jnp.int32, (SSD_STATE, LANES), 1)
        dup = jnp.where((c & (SSD_STATE - 1)) == r, 1.0, 0.0).astype(BF16)
        return _dot3_l(ref[...], dup).T * gm_s[:, 0:SSD_INNER]

    def store_state(st_ref, ref):
        st_t = st_ref[...].T
        ref[...] = (st_t + _swap_halves(st_t))[:, 0:SSD_STATE]

    stf_s[...] = load_state(sf0_ref) if has_init else jnp.zeros(stf_s.shape, F32)
    stb_s[...] = load_state(sb0_ref) if has_init else jnp.zeros(stb_s.shape, F32)

    def forward_pass(c, carry):
        r0 = c * L
        rows = chunk_rows(c)
        dt_c = _softplus(dt_ref[rows, :] + dtb_ref[...])

        d1, d2, d3 = _split3(dt_c * a_row)
        acs_f = _dot(tri_s[0], d1) + (_dot(tri_s[0], d2) + _dot(tri_s[0], d3))
        acs_b = _dot(tri_s[1], d1) + (_dot(tri_s[1], d2) + _dot(tri_s[1], d3))
        acs = jnp.where(lax.broadcasted_iota(jnp.int32, (L, LANES), 1) < SSD_HEADS, acs_f, acs_b)
        acs_t = acs.T
        expand = exp_s[...]
        dt_e = _dot2_l(dt_c, expand)
        acs_e = _dot3_l(acs, expand)
        edge = jnp.concatenate([acs_e[L - 1:L, 0:SSD_INNER], acs_e[0:1, SSD_INNER:W2]], axis=1)
        eacs = jnp.exp(acs_e)
        cdec = jnp.exp(edge)

        prev = xbc_ref[pl.ds(pl.multiple_of(jnp.maximum(r0 - halo, 0), halo), halo), :]
        nxt = xbc_ref[pl.ds(pl.multiple_of(jnp.minimum(r0 + L, seq - halo), halo), halo), :]
        cur = xbc_ref[rows, :]
        win = jnp.concatenate([jnp.where(c > 0, prev, 0.0), cur,
                               jnp.where(c < nchunks - 1, nxt, 0.0)], axis=0)
        acc = cb_ref[...] + cur * cw_ref[SSD_CONV // 2:SSD_CONV // 2 + 1, :]
        for j in range(SSD_CONV):
            if j != SSD_CONV // 2:
                shifted = pltpu.roll(win, (SSD_CONV // 2 - j) % (L + 2 * halo), 0)[halo:halo + L, :]
                acc = acc + shifted * cw_ref[j:j + 1, :]
        act = _silu(acc)
        x_c = act[:, 0:SSD_INNER]
        b_c = act[:, SSD_INNER:SSD_INNER + SSD_BC_W]
        c_c = act[:, SSD_INNER + SSD_BC_W:XBC_W]
        xd = jnp.concatenate([x_c, x_c], axis=1) * dt_e
        xd_b = xd.astype(BF16)
        xdw = (xd * jnp.exp(edge - acs_e)).astype(BF16)
        b_b = b_c.astype(BF16)
        c_b = c_c.astype(BF16)
        s_new = _dot(b_c.T.astype(BF16), xdw) * gm_s[...]

        st_f = stf_s[...]
        y = _dot(c_b, st_f.astype(BF16)) * eacs[:, 0:SSD_INNER]
        stf_s[...] = st_f * cdec[:, 0:SSD_INNER] + s_new[:, 0:SSD_INNER]
        inc_s[c] = s_new[:, SSD_INNER:W2]
        dec_s[c] = cdec[:, SSD_INNER:W2]
        eab_s[rows, :] = eacs[:, SSD_INNER:W2]
        cbf_s[rows, :] = c_b

        cbs = []
        for g in range(SSD_GROUPS):
            cg = jnp.where(_lane_mask(c_c.shape, g * SSD_STATE, (g + 1) * SSD_STATE), c_c, 0.0)
            cbs.append(_dot_nt(cg.astype(BF16), b_b))
        for direction, causal in ((0, lower), (1, upper)):
            pairs = []
            for g in range(SSD_GROUPS):
                cb = cbs[g]
                for hp in range(2):
                    pair = g * 2 + hp
                    res = []
                    for k in range(2):
                        j = direction * SSD_HEADS + pair * 2 + k
                        diff = acs[:, j:j + 1] - acs_t[j:j + 1, :]
                        dec = jnp.where(causal, jnp.exp(jnp.minimum(diff, 0.0)), 0.0)
                        sc = (cb * dec).astype(BF16)
                        lo = direction * SSD_INNER + pair * LANES
                        res.append(_dot(sc, xd_b[:, lo:lo + LANES]))
                    pairs.append(jnp.where(_lane_mask(res[0].shape, 0, SSD_HD), res[0], res[1]))
            y = y + jnp.concatenate(pairs, axis=-1)
        o_ref[rows, :] = y + x_c * dexp_ref[...]
        return carry

    lax.fori_loop(0, nchunks, forward_pass, 0)
    store_state(stf_s, sf_ref)

    def backward_pass(i, carry):
        c = nchunks - 1 - i
        rows = chunk_rows(c)
        st_b = stb_s[...]
        y = o_ref[rows, :] + _dot(cbf_s[rows, :], st_b.astype(BF16)) * eab_s[rows, :]
        stb_s[...] = st_b * dec_s[c] + inc_s[c]
        yt = y * _silu(z_ref[rows, :])
        ms = jnp.mean(yt * yt, axis=-1, keepdims=True)
        o_ref[rows, :] = yt * lax.rsqrt(ms + EPS) * ng_ref[...]
        return carry

    lax.fori_loop(0, nchunks, backward_pass, 0, unroll=2)
    store_state(stb_s, sb_ref)


def _ssd(z, xbc, dt, params, init, layer, batch, seq, state_prev=(), depth=1):
    cw, cb, alog, dtb, dexp, ng = params
    t = z.shape[0]
    has_init = init is not None
    nchunks = seq // SSD_CHUNK
    full = lambda a: pl.BlockSpec(a.shape, lambda b: (0,) * a.ndim)
    out_layer = layer if depth > 1 else 0
    st_spec = pl.BlockSpec((None, None, SSD_INNER, SSD_STATE), lambda b: (b, out_layer, 0, 0))
    in_specs = [
        pl.BlockSpec((seq, SSD_INNER), lambda b: (b, 0)),
        pl.BlockSpec((seq, XBC_W), lambda b: (b, 0)),
        pl.BlockSpec((seq, LANES), lambda b: (b, 0)),
        full(cw), full(cb), full(alog), full(dtb), full(dexp), full(ng),
    ]
    args = [z, xbc, dt, cw, cb, alog, dtb, dexp, ng]
    if has_init:
        init_spec = pl.BlockSpec((None, None, SSD_INNER, SSD_STATE), lambda b: (b, layer, 0, 0))
        in_specs += [init_spec, init_spec]
        args += list(init)
    aliases = {len(args) + k: 1 + k for k in range(len(state_prev))}
    in_specs += [pl.BlockSpec(memory_space=pl.ANY)] * len(state_prev)
    args += list(state_prev)
    st_shape = jax.ShapeDtypeStruct((batch, depth, SSD_INNER, SSD_STATE), F32)
    return pl.pallas_call(
        functools.partial(_ssd_kernel, has_init, len(state_prev), seq),
        grid=(batch,),
        in_specs=in_specs,
        out_specs=[pl.BlockSpec((seq, SSD_INNER), lambda b: (b, 0)), st_spec, st_spec],
        out_shape=[jax.ShapeDtypeStruct((t, SSD_INNER), F32), st_shape, st_shape],
        input_output_aliases=aliases,
        scratch_shapes=[
            pltpu.VMEM((LANES, SSD_INNER), F32), pltpu.VMEM((LANES, SSD_INNER), F32),
            pltpu.VMEM((nchunks, LANES, SSD_INNER), F32),
            pltpu.VMEM((nchunks, 1, SSD_INNER), F32),
            pltpu.VMEM((seq, SSD_INNER), F32),
            pltpu.VMEM((seq, SSD_BC_W), BF16),
            pltpu.VMEM((LANES, 2 * SSD_INNER), BF16),
            pltpu.VMEM((2, SSD_CHUNK, SSD_CHUNK), BF16),
            pltpu.VMEM((LANES, 2 * SSD_INNER), F32),
        ],
        compiler_params=pltpu.CompilerParams(
            dimension_semantics=("arbitrary",), vmem_limit_bytes=VMEM_LIMIT),
    )(*args)


def _post_kernel(alpha, d_ff, ff_chunk, x_ref, oa_ref, os_ref, g1_ref, sh2_ref, sc2_ref, g2_ref,
                 wo_ref, wfi_ref, wfo_ref, lng_ref, lnb_ref, y_ref):
    wa = oa_ref.shape[1]
    half = x_ref.shape[0] // 2
    bounds = list(range(0, d_ff, ff_chunk)) + [d_ff]
    n_chunks = len(bounds) - 1

    def mix_and_norm(r):
        o = (_dot(oa_ref[r, :].astype(BF16), wo_ref[0:wa, :])
             + _dot(os_ref[r, :].astype(BF16), wo_ref[wa:, :]))
        x1 = _layer_norm(alpha * x_ref[r, :] + g1_ref[...] * o, lng_ref[0:1, :], lnb_ref[0:1, :])
        return x1, (x1 * (1.0 + sc2_ref[...]) + sh2_ref[...]).astype(BF16)

    def up(h2, c):
        lo, hi = bounds[c], bounds[c + 1]
        return _dot(h2, wfi_ref[:, lo:hi]), _dot(h2, wfi_ref[:, d_ff + lo:d_ff + hi])

    def ffn_chunks(h2):
        f = jnp.zeros((half, x_ref.shape[1]), F32)
        g, u = up(h2, 0)
        for c in range(n_chunks):
            nxt = up(h2, c + 1) if c + 1 < n_chunks else None
            f = f + _dot((_silu(g) * u).astype(BF16), wfo_ref[bounds[c]:bounds[c + 1], :])
            if nxt is not None:
                g, u = nxt
            yield f

    def finish(r, x1, f):
        y_ref[r, :] = _layer_norm(alpha * x1 + g2_ref[...] * f, lng_ref[1:2, :], lnb_ref[1:2, :])

    ra, rb = slice(0, half), slice(half, 2 * half)
    x1a, h2a = mix_and_norm(ra)
    ffn_a = ffn_chunks(h2a)
    for _ in range(n_chunks // 2):
        fa = next(ffn_a)
    x1b, h2b = mix_and_norm(rb)
    for fa in ffn_a:
        pass
    ffn_b = ffn_chunks(h2b)
    for _ in range(n_chunks // 2):
        fb = next(ffn_b)
    finish(ra, x1a, fa)
    for fb in ffn_b:
        pass
    finish(rb, x1b, fb)


def _post(x, oa, os_, mods, layer, row_fn, wo_bf, wfi_bf, wfo_bf, lng, lnb, alpha, tm):
    t, d = x.shape
    d_ff = wfo_bf.shape[1]
    ff_chunk = 3 * MXU_TILE
    assert d_ff % MXU_TILE == 0
    per_layer = lambda a, **kw: pl.BlockSpec((None,) + a.shape[1:], lambda i: (layer, 0, 0), **kw)
    resident = lambda a: per_layer(a, pipeline_mode=pl.Buffered(1))
    return pl.pallas_call(
        functools.partial(_post_kernel, alpha, d_ff, ff_chunk),
        grid=(t // tm,),
        in_specs=[
            pl.BlockSpec((tm, d), lambda i: (i, 0)),
            pl.BlockSpec((tm, oa.shape[1]), lambda i: (i, 0)),
            pl.BlockSpec((tm, os_.shape[1]), lambda i: (i, 0)),
            _mod_spec(layer, 2, row_fn, d),
            _mod_spec(layer, 3, row_fn, d),
            _mod_spec(layer, 4, row_fn, d),
            _mod_spec(layer, 5, row_fn, d),
            resident(wo_bf), resident(wfi_bf), resident(wfo_bf),
            per_layer(lng), per_layer(lnb),
        ],
        out_specs=pl.BlockSpec((tm, d), lambda i: (i, 0)),
        out_shape=jax.ShapeDtypeStruct((t, d), F32),
        compiler_params=pltpu.CompilerParams(
            dimension_semantics=("arbitrary",), vmem_limit_bytes=VMEM_LIMIT),
    )(x, oa, os_, mods, mods, mods, mods, wo_bf, wfi_bf, wfo_bf, lng, lnb)


def _rope_tables(rows, dim, copies):
    row = jnp.repeat(jnp.arange(rows), GRID_W).astype(F32)
    col = jnp.tile(jnp.arange(GRID_W), rows).astype(F32)
    n_freq = dim // 4
    inv = ROPE_THETA ** (-jnp.arange(n_freq, dtype=F32) / n_freq)
    ang = jnp.concatenate([row[:, None] * inv, col[:, None] * inv], -1)
    cos, sin = jnp.cos(ang), jnp.sin(ang)
    cos_full = jnp.repeat(cos, 2, axis=-1)
    sin_signed = jnp.stack([-sin, sin], axis=-1).reshape(sin.shape[0], dim)
    return jnp.tile(cos_full, (1, copies)), jnp.tile(sin_signed, (1, copies))


def kernel(x_prompt, x_sample, cache_diff_k, cache_diff_v, cache_gqa_k, cache_gqa_v, state_ssd_fwd, state_ssd_bwd, c, c_ctx, w_ada, b_ada, w_in, w_out, diff_lambda, diff_subln_g, qk_norm_g, ssd_conv_w, ssd_conv_b, ssd_A_log, ssd_dt_bias, ssd_D, ssd_norm_g, ln_g, ln_b, w_ffn_in, w_ffn_out):
    batch, seq, d = x_prompt.shape
    dec_batch, dec_seq, _ = x_sample.shape
    depth = w_in.shape[0]
    past = cache_diff_k.shape[2]
    alpha = (2 * depth) ** 0.25
    rows = dec_seq // GRID_W

    n_vec = 1 + dec_batch
    n_pad = -(-n_vec // 8) * 8
    cvec = jnp.concatenate([c_ctx[None, :], c, jnp.zeros((n_pad - n_vec, d), F32)], axis=0)
    mods = _modulation(cvec, w_ada, b_ada).reshape(depth, n_pad, 1, 6 * d)

    cos_d, sin_d = _rope_tables(rows, DIFF_QK, DIFF_W // DIFF_QK)
    cos_g, sin_g = _rope_tables(rows, GQA_HD, GQA_Q_HEADS)
    cosq = jnp.concatenate([cos_d, cos_g], axis=-1)
    sinq = jnp.concatenate([sin_d, sin_g], axis=-1)
    cosk = jnp.concatenate([cos_d, cos_g[:, :GQA_KV_W]], axis=-1)
    sink = jnp.concatenate([sin_d, sin_g[:, :GQA_KV_W]], axis=-1)
    tables = (cosq, sinq, cosk, sink)

    feature_major = lambda a, w: jnp.transpose(a.reshape(dec_batch, depth, past, w), (0, 1, 3, 2))
    caches = (feature_major(cache_diff_k, DIFF_W), feature_major(cache_diff_v, DIFF_W),
              feature_major(cache_gqa_k, GQA_KV_W), feature_major(cache_gqa_v, GQA_KV_W))

    xp = x_prompt.reshape(batch * seq, d)
    xs = x_sample.reshape(dec_batch * dec_seq, d)
    tm_ctx = min(512, batch * seq)
    tm_lat = min(512, dec_seq)
    tm_post_ctx = min(1024, batch * seq)
    tm_post_lat = min(1024, dec_seq)
    tq = min(256, dec_seq)
    ctx_row = lambda i: 0
    lat_row = lambda i: 1 + (i * tm_lat) // dec_seq
    lat_row_post = lambda i: 1 + (i * tm_post_lat) // dec_seq

    w_in_bf = jnp.pad(w_in, ((0, 0), (0, 0), (0, IN_W_PAD - w_in.shape[2]))).astype(BF16)
    wo_bf = w_out.astype(BF16)
    wfi_bf = w_ffn_in.astype(BF16)
    wfo_bf = w_ffn_out.astype(BF16)
    init = (state_ssd_fwd.reshape(dec_batch, depth, SSD_INNER, SSD_STATE),
            state_ssd_bwd.reshape(dec_batch, depth, SSD_INNER, SSD_STATE))

    cache = lambda w: jnp.zeros((batch, depth, w, seq), F32)
    kv_cache = (cache(DIFF_W), cache(DIFF_W), cache(GQA_KV_W))
    gk_cache = (cache(GQA_KV_W),)
    ssd_states = tuple(jnp.zeros((batch, depth, SSD_INNER, SSD_STATE), F32) for _ in range(2))
    for l in range(depth):
        lam_init = 0.8 - 0.6 * math.exp(-0.3 * l)
        lam_p = diff_lambda[l]
        attn_gain = jnp.concatenate([jnp.tile(diff_subln_g[l], DIFF_HEADS), jnp.ones((GQA_W,), F32)])[None, :]
        gq_t = jnp.tile(qk_norm_g[l, 0], GQA_Q_HEADS)[None, :]
        gk_t = jnp.tile(qk_norm_g[l, 1], GQA_KV_HEADS)[None, :]
        pad_row = lambda v: jnp.pad(v.reshape(1, -1), ((0, 0), (0, LANES - v.size)))
        ssd_params = (ssd_conv_w[l], ssd_conv_b[l][None, :], pad_row(ssd_A_log[l]),
                      pad_row(ssd_dt_bias[l]), jnp.repeat(ssd_D[l], SSD_HD)[None, :],
                      ssd_norm_g[l][None, :])

        qa, dk, gk, z, xbc, dt, *kv_cache = _inproj(xp, mods, l, ctx_row, w_in_bf, tm_ctx,
                                                    cache_prev=tuple(kv_cache), depth=depth, seq=seq)
        oa, gkn = _attention(qa, (dk, kv_cache[1], gk, kv_cache[2]), None, l, None, lam_p, attn_gain,
                             gq_t, gk_t, lam_init, batch, seq, seq, gkn_prev=gk_cache)
        gk_cache = (gkn,)
        os_, *ssd_states = _ssd(z, xbc, dt, ssd_params, None, l, batch, seq,
                                state_prev=tuple(ssd_states), depth=depth)
        xp = _post(xp, oa, os_, mods, l, ctx_row, wo_bf, wfi_bf, wfo_bf, ln_g, ln_b, alpha, tm_post_ctx)

        qa, dk, dv, gk, gv, z, xbc, dt = _inproj(xs, mods, l, lat_row, w_in_bf, tm_lat)
        (oa,) = _attention(qa, (dk, dv, gk, gv), caches, l, tables, lam_p, attn_gain, gq_t, gk_t,
                           lam_init, dec_batch, dec_seq, tq)
        os_, _, _ = _ssd(z, xbc, dt, ssd_params, init, l, dec_batch, dec_seq)
        xs = _post(xs, oa, os_, mods, l, lat_row_post, wo_bf, wfi_bf, wfo_bf, ln_g, ln_b, alpha,
                   tm_post_lat)

    def token_major(a, heads, width):
        return jnp.transpose(a.reshape(batch, depth, heads, width, seq), (0, 1, 4, 2, 3))

    state = lambda a: a.reshape(batch, depth, SSD_HEADS, SSD_HD, SSD_STATE)
    return (xp.reshape(batch, seq, d), xs.reshape(dec_batch, dec_seq, d),
            token_major(kv_cache[0], DIFF_HEADS, 2 * DIFF_QK), token_major(kv_cache[1], DIFF_HEADS, DIFF_V),
            token_major(gk_cache[0], GQA_KV_HEADS, GQA_HD), token_major(kv_cache[2], GQA_KV_HEADS, GQA_HD),
            state(ssd_states[0]), state(ssd_states[1]))
```

```python
import functools
import math

import jax
import jax.numpy as jnp
from jax import lax
from jax.experimental import pallas as pl
from jax.experimental.pallas import tpu as pltpu

F32 = jnp.float32
BF16 = jnp.bfloat16

GRID_W = 64
DIFF_HEADS = 4
DIFF_QK = 32
DIFF_V = 64
DIFF_W = DIFF_HEADS * DIFF_V
GQA_HD = 64
GQA_Q_HEADS = 4
GQA_KV_HEADS = 2
GQA_W = GQA_Q_HEADS * GQA_HD
GQA_KV_W = GQA_KV_HEADS * GQA_HD
SSD_HD = 64
SSD_HEADS = 8
SSD_INNER = SSD_HEADS * SSD_HD
SSD_GROUPS = 2
SSD_STATE = 64
SSD_BC_W = SSD_GROUPS * SSD_STATE
SSD_CONV = 5
SSD_CHUNK = 128
XBC_W = SSD_INNER + 2 * SSD_BC_W
DT_W = 2 * SSD_HEADS
ROPE_THETA = 10000.0
EPS = 1e-5
LANES = 128
MXU_TILE = 256
VMEM_LIMIT = 56 * 1024 * 1024

_C_DQ, _C_DK, _C_DV, _C_GQ, _C_GK, _C_GV, _C_Z, _C_XBC, _C_DT, _C_END = (
    0, 256, 512, 768, 1024, 1152, 1280, 1792, 2560, 2576)
IN_W_PAD = _C_DT + LANES


def _dot(a, b):
    return jnp.dot(a, b, preferred_element_type=F32)


def _dot_nt(a, b):
    return lax.dot_general(a, b, (((1,), (1,)), ((), ())), preferred_element_type=F32)


def _split3(a):
    a1 = a.astype(BF16)
    r1 = a - a1.astype(F32)
    a2 = r1.astype(BF16)
    a3 = (r1 - a2.astype(F32)).astype(BF16)
    return a1, a2, a3


def _dot3_l(a, b_exact):
    a1, a2, a3 = _split3(a)
    return _dot(a1, b_exact) + (_dot(a2, b_exact) + _dot(a3, b_exact))


def _sigmoid(x):
    return 1.0 / (1.0 + jnp.exp(-x))


def _silu(x):
    return x * _sigmoid(x)


def _layer_norm(x, g, b):
    mu = jnp.mean(x, axis=-1, keepdims=True)
    xc = x - mu
    var = jnp.mean(xc * xc, axis=-1, keepdims=True)
    return xc * lax.rsqrt(var + EPS) * g + b


def _group_avg_matrix(width, group):
    sh = int(math.log2(group))
    r = lax.shift_right_logical(lax.broadcasted_iota(jnp.int32, (width, width), 0), sh)
    c = lax.shift_right_logical(lax.broadcasted_iota(jnp.int32, (width, width), 1), sh)
    return jnp.where(r == c, 1.0 / group, 0.0).astype(BF16)


def _group_mean_sq(x, gmat):
    xx = x * x
    hi = xx.astype(BF16)
    lo = (xx - hi.astype(F32)).astype(BF16)
    return _dot(hi, gmat) + _dot(lo, gmat)


def _rope(x, cos, sin_signed):
    w = x.shape[-1]
    lane = lax.broadcasted_iota(jnp.int32, x.shape, 1)
    nxt = pltpu.roll(x, w - 1, 1)
    prv = pltpu.roll(x, 1, 1)
    partner = jnp.where((lane & 1) == 0, nxt, prv)
    return x * cos + partner * sin_signed


def _lane_mask(shape, lo, hi):
    lane = lax.broadcasted_iota(jnp.int32, shape, 1)
    return (lane >= lo) & (lane < hi)


def _mod_kernel(c_ref, w_ref, b_ref, o_ref):
    a = _silu(c_ref[...])
    a_hi = a.astype(BF16)
    a_lo = (a - a_hi.astype(F32)).astype(BF16)
    w = w_ref[...]
    w_hi = w.astype(BF16)
    w_lo = (w - w_hi.astype(F32)).astype(BF16)
    o_ref[...] = _dot(a_hi, w_hi) + (_dot(a_lo, w_hi) + _dot(a_hi, w_lo)) + b_ref[...]


def _modulation(cvec, w_ada, b_ada):
    depth, d, n = w_ada.shape
    tn = 1536
    rows = cvec.shape[0]
    return pl.pallas_call(
        _mod_kernel,
        grid=(depth, n // tn),
        in_specs=[
            pl.BlockSpec((rows, d), lambda l, j: (0, 0)),
            pl.BlockSpec((None, d, tn), lambda l, j: (l, 0, j)),
            pl.BlockSpec((None, 1, tn), lambda l, j: (l, 0, j)),
        ],
        out_specs=pl.BlockSpec((None, rows, tn), lambda l, j: (l, 0, j)),
        out_shape=jax.ShapeDtypeStruct((depth, rows, n), F32),
        compiler_params=pltpu.CompilerParams(
            dimension_semantics=("arbitrary", "arbitrary"), vmem_limit_bytes=VMEM_LIMIT),
    )(cvec, w_ada, b_ada.reshape(depth, 1, n))


def _put_layer_slot(ref, lead, layer, val):
    if len(ref.shape) == val.ndim + len(lead):
        ref[lead if lead else ...] = val
    else:
        for l in range(ref.shape[len(lead)]):
            ref[lead + (l,)] = val if l == layer else jnp.zeros_like(val)


def _inproj_kernel(feature_major_cache, n_prev, layer, *refs):
    x_ref, sh_ref, sc_ref, w_ref = refs[:4]
    outs = refs[4 + n_prev:]
    h = (x_ref[...] * (1.0 + sc_ref[...]) + sh_ref[...]).astype(BF16)

    def mm(lo, hi):
        return _dot(h, w_ref[:, lo:hi])

    gkv = mm(_C_GK, _C_Z)
    if feature_major_cache:
        (qa_ref, dk_ref, gk_ref, z_ref, xbc_ref, dt_ref, dkt_ref, dvt_ref, gvt_ref,
         dv_s, gv_s) = outs
        dk_ref[...] = mm(_C_DK, _C_DV)
        dv_s[...] = mm(_C_DV, _C_GQ)
        gv_s[...] = gkv[:, GQA_KV_W:]
        seq = dkt_ref.shape[-1]
        for j in range(dkt_ref.shape[0]):
            rows = slice(j * seq, (j + 1) * seq)
            _put_layer_slot(dkt_ref, (j,), layer, dk_ref[rows, :].T)
            _put_layer_slot(dvt_ref, (j,), layer, dv_s[rows, :].T)
            _put_layer_slot(gvt_ref, (j,), layer, gv_s[rows, :].T)
    else:
        qa_ref, dk_ref, dv_ref, gk_ref, gv_ref, z_ref, xbc_ref, dt_ref = outs
        dk_ref[...] = mm(_C_DK, _C_DV)
        dv_ref[...] = mm(_C_DV, _C_GQ)
        gv_ref[...] = gkv[:, GQA_KV_W:]
    qa_ref[:, 0:DIFF_W] = mm(_C_DQ, _C_DK)
    qa_ref[:, DIFF_W:DIFF_W + GQA_W] = mm(_C_GQ, _C_GK)
    gk_ref[...] = gkv[:, 0:GQA_KV_W]
    z_ref[...] = mm(_C_Z, _C_XBC)
    xbc_ref[...] = mm(_C_XBC, _C_DT)
    dt_ref[...] = mm(_C_DT, IN_W_PAD)


def _mod_spec(layer, which, row_fn, d):
    return pl.BlockSpec((None, None, 1, d), lambda i: (layer, row_fn(i), 0, which))


def _inproj(x, mods, layer, row_fn, w_in_bf, tm, cache_prev=None, depth=None, seq=None):
    t, d = x.shape
    feature_major = cache_prev is not None
    token_spec = lambda w: pl.BlockSpec((tm, w), lambda i: (i, 0))
    token_shape = lambda w: jax.ShapeDtypeStruct((t, w), F32)
    if feature_major:
        widths = (DIFF_W + GQA_W, DIFF_W, GQA_KV_W, SSD_INNER, XBC_W, LANES)
        cache_w = (DIFF_W, DIFF_W, GQA_KV_W)
        assert tm % seq == 0
        if cache_prev:
            cache_spec = lambda w: pl.BlockSpec((tm // seq, None, w, seq), lambda i: (i, layer, 0, 0))
        else:
            cache_spec = lambda w: pl.BlockSpec((tm // seq, depth, w, seq), lambda i: (i, 0, 0, 0))
        out_specs = [token_spec(w) for w in widths] + [cache_spec(w) for w in cache_w]
        out_shape = [token_shape(w) for w in widths] + [
            jax.ShapeDtypeStruct((t // seq, depth, w, seq), F32) for w in cache_w]
        aliases = {4 + k: len(widths) + k for k in range(len(cache_prev))}
    else:
        widths = (DIFF_W + GQA_W, DIFF_W, DIFF_W, GQA_KV_W, GQA_KV_W, SSD_INNER, XBC_W, LANES)
        out_specs = [token_spec(w) for w in widths]
        out_shape = [token_shape(w) for w in widths]
        cache_prev, aliases = (), {}
    return pl.pallas_call(
        functools.partial(_inproj_kernel, feature_major, len(cache_prev), layer),
        grid=(t // tm,),
        in_specs=[
            pl.BlockSpec((tm, d), lambda i: (i, 0)),
            _mod_spec(layer, 0, row_fn, d),
            _mod_spec(layer, 1, row_fn, d),
            pl.BlockSpec((None, d, IN_W_PAD), lambda i: (layer, 0, 0)),
        ] + [pl.BlockSpec(memory_space=pl.ANY)] * len(cache_prev),
        out_specs=out_specs,
        out_shape=out_shape,
        input_output_aliases=aliases,
        scratch_shapes=([pltpu.VMEM((tm, DIFF_W), F32), pltpu.VMEM((tm, GQA_KV_W), F32)]
                        if feature_major else []),
        compiler_params=pltpu.CompilerParams(
            dimension_semantics=("arbitrary",), vmem_limit_bytes=VMEM_LIMIT),
    )(x, mods, mods, w_in_bf, *cache_prev)


LOG2E = 1.4426950408889634
NEG_BIG = -1e30
N_SCORE_HEADS = 2 * DIFF_HEADS + GQA_Q_HEADS
ONES_ROWS = 16


def _diff_lambda(lam_ref, lam_init):
    lp = lam_ref[...]
    s1 = jnp.sum(lp[0:1, :] * lp[1:2, :], axis=-1, keepdims=True)
    s2 = jnp.sum(lp[2:3, :] * lp[3:4, :], axis=-1, keepdims=True)
    return jnp.exp(s1) - jnp.exp(s2) + lam_init


def _swap_halves(x):
    return pltpu.roll(x, GQA_HD, 1)


def _attn_kernel(cfg, *refs):
    lam_init, seq, past, kb, rope, feature_major, n_prev, layer = cfg
    it = iter(refs)
    qa_ref, dk_ref, dv_ref, gk_in_ref, gv_ref = next(it), next(it), next(it), next(it), next(it)
    for _ in range(n_prev):
        next(it)
    if past:
        cdk_ref, cdv_ref, cgk_ref, cgv_ref = next(it), next(it), next(it), next(it)
    if rope:
        cosq_ref, sinq_ref, cosk_ref, sink_ref = next(it), next(it), next(it), next(it)
    lam_ref, gain_ref, gq_ref, gk_ref = next(it), next(it), next(it), next(it)
    o_ref = next(it)
    gkn_ref = next(it) if feature_major else None
    kd_s, kg_s, vdt_s, vgt_s, wq_s, wg_s, s0_s, s1_s, m_s, l_s, acc_s = it
    tq = qa_ref.shape[0]
    n_blocks = (seq + past) // kb

    @pl.when(pl.program_id(1) == 0)
    def _prepare_keys():
        gmat = _group_avg_matrix(GQA_KV_W, GQA_HD)
        for i in range(seq // kb):
            rows = slice(i * kb, (i + 1) * kb)
            dk = dk_ref[rows, :]
            gk = gk_in_ref[rows, :]
            gk = gk * lax.rsqrt(_group_mean_sq(gk, gmat) + EPS) * gk_ref[...]
            if feature_major:
                gk_t = gk.T
                if len(gkn_ref.shape) == 2:
                    gkn_ref[:, rows] = gk_t
                else:
                    for l in range(gkn_ref.shape[0]):
                        gkn_ref[l, :, rows] = gk_t if l == layer else jnp.zeros_like(gk_t)
            if rope:
                ck = cosk_ref[rows, :]
                sk = sink_ref[rows, :]
                dk = _rope(dk, ck[:, 0:DIFF_W], sk[:, 0:DIFF_W])
                gk = _rope(gk, ck[:, DIFF_W:DIFF_W + GQA_KV_W], sk[:, DIFF_W:DIFF_W + GQA_KV_W])
            kd_s[rows, :] = dk.astype(BF16)
            kg_s[rows, :] = gk.astype(BF16)
            if feature_major:
                vdt_s[i] = dv_ref[:, rows].astype(BF16)
                vgt_s[i] = gv_ref[:, rows].astype(BF16)
            else:
                vdt_s[i] = dv_ref[rows, :].T.astype(BF16)
                vgt_s[i] = gv_ref[rows, :].T.astype(BF16)
        for j in range(past // kb):
            src = slice(j * kb, (j + 1) * kb)
            dst = slice(seq + j * kb, seq + (j + 1) * kb)
            kd_s[dst, :] = cdk_ref[:, src].T.astype(BF16)
            kg_s[dst, :] = cgk_ref[:, src].T.astype(BF16)
            vdt_s[seq // kb + j] = cdv_ref[:, src].astype(BF16)
            vgt_s[seq // kb + j] = cgv_ref[:, src].astype(BF16)

    qa = qa_ref[...]
    qd = qa[:, 0:DIFF_W]
    gq = qa[:, DIFF_W:DIFF_W + GQA_W]
    gq = gq * lax.rsqrt(_group_mean_sq(gq, _group_avg_matrix(GQA_W, GQA_HD)) + EPS) * gq_ref[...]
    if rope:
        cq = cosq_ref[...]
        sq = sinq_ref[...]
        qd = _rope(qd, cq[:, 0:DIFF_W], sq[:, 0:DIFF_W])
        gq = _rope(gq, cq[:, DIFF_W:DIFF_W + GQA_W], sq[:, DIFF_W:DIFF_W + GQA_W])
    qd_t = (qd * (DIFF_QK ** -0.5 * LOG2E)).T
    gq_t = (gq * (GQA_HD ** -0.5 * LOG2E)).T
    row = lax.broadcasted_iota(jnp.int32, (DIFF_W, tq), 0)
    for hm in range(2 * DIFF_HEADS):
        lo = hm * DIFF_QK
        wq_s[:, hm * tq:(hm + 1) * tq] = jnp.where((row >= lo) & (row < lo + DIFF_QK), qd_t, 0.0).astype(BF16)
    zeros = jnp.zeros((GQA_HD, tq), F32)
    for h in range(GQA_Q_HEADS):
        piece = gq_t[h * GQA_HD:(h + 1) * GQA_HD, :]
        pair = [piece, zeros] if h // (GQA_Q_HEADS // GQA_KV_HEADS) == 0 else [zeros, piece]
        wg_s[:, h * tq:(h + 1) * tq] = jnp.concatenate(pair, axis=0).astype(BF16)
    m_s[...] = jnp.full(m_s.shape, NEG_BIG, F32)
    l_s[...] = jnp.zeros(l_s.shape, F32)
    acc_s[...] = jnp.zeros(acc_s.shape, F32)
    n_diff = 2 * DIFF_HEADS * tq
    n_all = N_SCORE_HEADS * tq

    slabs = tq // LANES

    def key_rows(j):
        return pl.ds(j * kb if isinstance(j, int) else pl.multiple_of(j * kb, kb), kb)

    def head_scores(idx, k_d, k_g, s_buf):
        if idx < 2 * DIFF_HEADS:
            s = _dot(k_d, wq_s[:, idx * tq:(idx + 1) * tq])
        else:
            h = idx - 2 * DIFF_HEADS
            s = _dot(k_g, wg_s[:, h * tq:(h + 1) * tq])
        for k in range(slabs):
            s_buf[idx * slabs + k] = s[:, k * LANES:(k + 1) * LANES]

    def head_update(idx, v_d, v_g, s_buf):
        ps, alphas = [], []
        for k in range(slabs):
            c = idx * slabs + k
            cols = slice(c * LANES, (c + 1) * LANES)
            s = s_buf[c]
            m_old = m_s[:, cols]
            m_new = jnp.maximum(m_old, jnp.max(s, axis=0, keepdims=True))
            alphas.append(jnp.exp2(m_old - m_new))
            ps.append(jnp.exp2(s - m_new).astype(BF16))
            m_s[:, cols] = m_new
        if idx < 2 * DIFF_HEADS:
            vh = idx // 2
            v_t = v_d[vh * DIFF_V:(vh + 1) * DIFF_V, :]
        else:
            vh = (idx - 2 * DIFF_HEADS) // (GQA_Q_HEADS // GQA_KV_HEADS)
            v_t = v_g[vh * GQA_HD:(vh + 1) * GQA_HD, :]
        v_ext = jnp.concatenate([v_t, jnp.ones((ONES_ROWS, kb), BF16)], axis=0)
        alpha = jnp.concatenate(alphas, axis=1)
        pv = _dot(v_ext, jnp.concatenate(ps, axis=1))
        cols = slice(idx * tq, (idx + 1) * tq)
        acc_s[idx] = alpha * acc_s[idx] + pv[0:GQA_HD, :]
        l_s[:, cols] = alpha * l_s[:, cols] + pv[GQA_HD:GQA_HD + 1, :]

    def key_block(j, s_cur, j_next, s_next):
        v_d = vdt_s[j]
        v_g = vgt_s[j]
        if j_next is not None:
            k_d = kd_s[key_rows(j_next), :]
            k_g = kg_s[key_rows(j_next), :]
        for idx in range(N_SCORE_HEADS):
            if j_next is not None:
                head_scores(idx, k_d, k_g, s_next)
            head_update(idx, v_d, v_g, s_cur)

    for idx in range(N_SCORE_HEADS):
        head_scores(idx, kd_s[key_rows(0), :], kg_s[key_rows(0), :], s0_s)
    if n_blocks > 1:
        def block_pair(i, carry):
            key_block(2 * i, s0_s, 2 * i + 1, s1_s)
            key_block(2 * i + 1, s1_s, 2 * i + 2, s0_s)
            return carry

        lax.fori_loop(0, n_blocks // 2 - 1, block_pair, 0)
        key_block(n_blocks - 2, s0_s, n_blocks - 1, s1_s)
        key_block(n_blocks - 1, s1_s, None, None)
    else:
        key_block(0, s0_s, None, None)

    lam = _diff_lambda(lam_ref, lam_init)
    outs = []
    for h in range(DIFF_HEADS):
        c0 = slice(2 * h * tq, (2 * h + 1) * tq)
        c1 = slice((2 * h + 1) * tq, (2 * h + 2) * tq)
        o = acc_s[2 * h] * (1.0 / l_s[:, c0]) - acc_s[2 * h + 1] * (lam / l_s[:, c1])
        ms = jnp.mean(o * o, axis=0, keepdims=True)
        outs.append(o * (lax.rsqrt(ms + EPS) * (1.0 - lam_init)))
    for h in range(GQA_Q_HEADS):
        idx = 2 * DIFF_HEADS + h
        outs.append(acc_s[idx] * (1.0 / l_s[:, idx * tq:(idx + 1) * tq]))
    o_ref[...] = jnp.concatenate(outs, axis=0).T * gain_ref[...]


def _attention(qa, kv, caches, layer, tables, lam_p, gain, gq_t, gk_t, lam_init, batch, seq, tq,
               gkn_prev=None):
    t = qa.shape[0]
    nq = seq // tq
    feature_major = gkn_prev is not None
    past = caches[0].shape[3] if caches is not None else 0
    kb = min(256, seq)
    assert seq % kb == 0 and past % kb == 0 and seq % tq == 0
    n_blocks = (seq + past) // kb
    assert n_blocks == 1 or n_blocks % 2 == 0
    n_all = N_SCORE_HEADS * tq
    full = lambda a: pl.BlockSpec(a.shape, lambda b, q: (0,) * a.ndim)
    token_major = lambda a: pl.BlockSpec((seq, a.shape[1]), lambda b, q: (b, 0))
    by_feature = lambda a: pl.BlockSpec((None, None, a.shape[2], seq), lambda b, q: (b, layer, 0, 0))
    dk, dv, gk, gv = kv
    in_specs = [pl.BlockSpec((tq, qa.shape[1]), lambda b, q: (b * nq + q, 0)),
                token_major(dk), by_feature(dv) if feature_major else token_major(dv),
                token_major(gk), by_feature(gv) if feature_major else token_major(gv)]
    args = [qa, dk, dv, gk, gv]
    aliases = {}
    if feature_major:
        assert nq == 1
        in_specs += [pl.BlockSpec(memory_space=pl.ANY)] * len(gkn_prev)
        args += list(gkn_prev)
        aliases = {5 + k: 1 + k for k in range(len(gkn_prev))}
    if caches is not None:
        in_specs += [pl.BlockSpec((None, None, a.shape[2], past), lambda b, q: (b, layer, 0, 0))
                     for a in caches]
        args += list(caches)
    if tables is not None:
        cosq, sinq, cosk, sink = tables
        in_specs += [pl.BlockSpec((tq, cosq.shape[1]), lambda b, q: (q, 0)),
                     pl.BlockSpec((tq, sinq.shape[1]), lambda b, q: (q, 0)),
                     full(cosk), full(sink)]
        args += [cosq, sinq, cosk, sink]
    in_specs += [full(lam_p), full(gain), full(gq_t), full(gk_t)]
    args += [lam_p, gain, gq_t, gk_t]
    out_specs = [pl.BlockSpec((tq, DIFF_W + GQA_W), lambda b, q: (b * nq + q, 0))]
    out_shape = [jax.ShapeDtypeStruct((t, DIFF_W + GQA_W), F32)]
    if feature_major:
        depth = dv.shape[1]
        if gkn_prev:
            out_specs.append(pl.BlockSpec((None, None, GQA_KV_W, seq), lambda b, q: (b, layer, 0, 0)))
        else:
            out_specs.append(pl.BlockSpec((None, depth, GQA_KV_W, seq), lambda b, q: (b, 0, 0, 0)))
        out_shape.append(jax.ShapeDtypeStruct((batch, depth, GQA_KV_W, seq), F32))
    cfg = (lam_init, seq, past, kb, tables is not None, feature_major,
           len(gkn_prev) if feature_major else 0, layer)
    return pl.pallas_call(
        functools.partial(_attn_kernel, cfg),
        grid=(batch, nq),
        in_specs=in_specs,
        out_specs=out_specs,
        out_shape=out_shape,
        input_output_aliases=aliases,
        scratch_shapes=[
            pltpu.VMEM((seq + past, DIFF_W), BF16), pltpu.VMEM((seq + past, GQA_KV_W), BF16),
            pltpu.VMEM((n_blocks, DIFF_W, kb), BF16), pltpu.VMEM((n_blocks, GQA_KV_W, kb), BF16),
            pltpu.VMEM((DIFF_W, 2 * DIFF_HEADS * tq), BF16), pltpu.VMEM((GQA_KV_W, GQA_Q_HEADS * tq), BF16),
            pltpu.VMEM((n_all // LANES, kb, LANES), F32), pltpu.VMEM((n_all // LANES, kb, LANES), F32),
            pltpu.VMEM((1, n_all), F32), pltpu.VMEM((1, n_all), F32),
            pltpu.VMEM((N_SCORE_HEADS, GQA_HD, tq), F32),
        ],
        compiler_params=pltpu.CompilerParams(
            dimension_semantics=("arbitrary", "arbitrary"), vmem_limit_bytes=VMEM_LIMIT),
    )(*args)


def _softplus(x):
    return jnp.maximum(x, 0.0) + jnp.log1p(jnp.exp(-jnp.abs(x)))


def _dot2_l(a, b_exact):
    a1 = a.astype(BF16)
    a2 = (a - a1.astype(F32)).astype(BF16)
    return _dot(a1, b_exact) + _dot(a2, b_exact)


def _ssd_kernel(has_init, n_prev, seq, out_layer, *refs):
    it = iter(refs)
    z_ref, xbc_ref, dt_ref, cw_ref, cb_ref, alog_ref, dtb_ref, dexp_ref, ng_ref = (
        next(it) for _ in range(9))
    sf0_ref, sb0_ref = (next(it), next(it)) if has_init else (None, None)
    for _ in range(n_prev):
        next(it)
    o_ref, sf_ref, sb_ref = next(it), next(it), next(it)
    stf_s, stb_s, inc_s, dec_s, eab_s, cbf_s, exp_s, tri_s, gm_s = it
    L = SSD_CHUNK
    W2 = 2 * SSD_INNER
    nchunks = seq // L
    halo = 8

    ri = lax.broadcasted_iota(jnp.int32, (L, L), 0)
    ci = lax.broadcasted_iota(jnp.int32, (L, L), 1)
    lower = ri >= ci
    upper = ri <= ci
    tri_s[0] = jnp.where(lower, 1.0, 0.0).astype(BF16)
    tri_s[1] = jnp.where(upper, 1.0, 0.0).astype(BF16)
    lane_row = lax.broadcasted_iota(jnp.int32, (1, LANES), 1)
    a_row = jnp.where(lane_row < DT_W, -jnp.exp(alog_ref[...]), 0.0)
    ej = lax.broadcasted_iota(jnp.int32, (LANES, W2), 0)
    eh = lax.shift_right_logical(lax.broadcasted_iota(jnp.int32, (LANES, W2), 1), 6)
    exp_s[...] = jnp.where(ej == eh, 1.0, 0.0).astype(BF16)
    gm_s[...] = jnp.where(lax.shift_right_logical(ej, 6) == (lax.shift_right_logical(eh, 2) & 1), 1.0, 0.0)

    def chunk_rows(c):
        return pl.ds(pl.multiple_of(c * L, L), L)

    def load_state(ref):
        r = lax.broadcasted_iota(jnp.int32, (SSD_STATE, LANES), 0)
        c = lax.broadcasted_iota(jnp.int32, (SSD_STATE, LANES), 1)
        dup = jnp.where((c & (SSD_STATE - 1)) == r, 1.0, 0.0).astype(BF16)
        return _dot3_l(ref[...], dup).T * gm_s[:, 0:SSD_INNER]

    def store_state(st_ref, ref):
        st_t = st_ref[...].T
        _put_layer_slot(ref, (), out_layer, (st_t + _swap_halves(st_t))[:, 0:SSD_STATE])

    stf_s[...] = load_state(sf0_ref) if has_init else jnp.zeros(stf_s.shape, F32)
    stb_s[...] = load_state(sb0_ref) if has_init else jnp.zeros(stb_s.shape, F32)

    def forward_pass(c, carry):
        r0 = c * L
        rows = chunk_rows(c)
        dt_c = _softplus(dt_ref[rows, :] + dtb_ref[...])

        d1, d2, d3 = _split3(dt_c * a_row)
        acs_f = _dot(tri_s[0], d1) + (_dot(tri_s[0], d2) + _dot(tri_s[0], d3))
        acs_b = _dot(tri_s[1], d1) + (_dot(tri_s[1], d2) + _dot(tri_s[1], d3))
        acs = jnp.where(lax.broadcasted_iota(jnp.int32, (L, LANES), 1) < SSD_HEADS, acs_f, acs_b)
        acs_t = acs.T
        expand = exp_s[...]
        dt_e = _dot2_l(dt_c, expand)
        acs_e = _dot3_l(acs, expand)
        edge = jnp.concatenate([acs_e[L - 1:L, 0:SSD_INNER], acs_e[0:1, SSD_INNER:W2]], axis=1)
        eacs = jnp.exp(acs_e)
        cdec = jnp.exp(edge)

        prev = xbc_ref[pl.ds(pl.multiple_of(jnp.maximum(r0 - halo, 0), halo), halo), :]
        nxt = xbc_ref[pl.ds(pl.multiple_of(jnp.minimum(r0 + L, seq - halo), halo), halo), :]
        cur = xbc_ref[rows, :]
        win = jnp.concatenate([jnp.where(c > 0, prev, 0.0), cur,
                               jnp.where(c < nchunks - 1, nxt, 0.0)], axis=0)
        acc = cb_ref[...] + cur * cw_ref[SSD_CONV // 2:SSD_CONV // 2 + 1, :]
        for j in range(SSD_CONV):
            if j != SSD_CONV // 2:
                shifted = pltpu.roll(win, (SSD_CONV // 2 - j) % (L + 2 * halo), 0)[halo:halo + L, :]
                acc = acc + shifted * cw_ref[j:j + 1, :]
        act = _silu(acc)
        x_c = act[:, 0:SSD_INNER]
        b_c = act[:, SSD_INNER:SSD_INNER + SSD_BC_W]
        c_c = act[:, SSD_INNER + SSD_BC_W:XBC_W]
        xd = jnp.concatenate([x_c, x_c], axis=1) * dt_e
        xd_b = xd.astype(BF16)
        xdw = (xd * jnp.exp(edge - acs_e)).astype(BF16)
        b_b = b_c.astype(BF16)
        c_b = c_c.astype(BF16)
        s_new = _dot(b_c.T.astype(BF16), xdw) * gm_s[...]

        st_f = stf_s[...]
        y = _dot(c_b, st_f.astype(BF16)) * eacs[:, 0:SSD_INNER]
        stf_s[...] = st_f * cdec[:, 0:SSD_INNER] + s_new[:, 0:SSD_INNER]
        inc_s[c] = s_new[:, SSD_INNER:W2]
        dec_s[c] = cdec[:, SSD_INNER:W2]
        eab_s[rows, :] = eacs[:, SSD_INNER:W2]
        cbf_s[rows, :] = c_b

        cbs = []
        for g in range(SSD_GROUPS):
            cg = jnp.where(_lane_mask(c_c.shape, g * SSD_STATE, (g + 1) * SSD_STATE), c_c, 0.0)
            cbs.append(_dot_nt(cg.astype(BF16), b_b))
        for direction, causal in ((0, lower), (1, upper)):
            pairs = []
            for g in range(SSD_GROUPS):
                cb = cbs[g]
                for hp in range(2):
                    pair = g * 2 + hp
                    res = []
                    for k in range(2):
                        j = direction * SSD_HEADS + pair * 2 + k
                        diff = acs[:, j:j + 1] - acs_t[j:j + 1, :]
                        dec = jnp.where(causal, jnp.exp(diff), 0.0)
                        sc = (cb * dec).astype(BF16)
                        lo = direction * SSD_INNER + pair * LANES
                        res.append(_dot(sc, xd_b[:, lo:lo + LANES]))
                    pairs.append(jnp.where(_lane_mask(res[0].shape, 0, SSD_HD), res[0], res[1]))
            y = y + jnp.concatenate(pairs, axis=-1)
        o_ref[rows, :] = y + x_c * dexp_ref[...]
        return carry

    lax.fori_loop(0, nchunks, forward_pass, 0)
    store_state(stf_s, sf_ref)

    def backward_pass(i, carry):
        c = nchunks - 1 - i
        rows = chunk_rows(c)
        st_b = stb_s[...]
        y = o_ref[rows, :] + _dot(cbf_s[rows, :], st_b.astype(BF16)) * eab_s[rows, :]
        stb_s[...] = st_b * dec_s[c] + inc_s[c]
        yt = y * _silu(z_ref[rows, :])
        ms = jnp.mean(yt * yt, axis=-1, keepdims=True)
        o_ref[rows, :] = yt * lax.rsqrt(ms + EPS) * ng_ref[...]
        return carry

    lax.fori_loop(0, nchunks, backward_pass, 0, unroll=2)
    store_state(stb_s, sb_ref)


def _ssd(z, xbc, dt, params, init, layer, batch, seq, state_prev=(), depth=1):
    cw, cb, alog, dtb, dexp, ng = params
    t = z.shape[0]
    has_init = init is not None
    nchunks = seq // SSD_CHUNK
    full = lambda a: pl.BlockSpec(a.shape, lambda b: (0,) * a.ndim)
    out_layer = layer if depth > 1 else 0
    if state_prev or depth == 1:
        st_spec = pl.BlockSpec((None, None, SSD_INNER, SSD_STATE), lambda b: (b, out_layer, 0, 0))
    else:
        st_spec = pl.BlockSpec((None, depth, SSD_INNER, SSD_STATE), lambda b: (b, 0, 0, 0))
    in_specs = [
        pl.BlockSpec((seq, SSD_INNER), lambda b: (b, 0)),
        pl.BlockSpec((seq, XBC_W), lambda b: (b, 0)),
        pl.BlockSpec((seq, LANES), lambda b: (b, 0)),
        full(cw), full(cb), full(alog), full(dtb), full(dexp), full(ng),
    ]
    args = [z, xbc, dt, cw, cb, alog, dtb, dexp, ng]
    if has_init:
        init_spec = pl.BlockSpec((None, None, SSD_INNER, SSD_STATE), lambda b: (b, layer, 0, 0))
        in_specs += [init_spec, init_spec]
        args += list(init)
    aliases = {len(args) + k: 1 + k for k in range(len(state_prev))}
    in_specs += [pl.BlockSpec(memory_space=pl.ANY)] * len(state_prev)
    args += list(state_prev)
    st_shape = jax.ShapeDtypeStruct((batch, depth, SSD_INNER, SSD_STATE), F32)
    return pl.pallas_call(
        functools.partial(_ssd_kernel, has_init, len(state_prev), seq, out_layer),
        grid=(batch,),
        in_specs=in_specs,
        out_specs=[pl.BlockSpec((seq, SSD_INNER), lambda b: (b, 0)), st_spec, st_spec],
        out_shape=[jax.ShapeDtypeStruct((t, SSD_INNER), F32), st_shape, st_shape],
        input_output_aliases=aliases,
        scratch_shapes=[
            pltpu.VMEM((LANES, SSD_INNER), F32), pltpu.VMEM((LANES, SSD_INNER), F32),
            pltpu.VMEM((nchunks, LANES, SSD_INNER), F32),
            pltpu.VMEM((nchunks, 1, SSD_INNER), F32),
            pltpu.VMEM((seq, SSD_INNER), F32),
            pltpu.VMEM((seq, SSD_BC_W), BF16),
            pltpu.VMEM((LANES, 2 * SSD_INNER), BF16),
            pltpu.VMEM((2, SSD_CHUNK, SSD_CHUNK), BF16),
            pltpu.VMEM((LANES, 2 * SSD_INNER), F32),
        ],
        compiler_params=pltpu.CompilerParams(
            dimension_semantics=("arbitrary",), vmem_limit_bytes=VMEM_LIMIT),
    )(*args)


def _post_kernel(alpha, d_ff, ff_chunk, x_ref, oa_ref, os_ref, g1_ref, sh2_ref, sc2_ref, g2_ref,
                 wo_ref, wfi_ref, wfo_ref, lng_ref, lnb_ref, y_ref):
    wa = oa_ref.shape[1]
    half = x_ref.shape[0] // 2
    bounds = list(range(0, d_ff, ff_chunk)) + [d_ff]
    n_chunks = len(bounds) - 1

    def mix_and_norm(r):
        o = (_dot(oa_ref[r, :].astype(BF16), wo_ref[0:wa, :])
             + _dot(os_ref[r, :].astype(BF16), wo_ref[wa:, :]))
        x1 = _layer_norm(alpha * x_ref[r, :] + g1_ref[...] * o, lng_ref[0:1, :], lnb_ref[0:1, :])
        return x1, (x1 * (1.0 + sc2_ref[...]) + sh2_ref[...]).astype(BF16)

    def up(h2, c):
        lo, hi = bounds[c], bounds[c + 1]
        return _dot(h2, wfi_ref[:, lo:hi]), _dot(h2, wfi_ref[:, d_ff + lo:d_ff + hi])

    def ffn_chunks(h2):
        f = jnp.zeros((half, x_ref.shape[1]), F32)
        g, u = up(h2, 0)
        for c in range(n_chunks):
            nxt = up(h2, c + 1) if c + 1 < n_chunks else None
            f = f + _dot((_silu(g) * u).astype(BF16), wfo_ref[bounds[c]:bounds[c + 1], :])
            if nxt is not None:
                g, u = nxt
            yield f

    def finish(r, x1, f):
        y_ref[r, :] = _layer_norm(alpha * x1 + g2_ref[...] * f, lng_ref[1:2, :], lnb_ref[1:2, :])

    ra, rb = slice(0, half), slice(half, 2 * half)
    x1a, h2a = mix_and_norm(ra)
    ffn_a = ffn_chunks(h2a)
    for _ in range(n_chunks // 2):
        fa = next(ffn_a)
    x1b, h2b = mix_and_norm(rb)
    for fa in ffn_a:
        pass
    ffn_b = ffn_chunks(h2b)
    for _ in range(n_chunks // 2):
        fb = next(ffn_b)
    finish(ra, x1a, fa)
    for fb in ffn_b:
        pass
    finish(rb, x1b, fb)


def _post(x, oa, os_, mods, layer, row_fn, wo_bf, wfi_bf, wfo_bf, lng, lnb, alpha, tm):
    t, d = x.shape
    d_ff = wfo_bf.shape[1]
    ff_chunk = 3 * MXU_TILE
    assert d_ff % MXU_TILE == 0
    per_layer = lambda a, **kw: pl.BlockSpec((None,) + a.shape[1:], lambda i: (layer, 0, 0), **kw)
    resident = lambda a: per_layer(a, pipeline_mode=pl.Buffered(1))
    return pl.pallas_call(
        functools.partial(_post_kernel, alpha, d_ff, ff_chunk),
        grid=(t // tm,),
        in_specs=[
            pl.BlockSpec((tm, d), lambda i: (i, 0)),
            pl.BlockSpec((tm, oa.shape[1]), lambda i: (i, 0)),
            pl.BlockSpec((tm, os_.shape[1]), lambda i: (i, 0)),
            _mod_spec(layer, 2, row_fn, d),
            _mod_spec(layer, 3, row_fn, d),
            _mod_spec(layer, 4, row_fn, d),
            _mod_spec(layer, 5, row_fn, d),
            resident(wo_bf), resident(wfi_bf), resident(wfo_bf),
            per_layer(lng), per_layer(lnb),
        ],
        out_specs=pl.BlockSpec((tm, d), lambda i: (i, 0)),
        out_shape=jax.ShapeDtypeStruct((t, d), F32),
        compiler_params=pltpu.CompilerParams(
            dimension_semantics=("arbitrary",), vmem_limit_bytes=VMEM_LIMIT),
    )(x, oa, os_, mods, mods, mods, mods, wo_bf, wfi_bf, wfo_bf, lng, lnb)


def _rope_tables(rows, dim, copies):
    row = jnp.repeat(jnp.arange(rows), GRID_W).astype(F32)
    col = jnp.tile(jnp.arange(GRID_W), rows).astype(F32)
    n_freq = dim // 4
    inv = ROPE_THETA ** (-jnp.arange(n_freq, dtype=F32) / n_freq)
    ang = jnp.concatenate([row[:, None] * inv, col[:, None] * inv], -1)
    cos, sin = jnp.cos(ang), jnp.sin(ang)
    cos_full = jnp.repeat(cos, 2, axis=-1)
    sin_signed = jnp.stack([-sin, sin], axis=-1).reshape(sin.shape[0], dim)
    return jnp.tile(cos_full, (1, copies)), jnp.tile(sin_signed, (1, copies))


def kernel(x_prompt, x_sample, cache_diff_k, cache_diff_v, cache_gqa_k, cache_gqa_v, state_ssd_fwd, state_ssd_bwd, c, c_ctx, w_ada, b_ada, w_in, w_out, diff_lambda, diff_subln_g, qk_norm_g, ssd_conv_w, ssd_conv_b, ssd_A_log, ssd_dt_bias, ssd_D, ssd_norm_g, ln_g, ln_b, w_ffn_in, w_ffn_out):
    batch, seq, d = x_prompt.shape
    dec_batch, dec_seq, _ = x_sample.shape
    depth = w_in.shape[0]
    past = cache_diff_k.shape[2]
    alpha = (2 * depth) ** 0.25
    rows = dec_seq // GRID_W

    n_vec = 1 + dec_batch
    n_pad = -(-n_vec // 8) * 8
    cvec = jnp.concatenate([c_ctx[None, :], c, jnp.zeros((n_pad - n_vec, d), F32)], axis=0)
    mods = _modulation(cvec, w_ada, b_ada).reshape(depth, n_pad, 1, 6 * d)

    cos_d, sin_d = _rope_tables(rows, DIFF_QK, DIFF_W // DIFF_QK)
    cos_g, sin_g = _rope_tables(rows, GQA_HD, GQA_Q_HEADS)
    cosq = jnp.concatenate([cos_d, cos_g], axis=-1)
    sinq = jnp.concatenate([sin_d, sin_g], axis=-1)
    cosk = jnp.concatenate([cos_d, cos_g[:, :GQA_KV_W]], axis=-1)
    sink = jnp.concatenate([sin_d, sin_g[:, :GQA_KV_W]], axis=-1)
    tables = (cosq, sinq, cosk, sink)

    feature_major = lambda a, w: jnp.transpose(a.reshape(dec_batch, depth, past, w), (0, 1, 3, 2))
    caches = (feature_major(cache_diff_k, DIFF_W), feature_major(cache_diff_v, DIFF_W),
              feature_major(cache_gqa_k, GQA_KV_W), feature_major(cache_gqa_v, GQA_KV_W))

    xp = x_prompt.reshape(batch * seq, d)
    xs = x_sample.reshape(dec_batch * dec_seq, d)
    tm_ctx = min(512, batch * seq)
    tm_lat = min(512, dec_seq)
    tm_post_ctx = min(1024, batch * seq)
    tm_post_lat = min(1024, dec_seq)
    tq = min(256, dec_seq)
    ctx_row = lambda i: 0
    lat_row = lambda i: 1 + (i * tm_lat) // dec_seq
    lat_row_post = lambda i: 1 + (i * tm_post_lat) // dec_seq

    w_in_bf = jnp.pad(w_in, ((0, 0), (0, 0), (0, IN_W_PAD - w_in.shape[2]))).astype(BF16)
    wo_bf = w_out.astype(BF16)
    wfi_bf = w_ffn_in.astype(BF16)
    wfo_bf = w_ffn_out.astype(BF16)
    init = (state_ssd_fwd.reshape(dec_batch, depth, SSD_INNER, SSD_STATE),
            state_ssd_bwd.reshape(dec_batch, depth, SSD_INNER, SSD_STATE))

    kv_cache = ()
    gk_cache = ()
    ssd_states = ()
    for l in range(depth):
        lam_init = 0.8 - 0.6 * math.exp(-0.3 * l)
        lam_p = diff_lambda[l]
        attn_gain = jnp.concatenate([jnp.tile(diff_subln_g[l], DIFF_HEADS), jnp.ones((GQA_W,), F32)])[None, :]
        gq_t = jnp.tile(qk_norm_g[l, 0], GQA_Q_HEADS)[None, :]
        gk_t = jnp.tile(qk_norm_g[l, 1], GQA_KV_HEADS)[None, :]
        pad_row = lambda v: jnp.pad(v.reshape(1, -1), ((0, 0), (0, LANES - v.size)))
        ssd_params = (ssd_conv_w[l], ssd_conv_b[l][None, :], pad_row(ssd_A_log[l]),
                      pad_row(ssd_dt_bias[l]), jnp.repeat(ssd_D[l], SSD_HD)[None, :],
                      ssd_norm_g[l][None, :])

        qa, dk, gk, z, xbc, dt, *kv_cache = _inproj(xp, mods, l, ctx_row, w_in_bf, tm_ctx,
                                                    cache_prev=tuple(kv_cache), depth=depth, seq=seq)
        oa, gkn = _attention(qa, (dk, kv_cache[1], gk, kv_cache[2]), None, l, None, lam_p, attn_gain,
                             gq_t, gk_t, lam_init, batch, seq, seq, gkn_prev=gk_cache)
        gk_cache = (gkn,)
        os_, *ssd_states = _ssd(z, xbc, dt, ssd_params, None, l, batch, seq,
                                state_prev=tuple(ssd_states), depth=depth)
        xp = _post(xp, oa, os_, mods, l, ctx_row, wo_bf, wfi_bf, wfo_bf, ln_g, ln_b, alpha, tm_post_ctx)

        qa, dk, dv, gk, gv, z, xbc, dt = _inproj(xs, mods, l, lat_row, w_in_bf, tm_lat)
        (oa,) = _attention(qa, (dk, dv, gk, gv), caches, l, tables, lam_p, attn_gain, gq_t, gk_t,
                           lam_init, dec_batch, dec_seq, tq)
        os_, _, _ = _ssd(z, xbc, dt, ssd_params, init, l, dec_batch, dec_seq)
        xs = _post(xs, oa, os_, mods, l, lat_row_post, wo_bf, wfi_bf, wfo_bf, ln_g, ln_b, alpha,
                   tm_post_lat)

    def token_major(a, heads, width):
        return jnp.transpose(a.reshape(batch, depth, heads, width, seq), (0, 1, 4, 2, 3))

    state = lambda a: a.reshape(batch, depth, SSD_HEADS, SSD_HD, SSD_STATE)
    return (xp.reshape(batch, seq, d), xs.reshape(dec_batch, dec_seq, d),
            token_major(kv_cache[0], DIFF_HEADS, 2 * DIFF_QK), token_major(kv_cache[1], DIFF_HEADS, DIFF_V),
            token_major(gk_cache[0], GQA_KV_HEADS, GQA_HD), token_major(kv_cache[2], GQA_KV_HEADS, GQA_HD),
            state(ssd_states[0]), state(ssd_states[1]))
```

```python
import functools
import math

import jax
import jax.numpy as jnp
from jax import lax
from jax.experimental import pallas as pl
from jax.experimental.pallas import tpu as pltpu

F32 = jnp.float32
BF16 = jnp.bfloat16

GRID_W = 64
DIFF_HEADS = 4
DIFF_QK = 32
DIFF_V = 64
DIFF_W = DIFF_HEADS * DIFF_V
GQA_HD = 64
GQA_Q_HEADS = 4
GQA_KV_HEADS = 2
GQA_W = GQA_Q_HEADS * GQA_HD
GQA_KV_W = GQA_KV_HEADS * GQA_HD
SSD_HD = 64
SSD_HEADS = 8
SSD_INNER = SSD_HEADS * SSD_HD
SSD_GROUPS = 2
SSD_STATE = 64
SSD_BC_W = SSD_GROUPS * SSD_STATE
SSD_CONV = 5
SSD_CHUNK = 128
XBC_W = SSD_INNER + 2 * SSD_BC_W
DT_W = 2 * SSD_HEADS
ROPE_THETA = 10000.0
EPS = 1e-5
LANES = 128
MXU_TILE = 256
VMEM_LIMIT = 56 * 1024 * 1024

_C_DQ, _C_DK, _C_DV, _C_GQ, _C_GK, _C_GV, _C_Z, _C_XBC, _C_DT, _C_END = (
    0, 256, 512, 768, 1024, 1152, 1280, 1792, 2560, 2576)
IN_W_PAD = _C_DT + LANES


def _dot(a, b):
    return jnp.dot(a, b, preferred_element_type=F32)


def _dot_nt(a, b):
    return lax.dot_general(a, b, (((1,), (1,)), ((), ())), preferred_element_type=F32)


def _split3(a):
    a1 = a.astype(BF16)
    r1 = a - a1.astype(F32)
    a2 = r1.astype(BF16)
    a3 = (r1 - a2.astype(F32)).astype(BF16)
    return a1, a2, a3


def _dot3_l(a, b_exact):
    a1, a2, a3 = _split3(a)
    return _dot(a1, b_exact) + (_dot(a2, b_exact) + _dot(a3, b_exact))


def _sigmoid(x):
    return 1.0 / (1.0 + jnp.exp(-x))


def _silu(x):
    return x * _sigmoid(x)


def _layer_norm(x, g, b):
    mu = jnp.mean(x, axis=-1, keepdims=True)
    xc = x - mu
    var = jnp.mean(xc * xc, axis=-1, keepdims=True)
    return xc * lax.rsqrt(var + EPS) * g + b


def _group_avg_matrix(width, group):
    sh = int(math.log2(group))
    r = lax.shift_right_logical(lax.broadcasted_iota(jnp.int32, (width, width), 0), sh)
    c = lax.shift_right_logical(lax.broadcasted_iota(jnp.int32, (width, width), 1), sh)
    return jnp.where(r == c, 1.0 / group, 0.0).astype(BF16)


def _group_mean_sq(x, gmat):
    xx = x * x
    hi = xx.astype(BF16)
    lo = (xx - hi.astype(F32)).astype(BF16)
    return _dot(hi, gmat) + _dot(lo, gmat)


def _rope(x, cos, sin_signed):
    w = x.shape[-1]
    lane = lax.broadcasted_iota(jnp.int32, x.shape, 1)
    nxt = pltpu.roll(x, w - 1, 1)
    prv = pltpu.roll(x, 1, 1)
    partner = jnp.where((lane & 1) == 0, nxt, prv)
    return x * cos + partner * sin_signed


def _lane_mask(shape, lo, hi):
    lane = lax.broadcasted_iota(jnp.int32, shape, 1)
    return (lane >= lo) & (lane < hi)


def _mod_kernel(c_ref, w_ref, b_ref, o_ref):
    a = _silu(c_ref[...])
    a_hi = a.astype(BF16)
    a_lo = (a - a_hi.astype(F32)).astype(BF16)
    w = w_ref[...]
    w_hi = w.astype(BF16)
    w_lo = (w - w_hi.astype(F32)).astype(BF16)
    o_ref[...] = _dot(a_hi, w_hi) + (_dot(a_lo, w_hi) + _dot(a_hi, w_lo)) + b_ref[...]


def _modulation(cvec, w_ada, b_ada):
    depth, d, n = w_ada.shape
    tn = 1536
    rows = cvec.shape[0]
    return pl.pallas_call(
        _mod_kernel,
        grid=(depth, n // tn),
        in_specs=[
            pl.BlockSpec((rows, d), lambda l, j: (0, 0)),
            pl.BlockSpec((None, d, tn), lambda l, j: (l, 0, j)),
            pl.BlockSpec((None, 1, tn), lambda l, j: (l, 0, j)),
        ],
        out_specs=pl.BlockSpec((None, rows, tn), lambda l, j: (l, 0, j)),
        out_shape=jax.ShapeDtypeStruct((depth, rows, n), F32),
        compiler_params=pltpu.CompilerParams(
            dimension_semantics=("arbitrary", "arbitrary"), vmem_limit_bytes=VMEM_LIMIT),
    )(cvec, w_ada, b_ada.reshape(depth, 1, n))


def _put_layer_slot(ref, lead, layer, val):
    if len(ref.shape) == val.ndim + len(lead):
        ref[lead if lead else ...] = val
    else:
        for l in range(ref.shape[len(lead)]):
            ref[lead + (l,)] = val if l == layer else jnp.zeros_like(val)


def _inproj_kernel(feature_major_cache, n_cast, n_prev, layer, *refs):
    x_ref, sh_ref, sc_ref, w_ref = refs[:4]
    cast_in = refs[4:4 + n_cast]
    outs = refs[4 + n_cast + n_prev:]
    if n_cast:
        cast_out = outs[-n_cast - 2:-2] if feature_major_cache else outs[-n_cast:]
        outs = outs[:-n_cast - 2] + outs[-2:] if feature_major_cache else outs[:-n_cast]
        for src, dst in zip(cast_in, cast_out):
            dst[...] = src[...].astype(BF16)
    h = (x_ref[...] * (1.0 + sc_ref[...]) + sh_ref[...]).astype(BF16)

    def mm(lo, hi):
        return _dot(h, w_ref[:, lo:hi])

    gkv = mm(_C_GK, _C_Z)
    if feature_major_cache:
        (qa_ref, dk_ref, gk_ref, z_ref, xbc_ref, dt_ref, dkt_ref, dvt_ref, gvt_ref,
         dv_s, gv_s) = outs
        dk_ref[...] = mm(_C_DK, _C_DV)
        dv_s[...] = mm(_C_DV, _C_GQ)
        gv_s[...] = gkv[:, GQA_KV_W:]
        seq = dkt_ref.shape[-1]
        for j in range(dkt_ref.shape[0]):
            rows = slice(j * seq, (j + 1) * seq)
            _put_layer_slot(dkt_ref, (j,), layer, dk_ref[rows, :].T)
            _put_layer_slot(dvt_ref, (j,), layer, dv_s[rows, :].T)
            _put_layer_slot(gvt_ref, (j,), layer, gv_s[rows, :].T)
    else:
        qa_ref, dk_ref, dv_ref, gk_ref, gv_ref, z_ref, xbc_ref, dt_ref = outs
        dk_ref[...] = mm(_C_DK, _C_DV)
        dv_ref[...] = mm(_C_DV, _C_GQ)
        gv_ref[...] = gkv[:, GQA_KV_W:]
    qa_ref[:, 0:DIFF_W] = mm(_C_DQ, _C_DK)
    qa_ref[:, DIFF_W:DIFF_W + GQA_W] = mm(_C_GQ, _C_GK)
    gk_ref[...] = gkv[:, 0:GQA_KV_W]
    z_ref[...] = mm(_C_Z, _C_XBC)
    xbc_ref[...] = mm(_C_XBC, _C_DT)
    dt_ref[...] = mm(_C_DT, IN_W_PAD)


def _mod_spec(layer, which, row_fn, d):
    return pl.BlockSpec((None, None, 1, d), lambda i: (layer, row_fn(i), 0, which))


def _inproj(x, mods, layer, row_fn, w_in_bf, tm, cache_prev=None, depth=None, seq=None, cast=()):
    t, d = x.shape
    steps = t // tm
    feature_major = cache_prev is not None
    token_spec = lambda w: pl.BlockSpec((tm, w), lambda i: (i, 0))
    token_shape = lambda w: jax.ShapeDtypeStruct((t, w), F32)
    if feature_major:
        widths = (DIFF_W + GQA_W, DIFF_W, GQA_KV_W, SSD_INNER, XBC_W, LANES)
        cache_w = (DIFF_W, DIFF_W, GQA_KV_W)
        assert tm % seq == 0
        if cache_prev:
            cache_spec = lambda w: pl.BlockSpec((tm // seq, None, w, seq), lambda i: (i, layer, 0, 0))
        else:
            cache_spec = lambda w: pl.BlockSpec((tm // seq, depth, w, seq), lambda i: (i, 0, 0, 0))
        out_specs = [token_spec(w) for w in widths] + [cache_spec(w) for w in cache_w]
        out_shape = [token_shape(w) for w in widths] + [
            jax.ShapeDtypeStruct((t // seq, depth, w, seq), F32) for w in cache_w]
        aliases = {4 + k: len(widths) + k for k in range(len(cache_prev))}
    else:
        widths = (DIFF_W + GQA_W, DIFF_W, DIFF_W, GQA_KV_W, GQA_KV_W, SSD_INNER, XBC_W, LANES)
        out_specs = [token_spec(w) for w in widths]
        out_shape = [token_shape(w) for w in widths]
        cache_prev, aliases = (), {}
    aliases = {k + len(cast): v for k, v in aliases.items()}
    cast_in_specs, cast_out_specs, cast_out_shape = [], [], []
    for w in cast:
        rows = w.shape[1] // steps
        assert w.shape[1] % steps == 0 and rows % 16 == 0
        cast_in_specs.append(pl.BlockSpec((None, rows, w.shape[2]), lambda i: (layer, i, 0)))
        cast_out_specs.append(pl.BlockSpec((rows, w.shape[2]), lambda i: (i, 0)))
        cast_out_shape.append(jax.ShapeDtypeStruct(w.shape[1:], BF16))
    return pl.pallas_call(
        functools.partial(_inproj_kernel, feature_major, len(cast), len(cache_prev), layer),
        grid=(steps,),
        in_specs=[
            pl.BlockSpec((tm, d), lambda i: (i, 0)),
            _mod_spec(layer, 0, row_fn, d),
            _mod_spec(layer, 1, row_fn, d),
            pl.BlockSpec((None, d, IN_W_PAD), lambda i: (layer, 0, 0)),
        ] + cast_in_specs + [pl.BlockSpec(memory_space=pl.ANY)] * len(cache_prev),
        out_specs=out_specs + cast_out_specs,
        out_shape=out_shape + cast_out_shape,
        input_output_aliases=aliases,
        scratch_shapes=([pltpu.VMEM((tm, DIFF_W), F32), pltpu.VMEM((tm, GQA_KV_W), F32)]
                        if feature_major else []),
        compiler_params=pltpu.CompilerParams(
            dimension_semantics=("arbitrary",), vmem_limit_bytes=VMEM_LIMIT),
    )(x, mods, mods, w_in_bf, *cast, *cache_prev)


LOG2E = 1.4426950408889634
NEG_BIG = -1e30
N_SCORE_HEADS = 2 * DIFF_HEADS + GQA_Q_HEADS
ONES_ROWS = 16


def _diff_lambda(lam_ref, lam_init):
    lp = lam_ref[...]
    s1 = jnp.sum(lp[0:1, :] * lp[1:2, :], axis=-1, keepdims=True)
    s2 = jnp.sum(lp[2:3, :] * lp[3:4, :], axis=-1, keepdims=True)
    return jnp.exp(s1) - jnp.exp(s2) + lam_init


def _swap_halves(x):
    return pltpu.roll(x, GQA_HD, 1)


def _attn_kernel(cfg, *refs):
    lam_init, seq, past, kb, rope, feature_major, n_prev, layer = cfg
    it = iter(refs)
    qa_ref, dk_ref, dv_ref, gk_in_ref, gv_ref = next(it), next(it), next(it), next(it), next(it)
    for _ in range(n_prev):
        next(it)
    if past:
        cdk_ref, cdv_ref, cgk_ref, cgv_ref = next(it), next(it), next(it), next(it)
    if rope:
        cosq_ref, sinq_ref, cosk_ref, sink_ref = next(it), next(it), next(it), next(it)
    lam_ref, gain_ref, gq_ref, gk_ref = next(it), next(it), next(it), next(it)
    o_ref = next(it)
    gkn_ref = next(it) if feature_major else None
    kd_s, kg_s, vdt_s, vgt_s, wq_s, wg_s, s0_s, s1_s, m_s, l_s, acc_s = it
    tq = qa_ref.shape[0]
    n_blocks = (seq + past) // kb

    @pl.when(pl.program_id(1) == 0)
    def _prepare_keys():
        gmat = _group_avg_matrix(GQA_KV_W, GQA_HD)
        for i in range(seq // kb):
            rows = slice(i * kb, (i + 1) * kb)
            dk = dk_ref[rows, :]
            gk = gk_in_ref[rows, :]
            gk = gk * lax.rsqrt(_group_mean_sq(gk, gmat) + EPS) * gk_ref[...]
            if feature_major:
                gk_t = gk.T
                if len(gkn_ref.shape) == 2:
                    gkn_ref[:, rows] = gk_t
                else:
                    for l in range(gkn_ref.shape[0]):
                        gkn_ref[l, :, rows] = gk_t if l == layer else jnp.zeros_like(gk_t)
            if rope:
                ck = cosk_ref[rows, :]
                sk = sink_ref[rows, :]
                dk = _rope(dk, ck[:, 0:DIFF_W], sk[:, 0:DIFF_W])
                gk = _rope(gk, ck[:, DIFF_W:DIFF_W + GQA_KV_W], sk[:, DIFF_W:DIFF_W + GQA_KV_W])
            kd_s[rows, :] = dk.astype(BF16)
            kg_s[rows, :] = gk.astype(BF16)
            if feature_major:
                vdt_s[i] = dv_ref[:, rows].astype(BF16)
                vgt_s[i] = gv_ref[:, rows].astype(BF16)
            else:
                vdt_s[i] = dv_ref[rows, :].T.astype(BF16)
                vgt_s[i] = gv_ref[rows, :].T.astype(BF16)
        for j in range(past // kb):
            src = slice(j * kb, (j + 1) * kb)
            dst = slice(seq + j * kb, seq + (j + 1) * kb)
            kd_s[dst, :] = cdk_ref[:, src].T.astype(BF16)
            kg_s[dst, :] = cgk_ref[:, src].T.astype(BF16)
            vdt_s[seq // kb + j] = cdv_ref[:, src].astype(BF16)
            vgt_s[seq // kb + j] = cgv_ref[:, src].astype(BF16)

    qa = qa_ref[...]
    qd = qa[:, 0:DIFF_W]
    gq = qa[:, DIFF_W:DIFF_W + GQA_W]
    gq = gq * lax.rsqrt(_group_mean_sq(gq, _group_avg_matrix(GQA_W, GQA_HD)) + EPS) * gq_ref[...]
    if rope:
        cq = cosq_ref[...]
        sq = sinq_ref[...]
        qd = _rope(qd, cq[:, 0:DIFF_W], sq[:, 0:DIFF_W])
        gq = _rope(gq, cq[:, DIFF_W:DIFF_W + GQA_W], sq[:, DIFF_W:DIFF_W + GQA_W])
    qd_t = (qd * (DIFF_QK ** -0.5 * LOG2E)).T
    gq_t = (gq * (GQA_HD ** -0.5 * LOG2E)).T
    row = lax.broadcasted_iota(jnp.int32, (DIFF_W, tq), 0)
    for hm in range(2 * DIFF_HEADS):
        lo = hm * DIFF_QK
        wq_s[:, hm * tq:(hm + 1) * tq] = jnp.where((row >= lo) & (row < lo + DIFF_QK), qd_t, 0.0).astype(BF16)
    zeros = jnp.zeros((GQA_HD, tq), F32)
    for h in range(GQA_Q_HEADS):
        piece = gq_t[h * GQA_HD:(h + 1) * GQA_HD, :]
        pair = [piece, zeros] if h // (GQA_Q_HEADS // GQA_KV_HEADS) == 0 else [zeros, piece]
        wg_s[:, h * tq:(h + 1) * tq] = jnp.concatenate(pair, axis=0).astype(BF16)
    m_s[...] = jnp.full(m_s.shape, NEG_BIG, F32)
    l_s[...] = jnp.zeros(l_s.shape, F32)
    acc_s[...] = jnp.zeros(acc_s.shape, F32)
    n_diff = 2 * DIFF_HEADS * tq
    n_all = N_SCORE_HEADS * tq

    slabs = tq // LANES

    def key_rows(j):
        return pl.ds(j * kb if isinstance(j, int) else pl.multiple_of(j * kb, kb), kb)

    def head_scores(idx, k_d, k_g, s_buf):
        if idx < 2 * DIFF_HEADS:
            s = _dot(k_d, wq_s[:, idx * tq:(idx + 1) * tq])
        else:
            h = idx - 2 * DIFF_HEADS
            s = _dot(k_g, wg_s[:, h * tq:(h + 1) * tq])
        for k in range(slabs):
            s_buf[idx * slabs + k] = s[:, k * LANES:(k + 1) * LANES]

    def head_update(idx, v_d, v_g, s_buf):
        ps, alphas = [], []
        for k in range(slabs):
            c = idx * slabs + k
            cols = slice(c * LANES, (c + 1) * LANES)
            s = s_buf[c]
            m_old = m_s[:, cols]
            m_new = jnp.maximum(m_old, jnp.max(s, axis=0, keepdims=True))
            alphas.append(jnp.exp2(m_old - m_new))
            ps.append(jnp.exp2(s - m_new).astype(BF16))
            m_s[:, cols] = m_new
        if idx < 2 * DIFF_HEADS:
            vh = idx // 2
            v_t = v_d[vh * DIFF_V:(vh + 1) * DIFF_V, :]
        else:
            vh = (idx - 2 * DIFF_HEADS) // (GQA_Q_HEADS // GQA_KV_HEADS)
            v_t = v_g[vh * GQA_HD:(vh + 1) * GQA_HD, :]
        v_ext = jnp.concatenate([v_t, jnp.ones((ONES_ROWS, kb), BF16)], axis=0)
        alpha = jnp.concatenate(alphas, axis=1)
        pv = _dot(v_ext, jnp.concatenate(ps, axis=1))
        cols = slice(idx * tq, (idx + 1) * tq)
        acc_s[idx] = alpha * acc_s[idx] + pv[0:GQA_HD, :]
        l_s[:, cols] = alpha * l_s[:, cols] + pv[GQA_HD:GQA_HD + 1, :]

    def key_block(j, s_cur, j_next, s_next):
        v_d = vdt_s[j]
        v_g = vgt_s[j]
        if j_next is not None:
            k_d = kd_s[key_rows(j_next), :]
            k_g = kg_s[key_rows(j_next), :]
        for idx in range(N_SCORE_HEADS):
            if j_next is not None:
                head_scores(idx, k_d, k_g, s_next)
            head_update(idx, v_d, v_g, s_cur)

    for idx in range(N_SCORE_HEADS):
        head_scores(idx, kd_s[key_rows(0), :], kg_s[key_rows(0), :], s0_s)
    if n_blocks > 1:
        assert n_blocks % 2 == 0

        def block_pair(i, carry):
            key_block(2 * i, s0_s, 2 * i + 1, s1_s)
            key_block(2 * i + 1, s1_s, 2 * i + 2, s0_s)
            return carry

        lax.fori_loop(0, n_blocks // 2 - 1, block_pair, 0)
        key_block(n_blocks - 2, s0_s, n_blocks - 1, s1_s)
        key_block(n_blocks - 1, s1_s, None, None)
    else:
        key_block(0, s0_s, None, None)

    lam = _diff_lambda(lam_ref, lam_init)
    outs = []
    for h in range(DIFF_HEADS):
        c0 = slice(2 * h * tq, (2 * h + 1) * tq)
        c1 = slice((2 * h + 1) * tq, (2 * h + 2) * tq)
        o = acc_s[2 * h] * (1.0 / l_s[:, c0]) - acc_s[2 * h + 1] * (lam / l_s[:, c1])
        ms = jnp.mean(o * o, axis=0, keepdims=True)
        outs.append(o * (lax.rsqrt(ms + EPS) * (1.0 - lam_init)))
    for h in range(GQA_Q_HEADS):
        idx = 2 * DIFF_HEADS + h
        outs.append(acc_s[idx] * (1.0 / l_s[:, idx * tq:(idx + 1) * tq]))
    o_ref[...] = jnp.concatenate(outs, axis=0).T * gain_ref[...]


def _attention(qa, kv, caches, layer, tables, lam_p, gain, gq_t, gk_t, lam_init, batch, seq, tq,
               gkn_prev=None):
    t = qa.shape[0]
    nq = seq // tq
    feature_major = gkn_prev is not None
    past = caches[0].shape[3] if caches is not None else 0
    kb = min(256, seq)
    assert seq % kb == 0 and past % kb == 0 and seq % tq == 0
    n_blocks = (seq + past) // kb
    n_all = N_SCORE_HEADS * tq
    full = lambda a: pl.BlockSpec(a.shape, lambda b, q: (0,) * a.ndim)
    token_major = lambda a: pl.BlockSpec((seq, a.shape[1]), lambda b, q: (b, 0))
    by_feature = lambda a: pl.BlockSpec((None, None, a.shape[2], seq), lambda b, q: (b, layer, 0, 0))
    dk, dv, gk, gv = kv
    in_specs = [pl.BlockSpec((tq, qa.shape[1]), lambda b, q: (b * nq + q, 0)),
                token_major(dk), by_feature(dv) if feature_major else token_major(dv),
                token_major(gk), by_feature(gv) if feature_major else token_major(gv)]
    args = [qa, dk, dv, gk, gv]
    aliases = {}
    if feature_major:
        assert nq == 1
        in_specs += [pl.BlockSpec(memory_space=pl.ANY)] * len(gkn_prev)
        args += list(gkn_prev)
        aliases = {5 + k: 1 + k for k in range(len(gkn_prev))}
    if caches is not None:
        in_specs += [pl.BlockSpec((None, None, a.shape[2], past), lambda b, q: (b, layer, 0, 0))
                     for a in caches]
        args += list(caches)
    if tables is not None:
        cosq, sinq, cosk, sink = tables
        in_specs += [pl.BlockSpec((tq, cosq.shape[1]), lambda b, q: (q, 0)),
                     pl.BlockSpec((tq, sinq.shape[1]), lambda b, q: (q, 0)),
                     full(cosk), full(sink)]
        args += [cosq, sinq, cosk, sink]
    in_specs += [full(lam_p), full(gain), full(gq_t), full(gk_t)]
    args += [lam_p, gain, gq_t, gk_t]
    out_specs = [pl.BlockSpec((tq, DIFF_W + GQA_W), lambda b, q: (b * nq + q, 0))]
    out_shape = [jax.ShapeDtypeStruct((t, DIFF_W + GQA_W), F32)]
    if feature_major:
        depth = dv.shape[1]
        if gkn_prev:
            out_specs.append(pl.BlockSpec((None, None, GQA_KV_W, seq), lambda b, q: (b, layer, 0, 0)))
        else:
            out_specs.append(pl.BlockSpec((None, depth, GQA_KV_W, seq), lambda b, q: (b, 0, 0, 0)))
        out_shape.append(jax.ShapeDtypeStruct((batch, depth, GQA_KV_W, seq), F32))
    cfg = (lam_init, seq, past, kb, tables is not None, feature_major,
           len(gkn_prev) if feature_major else 0, layer)
    return pl.pallas_call(
        functools.partial(_attn_kernel, cfg),
        grid=(batch, nq),
        in_specs=in_specs,
        out_specs=out_specs,
        out_shape=out_shape,
        input_output_aliases=aliases,
        scratch_shapes=[
            pltpu.VMEM((seq + past, DIFF_W), BF16), pltpu.VMEM((seq + past, GQA_KV_W), BF16),
            pltpu.VMEM((n_blocks, DIFF_W, kb), BF16), pltpu.VMEM((n_blocks, GQA_KV_W, kb), BF16),
            pltpu.VMEM((DIFF_W, 2 * DIFF_HEADS * tq), BF16), pltpu.VMEM((GQA_KV_W, GQA_Q_HEADS * tq), BF16),
            pltpu.VMEM((n_all // LANES, kb, LANES), F32), pltpu.VMEM((n_all // LANES, kb, LANES), F32),
            pltpu.VMEM((1, n_all), F32), pltpu.VMEM((1, n_all), F32),
            pltpu.VMEM((N_SCORE_HEADS, GQA_HD, tq), F32),
        ],
        compiler_params=pltpu.CompilerParams(
            dimension_semantics=("arbitrary", "arbitrary"), vmem_limit_bytes=VMEM_LIMIT),
    )(*args)


def _softplus(x):
    return jnp.maximum(x, 0.0) + jnp.log1p(jnp.exp(-jnp.abs(x)))


def _dot2_l(a, b_exact):
    a1 = a.astype(BF16)
    a2 = (a - a1.astype(F32)).astype(BF16)
    return _dot(a1, b_exact) + _dot(a2, b_exact)


def _ssd_kernel(has_init, n_prev, seq, out_layer, *refs):
    it = iter(refs)
    z_ref, xbc_ref, dt_ref, cw_ref, cb_ref, alog_ref, dtb_ref, dexp_ref, ng_ref = (
        next(it) for _ in range(9))
    sf0_ref, sb0_ref = (next(it), next(it)) if has_init else (None, None)
    for _ in range(n_prev):
        next(it)
    o_ref, sf_ref, sb_ref = next(it), next(it), next(it)
    stf_s, stb_s, inc_s, dec_s, eab_s, cbf_s, exp_s, tri_s, gm_s = it
    L = SSD_CHUNK
    W2 = 2 * SSD_INNER
    nchunks = seq // L
    halo = 8

    ri = lax.broadcasted_iota(jnp.int32, (L, L), 0)
    ci = lax.broadcasted_iota(jnp.int32, (L, L), 1)
    lower = ri >= ci
    upper = ri <= ci
    tri_s[0] = jnp.where(lower, 1.0, 0.0).astype(BF16)
    tri_s[1] = jnp.where(upper, 1.0, 0.0).astype(BF16)
    lane_row = lax.broadcasted_iota(jnp.int32, (1, LANES), 1)
    a_row = jnp.where(lane_row < DT_W, -jnp.exp(alog_ref[...]), 0.0)
    ej = lax.broadcasted_iota(jnp.int32, (LANES, W2), 0)
    eh = lax.shift_right_logical(lax.broadcasted_iota(jnp.int32, (LANES, W2), 1), 6)
    exp_s[...] = jnp.where(ej == eh, 1.0, 0.0).astype(BF16)
    gm_s[...] = jnp.where(lax.shift_right_logical(ej, 6) == (lax.shift_right_logical(eh, 2) & 1), 1.0, 0.0)

    def chunk_rows(c):
        return pl.ds(pl.multiple_of(c * L, L), L)

    def load_state(ref):
        r = lax.broadcasted_iota(jnp.int32, (SSD_STATE, LANES), 0)
        c = lax.broadcasted_iota(jnp.int32, (SSD_STATE, LANES), 1)
        dup = jnp.where((c & (SSD_STATE - 1)) == r, 1.0, 0.0).astype(BF16)
        return _dot3_l(ref[...], dup).T * gm_s[:, 0:SSD_INNER]

    def store_state(st_ref, ref):
        st_t = st_ref[...].T
        _put_layer_slot(ref, (), out_layer, (st_t + _swap_halves(st_t))[:, 0:SSD_STATE])

    stf_s[...] = load_state(sf0_ref) if has_init else jnp.zeros(stf_s.shape, F32)
    stb_s[...] = load_state(sb0_ref) if has_init else jnp.zeros(stb_s.shape, F32)

    def forward_pass(c, carry):
        r0 = c * L
        rows = chunk_rows(c)
        dt_c = _softplus(dt_ref[rows, :] + dtb_ref[...])

        d1, d2, d3 = _split3(dt_c * a_row)
        acs_f = _dot(tri_s[0], d1) + (_dot(tri_s[0], d2) + _dot(tri_s[0], d3))
        acs_b = _dot(tri_s[1], d1) + (_dot(tri_s[1], d2) + _dot(tri_s[1], d3))
        acs = jnp.where(lax.broadcasted_iota(jnp.int32, (L, LANES), 1) < SSD_HEADS, acs_f, acs_b)
        acs_t = acs.T
        expand = exp_s[...]
        dt_e = _dot2_l(dt_c, expand)
        acs_e = _dot3_l(acs, expand)
        edge = jnp.concatenate([acs_e[L - 1:L, 0:SSD_INNER], acs_e[0:1, SSD_INNER:W2]], axis=1)
        eacs = jnp.exp(acs_e)
        cdec = jnp.exp(edge)

        prev = xbc_ref[pl.ds(pl.multiple_of(jnp.maximum(r0 - halo, 0), halo), halo), :]
        nxt = xbc_ref[pl.ds(pl.multiple_of(jnp.minimum(r0 + L, seq - halo), halo), halo), :]
        cur = xbc_ref[rows, :]
        win = jnp.concatenate([jnp.where(c > 0, prev, 0.0), cur,
                               jnp.where(c < nchunks - 1, nxt, 0.0)], axis=0)
        acc = cb_ref[...] + cur * cw_ref[SSD_CONV // 2:SSD_CONV // 2 + 1, :]
        for j in range(SSD_CONV):
            if j != SSD_CONV // 2:
                shifted = pltpu.roll(win, (SSD_CONV // 2 - j) % (L + 2 * halo), 0)[halo:halo + L, :]
                acc = acc + shifted * cw_ref[j:j + 1, :]
        act = _silu(acc)
        x_c = act[:, 0:SSD_INNER]
        b_c = act[:, SSD_INNER:SSD_INNER + SSD_BC_W]
        c_c = act[:, SSD_INNER + SSD_BC_W:XBC_W]
        xd = jnp.concatenate([x_c, x_c], axis=1) * dt_e
        xd_b = xd.astype(BF16)
        xdw = (xd * jnp.exp(edge - acs_e)).astype(BF16)
        b_b = b_c.astype(BF16)
        c_b = c_c.astype(BF16)
        s_new = _dot(b_c.T.astype(BF16), xdw) * gm_s[...]

        st_f = stf_s[...]
        y = _dot(c_b, st_f.astype(BF16)) * eacs[:, 0:SSD_INNER]
        stf_s[...] = st_f * cdec[:, 0:SSD_INNER] + s_new[:, 0:SSD_INNER]
        inc_s[c] = s_new[:, SSD_INNER:W2]
        dec_s[c] = cdec[:, SSD_INNER:W2]
        eab_s[rows, :] = eacs[:, SSD_INNER:W2]
        cbf_s[rows, :] = c_b

        cbs = []
        for g in range(SSD_GROUPS):
            cg = jnp.where(_lane_mask(c_c.shape, g * SSD_STATE, (g + 1) * SSD_STATE), c_c, 0.0)
            cbs.append(_dot_nt(cg.astype(BF16), b_b))
        for direction, causal in ((0, lower), (1, upper)):
            pairs = []
            for g in range(SSD_GROUPS):
                cb = cbs[g]
                for hp in range(2):
                    pair = g * 2 + hp
                    res = []
                    for k in range(2):
                        j = direction * SSD_HEADS + pair * 2 + k
                        diff = acs[:, j:j + 1] - acs_t[j:j + 1, :]
                        dec = jnp.where(causal, jnp.exp(diff), 0.0)
                        sc = (cb * dec).astype(BF16)
                        lo = direction * SSD_INNER + pair * LANES
                        res.append(_dot(sc, xd_b[:, lo:lo + LANES]))
                    pairs.append(jnp.where(_lane_mask(res[0].shape, 0, SSD_HD), res[0], res[1]))
            y = y + jnp.concatenate(pairs, axis=-1)
        o_ref[rows, :] = y + x_c * dexp_ref[...]
        return carry

    lax.fori_loop(0, nchunks, forward_pass, 0)
    store_state(stf_s, sf_ref)

    def backward_pass(i, carry):
        c = nchunks - 1 - i
        rows = chunk_rows(c)
        st_b = stb_s[...]
        y = o_ref[rows, :] + _dot(cbf_s[rows, :], st_b.astype(BF16)) * eab_s[rows, :]
        stb_s[...] = st_b * dec_s[c] + inc_s[c]
        yt = y * _silu(z_ref[rows, :])
        ms = jnp.mean(yt * yt, axis=-1, keepdims=True)
        o_ref[rows, :] = yt * lax.rsqrt(ms + EPS) * ng_ref[...]
        return carry

    lax.fori_loop(0, nchunks, backward_pass, 0, unroll=2)
    store_state(stb_s, sb_ref)


def _ssd(z, xbc, dt, params, init, layer, batch, seq, state_prev=(), depth=1):
    cw, cb, alog, dtb, dexp, ng = params
    t = z.shape[0]
    has_init = init is not None
    nchunks = seq // SSD_CHUNK
    full = lambda a: pl.BlockSpec(a.shape, lambda b: (0,) * a.ndim)
    out_layer = layer if depth > 1 else 0
    if state_prev or depth == 1:
        st_spec = pl.BlockSpec((None, None, SSD_INNER, SSD_STATE), lambda b: (b, out_layer, 0, 0))
    else:
        st_spec = pl.BlockSpec((None, depth, SSD_INNER, SSD_STATE), lambda b: (b, 0, 0, 0))
    in_specs = [
        pl.BlockSpec((seq, SSD_INNER), lambda b: (b, 0)),
        pl.BlockSpec((seq, XBC_W), lambda b: (b, 0)),
        pl.BlockSpec((seq, LANES), lambda b: (b, 0)),
        full(cw), full(cb), full(alog), full(dtb), full(dexp), full(ng),
    ]
    args = [z, xbc, dt, cw, cb, alog, dtb, dexp, ng]
    if has_init:
        init_spec = pl.BlockSpec((None, None, SSD_INNER, SSD_STATE), lambda b: (b, layer, 0, 0))
        in_specs += [init_spec, init_spec]
        args += list(init)
    aliases = {len(args) + k: 1 + k for k in range(len(state_prev))}
    in_specs += [pl.BlockSpec(memory_space=pl.ANY)] * len(state_prev)
    args += list(state_prev)
    st_shape = jax.ShapeDtypeStruct((batch, depth, SSD_INNER, SSD_STATE), F32)
    return pl.pallas_call(
        functools.partial(_ssd_kernel, has_init, len(state_prev), seq, out_layer),
        grid=(batch,),
        in_specs=in_specs,
        out_specs=[pl.BlockSpec((seq, SSD_INNER), lambda b: (b, 0)), st_spec, st_spec],
        out_shape=[jax.ShapeDtypeStruct((t, SSD_INNER), F32), st_shape, st_shape],
        input_output_aliases=aliases,
        scratch_shapes=[
            pltpu.VMEM((LANES, SSD_INNER), F32), pltpu.VMEM((LANES, SSD_INNER), F32),
            pltpu.VMEM((nchunks, LANES, SSD_INNER), F32),
            pltpu.VMEM((nchunks, 1, SSD_INNER), F32),
            pltpu.VMEM((seq, SSD_INNER), F32),
            pltpu.VMEM((seq, SSD_BC_W), BF16),
            pltpu.VMEM((LANES, 2 * SSD_INNER), BF16),
            pltpu.VMEM((2, SSD_CHUNK, SSD_CHUNK), BF16),
            pltpu.VMEM((LANES, 2 * SSD_INNER), F32),
        ],
        compiler_params=pltpu.CompilerParams(
            dimension_semantics=("arbitrary",), vmem_limit_bytes=VMEM_LIMIT),
    )(*args)


def _post_kernel(alpha, d_ff, ff_chunk, x_ref, oa_ref, os_ref, g1_ref, sh2_ref, sc2_ref, g2_ref,
                 wo_ref, wfi_ref, wfo_ref, lng_ref, lnb_ref, y_ref):
    wa = oa_ref.shape[1]
    half = x_ref.shape[0] // 2
    bounds = list(range(0, d_ff, ff_chunk)) + [d_ff]
    n_chunks = len(bounds) - 1

    def mix_and_norm(r):
        o = (_dot(oa_ref[r, :].astype(BF16), wo_ref[0:wa, :])
             + _dot(os_ref[r, :].astype(BF16), wo_ref[wa:, :]))
        x1 = _layer_norm(alpha * x_ref[r, :] + g1_ref[...] * o, lng_ref[0:1, :], lnb_ref[0:1, :])
        return x1, (x1 * (1.0 + sc2_ref[...]) + sh2_ref[...]).astype(BF16)

    def up(h2, c):
        lo, hi = bounds[c], bounds[c + 1]
        return _dot(h2, wfi_ref[:, lo:hi]), _dot(h2, wfi_ref[:, d_ff + lo:d_ff + hi])

    def ffn_chunks(h2):
        f = jnp.zeros((half, x_ref.shape[1]), F32)
        g, u = up(h2, 0)
        for c in range(n_chunks):
            nxt = up(h2, c + 1) if c + 1 < n_chunks else None
            f = f + _dot((_silu(g) * u).astype(BF16), wfo_ref[bounds[c]:bounds[c + 1], :])
            if nxt is not None:
                g, u = nxt
            yield f

    def finish(r, x1, f):
        y_ref[r, :] = _layer_norm(alpha * x1 + g2_ref[...] * f, lng_ref[1:2, :], lnb_ref[1:2, :])

    ra, rb = slice(0, half), slice(half, 2 * half)
    x1a, h2a = mix_and_norm(ra)
    ffn_a = ffn_chunks(h2a)
    for _ in range(n_chunks // 2):
        fa = next(ffn_a)
    x1b, h2b = mix_and_norm(rb)
    for fa in ffn_a:
        pass
    ffn_b = ffn_chunks(h2b)
    for _ in range(n_chunks // 2):
        fb = next(ffn_b)
    finish(ra, x1a, fa)
    for fb in ffn_b:
        pass
    finish(rb, x1b, fb)


def _post(x, oa, os_, mods, layer, row_fn, wo_bf, wfi_bf, wfo_bf, lng, lnb, alpha, tm):
    t, d = x.shape
    d_ff = wfo_bf.shape[0]
    ff_chunk = 3 * MXU_TILE
    assert d_ff % MXU_TILE == 0
    per_layer = lambda a: pl.BlockSpec((None,) + a.shape[1:], lambda i: (layer, 0, 0))
    resident = lambda a: pl.BlockSpec(a.shape, lambda i: (0, 0), pipeline_mode=pl.Buffered(1))
    return pl.pallas_call(
        functools.partial(_post_kernel, alpha, d_ff, ff_chunk),
        grid=(t // tm,),
        in_specs=[
            pl.BlockSpec((tm, d), lambda i: (i, 0)),
            pl.BlockSpec((tm, oa.shape[1]), lambda i: (i, 0)),
            pl.BlockSpec((tm, os_.shape[1]), lambda i: (i, 0)),
            _mod_spec(layer, 2, row_fn, d),
            _mod_spec(layer, 3, row_fn, d),
            _mod_spec(layer, 4, row_fn, d),
            _mod_spec(layer, 5, row_fn, d),
            resident(wo_bf), resident(wfi_bf), resident(wfo_bf),
            per_layer(lng), per_layer(lnb),
        ],
        out_specs=pl.BlockSpec((tm, d), lambda i: (i, 0)),
        out_shape=jax.ShapeDtypeStruct((t, d), F32),
        compiler_params=pltpu.CompilerParams(
            dimension_semantics=("arbitrary",), vmem_limit_bytes=VMEM_LIMIT),
    )(x, oa, os_, mods, mods, mods, mods, wo_bf, wfi_bf, wfo_bf, lng, lnb)


def _rope_tables(rows, dim, copies):
    row = jnp.repeat(jnp.arange(rows), GRID_W).astype(F32)
    col = jnp.tile(jnp.arange(GRID_W), rows).astype(F32)
    n_freq = dim // 4
    inv = ROPE_THETA ** (-jnp.arange(n_freq, dtype=F32) / n_freq)
    ang = jnp.concatenate([row[:, None] * inv, col[:, None] * inv], -1)
    cos, sin = jnp.cos(ang), jnp.sin(ang)
    cos_full = jnp.repeat(cos, 2, axis=-1)
    sin_signed = jnp.stack([-sin, sin], axis=-1).reshape(sin.shape[0], dim)
    return jnp.tile(cos_full, (1, copies)), jnp.tile(sin_signed, (1, copies))


def kernel(x_prompt, x_sample, cache_diff_k, cache_diff_v, cache_gqa_k, cache_gqa_v, state_ssd_fwd, state_ssd_bwd, c, c_ctx, w_ada, b_ada, w_in, w_out, diff_lambda, diff_subln_g, qk_norm_g, ssd_conv_w, ssd_conv_b, ssd_A_log, ssd_dt_bias, ssd_D, ssd_norm_g, ln_g, ln_b, w_ffn_in, w_ffn_out):
    batch, seq, d = x_prompt.shape
    dec_batch, dec_seq, _ = x_sample.shape
    depth = w_in.shape[0]
    past = cache_diff_k.shape[2]
    alpha = (2 * depth) ** 0.25
    rows = dec_seq // GRID_W

    n_vec = 1 + dec_batch
    n_pad = -(-n_vec // 8) * 8
    cvec = jnp.concatenate([c_ctx[None, :], c, jnp.zeros((n_pad - n_vec, d), F32)], axis=0)
    mods = _modulation(cvec, w_ada, b_ada).reshape(depth, n_pad, 1, 6 * d)

    cos_d, sin_d = _rope_tables(rows, DIFF_QK, DIFF_W // DIFF_QK)
    cos_g, sin_g = _rope_tables(rows, GQA_HD, GQA_Q_HEADS)
    cosq = jnp.concatenate([cos_d, cos_g], axis=-1)
    sinq = jnp.concatenate([sin_d, sin_g], axis=-1)
    cosk = jnp.concatenate([cos_d, cos_g[:, :GQA_KV_W]], axis=-1)
    sink = jnp.concatenate([sin_d, sin_g[:, :GQA_KV_W]], axis=-1)
    tables = (cosq, sinq, cosk, sink)

    feature_major = lambda a, w: jnp.transpose(a.reshape(dec_batch, depth, past, w), (0, 1, 3, 2))
    caches = (feature_major(cache_diff_k, DIFF_W), feature_major(cache_diff_v, DIFF_W),
              feature_major(cache_gqa_k, GQA_KV_W), feature_major(cache_gqa_v, GQA_KV_W))

    xp = x_prompt.reshape(batch * seq, d)
    xs = x_sample.reshape(dec_batch * dec_seq, d)
    tm_ctx = min(512, batch * seq)
    tm_lat = min(512, dec_seq)
    tm_post_ctx = min(1024, batch * seq)
    tm_post_lat = min(1024, dec_seq)
    tq = min(256, dec_seq)
    ctx_row = lambda i: 0
    lat_row = lambda i: 1 + (i * tm_lat) // dec_seq
    lat_row_post = lambda i: 1 + (i * tm_post_lat) // dec_seq

    w_in_bf = jnp.pad(w_in, ((0, 0), (0, 0), (0, IN_W_PAD - w_in.shape[2]))).astype(BF16)
    init = (state_ssd_fwd.reshape(dec_batch, depth, SSD_INNER, SSD_STATE),
            state_ssd_bwd.reshape(dec_batch, depth, SSD_INNER, SSD_STATE))

    kv_cache = ()
    gk_cache = ()
    ssd_states = ()
    for l in range(depth):
        lam_init = 0.8 - 0.6 * math.exp(-0.3 * l)
        lam_p = diff_lambda[l]
        attn_gain = jnp.concatenate([jnp.tile(diff_subln_g[l], DIFF_HEADS), jnp.ones((GQA_W,), F32)])[None, :]
        gq_t = jnp.tile(qk_norm_g[l, 0], GQA_Q_HEADS)[None, :]
        gk_t = jnp.tile(qk_norm_g[l, 1], GQA_KV_HEADS)[None, :]
        pad_row = lambda v: jnp.pad(v.reshape(1, -1), ((0, 0), (0, LANES - v.size)))
        ssd_params = (ssd_conv_w[l], ssd_conv_b[l][None, :], pad_row(ssd_A_log[l]),
                      pad_row(ssd_dt_bias[l]), jnp.repeat(ssd_D[l], SSD_HD)[None, :],
                      ssd_norm_g[l][None, :])

        qa, dk, gk, z, xbc, dt, *rest = _inproj(xp, mods, l, ctx_row, w_in_bf, tm_ctx,
                                                cache_prev=tuple(kv_cache), depth=depth, seq=seq,
                                                cast=(w_out, w_ffn_in, w_ffn_out))
        kv_cache, (wo_bf, wfi_bf, wfo_bf) = rest[:3], rest[3:]
        oa, gkn = _attention(qa, (dk, kv_cache[1], gk, kv_cache[2]), None, l, None, lam_p, attn_gain,
                             gq_t, gk_t, lam_init, batch, seq, seq, gkn_prev=gk_cache)
        gk_cache = (gkn,)
        os_, *ssd_states = _ssd(z, xbc, dt, ssd_params, None, l, batch, seq,
                                state_prev=tuple(ssd_states), depth=depth)
        xp = _post(xp, oa, os_, mods, l, ctx_row, wo_bf, wfi_bf, wfo_bf, ln_g, ln_b, alpha, tm_post_ctx)

        qa, dk, dv, gk, gv, z, xbc, dt = _inproj(xs, mods, l, lat_row, w_in_bf, tm_lat)
        (oa,) = _attention(qa, (dk, dv, gk, gv), caches, l, tables, lam_p, attn_gain, gq_t, gk_t,
                           lam_init, dec_batch, dec_seq, tq)
        os_, _, _ = _ssd(z, xbc, dt, ssd_params, init, l, dec_batch, dec_seq)
        xs = _post(xs, oa, os_, mods, l, lat_row_post, wo_bf, wfi_bf, wfo_bf, ln_g, ln_b, alpha,
                   tm_post_lat)

    def token_major(a, heads, width):
        return jnp.transpose(a.reshape(batch, depth, heads, width, seq), (0, 1, 4, 2, 3))

    state = lambda a: a.reshape(batch, depth, SSD_HEADS, SSD_HD, SSD_STATE)
    return (xp.reshape(batch, seq, d), xs.reshape(dec_batch, dec_seq, d),
            token_major(kv_cache[0], DIFF_HEADS, 2 * DIFF_QK), token_major(kv_cache[1], DIFF_HEADS, DIFF_V),
            token_major(gk_cache[0], GQA_KV_HEADS, GQA_HD), token_major(kv_cache[2], GQA_KV_HEADS, GQA_HD),
            state(ssd_states[0]), state(ssd_states[1]))
```

```python
import functools
import math

import jax
import jax.numpy as jnp
from jax import lax
from jax.experimental import pallas as pl
from jax.experimental.pallas import tpu as pltpu

F32 = jnp.float32
BF16 = jnp.bfloat16

GRID_W = 64
DIFF_HEADS = 4
DIFF_QK = 32
DIFF_V = 64
DIFF_W = DIFF_HEADS * DIFF_V
GQA_HD = 64
GQA_Q_HEADS = 4
GQA_KV_HEADS = 2
GQA_W = GQA_Q_HEADS * GQA_HD
GQA_KV_W = GQA_KV_HEADS * GQA_HD
SSD_HD = 64
SSD_HEADS = 8
SSD_INNER = SSD_HEADS * SSD_HD
SSD_GROUPS = 2
SSD_STATE = 64
SSD_BC_W = SSD_GROUPS * SSD_STATE
SSD_CONV = 5
SSD_CHUNK = 128
XBC_W = SSD_INNER + 2 * SSD_BC_W
DT_W = 2 * SSD_HEADS
ROPE_THETA = 10000.0
EPS = 1e-5
LANES = 128
MXU_TILE = 256
VMEM_LIMIT = 56 * 1024 * 1024

_C_DQ, _C_DK, _C_DV, _C_GQ, _C_GK, _C_GV, _C_Z, _C_XBC, _C_DT, _C_END = (
    0, 256, 512, 768, 1024, 1152, 1280, 1792, 2560, 2576)
IN_W_PAD = _C_DT + LANES


def _dot(a, b):
    return jnp.dot(a, b, preferred_element_type=F32)


def _dot_nt(a, b):
    return lax.dot_general(a, b, (((1,), (1,)), ((), ())), preferred_element_type=F32)


def _split3(a):
    a1 = a.astype(BF16)
    r1 = a - a1.astype(F32)
    a2 = r1.astype(BF16)
    a3 = (r1 - a2.astype(F32)).astype(BF16)
    return a1, a2, a3


def _dot3_l(a, b_exact):
    a1, a2, a3 = _split3(a)
    return _dot(a1, b_exact) + (_dot(a2, b_exact) + _dot(a3, b_exact))


def _sigmoid(x):
    return 1.0 / (1.0 + jnp.exp(-x))


def _silu(x):
    return x * _sigmoid(x)


def _layer_norm(x, g, b):
    mu = jnp.mean(x, axis=-1, keepdims=True)
    xc = x - mu
    var = jnp.mean(xc * xc, axis=-1, keepdims=True)
    return xc * lax.rsqrt(var + EPS) * g + b


def _group_avg_matrix(width, group):
    sh = int(math.log2(group))
    r = lax.shift_right_logical(lax.broadcasted_iota(jnp.int32, (width, width), 0), sh)
    c = lax.shift_right_logical(lax.broadcasted_iota(jnp.int32, (width, width), 1), sh)
    return jnp.where(r == c, 1.0 / group, 0.0).astype(BF16)


def _group_mean_sq(x, gmat):
    xx = x * x
    hi = xx.astype(BF16)
    lo = (xx - hi.astype(F32)).astype(BF16)
    return _dot(hi, gmat) + _dot(lo, gmat)


def _rope(x, cos, sin_signed):
    w = x.shape[-1]
    lane = lax.broadcasted_iota(jnp.int32, x.shape, 1)
    nxt = pltpu.roll(x, w - 1, 1)
    prv = pltpu.roll(x, 1, 1)
    partner = jnp.where((lane & 1) == 0, nxt, prv)
    return x * cos + partner * sin_signed


def _lane_mask(shape, lo, hi):
    lane = lax.broadcasted_iota(jnp.int32, shape, 1)
    return (lane >= lo) & (lane < hi)


def _mod_kernel(c_ref, w_ref, b_ref, o_ref):
    a = _silu(c_ref[...])
    a_hi = a.astype(BF16)
    a_lo = (a - a_hi.astype(F32)).astype(BF16)
    w = w_ref[...]
    w_hi = w.astype(BF16)
    w_lo = (w - w_hi.astype(F32)).astype(BF16)
    o_ref[...] = _dot(a_hi, w_hi) + (_dot(a_lo, w_hi) + _dot(a_hi, w_lo)) + b_ref[...]


def _modulation(cvec, w_ada, b_ada):
    depth, d, n = w_ada.shape
    tn = 1536
    rows = cvec.shape[0]
    return pl.pallas_call(
        _mod_kernel,
        grid=(depth, n // tn),
        in_specs=[
            pl.BlockSpec((rows, d), lambda l, j: (0, 0)),
            pl.BlockSpec((None, d, tn), lambda l, j: (l, 0, j)),
            pl.BlockSpec((None, 1, tn), lambda l, j: (l, 0, j)),
        ],
        out_specs=pl.BlockSpec((None, rows, tn), lambda l, j: (l, 0, j)),
        out_shape=jax.ShapeDtypeStruct((depth, rows, n), F32),
        compiler_params=pltpu.CompilerParams(
            dimension_semantics=("arbitrary", "arbitrary"), vmem_limit_bytes=VMEM_LIMIT),
    )(cvec, w_ada, b_ada.reshape(depth, 1, n))


def _put_layer_slot(ref, lead, layer, val):
    if len(ref.shape) == val.ndim + len(lead):
        ref[lead if lead else ...] = val
    else:
        for l in range(ref.shape[len(lead)]):
            ref[lead + (l,)] = val if l == layer else jnp.zeros_like(val)


def _inproj_kernel(feature_major_cache, n_cast, n_prev, layer, *refs):
    x_ref, sh_ref, sc_ref, w_ref = refs[:4]
    cast_in = refs[4:4 + n_cast]
    outs = refs[4 + n_cast + n_prev:]
    if n_cast:
        cast_out = outs[-n_cast - 2:-2] if feature_major_cache else outs[-n_cast:]
        outs = outs[:-n_cast - 2] + outs[-2:] if feature_major_cache else outs[:-n_cast]
        for src, dst in zip(cast_in, cast_out):
            dst[...] = src[...].astype(BF16)
    h = (x_ref[...] * (1.0 + sc_ref[...]) + sh_ref[...]).astype(BF16)

    def mm(lo, hi):
        return _dot(h, w_ref[:, lo:hi])

    gkv = mm(_C_GK, _C_Z)
    if feature_major_cache:
        (qa_ref, dk_ref, gk_ref, z_ref, xbc_ref, dt_ref, dkt_ref, dvt_ref, gvt_ref,
         dv_s, gv_s) = outs
        dk_ref[...] = mm(_C_DK, _C_DV)
        dv_s[...] = mm(_C_DV, _C_GQ)
        gv_s[...] = gkv[:, GQA_KV_W:]
        seq = dkt_ref.shape[-1]
        for j in range(dkt_ref.shape[0]):
            rows = slice(j * seq, (j + 1) * seq)
            _put_layer_slot(dkt_ref, (j,), layer, dk_ref[rows, :].T)
            _put_layer_slot(dvt_ref, (j,), layer, dv_s[rows, :].T)
            _put_layer_slot(gvt_ref, (j,), layer, gv_s[rows, :].T)
    else:
        qa_ref, dk_ref, dv_ref, gk_ref, gv_ref, z_ref, xbc_ref, dt_ref = outs
        dk_ref[...] = mm(_C_DK, _C_DV)
        dv_ref[...] = mm(_C_DV, _C_GQ)
        gv_ref[...] = gkv[:, GQA_KV_W:]
    qa_ref[:, 0:DIFF_W] = mm(_C_DQ, _C_DK)
    qa_ref[:, DIFF_W:DIFF_W + GQA_W] = mm(_C_GQ, _C_GK)
    gk_ref[...] = gkv[:, 0:GQA_KV_W]
    z_ref[...] = mm(_C_Z, _C_XBC)
    xbc_ref[...] = mm(_C_XBC, _C_DT)
    dt_ref[...] = mm(_C_DT, IN_W_PAD)


def _mod_spec(layer, which, row_fn, d):
    return pl.BlockSpec((None, None, 1, d), lambda i: (layer, row_fn(i), 0, which))


def _inproj(x, mods, layer, row_fn, w_in_bf, tm, cache_prev=None, depth=None, seq=None, cast=()):
    t, d = x.shape
    steps = t // tm
    feature_major = cache_prev is not None
    token_spec = lambda w: pl.BlockSpec((tm, w), lambda i: (i, 0))
    token_shape = lambda w: jax.ShapeDtypeStruct((t, w), F32)
    if feature_major:
        widths = (DIFF_W + GQA_W, DIFF_W, GQA_KV_W, SSD_INNER, XBC_W, LANES)
        cache_w = (DIFF_W, DIFF_W, GQA_KV_W)
        assert tm % seq == 0
        if cache_prev:
            cache_spec = lambda w: pl.BlockSpec((tm // seq, None, w, seq), lambda i: (i, layer, 0, 0))
        else:
            cache_spec = lambda w: pl.BlockSpec((tm // seq, depth, w, seq), lambda i: (i, 0, 0, 0))
        out_specs = [token_spec(w) for w in widths] + [cache_spec(w) for w in cache_w]
        out_shape = [token_shape(w) for w in widths] + [
            jax.ShapeDtypeStruct((t // seq, depth, w, seq), F32) for w in cache_w]
        aliases = {4 + k: len(widths) + k for k in range(len(cache_prev))}
    else:
        widths = (DIFF_W + GQA_W, DIFF_W, DIFF_W, GQA_KV_W, GQA_KV_W, SSD_INNER, XBC_W, LANES)
        out_specs = [token_spec(w) for w in widths]
        out_shape = [token_shape(w) for w in widths]
        cache_prev, aliases = (), {}
    aliases = {k + len(cast): v for k, v in aliases.items()}
    cast_in_specs, cast_out_specs, cast_out_shape = [], [], []
    for w in cast:
        rows = w.shape[1] // steps
        assert w.shape[1] % steps == 0 and rows % 16 == 0
        cast_in_specs.append(pl.BlockSpec((None, rows, w.shape[2]), lambda i: (layer, i, 0)))
        cast_out_specs.append(pl.BlockSpec((rows, w.shape[2]), lambda i: (i, 0)))
        cast_out_shape.append(jax.ShapeDtypeStruct(w.shape[1:], BF16))
    return pl.pallas_call(
        functools.partial(_inproj_kernel, feature_major, len(cast), len(cache_prev), layer),
        grid=(steps,),
        in_specs=[
            pl.BlockSpec((tm, d), lambda i: (i, 0)),
            _mod_spec(layer, 0, row_fn, d),
            _mod_spec(layer, 1, row_fn, d),
            pl.BlockSpec((None, d, IN_W_PAD), lambda i: (layer, 0, 0)),
        ] + cast_in_specs + [pl.BlockSpec(memory_space=pl.ANY)] * len(cache_prev),
        out_specs=out_specs + cast_out_specs,
        out_shape=out_shape + cast_out_shape,
        input_output_aliases=aliases,
        scratch_shapes=([pltpu.VMEM((tm, DIFF_W), F32), pltpu.VMEM((tm, GQA_KV_W), F32)]
                        if feature_major else []),
        compiler_params=pltpu.CompilerParams(
            dimension_semantics=("arbitrary",), vmem_limit_bytes=VMEM_LIMIT),
    )(x, mods, mods, w_in_bf, *cast, *cache_prev)


LOG2E = 1.4426950408889634
NEG_BIG = -1e30
N_SCORE_HEADS = 2 * DIFF_HEADS + GQA_Q_HEADS
ONES_ROWS = 16


def _diff_lambda(lam_ref, lam_init):
    lp = lam_ref[...]
    s1 = jnp.sum(lp[0:1, :] * lp[1:2, :], axis=-1, keepdims=True)
    s2 = jnp.sum(lp[2:3, :] * lp[3:4, :], axis=-1, keepdims=True)
    return jnp.exp(s1) - jnp.exp(s2) + lam_init


def _swap_halves(x):
    return pltpu.roll(x, GQA_HD, 1)


def _attn_kernel(cfg, *refs):
    lam_init, seq, past, kb, rope, feature_major, n_prev, layer = cfg
    it = iter(refs)
    qa_ref, dk_ref, dv_ref, gk_in_ref, gv_ref = next(it), next(it), next(it), next(it), next(it)
    for _ in range(n_prev):
        next(it)
    if past:
        cdk_ref, cdv_ref, cgk_ref, cgv_ref = next(it), next(it), next(it), next(it)
    if rope:
        cosq_ref, sinq_ref, cosk_ref, sink_ref = next(it), next(it), next(it), next(it)
    lam_ref, gain_ref, gq_ref, gk_ref = next(it), next(it), next(it), next(it)
    o_ref = next(it)
    gkn_ref = next(it) if feature_major else None
    kd_s, kg_s, vdt_s, vgt_s, wq_s, wg_s, s0_s, s1_s, m_s, l_s, acc_s = it
    tq = qa_ref.shape[0]
    n_blocks = (seq + past) // kb

    @pl.when(pl.program_id(1) == 0)
    def _prepare_keys():
        gmat = _group_avg_matrix(GQA_KV_W, GQA_HD)
        for i in range(seq // kb):
            rows = slice(i * kb, (i + 1) * kb)
            dk = dk_ref[rows, :]
            gk = gk_in_ref[rows, :]
            gk = gk * lax.rsqrt(_group_mean_sq(gk, gmat) + EPS) * gk_ref[...]
            if feature_major:
                gk_t = gk.T
                if len(gkn_ref.shape) == 2:
                    gkn_ref[:, rows] = gk_t
                else:
                    for l in range(gkn_ref.shape[0]):
                        gkn_ref[l, :, rows] = gk_t if l == layer else jnp.zeros_like(gk_t)
            if rope:
                ck = cosk_ref[rows, :]
                sk = sink_ref[rows, :]
                dk = _rope(dk, ck[:, 0:DIFF_W], sk[:, 0:DIFF_W])
                gk = _rope(gk, ck[:, DIFF_W:DIFF_W + GQA_KV_W], sk[:, DIFF_W:DIFF_W + GQA_KV_W])
            kd_s[rows, :] = dk.astype(BF16)
            kg_s[rows, :] = gk.astype(BF16)
            if feature_major:
                vdt_s[i] = dv_ref[:, rows].astype(BF16)
                vgt_s[i] = gv_ref[:, rows].astype(BF16)
            else:
                vdt_s[i] = dv_ref[rows, :].T.astype(BF16)
                vgt_s[i] = gv_ref[rows, :].T.astype(BF16)
        for j in range(past // kb):
            src = slice(j * kb, (j + 1) * kb)
            dst = slice(seq + j * kb, seq + (j + 1) * kb)
            kd_s[dst, :] = cdk_ref[:, src].T.astype(BF16)
            kg_s[dst, :] = cgk_ref[:, src].T.astype(BF16)
            vdt_s[seq // kb + j] = cdv_ref[:, src].astype(BF16)
            vgt_s[seq // kb + j] = cgv_ref[:, src].astype(BF16)

    qa = qa_ref[...]
    qd = qa[:, 0:DIFF_W]
    gq = qa[:, DIFF_W:DIFF_W + GQA_W]
    gq = gq * lax.rsqrt(_group_mean_sq(gq, _group_avg_matrix(GQA_W, GQA_HD)) + EPS) * gq_ref[...]
    if rope:
        cq = cosq_ref[...]
        sq = sinq_ref[...]
        qd = _rope(qd, cq[:, 0:DIFF_W], sq[:, 0:DIFF_W])
        gq = _rope(gq, cq[:, DIFF_W:DIFF_W + GQA_W], sq[:, DIFF_W:DIFF_W + GQA_W])
    qd_t = (qd * (DIFF_QK ** -0.5 * LOG2E)).T
    gq_t = (gq * (GQA_HD ** -0.5 * LOG2E)).T
    row = lax.broadcasted_iota(jnp.int32, (DIFF_W, tq), 0)
    for hm in range(2 * DIFF_HEADS):
        lo = hm * DIFF_QK
        wq_s[:, hm * tq:(hm + 1) * tq] = jnp.where((row >= lo) & (row < lo + DIFF_QK), qd_t, 0.0).astype(BF16)
    zeros = jnp.zeros((GQA_HD, tq), F32)
    for h in range(GQA_Q_HEADS):
        piece = gq_t[h * GQA_HD:(h + 1) * GQA_HD, :]
        pair = [piece, zeros] if h // (GQA_Q_HEADS // GQA_KV_HEADS) == 0 else [zeros, piece]
        wg_s[:, h * tq:(h + 1) * tq] = jnp.concatenate(pair, axis=0).astype(BF16)
    m_s[...] = jnp.full(m_s.shape, NEG_BIG, F32)
    l_s[...] = jnp.zeros(l_s.shape, F32)
    acc_s[...] = jnp.zeros(acc_s.shape, F32)
    n_diff = 2 * DIFF_HEADS * tq
    n_all = N_SCORE_HEADS * tq

    def key_rows(j):
        return pl.ds(j * kb if isinstance(j, int) else pl.multiple_of(j * kb, kb), kb)

    n_qt = tq // MXU_TILE
    tile_slabs = MXU_TILE // LANES

    def head_scores(idx, qt, k_d, k_g, s_buf):
        lo = idx * tq + qt * MXU_TILE
        if idx < 2 * DIFF_HEADS:
            s = _dot(k_d, wq_s[:, lo:lo + MXU_TILE])
        else:
            s = _dot(k_g, wg_s[:, lo - n_diff:lo - n_diff + MXU_TILE])
        for k in range(tile_slabs):
            s_buf[lo // LANES + k] = s[:, k * LANES:(k + 1) * LANES]

    def head_update(idx, qt, v_d, v_g, s_buf):
        lo = idx * tq + qt * MXU_TILE
        ps, alphas = [], []
        for k in range(tile_slabs):
            c = lo // LANES + k
            cols = slice(c * LANES, (c + 1) * LANES)
            s = s_buf[c]
            m_old = m_s[:, cols]
            m_new = jnp.maximum(m_old, jnp.max(s, axis=0, keepdims=True))
            alphas.append(jnp.exp2(m_old - m_new))
            ps.append(jnp.exp2(s - m_new).astype(BF16))
            m_s[:, cols] = m_new
        if idx < 2 * DIFF_HEADS:
            vh = idx // 2
            v_t = v_d[vh * DIFF_V:(vh + 1) * DIFF_V, :]
        else:
            vh = (idx - 2 * DIFF_HEADS) // (GQA_Q_HEADS // GQA_KV_HEADS)
            v_t = v_g[vh * GQA_HD:(vh + 1) * GQA_HD, :]
        v_ext = jnp.concatenate([v_t, jnp.ones((ONES_ROWS, kb), BF16)], axis=0)
        alpha = jnp.concatenate(alphas, axis=1)
        pv = _dot(v_ext, jnp.concatenate(ps, axis=1))
        cols = slice(lo, lo + MXU_TILE)
        qcols = slice(qt * MXU_TILE, (qt + 1) * MXU_TILE)
        acc_s[idx, :, qcols] = alpha * acc_s[idx, :, qcols] + pv[0:GQA_HD, :]
        l_s[:, cols] = alpha * l_s[:, cols] + pv[GQA_HD:GQA_HD + 1, :]

    def key_block(j, s_cur, j_next, s_next):
        v_d = vdt_s[j]
        v_g = vgt_s[j]
        if j_next is not None:
            k_d = kd_s[key_rows(j_next), :]
            k_g = kg_s[key_rows(j_next), :]
        for idx in range(N_SCORE_HEADS):
            for qt in range(n_qt):
                if j_next is not None:
                    head_scores(idx, qt, k_d, k_g, s_next)
                head_update(idx, qt, v_d, v_g, s_cur)

    for idx in range(N_SCORE_HEADS):
        for qt in range(n_qt):
            head_scores(idx, qt, kd_s[key_rows(0), :], kg_s[key_rows(0), :], s0_s)
    if n_blocks > 1:
        assert n_blocks % 2 == 0

        def block_pair(i, carry):
            key_block(2 * i, s0_s, 2 * i + 1, s1_s)
            key_block(2 * i + 1, s1_s, 2 * i + 2, s0_s)
            return carry

        lax.fori_loop(0, n_blocks // 2 - 1, block_pair, 0)
        key_block(n_blocks - 2, s0_s, n_blocks - 1, s1_s)
        key_block(n_blocks - 1, s1_s, None, None)
    else:
        key_block(0, s0_s, None, None)

    lam = _diff_lambda(lam_ref, lam_init)
    outs = []
    for h in range(DIFF_HEADS):
        c0 = slice(2 * h * tq, (2 * h + 1) * tq)
        c1 = slice((2 * h + 1) * tq, (2 * h + 2) * tq)
        o = acc_s[2 * h] * (1.0 / l_s[:, c0]) - acc_s[2 * h + 1] * (lam / l_s[:, c1])
        ms = jnp.mean(o * o, axis=0, keepdims=True)
        outs.append(o * (lax.rsqrt(ms + EPS) * (1.0 - lam_init)))
    for h in range(GQA_Q_HEADS):
        idx = 2 * DIFF_HEADS + h
        outs.append(acc_s[idx] * (1.0 / l_s[:, idx * tq:(idx + 1) * tq]))
    o_ref[...] = jnp.concatenate(outs, axis=0).T * gain_ref[...]


def _attention(qa, kv, caches, layer, tables, lam_p, gain, gq_t, gk_t, lam_init, batch, seq, tq,
               gkn_prev=None):
    t = qa.shape[0]
    nq = seq // tq
    feature_major = gkn_prev is not None
    past = caches[0].shape[3] if caches is not None else 0
    kb = min(256, seq)
    assert seq % kb == 0 and past % kb == 0 and seq % tq == 0
    n_blocks = (seq + past) // kb
    n_all = N_SCORE_HEADS * tq
    full = lambda a: pl.BlockSpec(a.shape, lambda b, q: (0,) * a.ndim)
    token_major = lambda a: pl.BlockSpec((seq, a.shape[1]), lambda b, q: (b, 0))
    by_feature = lambda a: pl.BlockSpec((None, None, a.shape[2], seq), lambda b, q: (b, layer, 0, 0))
    dk, dv, gk, gv = kv
    in_specs = [pl.BlockSpec((tq, qa.shape[1]), lambda b, q: (b * nq + q, 0)),
                token_major(dk), by_feature(dv) if feature_major else token_major(dv),
                token_major(gk), by_feature(gv) if feature_major else token_major(gv)]
    args = [qa, dk, dv, gk, gv]
    aliases = {}
    if feature_major:
        assert nq == 1
        in_specs += [pl.BlockSpec(memory_space=pl.ANY)] * len(gkn_prev)
        args += list(gkn_prev)
        aliases = {5 + k: 1 + k for k in range(len(gkn_prev))}
    if caches is not None:
        in_specs += [pl.BlockSpec((None, None, a.shape[2], past), lambda b, q: (b, layer, 0, 0))
                     for a in caches]
        args += list(caches)
    if tables is not None:
        cosq, sinq, cosk, sink = tables
        in_specs += [pl.BlockSpec((tq, cosq.shape[1]), lambda b, q: (q, 0)),
                     pl.BlockSpec((tq, sinq.shape[1]), lambda b, q: (q, 0)),
                     full(cosk), full(sink)]
        args += [cosq, sinq, cosk, sink]
    in_specs += [full(lam_p), full(gain), full(gq_t), full(gk_t)]
    args += [lam_p, gain, gq_t, gk_t]
    out_specs = [pl.BlockSpec((tq, DIFF_W + GQA_W), lambda b, q: (b * nq + q, 0))]
    out_shape = [jax.ShapeDtypeStruct((t, DIFF_W + GQA_W), F32)]
    if feature_major:
        depth = dv.shape[1]
        if gkn_prev:
            out_specs.append(pl.BlockSpec((None, None, GQA_KV_W, seq), lambda b, q: (b, layer, 0, 0)))
        else:
            out_specs.append(pl.BlockSpec((None, depth, GQA_KV_W, seq), lambda b, q: (b, 0, 0, 0)))
        out_shape.append(jax.ShapeDtypeStruct((batch, depth, GQA_KV_W, seq), F32))
    cfg = (lam_init, seq, past, kb, tables is not None, feature_major,
           len(gkn_prev) if feature_major else 0, layer)
    return pl.pallas_call(
        functools.partial(_attn_kernel, cfg),
        grid=(batch, nq),
        in_specs=in_specs,
        out_specs=out_specs,
        out_shape=out_shape,
        input_output_aliases=aliases,
        scratch_shapes=[
            pltpu.VMEM((seq + past, DIFF_W), BF16), pltpu.VMEM((seq + past, GQA_KV_W), BF16),
            pltpu.VMEM((n_blocks, DIFF_W, kb), BF16), pltpu.VMEM((n_blocks, GQA_KV_W, kb), BF16),
            pltpu.VMEM((DIFF_W, 2 * DIFF_HEADS * tq), BF16), pltpu.VMEM((GQA_KV_W, GQA_Q_HEADS * tq), BF16),
            pltpu.VMEM((n_all // LANES, kb, LANES), F32), pltpu.VMEM((n_all // LANES, kb, LANES), F32),
            pltpu.VMEM((1, n_all), F32), pltpu.VMEM((1, n_all), F32),
            pltpu.VMEM((N_SCORE_HEADS, GQA_HD, tq), F32),
        ],
        compiler_params=pltpu.CompilerParams(
            dimension_semantics=("arbitrary", "arbitrary"), vmem_limit_bytes=VMEM_LIMIT),
    )(*args)


def _softplus(x):
    return jnp.maximum(x, 0.0) + jnp.log1p(jnp.exp(-jnp.abs(x)))


def _dot2_l(a, b_exact):
    a1 = a.astype(BF16)
    a2 = (a - a1.astype(F32)).astype(BF16)
    return _dot(a1, b_exact) + _dot(a2, b_exact)


def _ssd_kernel(has_init, n_prev, seq, out_layer, *refs):
    it = iter(refs)
    z_ref, xbc_ref, dt_ref, cw_ref, cb_ref, alog_ref, dtb_ref, dexp_ref, ng_ref = (
        next(it) for _ in range(9))
    sf0_ref, sb0_ref = (next(it), next(it)) if has_init else (None, None)
    for _ in range(n_prev):
        next(it)
    o_ref, sf_ref, sb_ref = next(it), next(it), next(it)
    stf_s, stb_s, inc_s, dec_s, eab_s, cbf_s, exp_s, tri_s, gm_s = it
    L = SSD_CHUNK
    W2 = 2 * SSD_INNER
    nchunks = seq // L
    halo = 8

    ri = lax.broadcasted_iota(jnp.int32, (L, L), 0)
    ci = lax.broadcasted_iota(jnp.int32, (L, L), 1)
    lower = ri >= ci
    upper = ri <= ci
    tri_s[0] = jnp.where(lower, 1.0, 0.0).astype(BF16)
    tri_s[1] = jnp.where(upper, 1.0, 0.0).astype(BF16)
    lane_row = lax.broadcasted_iota(jnp.int32, (1, LANES), 1)
    a_row = jnp.where(lane_row < DT_W, -jnp.exp(alog_ref[...]), 0.0)
    ej = lax.broadcasted_iota(jnp.int32, (LANES, W2), 0)
    eh = lax.shift_right_logical(lax.broadcasted_iota(jnp.int32, (LANES, W2), 1), 6)
    exp_s[...] = jnp.where(ej == eh, 1.0, 0.0).astype(BF16)
    gm_s[...] = jnp.where(lax.shift_right_logical(ej, 6) == (lax.shift_right_logical(eh, 2) & 1), 1.0, 0.0)

    def chunk_rows(c):
        return pl.ds(pl.multiple_of(c * L, L), L)

    def load_state(ref):
        r = lax.broadcasted_iota(jnp.int32, (SSD_STATE, LANES), 0)
        c = lax.broadcasted_iota(jnp.int32, (SSD_STATE, LANES), 1)
        dup = jnp.where((c & (SSD_STATE - 1)) == r, 1.0, 0.0).astype(BF16)
        return _dot3_l(ref[...], dup).T * gm_s[:, 0:SSD_INNER]

    def store_state(st_ref, ref):
        st_t = st_ref[...].T
        _put_layer_slot(ref, (), out_layer, (st_t + _swap_halves(st_t))[:, 0:SSD_STATE])

    stf_s[...] = load_state(sf0_ref) if has_init else jnp.zeros(stf_s.shape, F32)
    stb_s[...] = load_state(sb0_ref) if has_init else jnp.zeros(stb_s.shape, F32)

    def forward_pass(c, carry):
        r0 = c * L
        rows = chunk_rows(c)
        dt_c = _softplus(dt_ref[rows, :] + dtb_ref[...])

        d1, d2, d3 = _split3(dt_c * a_row)
        acs_f = _dot(tri_s[0], d1) + (_dot(tri_s[0], d2) + _dot(tri_s[0], d3))
        acs_b = _dot(tri_s[1], d1) + (_dot(tri_s[1], d2) + _dot(tri_s[1], d3))
        acs = jnp.where(lax.broadcasted_iota(jnp.int32, (L, LANES), 1) < SSD_HEADS, acs_f, acs_b)
        acs_t = acs.T
        expand = exp_s[...]
        dt_e = _dot2_l(dt_c, expand)
        acs_e = _dot3_l(acs, expand)
        edge = jnp.concatenate([acs_e[L - 1:L, 0:SSD_INNER], acs_e[0:1, SSD_INNER:W2]], axis=1)
        eacs = jnp.exp(acs_e)
        cdec = jnp.exp(edge)

        prev = xbc_ref[pl.ds(pl.multiple_of(jnp.maximum(r0 - halo, 0), halo), halo), :]
        nxt = xbc_ref[pl.ds(pl.multiple_of(jnp.minimum(r0 + L, seq - halo), halo), halo), :]
        cur = xbc_ref[rows, :]
        win = jnp.concatenate([jnp.where(c > 0, prev, 0.0), cur,
                               jnp.where(c < nchunks - 1, nxt, 0.0)], axis=0)
        acc = cb_ref[...] + cur * cw_ref[SSD_CONV // 2:SSD_CONV // 2 + 1, :]
        for j in range(SSD_CONV):
            if j != SSD_CONV // 2:
                shifted = pltpu.roll(win, (SSD_CONV // 2 - j) % (L + 2 * halo), 0)[halo:halo + L, :]
                acc = acc + shifted * cw_ref[j:j + 1, :]
        act = _silu(acc)
        x_c = act[:, 0:SSD_INNER]
        b_c = act[:, SSD_INNER:SSD_INNER + SSD_BC_W]
        c_c = act[:, SSD_INNER + SSD_BC_W:XBC_W]
        xd = jnp.concatenate([x_c, x_c], axis=1) * dt_e
        xd_b = xd.astype(BF16)
        xdw = (xd * jnp.exp(edge - acs_e)).astype(BF16)
        b_b = b_c.astype(BF16)
        c_b = c_c.astype(BF16)
        s_new = _dot(b_c.T.astype(BF16), xdw) * gm_s[...]

        st_f = stf_s[...]
        y = _dot(c_b, st_f.astype(BF16)) * eacs[:, 0:SSD_INNER]
        stf_s[...] = st_f * cdec[:, 0:SSD_INNER] + s_new[:, 0:SSD_INNER]
        inc_s[c] = s_new[:, SSD_INNER:W2]
        dec_s[c] = cdec[:, SSD_INNER:W2]
        eab_s[rows, :] = eacs[:, SSD_INNER:W2]
        cbf_s[rows, :] = c_b

        cbs = []
        for g in range(SSD_GROUPS):
            cg = jnp.where(_lane_mask(c_c.shape, g * SSD_STATE, (g + 1) * SSD_STATE), c_c, 0.0)
            cbs.append(_dot_nt(cg.astype(BF16), b_b))
        for direction, causal in ((0, lower), (1, upper)):
            pairs = []
            for g in range(SSD_GROUPS):
                cb = cbs[g]
                for hp in range(2):
                    pair = g * 2 + hp
                    res = []
                    for k in range(2):
                        j = direction * SSD_HEADS + pair * 2 + k
                        diff = acs[:, j:j + 1] - acs_t[j:j + 1, :]
                        dec = jnp.where(causal, jnp.exp(diff), 0.0)
                        sc = (cb * dec).astype(BF16)
                        lo = direction * SSD_INNER + pair * LANES
                        res.append(_dot(sc, xd_b[:, lo:lo + LANES]))
                    pairs.append(jnp.where(_lane_mask(res[0].shape, 0, SSD_HD), res[0], res[1]))
            y = y + jnp.concatenate(pairs, axis=-1)
        o_ref[rows, :] = y + x_c * dexp_ref[...]
        return carry

    lax.fori_loop(0, nchunks, forward_pass, 0)
    store_state(stf_s, sf_ref)

    def backward_pass(i, carry):
        c = nchunks - 1 - i
        rows = chunk_rows(c)
        st_b = stb_s[...]
        y = o_ref[rows, :] + _dot(cbf_s[rows, :], st_b.astype(BF16)) * eab_s[rows, :]
        stb_s[...] = st_b * dec_s[c] + inc_s[c]
        yt = y * _silu(z_ref[rows, :])
        ms = jnp.mean(yt * yt, axis=-1, keepdims=True)
        o_ref[rows, :] = yt * lax.rsqrt(ms + EPS) * ng_ref[...]
        return carry

    lax.fori_loop(0, nchunks, backward_pass, 0, unroll=2)
    store_state(stb_s, sb_ref)


def _ssd(z, xbc, dt, params, init, layer, batch, seq, state_prev=(), depth=1):
    cw, cb, alog, dtb, dexp, ng = params
    t = z.shape[0]
    has_init = init is not None
    nchunks = seq // SSD_CHUNK
    full = lambda a: pl.BlockSpec(a.shape, lambda b: (0,) * a.ndim)
    out_layer = layer if depth > 1 else 0
    if state_prev or depth == 1:
        st_spec = pl.BlockSpec((None, None, SSD_INNER, SSD_STATE), lambda b: (b, out_layer, 0, 0))
    else:
        st_spec = pl.BlockSpec((None, depth, SSD_INNER, SSD_STATE), lambda b: (b, 0, 0, 0))
    in_specs = [
        pl.BlockSpec((seq, SSD_INNER), lambda b: (b, 0)),
        pl.BlockSpec((seq, XBC_W), lambda b: (b, 0)),
        pl.BlockSpec((seq, LANES), lambda b: (b, 0)),
        full(cw), full(cb), full(alog), full(dtb), full(dexp), full(ng),
    ]
    args = [z, xbc, dt, cw, cb, alog, dtb, dexp, ng]
    if has_init:
        init_spec = pl.BlockSpec((None, None, SSD_INNER, SSD_STATE), lambda b: (b, layer, 0, 0))
        in_specs += [init_spec, init_spec]
        args += list(init)
    aliases = {len(args) + k: 1 + k for k in range(len(state_prev))}
    in_specs += [pl.BlockSpec(memory_space=pl.ANY)] * len(state_prev)
    args += list(state_prev)
    st_shape = jax.ShapeDtypeStruct((batch, depth, SSD_INNER, SSD_STATE), F32)
    return pl.pallas_call(
        functools.partial(_ssd_kernel, has_init, len(state_prev), seq, out_layer),
        grid=(batch,),
        in_specs=in_specs,
        out_specs=[pl.BlockSpec((seq, SSD_INNER), lambda b: (b, 0)), st_spec, st_spec],
        out_shape=[jax.ShapeDtypeStruct((t, SSD_INNER), F32), st_shape, st_shape],
        input_output_aliases=aliases,
        scratch_shapes=[
            pltpu.VMEM((LANES, SSD_INNER), F32), pltpu.VMEM((LANES, SSD_INNER), F32),
            pltpu.VMEM((nchunks, LANES, SSD_INNER), F32),
            pltpu.VMEM((nchunks, 1, SSD_INNER), F32),
            pltpu.VMEM((seq, SSD_INNER), F32),
            pltpu.VMEM((seq, SSD_BC_W), BF16),
            pltpu.VMEM((LANES, 2 * SSD_INNER), BF16),
            pltpu.VMEM((2, SSD_CHUNK, SSD_CHUNK), BF16),
            pltpu.VMEM((LANES, 2 * SSD_INNER), F32),
        ],
        compiler_params=pltpu.CompilerParams(
            dimension_semantics=("arbitrary",), vmem_limit_bytes=VMEM_LIMIT),
    )(*args)


def _post_kernel(alpha, d_ff, ff_chunk, x_ref, oa_ref, os_ref, g1_ref, sh2_ref, sc2_ref, g2_ref,
                 wo_ref, wfi_ref, wfo_ref, lng_ref, lnb_ref, y_ref):
    wa = oa_ref.shape[1]
    half = x_ref.shape[0] // 2
    bounds = list(range(0, d_ff, ff_chunk)) + [d_ff]
    n_chunks = len(bounds) - 1

    def mix_and_norm(r):
        o = (_dot(oa_ref[r, :].astype(BF16), wo_ref[0:wa, :])
             + _dot(os_ref[r, :].astype(BF16), wo_ref[wa:, :]))
        x1 = _layer_norm(alpha * x_ref[r, :] + g1_ref[...] * o, lng_ref[0:1, :], lnb_ref[0:1, :])
        return x1, (x1 * (1.0 + sc2_ref[...]) + sh2_ref[...]).astype(BF16)

    def up(h2, c):
        lo, hi = bounds[c], bounds[c + 1]
        return _dot(h2, wfi_ref[:, lo:hi]), _dot(h2, wfi_ref[:, d_ff + lo:d_ff + hi])

    def ffn_chunks(h2):
        f = jnp.zeros((half, x_ref.shape[1]), F32)
        g, u = up(h2, 0)
        for c in range(n_chunks):
            nxt = up(h2, c + 1) if c + 1 < n_chunks else None
            f = f + _dot((_silu(g) * u).astype(BF16), wfo_ref[bounds[c]:bounds[c + 1], :])
            if nxt is not None:
                g, u = nxt
            yield f

    def finish(r, x1, f):
        y_ref[r, :] = _layer_norm(alpha * x1 + g2_ref[...] * f, lng_ref[1:2, :], lnb_ref[1:2, :])

    ra, rb = slice(0, half), slice(half, 2 * half)
    x1a, h2a = mix_and_norm(ra)
    ffn_a = ffn_chunks(h2a)
    for _ in range(n_chunks // 2):
        fa = next(ffn_a)
    x1b, h2b = mix_and_norm(rb)
    for fa in ffn_a:
        pass
    ffn_b = ffn_chunks(h2b)
    for _ in range(n_chunks // 2):
        fb = next(ffn_b)
    finish(ra, x1a, fa)
    for fb in ffn_b:
        pass
    finish(rb, x1b, fb)


def _post(x, oa, os_, mods, layer, row_fn, wo_bf, wfi_bf, wfo_bf, lng, lnb, alpha, tm):
    t, d = x.shape
    d_ff = wfo_bf.shape[0]
    ff_chunk = 3 * MXU_TILE
    assert d_ff % MXU_TILE == 0
    per_layer = lambda a: pl.BlockSpec((None,) + a.shape[1:], lambda i: (layer, 0, 0))
    resident = lambda a: pl.BlockSpec(a.shape, lambda i: (0, 0), pipeline_mode=pl.Buffered(1))
    return pl.pallas_call(
        functools.partial(_post_kernel, alpha, d_ff, ff_chunk),
        grid=(t // tm,),
        in_specs=[
            pl.BlockSpec((tm, d), lambda i: (i, 0)),
            pl.BlockSpec((tm, oa.shape[1]), lambda i: (i, 0)),
            pl.BlockSpec((tm, os_.shape[1]), lambda i: (i, 0)),
            _mod_spec(layer, 2, row_fn, d),
            _mod_spec(layer, 3, row_fn, d),
            _mod_spec(layer, 4, row_fn, d),
            _mod_spec(layer, 5, row_fn, d),
            resident(wo_bf), resident(wfi_bf), resident(wfo_bf),
            per_layer(lng), per_layer(lnb),
        ],
        out_specs=pl.BlockSpec((tm, d), lambda i: (i, 0)),
        out_shape=jax.ShapeDtypeStruct((t, d), F32),
        compiler_params=pltpu.CompilerParams(
            dimension_semantics=("arbitrary",), vmem_limit_bytes=VMEM_LIMIT),
    )(x, oa, os_, mods, mods, mods, mods, wo_bf, wfi_bf, wfo_bf, lng, lnb)


def _rope_tables(rows, dim, copies):
    row = jnp.repeat(jnp.arange(rows), GRID_W).astype(F32)
    col = jnp.tile(jnp.arange(GRID_W), rows).astype(F32)
    n_freq = dim // 4
    inv = ROPE_THETA ** (-jnp.arange(n_freq, dtype=F32) / n_freq)
    ang = jnp.concatenate([row[:, None] * inv, col[:, None] * inv], -1)
    cos, sin = jnp.cos(ang), jnp.sin(ang)
    cos_full = jnp.repeat(cos, 2, axis=-1)
    sin_signed = jnp.stack([-sin, sin], axis=-1).reshape(sin.shape[0], dim)
    return jnp.tile(cos_full, (1, copies)), jnp.tile(sin_signed, (1, copies))


def kernel(x_prompt, x_sample, cache_diff_k, cache_diff_v, cache_gqa_k, cache_gqa_v, state_ssd_fwd, state_ssd_bwd, c, c_ctx, w_ada, b_ada, w_in, w_out, diff_lambda, diff_subln_g, qk_norm_g, ssd_conv_w, ssd_conv_b, ssd_A_log, ssd_dt_bias, ssd_D, ssd_norm_g, ln_g, ln_b, w_ffn_in, w_ffn_out):
    batch, seq, d = x_prompt.shape
    dec_batch, dec_seq, _ = x_sample.shape
    depth = w_in.shape[0]
    past = cache_diff_k.shape[2]
    alpha = (2 * depth) ** 0.25
    rows = dec_seq // GRID_W

    n_vec = 1 + dec_batch
    n_pad = -(-n_vec // 8) * 8
    cvec = jnp.concatenate([c_ctx[None, :], c, jnp.zeros((n_pad - n_vec, d), F32)], axis=0)
    mods = _modulation(cvec, w_ada, b_ada).reshape(depth, n_pad, 1, 6 * d)

    cos_d, sin_d = _rope_tables(rows, DIFF_QK, DIFF_W // DIFF_QK)
    cos_g, sin_g = _rope_tables(rows, GQA_HD, GQA_Q_HEADS)
    cosq = jnp.concatenate([cos_d, cos_g], axis=-1)
    sinq = jnp.concatenate([sin_d, sin_g], axis=-1)
    cosk = jnp.concatenate([cos_d, cos_g[:, :GQA_KV_W]], axis=-1)
    sink = jnp.concatenate([sin_d, sin_g[:, :GQA_KV_W]], axis=-1)
    tables = (cosq, sinq, cosk, sink)

    feature_major = lambda a, w: jnp.transpose(a.reshape(dec_batch, depth, past, w), (0, 1, 3, 2))
    caches = (feature_major(cache_diff_k, DIFF_W), feature_major(cache_diff_v, DIFF_W),
              feature_major(cache_gqa_k, GQA_KV_W), feature_major(cache_gqa_v, GQA_KV_W))

    xp = x_prompt.reshape(batch * seq, d)
    xs = x_sample.reshape(dec_batch * dec_seq, d)
    tm_ctx = min(512, batch * seq)
    tm_lat = min(512, dec_seq)
    tm_post_ctx = min(1024, batch * seq)
    tm_post_lat = min(1024, dec_seq)
    tq = min(512, dec_seq)
    ctx_row = lambda i: 0
    lat_row = lambda i: 1 + (i * tm_lat) // dec_seq
    lat_row_post = lambda i: 1 + (i * tm_post_lat) // dec_seq

    w_in_bf = jnp.pad(w_in, ((0, 0), (0, 0), (0, IN_W_PAD - w_in.shape[2]))).astype(BF16)
    init = (state_ssd_fwd.reshape(dec_batch, depth, SSD_INNER, SSD_STATE),
            state_ssd_bwd.reshape(dec_batch, depth, SSD_INNER, SSD_STATE))

    kv_cache = ()
    gk_cache = ()
    ssd_states = ()
    for l in range(depth):
        lam_init = 0.8 - 0.6 * math.exp(-0.3 * l)
        lam_p = diff_lambda[l]
        attn_gain = jnp.concatenate([jnp.tile(diff_subln_g[l], DIFF_HEADS), jnp.ones((GQA_W,), F32)])[None, :]
        gq_t = jnp.tile(qk_norm_g[l, 0], GQA_Q_HEADS)[None, :]
        gk_t = jnp.tile(qk_norm_g[l, 1], GQA_KV_HEADS)[None, :]
        pad_row = lambda v: jnp.pad(v.reshape(1, -1), ((0, 0), (0, LANES - v.size)))
        ssd_params = (ssd_conv_w[l], ssd_conv_b[l][None, :], pad_row(ssd_A_log[l]),
                      pad_row(ssd_dt_bias[l]), jnp.repeat(ssd_D[l], SSD_HD)[None, :],
                      ssd_norm_g[l][None, :])

        qa, dk, gk, z, xbc, dt, *rest = _inproj(xp, mods, l, ctx_row, w_in_bf, tm_ctx,
                                                cache_prev=tuple(kv_cache), depth=depth, seq=seq,
                                                cast=(w_out, w_ffn_in, w_ffn_out))
        kv_cache, (wo_bf, wfi_bf, wfo_bf) = rest[:3], rest[3:]
        oa, gkn = _attention(qa, (dk, kv_cache[1], gk, kv_cache[2]), None, l, None, lam_p, attn_gain,
                             gq_t, gk_t, lam_init, batch, seq, seq, gkn_prev=gk_cache)
        gk_cache = (gkn,)
        os_, *ssd_states = _ssd(z, xbc, dt, ssd_params, None, l, batch, seq,
                                state_prev=tuple(ssd_states), depth=depth)
        xp = _post(xp, oa, os_, mods, l, ctx_row, wo_bf, wfi_bf, wfo_bf, ln_g, ln_b, alpha, tm_post_ctx)

        qa, dk, dv, gk, gv, z, xbc, dt = _inproj(xs, mods, l, lat_row, w_in_bf, tm_lat)
        (oa,) = _attention(qa, (dk, dv, gk, gv), caches, l, tables, lam_p, attn_gain, gq_t, gk_t,
                           lam_init, dec_batch, dec_seq, tq)
        os_, _, _ = _ssd(z, xbc, dt, ssd_params, init, l, dec_batch, dec_seq)
        xs = _post(xs, oa, os_, mods, l, lat_row_post, wo_bf, wfi_bf, wfo_bf, ln_g, ln_b, alpha,
                   tm_post_lat)

    def token_major(a, heads, width):
        return jnp.transpose(a.reshape(batch, depth, heads, width, seq), (0, 1, 4, 2, 3))

    state = lambda a: a.reshape(batch, depth, SSD_HEADS, SSD_HD, SSD_STATE)
    return (xp.reshape(batch, seq, d), xs.reshape(dec_batch, dec_seq, d),
            token_major(kv_cache[0], DIFF_HEADS, 2 * DIFF_QK), token_major(kv_cache[1], DIFF_HEADS, DIFF_V),
            token_major(gk_cache[0], GQA_KV_HEADS, GQA_HD), token_major(kv_cache[2], GQA_KV_HEADS, GQA_HD),
            state(ssd_states[0]), state(ssd_states[1]))
```

```python
import functools
import math

import jax
import jax.numpy as jnp
from jax import lax
from jax.experimental import pallas as pl
from jax.experimental.pallas import tpu as pltpu

F32 = jnp.float32
BF16 = jnp.bfloat16

GRID_W = 64
DIFF_HEADS = 4
DIFF_QK = 32
DIFF_V = 64
DIFF_W = DIFF_HEADS * DIFF_V
GQA_HD = 64
GQA_Q_HEADS = 4
GQA_KV_HEADS = 2
GQA_W = GQA_Q_HEADS * GQA_HD
GQA_KV_W = GQA_KV_HEADS * GQA_HD
SSD_HD = 64
SSD_HEADS = 8
SSD_INNER = SSD_HEADS * SSD_HD
SSD_GROUPS = 2
SSD_STATE = 64
SSD_BC_W = SSD_GROUPS * SSD_STATE
SSD_CONV = 5
SSD_CHUNK = 128
XBC_W = SSD_INNER + 2 * SSD_BC_W
DT_W = 2 * SSD_HEADS
ROPE_THETA = 10000.0
EPS = 1e-5
LANES = 128
MXU_TILE = 256
VMEM_LIMIT = 56 * 1024 * 1024

_C_DQ, _C_DK, _C_DV, _C_GQ, _C_GK, _C_GV, _C_Z, _C_XBC, _C_DT, _C_END = (
    0, 256, 512, 768, 1024, 1152, 1280, 1792, 2560, 2576)


def _dot(a, b):
    return jnp.dot(a, b, preferred_element_type=F32)


def _dot_nt(a, b):
    return lax.dot_general(a, b, (((1,), (1,)), ((), ())), preferred_element_type=F32)


def _split3(a):
    a1 = a.astype(BF16)
    r1 = a - a1.astype(F32)
    a2 = r1.astype(BF16)
    a3 = (r1 - a2.astype(F32)).astype(BF16)
    return a1, a2, a3


def _dot3_l(a, b_exact):
    a1, a2, a3 = _split3(a)
    return _dot(a1, b_exact) + (_dot(a2, b_exact) + _dot(a3, b_exact))


def _sigmoid(x):
    return 1.0 / (1.0 + jnp.exp(-x))


def _silu(x):
    return x * _sigmoid(x)


def _layer_norm(x, g, b):
    mu = jnp.mean(x, axis=-1, keepdims=True)
    xc = x - mu
    var = jnp.mean(xc * xc, axis=-1, keepdims=True)
    return xc * lax.rsqrt(var + EPS) * g + b


def _group_avg_matrix(width, group):
    sh = int(math.log2(group))
    r = lax.shift_right_logical(lax.broadcasted_iota(jnp.int32, (width, width), 0), sh)
    c = lax.shift_right_logical(lax.broadcasted_iota(jnp.int32, (width, width), 1), sh)
    return jnp.where(r == c, 1.0 / group, 0.0).astype(BF16)


def _group_mean_sq(x, gmat):
    xx = x * x
    hi = xx.astype(BF16)
    lo = (xx - hi.astype(F32)).astype(BF16)
    return _dot(hi, gmat) + _dot(lo, gmat)


def _rope(x, cos, sin_signed):
    w = x.shape[-1]
    lane = lax.broadcasted_iota(jnp.int32, x.shape, 1)
    nxt = pltpu.roll(x, w - 1, 1)
    prv = pltpu.roll(x, 1, 1)
    partner = jnp.where((lane & 1) == 0, nxt, prv)
    return x * cos + partner * sin_signed


def _lane_mask(shape, lo, hi):
    lane = lax.broadcasted_iota(jnp.int32, shape, 1)
    return (lane >= lo) & (lane < hi)


def _mod_kernel(c_ref, w_ref, b_ref, o_ref):
    @pl.when(pl.program_id(1) == 0)
    def _start_from_bias():
        o_ref[...] = jnp.broadcast_to(b_ref[...], o_ref.shape)

    a = _silu(c_ref[...])
    rows = a.shape[0]
    a_hi = a.astype(BF16)
    a_hi_f = a_hi.astype(F32)
    w = w_ref[...]
    w_hi = w.astype(BF16)
    w_lo = (w - w_hi.astype(F32)).astype(BF16)
    both = _dot(jnp.concatenate([a_hi_f, a - a_hi_f], axis=0).astype(BF16), w_hi)
    o_ref[...] += both[0:rows] + (both[rows:2 * rows] + _dot(a_hi, w_lo))


def _modulation(cvec, w_ada, b_ada):
    depth, d, n = w_ada.shape
    tk = 256
    rows = cvec.shape[0]
    return pl.pallas_call(
        _mod_kernel,
        grid=(depth, d // tk),
        in_specs=[
            pl.BlockSpec((rows, tk), lambda l, k: (0, k)),
            pl.BlockSpec((None, tk, n), lambda l, k: (l, k, 0)),
            pl.BlockSpec((None, 1, n), lambda l, k: (l, 0, 0)),
        ],
        out_specs=pl.BlockSpec((None, rows, n), lambda l, k: (l, 0, 0)),
        out_shape=jax.ShapeDtypeStruct((depth, rows, n), F32),
        compiler_params=pltpu.CompilerParams(
            dimension_semantics=("arbitrary", "arbitrary"), vmem_limit_bytes=VMEM_LIMIT),
    )(cvec, w_ada, b_ada.reshape(depth, 1, n))


def _put_layer_slot(ref, lead, layer, val):
    if len(ref.shape) == val.ndim + len(lead):
        ref[lead if lead else ...] = val
    else:
        for l in range(ref.shape[len(lead)]):
            ref[lead + (l,)] = val if l == layer else jnp.zeros_like(val)


def _inproj_kernel(feature_major_cache, n_cast, n_prev, layer, *refs):
    x_ref, sh_ref, sc_ref, w_ref = refs[:4]
    cast_in = refs[4:4 + n_cast]
    outs = refs[4 + n_cast + n_prev:]
    if n_cast:
        cast_out = outs[-n_cast - 2:-2] if feature_major_cache else outs[-n_cast:]
        outs = outs[:-n_cast - 2] + outs[-2:] if feature_major_cache else outs[:-n_cast]
        for src, dst in zip(cast_in, cast_out):
            dst[...] = src[...].astype(BF16)
    h = (x_ref[...] * (1.0 + sc_ref[...]) + sh_ref[...]).astype(BF16)

    def mm(lo, hi):
        return _dot(h, w_ref[:, lo:hi])

    gkv = mm(_C_GK, _C_Z)
    if feature_major_cache:
        (qa_ref, dk_ref, gk_ref, z_ref, xbc_ref, dt_ref, dkt_ref, dvt_ref, gvt_ref,
         dv_s, gv_s) = outs
        dk_ref[...] = mm(_C_DK, _C_DV)
        dv_s[...] = mm(_C_DV, _C_GQ)
        gv_s[...] = gkv[:, GQA_KV_W:]
        seq = dkt_ref.shape[-1]
        for j in range(dkt_ref.shape[0]):
            rows = slice(j * seq, (j + 1) * seq)
            _put_layer_slot(dkt_ref, (j,), layer, dk_ref[rows, :].T)
            _put_layer_slot(dvt_ref, (j,), layer, dv_s[rows, :].T)
            _put_layer_slot(gvt_ref, (j,), layer, gv_s[rows, :].T)
    else:
        qa_ref, dk_ref, dv_ref, gk_ref, gv_ref, z_ref, xbc_ref, dt_ref = outs
        dk_ref[...] = mm(_C_DK, _C_DV)
        dv_ref[...] = mm(_C_DV, _C_GQ)
        gv_ref[...] = gkv[:, GQA_KV_W:]
    qa_ref[:, 0:DIFF_W] = mm(_C_DQ, _C_DK)
    qa_ref[:, DIFF_W:DIFF_W + GQA_W] = mm(_C_GQ, _C_GK)
    gk_ref[...] = gkv[:, 0:GQA_KV_W]
    z_ref[...] = mm(_C_Z, _C_XBC)
    xbc_ref[...] = mm(_C_XBC, _C_DT)
    dt_ref[...] = jnp.zeros(dt_ref.shape, F32)
    dt_ref[:, 0:DT_W] = mm(_C_DT, _C_END)


def _mod_spec(layer, which, row_fn, d):
    return pl.BlockSpec((None, None, 1, d), lambda i: (layer, row_fn(i), 0, which))


def _inproj(x, mods, layer, row_fn, w_in_bf, tm, cache_prev=None, depth=None, seq=None, cast=()):
    t, d = x.shape
    steps = t // tm
    feature_major = cache_prev is not None
    token_spec = lambda w: pl.BlockSpec((tm, w), lambda i: (i, 0))
    token_shape = lambda w: jax.ShapeDtypeStruct((t, w), F32)
    if feature_major:
        widths = (DIFF_W + GQA_W, DIFF_W, GQA_KV_W, SSD_INNER, XBC_W, LANES)
        cache_w = (DIFF_W, DIFF_W, GQA_KV_W)
        assert tm % seq == 0
        if cache_prev:
            cache_spec = lambda w: pl.BlockSpec((tm // seq, None, w, seq), lambda i: (i, layer, 0, 0))
        else:
            cache_spec = lambda w: pl.BlockSpec((tm // seq, depth, w, seq), lambda i: (i, 0, 0, 0))
        out_specs = [token_spec(w) for w in widths] + [cache_spec(w) for w in cache_w]
        out_shape = [token_shape(w) for w in widths] + [
            jax.ShapeDtypeStruct((t // seq, depth, w, seq), F32) for w in cache_w]
        aliases = {4 + k: len(widths) + k for k in range(len(cache_prev))}
    else:
        widths = (DIFF_W + GQA_W, DIFF_W, DIFF_W, GQA_KV_W, GQA_KV_W, SSD_INNER, XBC_W, LANES)
        out_specs = [token_spec(w) for w in widths]
        out_shape = [token_shape(w) for w in widths]
        cache_prev, aliases = (), {}
    aliases = {k + len(cast): v for k, v in aliases.items()}
    cast_in_specs, cast_out_specs, cast_out_shape = [], [], []
    for w in cast:
        rows = w.shape[1] // steps
        assert w.shape[1] % steps == 0 and rows % 16 == 0
        cast_in_specs.append(pl.BlockSpec((None, rows, w.shape[2]), lambda i: (layer, i, 0)))
        cast_out_specs.append(pl.BlockSpec((rows, w.shape[2]), lambda i: (i, 0)))
        cast_out_shape.append(jax.ShapeDtypeStruct(w.shape[1:], BF16))
    return pl.pallas_call(
        functools.partial(_inproj_kernel, feature_major, len(cast), len(cache_prev), layer),
        grid=(steps,),
        in_specs=[
            pl.BlockSpec((tm, d), lambda i: (i, 0)),
            _mod_spec(layer, 0, row_fn, d),
            _mod_spec(layer, 1, row_fn, d),
            pl.BlockSpec((None, d, _C_END), lambda i: (layer, 0, 0)),
        ] + cast_in_specs + [pl.BlockSpec(memory_space=pl.ANY)] * len(cache_prev),
        out_specs=out_specs + cast_out_specs,
        out_shape=out_shape + cast_out_shape,
        input_output_aliases=aliases,
        scratch_shapes=([pltpu.VMEM((tm, DIFF_W), F32), pltpu.VMEM((tm, GQA_KV_W), F32)]
                        if feature_major else []),
        compiler_params=pltpu.CompilerParams(
            dimension_semantics=("arbitrary",), vmem_limit_bytes=VMEM_LIMIT),
    )(x, mods, mods, w_in_bf, *cast, *cache_prev)


LOG2E = 1.4426950408889634
NEG_BIG = -1e30
N_SCORE_HEADS = 2 * DIFF_HEADS + GQA_Q_HEADS
ONES_ROWS = 16


def _diff_lambda(lam_ref, lam_init):
    lp = lam_ref[...]
    s1 = jnp.sum(lp[0:1, :] * lp[1:2, :], axis=-1, keepdims=True)
    s2 = jnp.sum(lp[2:3, :] * lp[3:4, :], axis=-1, keepdims=True)
    return jnp.exp(s1) - jnp.exp(s2) + lam_init


def _swap_halves(x):
    return pltpu.roll(x, GQA_HD, 1)


def _attn_kernel(cfg, *refs):
    lam_init, seq, past, kb, rope, feature_major, n_prev, layer = cfg
    it = iter(refs)
    qa_ref, dk_ref, dv_ref, gk_in_ref, gv_ref = next(it), next(it), next(it), next(it), next(it)
    for _ in range(n_prev):
        next(it)
    if past:
        cdk_ref, cdv_ref, cgk_ref, cgv_ref = next(it), next(it), next(it), next(it)
    if rope:
        cosq_ref, sinq_ref, cosk_ref, sink_ref = next(it), next(it), next(it), next(it)
    lam_ref, gain_ref, gq_ref, gk_ref = next(it), next(it), next(it), next(it)
    o_ref = next(it)
    gkn_ref = next(it) if feature_major else None
    kd_s, kg_s, vdt_s, vgt_s, wq_s, wg_s, s0_s, s1_s, m_s, l_s, acc_s = it
    tq = qa_ref.shape[0]
    n_blocks = (seq + past) // kb

    @pl.when(pl.program_id(1) == 0)
    def _prepare_keys():
        gmat = _group_avg_matrix(GQA_KV_W, GQA_HD)
        for i in range(seq // kb):
            rows = slice(i * kb, (i + 1) * kb)
            dk = dk_ref[rows, :]
            gk = gk_in_ref[rows, :]
            gk = gk * lax.rsqrt(_group_mean_sq(gk, gmat) + EPS) * gk_ref[...]
            if feature_major:
                gk_t = gk.T
                if len(gkn_ref.shape) == 2:
                    gkn_ref[:, rows] = gk_t
                else:
                    for l in range(gkn_ref.shape[0]):
                        gkn_ref[l, :, rows] = gk_t if l == layer else jnp.zeros_like(gk_t)
            if rope:
                ck = cosk_ref[rows, :]
                sk = sink_ref[rows, :]
                dk = _rope(dk, ck[:, 0:DIFF_W], sk[:, 0:DIFF_W])
                gk = _rope(gk, ck[:, DIFF_W:DIFF_W + GQA_KV_W], sk[:, DIFF_W:DIFF_W + GQA_KV_W])
            kd_s[rows, :] = dk.astype(BF16)
            kg_s[rows, :] = gk.astype(BF16)
            if feature_major:
                vdt_s[i] = dv_ref[:, rows].astype(BF16)
                vgt_s[i] = gv_ref[:, rows].astype(BF16)
            else:
                vdt_s[i] = dv_ref[rows, :].T.astype(BF16)
                vgt_s[i] = gv_ref[rows, :].T.astype(BF16)
        wq_s[...] = jnp.zeros(wq_s.shape, BF16)
        wg_s[...] = jnp.zeros(wg_s.shape, BF16)
        for j in range(past // kb):
            src = slice(j * kb, (j + 1) * kb)
            dst = slice(seq + j * kb, seq + (j + 1) * kb)
            kd_s[dst, :] = cdk_ref[:, src].T.astype(BF16)
            kg_s[dst, :] = cgk_ref[:, src].T.astype(BF16)
            vdt_s[seq // kb + j] = cdv_ref[:, src].astype(BF16)
            vgt_s[seq // kb + j] = cgv_ref[:, src].astype(BF16)

    qa = qa_ref[...]
    qd = qa[:, 0:DIFF_W]
    gq = qa[:, DIFF_W:DIFF_W + GQA_W]
    gq = gq * lax.rsqrt(_group_mean_sq(gq, _group_avg_matrix(GQA_W, GQA_HD)) + EPS) * gq_ref[...]
    if rope:
        cq = cosq_ref[...]
        sq = sinq_ref[...]
        qd = _rope(qd, cq[:, 0:DIFF_W], sq[:, 0:DIFF_W])
        gq = _rope(gq, cq[:, DIFF_W:DIFF_W + GQA_W], sq[:, DIFF_W:DIFF_W + GQA_W])
    qd_t = (qd * (DIFF_QK ** -0.5 * LOG2E)).T
    gq_t = (gq * (GQA_HD ** -0.5 * LOG2E)).T
    for hm in range(2 * DIFF_HEADS):
        band = slice(hm * DIFF_QK, (hm + 1) * DIFF_QK)
        wq_s[band, hm * tq:(hm + 1) * tq] = qd_t[band, :].astype(BF16)
    for h in range(GQA_Q_HEADS):
        kvh = h // (GQA_Q_HEADS // GQA_KV_HEADS)
        wg_s[kvh * GQA_HD:(kvh + 1) * GQA_HD, h * tq:(h + 1) * tq] = (
            gq_t[h * GQA_HD:(h + 1) * GQA_HD, :].astype(BF16))
    m_s[...] = jnp.full(m_s.shape, NEG_BIG, F32)
    l_s[...] = jnp.zeros(l_s.shape, F32)
    acc_s[...] = jnp.zeros(acc_s.shape, F32)
    n_diff = 2 * DIFF_HEADS * tq
    n_all = N_SCORE_HEADS * tq

    def key_rows(j):
        return pl.ds(j * kb if isinstance(j, int) else pl.multiple_of(j * kb, kb), kb)

    n_qt = tq // MXU_TILE
    tile_slabs = MXU_TILE // LANES

    def head_scores(idx, qt, k_d, k_g, s_buf):
        lo = idx * tq + qt * MXU_TILE
        if idx < 2 * DIFF_HEADS:
            s = _dot(k_d, wq_s[:, lo:lo + MXU_TILE])
        else:
            s = _dot(k_g, wg_s[:, lo - n_diff:lo - n_diff + MXU_TILE])
        for k in range(tile_slabs):
            s_buf[lo // LANES + k] = s[:, k * LANES:(k + 1) * LANES]

    def head_update(idx, qt, v_d, v_g, s_buf):
        lo = idx * tq + qt * MXU_TILE
        ps, alphas = [], []
        for k in range(tile_slabs):
            c = lo // LANES + k
            cols = slice(c * LANES, (c + 1) * LANES)
            s = s_buf[c]
            m_old = m_s[:, cols]
            m_new = jnp.maximum(m_old, jnp.max(s, axis=0, keepdims=True))
            alphas.append(jnp.exp2(m_old - m_new))
            ps.append(jnp.exp2(s - m_new).astype(BF16))
            m_s[:, cols] = m_new
        if idx < 2 * DIFF_HEADS:
            vh = idx // 2
            v_t = v_d[vh * DIFF_V:(vh + 1) * DIFF_V, :]
        else:
            vh = (idx - 2 * DIFF_HEADS) // (GQA_Q_HEADS // GQA_KV_HEADS)
            v_t = v_g[vh * GQA_HD:(vh + 1) * GQA_HD, :]
        v_ext = jnp.concatenate([v_t, jnp.ones((ONES_ROWS, kb), BF16)], axis=0)
        alpha = jnp.concatenate(alphas, axis=1)
        pv = _dot(v_ext, jnp.concatenate(ps, axis=1))
        cols = slice(lo, lo + MXU_TILE)
        qcols = slice(qt * MXU_TILE, (qt + 1) * MXU_TILE)
        acc_s[idx, :, qcols] = alpha * acc_s[idx, :, qcols] + pv[0:GQA_HD, :]
        l_s[:, cols] = alpha * l_s[:, cols] + pv[GQA_HD:GQA_HD + 1, :]

    def key_block(j, s_cur, j_next, s_next):
        v_d = vdt_s[j]
        v_g = vgt_s[j]
        if j_next is not None:
            k_d = kd_s[key_rows(j_next), :]
            k_g = kg_s[key_rows(j_next), :]
        for idx in range(N_SCORE_HEADS):
            for qt in range(n_qt):
                if j_next is not None:
                    head_scores(idx, qt, k_d, k_g, s_next)
                head_update(idx, qt, v_d, v_g, s_cur)

    for idx in range(N_SCORE_HEADS):
        for qt in range(n_qt):
            head_scores(idx, qt, kd_s[key_rows(0), :], kg_s[key_rows(0), :], s0_s)
    if n_blocks > 1:
        assert n_blocks % 2 == 0

        def block_pair(i, carry):
            key_block(2 * i, s0_s, 2 * i + 1, s1_s)
            key_block(2 * i + 1, s1_s, 2 * i + 2, s0_s)
            return carry

        lax.fori_loop(0, n_blocks // 2 - 1, block_pair, 0)
        key_block(n_blocks - 2, s0_s, n_blocks - 1, s1_s)
        key_block(n_blocks - 1, s1_s, None, None)
    else:
        key_block(0, s0_s, None, None)

    lam = _diff_lambda(lam_ref, lam_init)
    outs = []
    for h in range(DIFF_HEADS):
        c0 = slice(2 * h * tq, (2 * h + 1) * tq)
        c1 = slice((2 * h + 1) * tq, (2 * h + 2) * tq)
        o = acc_s[2 * h] * (1.0 / l_s[:, c0]) - acc_s[2 * h + 1] * (lam / l_s[:, c1])
        ms = jnp.mean(o * o, axis=0, keepdims=True)
        outs.append(o * (lax.rsqrt(ms + EPS) * (1.0 - lam_init)))
    for h in range(GQA_Q_HEADS):
        idx = 2 * DIFF_HEADS + h
        outs.append(acc_s[idx] * (1.0 / l_s[:, idx * tq:(idx + 1) * tq]))
    o_ref[...] = jnp.concatenate(outs, axis=0).T * gain_ref[...]


def _attention(qa, kv, caches, layer, tables, lam_p, gain, gq_t, gk_t, lam_init, batch, seq, tq,
               gkn_prev=None):
    t = qa.shape[0]
    nq = seq // tq
    feature_major = gkn_prev is not None
    past = caches[0].shape[3] if caches is not None else 0
    kb = min(256, seq)
    assert seq % kb == 0 and past % kb == 0 and seq % tq == 0
    n_blocks = (seq + past) // kb
    n_all = N_SCORE_HEADS * tq
    full = lambda a: pl.BlockSpec(a.shape, lambda b, q: (0,) * a.ndim)
    token_major = lambda a: pl.BlockSpec((seq, a.shape[1]), lambda b, q: (b, 0))
    by_feature = lambda a: pl.BlockSpec((None, None, a.shape[2], seq), lambda b, q: (b, layer, 0, 0))
    dk, dv, gk, gv = kv
    in_specs = [pl.BlockSpec((tq, qa.shape[1]), lambda b, q: (b * nq + q, 0)),
                token_major(dk), by_feature(dv) if feature_major else token_major(dv),
                token_major(gk), by_feature(gv) if feature_major else token_major(gv)]
    args = [qa, dk, dv, gk, gv]
    aliases = {}
    if feature_major:
        assert nq == 1
        in_specs += [pl.BlockSpec(memory_space=pl.ANY)] * len(gkn_prev)
        args += list(gkn_prev)
        aliases = {5 + k: 1 + k for k in range(len(gkn_prev))}
    if caches is not None:
        in_specs += [pl.BlockSpec((None, None, a.shape[2], past), lambda b, q: (b, layer, 0, 0))
                     for a in caches]
        args += list(caches)
    if tables is not None:
        cosq, sinq, cosk, sink = tables
        in_specs += [pl.BlockSpec((tq, cosq.shape[1]), lambda b, q: (q, 0)),
                     pl.BlockSpec((tq, sinq.shape[1]), lambda b, q: (q, 0)),
                     full(cosk), full(sink)]
        args += [cosq, sinq, cosk, sink]
    in_specs += [full(lam_p), full(gain), full(gq_t), full(gk_t)]
    args += [lam_p, gain, gq_t, gk_t]
    out_specs = [pl.BlockSpec((tq, DIFF_W + GQA_W), lambda b, q: (b * nq + q, 0))]
    out_shape = [jax.ShapeDtypeStruct((t, DIFF_W + GQA_W), F32)]
    if feature_major:
        depth = dv.shape[1]
        if gkn_prev:
            out_specs.append(pl.BlockSpec((None, None, GQA_KV_W, seq), lambda b, q: (b, layer, 0, 0)))
        else:
            out_specs.append(pl.BlockSpec((None, depth, GQA_KV_W, seq), lambda b, q: (b, 0, 0, 0)))
        out_shape.append(jax.ShapeDtypeStruct((batch, depth, GQA_KV_W, seq), F32))
    cfg = (lam_init, seq, past, kb, tables is not None, feature_major,
           len(gkn_prev) if feature_major else 0, layer)
    return pl.pallas_call(
        functools.partial(_attn_kernel, cfg),
        grid=(batch, nq),
        in_specs=in_specs,
        out_specs=out_specs,
        out_shape=out_shape,
        input_output_aliases=aliases,
        scratch_shapes=[
            pltpu.VMEM((seq + past, DIFF_W), BF16), pltpu.VMEM((seq + past, GQA_KV_W), BF16),
            pltpu.VMEM((n_blocks, DIFF_W, kb), BF16), pltpu.VMEM((n_blocks, GQA_KV_W, kb), BF16),
            pltpu.VMEM((DIFF_W, 2 * DIFF_HEADS * tq), BF16), pltpu.VMEM((GQA_KV_W, GQA_Q_HEADS * tq), BF16),
            pltpu.VMEM((n_all // LANES, kb, LANES), F32), pltpu.VMEM((n_all // LANES, kb, LANES), F32),
            pltpu.VMEM((1, n_all), F32), pltpu.VMEM((1, n_all), F32),
            pltpu.VMEM((N_SCORE_HEADS, GQA_HD, tq), F32),
        ],
        compiler_params=pltpu.CompilerParams(
            dimension_semantics=("arbitrary", "arbitrary"), vmem_limit_bytes=VMEM_LIMIT),
    )(*args)


def _softplus(x):
    return jnp.maximum(x, 0.0) + jnp.log1p(jnp.exp(-jnp.abs(x)))


def _dot2_l(a, b_exact):
    a1 = a.astype(BF16)
    a2 = (a - a1.astype(F32)).astype(BF16)
    return _dot(a1, b_exact) + _dot(a2, b_exact)


def _ssd_kernel(has_init, n_prev, seq, out_layer, *refs):
    it = iter(refs)
    z_ref, xbc_ref, dt_ref, cw_ref, cb_ref, alog_ref, dtb_ref, dexp_ref, ng_ref = (
        next(it) for _ in range(9))
    sf0_ref, sb0_ref = (next(it), next(it)) if has_init else (None, None)
    for _ in range(n_prev):
        next(it)
    o_ref, sf_ref, sb_ref = next(it), next(it), next(it)
    stf_s, stb_s, inc_s, dec_s, eab_s, cbf_s, exp_s, tri_s, gm_s = it
    L = SSD_CHUNK
    W2 = 2 * SSD_INNER
    nchunks = seq // L
    halo = 8

    ri = lax.broadcasted_iota(jnp.int32, (L, L), 0)
    ci = lax.broadcasted_iota(jnp.int32, (L, L), 1)
    lower = ri >= ci
    upper = ri <= ci
    tri_s[0] = jnp.where(lower, 1.0, 0.0).astype(BF16)
    tri_s[1] = jnp.where(upper, 1.0, 0.0).astype(BF16)
    lane_row = lax.broadcasted_iota(jnp.int32, (1, LANES), 1)
    a_row = jnp.where(lane_row < DT_W, -jnp.exp(alog_ref[...]), 0.0)
    ej = lax.broadcasted_iota(jnp.int32, (LANES, W2), 0)
    eh = lax.shift_right_logical(lax.broadcasted_iota(jnp.int32, (LANES, W2), 1), 6)
    exp_s[...] = jnp.where(ej == eh, 1.0, 0.0).astype(BF16)
    gm_s[...] = jnp.where(lax.shift_right_logical(ej, 6) == (lax.shift_right_logical(eh, 2) & 1), 1.0, 0.0)

    def chunk_rows(c):
        return pl.ds(pl.multiple_of(c * L, L), L)

    def load_state(ref):
        r = lax.broadcasted_iota(jnp.int32, (SSD_STATE, LANES), 0)
        c = lax.broadcasted_iota(jnp.int32, (SSD_STATE, LANES), 1)
        dup = jnp.where((c & (SSD_STATE - 1)) == r, 1.0, 0.0).astype(BF16)
        return _dot3_l(ref[...], dup).T * gm_s[:, 0:SSD_INNER]

    def store_state(st_ref, ref):
        st_t = st_ref[...].T
        _put_layer_slot(ref, (), out_layer, (st_t + _swap_halves(st_t))[:, 0:SSD_STATE])

    stf_s[...] = load_state(sf0_ref) if has_init else jnp.zeros(stf_s.shape, F32)
    stb_s[...] = load_state(sb0_ref) if has_init else jnp.zeros(stb_s.shape, F32)

    def forward_pass(c, carry):
        r0 = c * L
        rows = chunk_rows(c)
        dt_c = _softplus(dt_ref[rows, :] + dtb_ref[...])

        d1, d2, d3 = _split3(dt_c * a_row)
        acs_f = _dot(tri_s[0], d1) + (_dot(tri_s[0], d2) + _dot(tri_s[0], d3))
        acs_b = _dot(tri_s[1], d1) + (_dot(tri_s[1], d2) + _dot(tri_s[1], d3))
        acs = jnp.where(lax.broadcasted_iota(jnp.int32, (L, LANES), 1) < SSD_HEADS, acs_f, acs_b)
        acs_t = acs.T
        expand = exp_s[...]
        dt_e = _dot2_l(dt_c, expand)
        acs_e = _dot3_l(acs, expand)
        edge = jnp.concatenate([acs_e[L - 1:L, 0:SSD_INNER], acs_e[0:1, SSD_INNER:W2]], axis=1)
        eacs = jnp.exp(acs_e)
        cdec = jnp.exp(edge)

        prev = xbc_ref[pl.ds(pl.multiple_of(jnp.maximum(r0 - halo, 0), halo), halo), :]
        nxt = xbc_ref[pl.ds(pl.multiple_of(jnp.minimum(r0 + L, seq - halo), halo), halo), :]
        cur = xbc_ref[rows, :]
        win = jnp.concatenate([jnp.where(c > 0, prev, 0.0), cur,
                               jnp.where(c < nchunks - 1, nxt, 0.0)], axis=0)
        acc = cb_ref[...] + cur * cw_ref[SSD_CONV // 2:SSD_CONV // 2 + 1, :]
        for j in range(SSD_CONV):
            if j != SSD_CONV // 2:
                shifted = pltpu.roll(win, (SSD_CONV // 2 - j) % (L + 2 * halo), 0)[halo:halo + L, :]
                acc = acc + shifted * cw_ref[j:j + 1, :]
        act = _silu(acc)
        x_c = act[:, 0:SSD_INNER]
        b_c = act[:, SSD_INNER:SSD_INNER + SSD_BC_W]
        c_c = act[:, SSD_INNER + SSD_BC_W:XBC_W]
        xd = jnp.concatenate([x_c, x_c], axis=1) * dt_e
        xd_b = xd.astype(BF16)
        xdw = (xd * jnp.exp(edge - acs_e)).astype(BF16)
        b_b = b_c.astype(BF16)
        c_b = c_c.astype(BF16)
        s_new = _dot(b_c.T.astype(BF16), xdw) * gm_s[...]

        st_f = stf_s[...]
        y = _dot(c_b, st_f.astype(BF16)) * eacs[:, 0:SSD_INNER]
        stf_s[...] = st_f * cdec[:, 0:SSD_INNER] + s_new[:, 0:SSD_INNER]
        inc_s[c] = s_new[:, SSD_INNER:W2]
        dec_s[c] = cdec[:, SSD_INNER:W2]
        eab_s[rows, :] = eacs[:, SSD_INNER:W2]
        cbf_s[rows, :] = c_b

        cbs = []
        for g in range(SSD_GROUPS):
            cg = jnp.where(_lane_mask(c_c.shape, g * SSD_STATE, (g + 1) * SSD_STATE), c_c, 0.0)
            cbs.append(_dot_nt(cg.astype(BF16), b_b))
        for direction, causal in ((0, lower), (1, upper)):
            pairs = []
            for g in range(SSD_GROUPS):
                cb = cbs[g]
                for hp in range(2):
                    pair = g * 2 + hp
                    res = []
                    for k in range(2):
                        j = direction * SSD_HEADS + pair * 2 + k
                        diff = acs[:, j:j + 1] - acs_t[j:j + 1, :]
                        dec = jnp.where(causal, jnp.exp(diff), 0.0)
                        sc = (cb * dec).astype(BF16)
                        lo = direction * SSD_INNER + pair * LANES
                        res.append(_dot(sc, xd_b[:, lo:lo + LANES]))
                    pairs.append(jnp.where(_lane_mask(res[0].shape, 0, SSD_HD), res[0], res[1]))
            y = y + jnp.concatenate(pairs, axis=-1)
        o_ref[rows, :] = y + x_c * dexp_ref[...]
        return carry

    lax.fori_loop(0, nchunks, forward_pass, 0)
    store_state(stf_s, sf_ref)

    def backward_pass(i, carry):
        c = nchunks - 1 - i
        rows = chunk_rows(c)
        st_b = stb_s[...]
        y = o_ref[rows, :] + _dot(cbf_s[rows, :], st_b.astype(BF16)) * eab_s[rows, :]
        stb_s[...] = st_b * dec_s[c] + inc_s[c]
        yt = y * _silu(z_ref[rows, :])
        ms = jnp.mean(yt * yt, axis=-1, keepdims=True)
        o_ref[rows, :] = yt * lax.rsqrt(ms + EPS) * ng_ref[...]
        return carry

    lax.fori_loop(0, nchunks, backward_pass, 0, unroll=2)
    store_state(stb_s, sb_ref)


def _ssd(z, xbc, dt, params, init, layer, batch, seq, state_prev=(), depth=1):
    cw, cb, alog, dtb, dexp, ng = params
    t = z.shape[0]
    has_init = init is not None
    nchunks = seq // SSD_CHUNK
    full = lambda a: pl.BlockSpec(a.shape, lambda b: (0,) * a.ndim)
    out_layer = layer if depth > 1 else 0
    if state_prev or depth == 1:
        st_spec = pl.BlockSpec((None, None, SSD_INNER, SSD_STATE), lambda b: (b, out_layer, 0, 0))
    else:
        st_spec = pl.BlockSpec((None, depth, SSD_INNER, SSD_STATE), lambda b: (b, 0, 0, 0))
    in_specs = [
        pl.BlockSpec((seq, SSD_INNER), lambda b: (b, 0)),
        pl.BlockSpec((seq, XBC_W), lambda b: (b, 0)),
        pl.BlockSpec((seq, LANES), lambda b: (b, 0)),
        full(cw), full(cb), full(alog), full(dtb), full(dexp), full(ng),
    ]
    args = [z, xbc, dt, cw, cb, alog, dtb, dexp, ng]
    if has_init:
        init_spec = pl.BlockSpec((None, None, SSD_INNER, SSD_STATE), lambda b: (b, layer, 0, 0))
        in_specs += [init_spec, init_spec]
        args += list(init)
    aliases = {len(args) + k: 1 + k for k in range(len(state_prev))}
    in_specs += [pl.BlockSpec(memory_space=pl.ANY)] * len(state_prev)
    args += list(state_prev)
    st_shape = jax.ShapeDtypeStruct((batch, depth, SSD_INNER, SSD_STATE), F32)
    return pl.pallas_call(
        functools.partial(_ssd_kernel, has_init, len(state_prev), seq, out_layer),
        grid=(batch,),
        in_specs=in_specs,
        out_specs=[pl.BlockSpec((seq, SSD_INNER), lambda b: (b, 0)), st_spec, st_spec],
        out_shape=[jax.ShapeDtypeStruct((t, SSD_INNER), F32), st_shape, st_shape],
        input_output_aliases=aliases,
        scratch_shapes=[
            pltpu.VMEM((LANES, SSD_INNER), F32), pltpu.VMEM((LANES, SSD_INNER), F32),
            pltpu.VMEM((nchunks, LANES, SSD_INNER), F32),
            pltpu.VMEM((nchunks, 1, SSD_INNER), F32),
            pltpu.VMEM((seq, SSD_INNER), F32),
            pltpu.VMEM((seq, SSD_BC_W), BF16),
            pltpu.VMEM((LANES, 2 * SSD_INNER), BF16),
            pltpu.VMEM((2, SSD_CHUNK, SSD_CHUNK), BF16),
            pltpu.VMEM((LANES, 2 * SSD_INNER), F32),
        ],
        compiler_params=pltpu.CompilerParams(
            dimension_semantics=("arbitrary",), vmem_limit_bytes=VMEM_LIMIT),
    )(*args)


def _post_kernel(alpha, d_ff, ff_chunk, x_ref, oa_ref, os_ref, g1_ref, sh2_ref, sc2_ref, g2_ref,
                 wo_ref, wfi_ref, wfo_ref, lng_ref, lnb_ref, y_ref):
    wa = oa_ref.shape[1]
    half = x_ref.shape[0] // 2
    bounds = list(range(0, d_ff, ff_chunk)) + [d_ff]
    n_chunks = len(bounds) - 1

    def mix_and_norm(r):
        o = (_dot(oa_ref[r, :].astype(BF16), wo_ref[0:wa, :])
             + _dot(os_ref[r, :].astype(BF16), wo_ref[wa:, :]))
        x1 = _layer_norm(alpha * x_ref[r, :] + g1_ref[...] * o, lng_ref[0:1, :], lnb_ref[0:1, :])
        return x1, (x1 * (1.0 + sc2_ref[...]) + sh2_ref[...]).astype(BF16)

    def up(h2, c):
        lo, hi = bounds[c], bounds[c + 1]
        return _dot(h2, wfi_ref[:, lo:hi]), _dot(h2, wfi_ref[:, d_ff + lo:d_ff + hi])

    def ffn_chunks(h2):
        f = jnp.zeros((half, x_ref.shape[1]), F32)
        g, u = up(h2, 0)
        for c in range(n_chunks):
            nxt = up(h2, c + 1) if c + 1 < n_chunks else None
            f = f + _dot((_silu(g) * u).astype(BF16), wfo_ref[bounds[c]:bounds[c + 1], :])
            if nxt is not None:
                g, u = nxt
            yield f

    def finish(r, x1, f):
        y_ref[r, :] = _layer_norm(alpha * x1 + g2_ref[...] * f, lng_ref[1:2, :], lnb_ref[1:2, :])

    ra, rb = slice(0, half), slice(half, 2 * half)
    x1a, h2a = mix_and_norm(ra)
    ffn_a = ffn_chunks(h2a)
    for _ in range(n_chunks // 2):
        fa = next(ffn_a)
    x1b, h2b = mix_and_norm(rb)
    for fa in ffn_a:
        pass
    ffn_b = ffn_chunks(h2b)
    for _ in range(n_chunks // 2):
        fb = next(ffn_b)
    finish(ra, x1a, fa)
    for fb in ffn_b:
        pass
    finish(rb, x1b, fb)


def _post(x, oa, os_, mods, layer, row_fn, wo_bf, wfi_bf, wfo_bf, lng, lnb, alpha, tm):
    t, d = x.shape
    d_ff = wfo_bf.shape[0]
    ff_chunk = 3 * MXU_TILE
    assert d_ff % MXU_TILE == 0
    per_layer = lambda a: pl.BlockSpec((None,) + a.shape[1:], lambda i: (layer, 0, 0))
    resident = lambda a: pl.BlockSpec(a.shape, lambda i: (0, 0), pipeline_mode=pl.Buffered(1))
    return pl.pallas_call(
        functools.partial(_post_kernel, alpha, d_ff, ff_chunk),
        grid=(t // tm,),
        in_specs=[
            pl.BlockSpec((tm, d), lambda i: (i, 0)),
            pl.BlockSpec((tm, oa.shape[1]), lambda i: (i, 0)),
            pl.BlockSpec((tm, os_.shape[1]), lambda i: (i, 0)),
            _mod_spec(layer, 2, row_fn, d),
            _mod_spec(layer, 3, row_fn, d),
            _mod_spec(layer, 4, row_fn, d),
            _mod_spec(layer, 5, row_fn, d),
            resident(wo_bf), resident(wfi_bf), resident(wfo_bf),
            per_layer(lng), per_layer(lnb),
        ],
        out_specs=pl.BlockSpec((tm, d), lambda i: (i, 0)),
        out_shape=jax.ShapeDtypeStruct((t, d), F32),
        compiler_params=pltpu.CompilerParams(
            dimension_semantics=("arbitrary",), vmem_limit_bytes=VMEM_LIMIT),
    )(x, oa, os_, mods, mods, mods, mods, wo_bf, wfi_bf, wfo_bf, lng, lnb)


def _rope_tables(rows, dim, copies):
    row = jnp.repeat(jnp.arange(rows), GRID_W).astype(F32)
    col = jnp.tile(jnp.arange(GRID_W), rows).astype(F32)
    n_freq = dim // 4
    inv = ROPE_THETA ** (-jnp.arange(n_freq, dtype=F32) / n_freq)
    ang = jnp.concatenate([row[:, None] * inv, col[:, None] * inv], -1)
    cos, sin = jnp.cos(ang), jnp.sin(ang)
    cos_full = jnp.repeat(cos, 2, axis=-1)
    sin_signed = jnp.stack([-sin, sin], axis=-1).reshape(sin.shape[0], dim)
    return jnp.tile(cos_full, (1, copies)), jnp.tile(sin_signed, (1, copies))


def kernel(x_prompt, x_sample, cache_diff_k, cache_diff_v, cache_gqa_k, cache_gqa_v, state_ssd_fwd, state_ssd_bwd, c, c_ctx, w_ada, b_ada, w_in, w_out, diff_lambda, diff_subln_g, qk_norm_g, ssd_conv_w, ssd_conv_b, ssd_A_log, ssd_dt_bias, ssd_D, ssd_norm_g, ln_g, ln_b, w_ffn_in, w_ffn_out):
    batch, seq, d = x_prompt.shape
    dec_batch, dec_seq, _ = x_sample.shape
    depth = w_in.shape[0]
    past = cache_diff_k.shape[2]
    alpha = (2 * depth) ** 0.25
    rows = dec_seq // GRID_W

    n_vec = 1 + dec_batch
    n_pad = -(-n_vec // 8) * 8
    cvec = jnp.concatenate([c_ctx[None, :], c, jnp.zeros((n_pad - n_vec, d), F32)], axis=0)
    mods = _modulation(cvec, w_ada, b_ada).reshape(depth, n_pad, 1, 6 * d)

    cos_d, sin_d = _rope_tables(rows, DIFF_QK, DIFF_W // DIFF_QK)
    cos_g, sin_g = _rope_tables(rows, GQA_HD, GQA_Q_HEADS)
    cosq = jnp.concatenate([cos_d, cos_g], axis=-1)
    sinq = jnp.concatenate([sin_d, sin_g], axis=-1)
    cosk = jnp.concatenate([cos_d, cos_g[:, :GQA_KV_W]], axis=-1)
    sink = jnp.concatenate([sin_d, sin_g[:, :GQA_KV_W]], axis=-1)
    tables = (cosq, sinq, cosk, sink)

    feature_major = lambda a, w: jnp.transpose(a.reshape(dec_batch, depth, past, w), (0, 1, 3, 2))
    caches = (feature_major(cache_diff_k, DIFF_W), feature_major(cache_diff_v, DIFF_W),
              feature_major(cache_gqa_k, GQA_KV_W), feature_major(cache_gqa_v, GQA_KV_W))

    xp = x_prompt.reshape(batch * seq, d)
    xs = x_sample.reshape(dec_batch * dec_seq, d)
    tm_ctx = min(512, batch * seq)
    tm_lat = min(512, dec_seq)
    tm_post_ctx = min(1024, batch * seq)
    tm_post_lat = min(1024, dec_seq)
    tq = min(512, dec_seq)
    ctx_row = lambda i: 0
    lat_row = lambda i: 1 + (i * tm_lat) // dec_seq
    lat_row_post = lambda i: 1 + (i * tm_post_lat) // dec_seq

    assert w_in.shape[2] == _C_END
    w_in_bf = w_in.astype(BF16)
    init = (state_ssd_fwd.reshape(dec_batch, depth, SSD_INNER, SSD_STATE),
            state_ssd_bwd.reshape(dec_batch, depth, SSD_INNER, SSD_STATE))

    kv_cache = ()
    gk_cache = ()
    ssd_states = ()
    for l in range(depth):
        lam_init = 0.8 - 0.6 * math.exp(-0.3 * l)
        lam_p = diff_lambda[l]
        attn_gain = jnp.concatenate([jnp.tile(diff_subln_g[l], DIFF_HEADS), jnp.ones((GQA_W,), F32)])[None, :]
        gq_t = jnp.tile(qk_norm_g[l, 0], GQA_Q_HEADS)[None, :]
        gk_t = jnp.tile(qk_norm_g[l, 1], GQA_KV_HEADS)[None, :]
        pad_row = lambda v: jnp.pad(v.reshape(1, -1), ((0, 0), (0, LANES - v.size)))
        ssd_params = (ssd_conv_w[l], ssd_conv_b[l][None, :], pad_row(ssd_A_log[l]),
                      pad_row(ssd_dt_bias[l]), jnp.repeat(ssd_D[l], SSD_HD)[None, :],
                      ssd_norm_g[l][None, :])

        qa, dk, gk, z, xbc, dt, *rest = _inproj(xp, mods, l, ctx_row, w_in_bf, tm_ctx,
                                                cache_prev=tuple(kv_cache), depth=depth, seq=seq,
                                                cast=(w_out, w_ffn_in, w_ffn_out))
        kv_cache, (wo_bf, wfi_bf, wfo_bf) = rest[:3], rest[3:]
        oa, gkn = _attention(qa, (dk, kv_cache[1], gk, kv_cache[2]), None, l, None, lam_p, attn_gain,
                             gq_t, gk_t, lam_init, batch, seq, seq, gkn_prev=gk_cache)
        gk_cache = (gkn,)
        os_, *ssd_states = _ssd(z, xbc, dt, ssd_params, None, l, batch, seq,
                                state_prev=tuple(ssd_states), depth=depth)
        xp = _post(xp, oa, os_, mods, l, ctx_row, wo_bf, wfi_bf, wfo_bf, ln_g, ln_b, alpha, tm_post_ctx)

        qa, dk, dv, gk, gv, z, xbc, dt = _inproj(xs, mods, l, lat_row, w_in_bf, tm_lat)
        (oa,) = _attention(qa, (dk, dv, gk, gv), caches, l, tables, lam_p, attn_gain, gq_t, gk_t,
                           lam_init, dec_batch, dec_seq, tq)
        os_, _, _ = _ssd(z, xbc, dt, ssd_params, init, l, dec_batch, dec_seq)
        xs = _post(xs, oa, os_, mods, l, lat_row_post, wo_bf, wfi_bf, wfo_bf, ln_g, ln_b, alpha,
                   tm_post_lat)

    def token_major(a, heads, width):
        return jnp.transpose(a.reshape(batch, depth, heads, width, seq), (0, 1, 4, 2, 3))

    state = lambda a: a.reshape(batch, depth, SSD_HEADS, SSD_HD, SSD_STATE)
    return (xp.reshape(batch, seq, d), xs.reshape(dec_batch, dec_seq, d),
            token_major(kv_cache[0], DIFF_HEADS, 2 * DIFF_QK), token_major(kv_cache[1], DIFF_HEADS, DIFF_V),
            token_major(gk_cache[0], GQA_KV_HEADS, GQA_HD), token_major(kv_cache[2], GQA_KV_HEADS, GQA_HD),
            state(ssd_states[0]), state(ssd_states[1]))
```

```python
import functools
import math

import jax
import jax.numpy as jnp
from jax import lax
from jax.experimental import pallas as pl
from jax.experimental.pallas import tpu as pltpu

F32 = jnp.float32
BF16 = jnp.bfloat16

GRID_W = 64
DIFF_HEADS = 4
DIFF_QK = 32
DIFF_V = 64
DIFF_W = DIFF_HEADS * DIFF_V
GQA_HD = 64
GQA_Q_HEADS = 4
GQA_KV_HEADS = 2
GQA_W = GQA_Q_HEADS * GQA_HD
GQA_KV_W = GQA_KV_HEADS * GQA_HD
SSD_HD = 64
SSD_HEADS = 8
SSD_INNER = SSD_HEADS * SSD_HD
SSD_GROUPS = 2
SSD_STATE = 64
SSD_BC_W = SSD_GROUPS * SSD_STATE
SSD_CONV = 5
SSD_CHUNK = 128
XBC_W = SSD_INNER + 2 * SSD_BC_W
DT_W = 2 * SSD_HEADS
ROPE_THETA = 10000.0
EPS = 1e-5
LANES = 128
MXU_TILE = 256
VMEM_LIMIT = 56 * 1024 * 1024

_C_DQ, _C_DK, _C_DV, _C_GQ, _C_GK, _C_GV, _C_Z, _C_XBC, _C_DT, _C_END = (
    0, 256, 512, 768, 1024, 1152, 1280, 1792, 2560, 2576)


def _dot(a, b):
    return jnp.dot(a, b, preferred_element_type=F32)


def _dot_nt(a, b):
    return lax.dot_general(a, b, (((1,), (1,)), ((), ())), preferred_element_type=F32)


def _split3(a):
    a1 = a.astype(BF16)
    r1 = a - a1.astype(F32)
    a2 = r1.astype(BF16)
    a3 = (r1 - a2.astype(F32)).astype(BF16)
    return a1, a2, a3


def _dot3_l(a, b_exact):
    a1, a2, a3 = _split3(a)
    return _dot(a1, b_exact) + (_dot(a2, b_exact) + _dot(a3, b_exact))


def _sigmoid(x):
    return 1.0 / (1.0 + jnp.exp(-x))


def _silu(x):
    return x * _sigmoid(x)


def _layer_norm(x, g, b):
    mu = jnp.mean(x, axis=-1, keepdims=True)
    xc = x - mu
    var = jnp.mean(xc * xc, axis=-1, keepdims=True)
    return xc * lax.rsqrt(var + EPS) * g + b


def _group_avg_matrix(width, group):
    sh = int(math.log2(group))
    r = lax.shift_right_logical(lax.broadcasted_iota(jnp.int32, (width, width), 0), sh)
    c = lax.shift_right_logical(lax.broadcasted_iota(jnp.int32, (width, width), 1), sh)
    return jnp.where(r == c, 1.0 / group, 0.0).astype(BF16)


def _group_mean_sq(x, gmat):
    xx = x * x
    hi = xx.astype(BF16)
    lo = (xx - hi.astype(F32)).astype(BF16)
    return _dot(hi, gmat) + _dot(lo, gmat)


def _rope(x, cos, sin_signed):
    w = x.shape[-1]
    lane = lax.broadcasted_iota(jnp.int32, x.shape, 1)
    nxt = pltpu.roll(x, w - 1, 1)
    prv = pltpu.roll(x, 1, 1)
    partner = jnp.where((lane & 1) == 0, nxt, prv)
    return x * cos + partner * sin_signed


def _lane_mask(shape, lo, hi):
    lane = lax.broadcasted_iota(jnp.int32, shape, 1)
    return (lane >= lo) & (lane < hi)


def _mod_kernel(c_ref, w_ref, b_ref, o_ref):
    @pl.when(pl.program_id(1) == 0)
    def _start_from_bias():
        o_ref[...] = jnp.broadcast_to(b_ref[...], o_ref.shape)

    a = _silu(c_ref[...])
    rows = a.shape[0]
    a_hi = a.astype(BF16)
    a_hi_f = a_hi.astype(F32)
    w = w_ref[...]
    w_hi = w.astype(BF16)
    w_lo = (w - w_hi.astype(F32)).astype(BF16)
    both = _dot(jnp.concatenate([a_hi_f, a - a_hi_f], axis=0).astype(BF16), w_hi)
    o_ref[...] += both[0:rows] + (both[rows:2 * rows] + _dot(a_hi, w_lo))


def _modulation(cvec, w_ada, b_ada):
    depth, d, n = w_ada.shape
    tk = 256
    rows = cvec.shape[0]
    return pl.pallas_call(
        _mod_kernel,
        grid=(depth, d // tk),
        in_specs=[
            pl.BlockSpec((rows, tk), lambda l, k: (0, k)),
            pl.BlockSpec((None, tk, n), lambda l, k: (l, k, 0)),
            pl.BlockSpec((None, 1, n), lambda l, k: (l, 0, 0)),
        ],
        out_specs=pl.BlockSpec((None, rows, n), lambda l, k: (l, 0, 0)),
        out_shape=jax.ShapeDtypeStruct((depth, rows, n), F32),
        compiler_params=pltpu.CompilerParams(
            dimension_semantics=("arbitrary", "arbitrary"), vmem_limit_bytes=VMEM_LIMIT),
    )(cvec, w_ada, b_ada.reshape(depth, 1, n))


def _put_layer_slot(ref, lead, layer, val):
    if len(ref.shape) == val.ndim + len(lead):
        ref[lead if lead else ...] = val
    else:
        for l in range(ref.shape[len(lead)]):
            ref[lead + (l,)] = val if l == layer else jnp.zeros_like(val)


def _inproj_kernel(feature_major_cache, n_cast, n_prev, layer, *refs):
    x_ref, sh_ref, sc_ref, w_ref = refs[:4]
    cast_in = refs[4:4 + n_cast]
    outs = refs[4 + n_cast + n_prev:]
    if n_cast:
        cast_out = outs[-n_cast - 2:-2] if feature_major_cache else outs[-n_cast:]
        outs = outs[:-n_cast - 2] + outs[-2:] if feature_major_cache else outs[:-n_cast]
        for src, dst in zip(cast_in, cast_out):
            dst[...] = src[...].astype(BF16)
    h = (x_ref[...] * (1.0 + sc_ref[...]) + sh_ref[...]).astype(BF16)

    def mm(lo, hi):
        return _dot(h, w_ref[:, lo:hi])

    gkv = mm(_C_GK, _C_Z)
    if feature_major_cache:
        (qa_ref, dk_ref, gk_ref, z_ref, xbc_ref, dt_ref, dkt_ref, dvt_ref, gvt_ref,
         dv_s, gv_s) = outs
        dk_ref[...] = mm(_C_DK, _C_DV)
        dv_s[...] = mm(_C_DV, _C_GQ)
        gv_s[...] = gkv[:, GQA_KV_W:]
        seq = dkt_ref.shape[-1]
        for j in range(dkt_ref.shape[0]):
            rows = slice(j * seq, (j + 1) * seq)
            _put_layer_slot(dkt_ref, (j,), layer, dk_ref[rows, :].T)
            _put_layer_slot(dvt_ref, (j,), layer, dv_s[rows, :].T)
            _put_layer_slot(gvt_ref, (j,), layer, gv_s[rows, :].T)
    else:
        qa_ref, dk_ref, dv_ref, gk_ref, gv_ref, z_ref, xbc_ref, dt_ref = outs
        dk_ref[...] = mm(_C_DK, _C_DV)
        dv_ref[...] = mm(_C_DV, _C_GQ)
        gv_ref[...] = gkv[:, GQA_KV_W:]
    qa_ref[:, 0:DIFF_W] = mm(_C_DQ, _C_DK)
    qa_ref[:, DIFF_W:DIFF_W + GQA_W] = mm(_C_GQ, _C_GK)
    gk_ref[...] = gkv[:, 0:GQA_KV_W]
    z_ref[...] = mm(_C_Z, _C_XBC)
    xbc_ref[...] = mm(_C_XBC, _C_DT)
    dt_ref[...] = jnp.zeros(dt_ref.shape, F32)
    dt_ref[:, 0:DT_W] = mm(_C_DT, _C_END)


def _cast_specs(cast, layer, steps):
    in_specs, out_specs, out_shape = [], [], []
    for w in cast:
        hold = 1
        while (w.shape[1] * hold) % steps or (w.shape[1] * hold // steps) % 16:
            hold *= 2
        rows = w.shape[1] * hold // steps
        in_specs.append(pl.BlockSpec((None, rows, w.shape[2]), lambda i, hold=hold: (layer, i // hold, 0)))
        out_specs.append(pl.BlockSpec((rows, w.shape[2]), lambda i, hold=hold: (i // hold, 0)))
        out_shape.append(jax.ShapeDtypeStruct(w.shape[1:], BF16))
    return in_specs, out_specs, out_shape


def _mod_spec(layer, which, row_fn, d):
    return pl.BlockSpec((None, None, 1, d), lambda i: (layer, row_fn(i), 0, which))


def _inproj(x, mods, layer, row_fn, w_in_bf, tm, cache_prev=None, depth=None, seq=None, cast=()):
    t, d = x.shape
    steps = t // tm
    feature_major = cache_prev is not None
    token_spec = lambda w: pl.BlockSpec((tm, w), lambda i: (i, 0))
    token_shape = lambda w: jax.ShapeDtypeStruct((t, w), F32)
    if feature_major:
        widths = (DIFF_W + GQA_W, DIFF_W, GQA_KV_W, SSD_INNER, XBC_W, LANES)
        cache_w = (DIFF_W, DIFF_W, GQA_KV_W)
        assert tm % seq == 0
        if cache_prev:
            cache_spec = lambda w: pl.BlockSpec((tm // seq, None, w, seq), lambda i: (i, layer, 0, 0))
        else:
            cache_spec = lambda w: pl.BlockSpec((tm // seq, depth, w, seq), lambda i: (i, 0, 0, 0))
        out_specs = [token_spec(w) for w in widths] + [cache_spec(w) for w in cache_w]
        out_shape = [token_shape(w) for w in widths] + [
            jax.ShapeDtypeStruct((t // seq, depth, w, seq), F32) for w in cache_w]
        aliases = {4 + k: len(widths) + k for k in range(len(cache_prev))}
    else:
        widths = (DIFF_W + GQA_W, DIFF_W, DIFF_W, GQA_KV_W, GQA_KV_W, SSD_INNER, XBC_W, LANES)
        out_specs = [token_spec(w) for w in widths]
        out_shape = [token_shape(w) for w in widths]
        cache_prev, aliases = (), {}
    aliases = {k + len(cast): v for k, v in aliases.items()}
    cast_in_specs, cast_out_specs, cast_out_shape = _cast_specs(cast, layer, steps)
    return pl.pallas_call(
        functools.partial(_inproj_kernel, feature_major, len(cast), len(cache_prev), layer),
        grid=(steps,),
        in_specs=[
            pl.BlockSpec((tm, d), lambda i: (i, 0)),
            _mod_spec(layer, 0, row_fn, d),
            _mod_spec(layer, 1, row_fn, d),
            pl.BlockSpec((None, d, _C_END), lambda i: (layer, 0, 0)),
        ] + cast_in_specs + [pl.BlockSpec(memory_space=pl.ANY)] * len(cache_prev),
        out_specs=out_specs + cast_out_specs,
        out_shape=out_shape + cast_out_shape,
        input_output_aliases=aliases,
        scratch_shapes=([pltpu.VMEM((tm, DIFF_W), F32), pltpu.VMEM((tm, GQA_KV_W), F32)]
                        if feature_major else []),
        compiler_params=pltpu.CompilerParams(
            dimension_semantics=("arbitrary",), vmem_limit_bytes=VMEM_LIMIT),
    )(x, mods, mods, w_in_bf, *cast, *cache_prev)


LOG2E = 1.4426950408889634
NEG_BIG = -1e30
N_SCORE_HEADS = 2 * DIFF_HEADS + GQA_Q_HEADS
ONES_ROWS = 16


def _diff_lambda(lam_ref, lam_init):
    lp = lam_ref[...]
    s1 = jnp.sum(lp[0:1, :] * lp[1:2, :], axis=-1, keepdims=True)
    s2 = jnp.sum(lp[2:3, :] * lp[3:4, :], axis=-1, keepdims=True)
    return jnp.exp(s1) - jnp.exp(s2) + lam_init


def _swap_halves(x):
    return pltpu.roll(x, GQA_HD, 1)


def _attn_kernel(cfg, *refs):
    lam_init, seq, past, kb, rope, feature_major, n_prev, layer = cfg
    it = iter(refs)
    qa_ref, dk_ref, dv_ref, gk_in_ref, gv_ref = next(it), next(it), next(it), next(it), next(it)
    for _ in range(n_prev):
        next(it)
    if past:
        cdk_ref, cdv_ref, cgk_ref, cgv_ref = next(it), next(it), next(it), next(it)
    if rope:
        cosq_ref, sinq_ref, cosk_ref, sink_ref = next(it), next(it), next(it), next(it)
    lam_ref, gain_ref, gq_ref, gk_ref = next(it), next(it), next(it), next(it)
    o_ref = next(it)
    gkn_ref = next(it) if feature_major else None
    kd_s, kg_s, vdt_s, vgt_s, wq_s, wg_s, s0_s, s1_s, m_s, l_s, acc_s = it
    tq = qa_ref.shape[0]
    n_blocks = (seq + past) // kb

    @pl.when(pl.program_id(1) == 0)
    def _prepare_keys():
        gmat = _group_avg_matrix(GQA_KV_W, GQA_HD)
        for i in range(seq // kb):
            rows = slice(i * kb, (i + 1) * kb)
            dk = dk_ref[rows, :]
            gk = gk_in_ref[rows, :]
            gk = gk * lax.rsqrt(_group_mean_sq(gk, gmat) + EPS) * gk_ref[...]
            if feature_major:
                gk_t = gk.T
                if len(gkn_ref.shape) == 2:
                    gkn_ref[:, rows] = gk_t
                else:
                    for l in range(gkn_ref.shape[0]):
                        gkn_ref[l, :, rows] = gk_t if l == layer else jnp.zeros_like(gk_t)
            if rope:
                ck = cosk_ref[rows, :]
                sk = sink_ref[rows, :]
                dk = _rope(dk, ck[:, 0:DIFF_W], sk[:, 0:DIFF_W])
                gk = _rope(gk, ck[:, DIFF_W:DIFF_W + GQA_KV_W], sk[:, DIFF_W:DIFF_W + GQA_KV_W])
            kd_s[rows, :] = dk.astype(BF16)
            kg_s[rows, :] = gk.astype(BF16)
            if feature_major:
                vdt_s[i] = dv_ref[:, rows].astype(BF16)
                vgt_s[i] = gv_ref[:, rows].astype(BF16)
            else:
                vdt_s[i] = dv_ref[rows, :].T.astype(BF16)
                vgt_s[i] = gv_ref[rows, :].T.astype(BF16)
        wq_s[...] = jnp.zeros(wq_s.shape, BF16)
        wg_s[...] = jnp.zeros(wg_s.shape, BF16)
        for j in range(past // kb):
            src = slice(j * kb, (j + 1) * kb)
            dst = slice(seq + j * kb, seq + (j + 1) * kb)
            kd_s[dst, :] = cdk_ref[:, src].T.astype(BF16)
            kg_s[dst, :] = cgk_ref[:, src].T.astype(BF16)
            vdt_s[seq // kb + j] = cdv_ref[:, src].astype(BF16)
            vgt_s[seq // kb + j] = cgv_ref[:, src].astype(BF16)

    qa = qa_ref[...]
    qd = qa[:, 0:DIFF_W]
    gq = qa[:, DIFF_W:DIFF_W + GQA_W]
    gq = gq * lax.rsqrt(_group_mean_sq(gq, _group_avg_matrix(GQA_W, GQA_HD)) + EPS) * gq_ref[...]
    if rope:
        cq = cosq_ref[...]
        sq = sinq_ref[...]
        qd = _rope(qd, cq[:, 0:DIFF_W], sq[:, 0:DIFF_W])
        gq = _rope(gq, cq[:, DIFF_W:DIFF_W + GQA_W], sq[:, DIFF_W:DIFF_W + GQA_W])
    qd_t = (qd * (DIFF_QK ** -0.5 * LOG2E)).T
    gq_t = (gq * (GQA_HD ** -0.5 * LOG2E)).T
    for hm in range(2 * DIFF_HEADS):
        band = slice(hm * DIFF_QK, (hm + 1) * DIFF_QK)
        wq_s[band, hm * tq:(hm + 1) * tq] = qd_t[band, :].astype(BF16)
    for h in range(GQA_Q_HEADS):
        kvh = h // (GQA_Q_HEADS // GQA_KV_HEADS)
        wg_s[kvh * GQA_HD:(kvh + 1) * GQA_HD, h * tq:(h + 1) * tq] = (
            gq_t[h * GQA_HD:(h + 1) * GQA_HD, :].astype(BF16))
    m_s[...] = jnp.full(m_s.shape, NEG_BIG, F32)
    l_s[...] = jnp.zeros(l_s.shape, F32)
    acc_s[...] = jnp.zeros(acc_s.shape, F32)
    n_diff = 2 * DIFF_HEADS * tq
    n_all = N_SCORE_HEADS * tq

    def key_rows(j):
        return pl.ds(j * kb if isinstance(j, int) else pl.multiple_of(j * kb, kb), kb)

    n_qt = tq // MXU_TILE
    tile_slabs = MXU_TILE // LANES

    def head_scores(idx, qt, k_d, k_g, s_buf):
        lo = idx * tq + qt * MXU_TILE
        if idx < 2 * DIFF_HEADS:
            s = _dot(k_d, wq_s[:, lo:lo + MXU_TILE])
        else:
            s = _dot(k_g, wg_s[:, lo - n_diff:lo - n_diff + MXU_TILE])
        for k in range(tile_slabs):
            s_buf[lo // LANES + k] = s[:, k * LANES:(k + 1) * LANES]

    def head_update(idx, qt, v_d, v_g, s_buf):
        lo = idx * tq + qt * MXU_TILE
        ps, alphas = [], []
        for k in range(tile_slabs):
            c = lo // LANES + k
            cols = slice(c * LANES, (c + 1) * LANES)
            s = s_buf[c]
            m_old = m_s[:, cols]
            m_new = jnp.maximum(m_old, jnp.max(s, axis=0, keepdims=True))
            alphas.append(jnp.exp2(m_old - m_new))
            ps.append(jnp.exp2(s - m_new).astype(BF16))
            m_s[:, cols] = m_new
        if idx < 2 * DIFF_HEADS:
            vh = idx // 2
            v_t = v_d[vh * DIFF_V:(vh + 1) * DIFF_V, :]
        else:
            vh = (idx - 2 * DIFF_HEADS) // (GQA_Q_HEADS // GQA_KV_HEADS)
            v_t = v_g[vh * GQA_HD:(vh + 1) * GQA_HD, :]
        v_ext = jnp.concatenate([v_t, jnp.ones((ONES_ROWS, kb), BF16)], axis=0)
        alpha = jnp.concatenate(alphas, axis=1)
        pv = _dot(v_ext, jnp.concatenate(ps, axis=1))
        cols = slice(lo, lo + MXU_TILE)
        qcols = slice(qt * MXU_TILE, (qt + 1) * MXU_TILE)
        acc_s[idx, :, qcols] = alpha * acc_s[idx, :, qcols] + pv[0:GQA_HD, :]
        l_s[:, cols] = alpha * l_s[:, cols] + pv[GQA_HD:GQA_HD + 1, :]

    def key_block(j, s_cur, j_next, s_next):
        v_d = vdt_s[j]
        v_g = vgt_s[j]
        if j_next is not None:
            k_d = kd_s[key_rows(j_next), :]
            k_g = kg_s[key_rows(j_next), :]
        for idx in range(N_SCORE_HEADS):
            for qt in range(n_qt):
                if j_next is not None:
                    head_scores(idx, qt, k_d, k_g, s_next)
                head_update(idx, qt, v_d, v_g, s_cur)

    for idx in range(N_SCORE_HEADS):
        for qt in range(n_qt):
            head_scores(idx, qt, kd_s[key_rows(0), :], kg_s[key_rows(0), :], s0_s)
    if n_blocks > 1:
        assert n_blocks % 2 == 0

        def block_pair(i, carry):
            key_block(2 * i, s0_s, 2 * i + 1, s1_s)
            key_block(2 * i + 1, s1_s, 2 * i + 2, s0_s)
            return carry

        lax.fori_loop(0, n_blocks // 2 - 1, block_pair, 0)
        key_block(n_blocks - 2, s0_s, n_blocks - 1, s1_s)
        key_block(n_blocks - 1, s1_s, None, None)
    else:
        key_block(0, s0_s, None, None)

    lam = _diff_lambda(lam_ref, lam_init)
    outs = []
    for h in range(DIFF_HEADS):
        c0 = slice(2 * h * tq, (2 * h + 1) * tq)
        c1 = slice((2 * h + 1) * tq, (2 * h + 2) * tq)
        o = acc_s[2 * h] * (1.0 / l_s[:, c0]) - acc_s[2 * h + 1] * (lam / l_s[:, c1])
        ms = jnp.mean(o * o, axis=0, keepdims=True)
        outs.append(o * (lax.rsqrt(ms + EPS) * (1.0 - lam_init)))
    for h in range(GQA_Q_HEADS):
        idx = 2 * DIFF_HEADS + h
        outs.append(acc_s[idx] * (1.0 / l_s[:, idx * tq:(idx + 1) * tq]))
    o_ref[...] = jnp.concatenate(outs, axis=0).T * gain_ref[...]


def _attention(qa, kv, caches, layer, tables, lam_p, gain, gq_t, gk_t, lam_init, batch, seq, tq,
               gkn_prev=None):
    t = qa.shape[0]
    nq = seq // tq
    feature_major = gkn_prev is not None
    past = caches[0].shape[3] if caches is not None else 0
    kb = min(256, seq)
    assert seq % kb == 0 and past % kb == 0 and seq % tq == 0
    n_blocks = (seq + past) // kb
    n_all = N_SCORE_HEADS * tq
    full = lambda a: pl.BlockSpec(a.shape, lambda b, q: (0,) * a.ndim)
    token_major = lambda a: pl.BlockSpec((seq, a.shape[1]), lambda b, q: (b, 0))
    by_feature = lambda a: pl.BlockSpec((None, None, a.shape[2], seq), lambda b, q: (b, layer, 0, 0))
    dk, dv, gk, gv = kv
    in_specs = [pl.BlockSpec((tq, qa.shape[1]), lambda b, q: (b * nq + q, 0)),
                token_major(dk), by_feature(dv) if feature_major else token_major(dv),
                token_major(gk), by_feature(gv) if feature_major else token_major(gv)]
    args = [qa, dk, dv, gk, gv]
    aliases = {}
    if feature_major:
        assert nq == 1
        in_specs += [pl.BlockSpec(memory_space=pl.ANY)] * len(gkn_prev)
        args += list(gkn_prev)
        aliases = {5 + k: 1 + k for k in range(len(gkn_prev))}
    if caches is not None:
        in_specs += [pl.BlockSpec((None, None, a.shape[2], past), lambda b, q: (b, layer, 0, 0))
                     for a in caches]
        args += list(caches)
    if tables is not None:
        cosq, sinq, cosk, sink = tables
        in_specs += [pl.BlockSpec((tq, cosq.shape[1]), lambda b, q: (q, 0)),
                     pl.BlockSpec((tq, sinq.shape[1]), lambda b, q: (q, 0)),
                     full(cosk), full(sink)]
        args += [cosq, sinq, cosk, sink]
    in_specs += [full(lam_p), full(gain), full(gq_t), full(gk_t)]
    args += [lam_p, gain, gq_t, gk_t]
    out_specs = [pl.BlockSpec((tq, DIFF_W + GQA_W), lambda b, q: (b * nq + q, 0))]
    out_shape = [jax.ShapeDtypeStruct((t, DIFF_W + GQA_W), F32)]
    if feature_major:
        depth = dv.shape[1]
        if gkn_prev:
            out_specs.append(pl.BlockSpec((None, None, GQA_KV_W, seq), lambda b, q: (b, layer, 0, 0)))
        else:
            out_specs.append(pl.BlockSpec((None, depth, GQA_KV_W, seq), lambda b, q: (b, 0, 0, 0)))
        out_shape.append(jax.ShapeDtypeStruct((batch, depth, GQA_KV_W, seq), F32))
    cfg = (lam_init, seq, past, kb, tables is not None, feature_major,
           len(gkn_prev) if feature_major else 0, layer)
    return pl.pallas_call(
        functools.partial(_attn_kernel, cfg),
        grid=(batch, nq),
        in_specs=in_specs,
        out_specs=out_specs,
        out_shape=out_shape,
        input_output_aliases=aliases,
        scratch_shapes=[
            pltpu.VMEM((seq + past, DIFF_W), BF16), pltpu.VMEM((seq + past, GQA_KV_W), BF16),
            pltpu.VMEM((n_blocks, DIFF_W, kb), BF16), pltpu.VMEM((n_blocks, GQA_KV_W, kb), BF16),
            pltpu.VMEM((DIFF_W, 2 * DIFF_HEADS * tq), BF16), pltpu.VMEM((GQA_KV_W, GQA_Q_HEADS * tq), BF16),
            pltpu.VMEM((n_all // LANES, kb, LANES), F32), pltpu.VMEM((n_all // LANES, kb, LANES), F32),
            pltpu.VMEM((1, n_all), F32), pltpu.VMEM((1, n_all), F32),
            pltpu.VMEM((N_SCORE_HEADS, GQA_HD, tq), F32),
        ],
        compiler_params=pltpu.CompilerParams(
            dimension_semantics=("arbitrary", "arbitrary"), vmem_limit_bytes=VMEM_LIMIT),
    )(*args)


def _softplus(x):
    return jnp.maximum(x, 0.0) + jnp.log1p(jnp.exp(-jnp.abs(x)))


def _dot2_l(a, b_exact):
    a1 = a.astype(BF16)
    a2 = (a - a1.astype(F32)).astype(BF16)
    return _dot(a1, b_exact) + _dot(a2, b_exact)


def _ssd_kernel(has_init, n_cast, n_prev, seq, out_layer, *refs):
    it = iter(refs)
    z_ref, xbc_ref, dt_ref, cw_ref, cb_ref, alog_ref, dtb_ref, dexp_ref, ng_ref = (
        next(it) for _ in range(9))
    sf0_ref, sb0_ref = (next(it), next(it)) if has_init else (None, None)
    cast_in = [next(it) for _ in range(n_cast)]
    for _ in range(n_prev):
        next(it)
    o_ref, sf_ref, sb_ref = next(it), next(it), next(it)
    for src in cast_in:
        next(it)[...] = src[...].astype(BF16)
    stf_s, stb_s, inc_s, dec_s, eab_s, cbf_s, exp_s, tri_s, gm_s = it
    L = SSD_CHUNK
    W2 = 2 * SSD_INNER
    nchunks = seq // L
    halo = 8

    ri = lax.broadcasted_iota(jnp.int32, (L, L), 0)
    ci = lax.broadcasted_iota(jnp.int32, (L, L), 1)
    lower = ri >= ci
    upper = ri <= ci
    tri_s[0] = jnp.where(lower, 1.0, 0.0).astype(BF16)
    tri_s[1] = jnp.where(upper, 1.0, 0.0).astype(BF16)
    lane_row = lax.broadcasted_iota(jnp.int32, (1, LANES), 1)
    a_row = jnp.where(lane_row < DT_W, -jnp.exp(alog_ref[...]), 0.0)
    ej = lax.broadcasted_iota(jnp.int32, (LANES, W2), 0)
    eh = lax.shift_right_logical(lax.broadcasted_iota(jnp.int32, (LANES, W2), 1), 6)
    exp_s[...] = jnp.where(ej == eh, 1.0, 0.0).astype(BF16)
    gm_s[...] = jnp.where(lax.shift_right_logical(ej, 6) == (lax.shift_right_logical(eh, 2) & 1), 1.0, 0.0)

    def chunk_rows(c):
        return pl.ds(pl.multiple_of(c * L, L), L)

    def load_state(ref):
        r = lax.broadcasted_iota(jnp.int32, (SSD_STATE, LANES), 0)
        c = lax.broadcasted_iota(jnp.int32, (SSD_STATE, LANES), 1)
        dup = jnp.where((c & (SSD_STATE - 1)) == r, 1.0, 0.0).astype(BF16)
        return _dot3_l(ref[...], dup).T * gm_s[:, 0:SSD_INNER]

    def store_state(st_ref, ref):
        st_t = st_ref[...].T
        _put_layer_slot(ref, (), out_layer, (st_t + _swap_halves(st_t))[:, 0:SSD_STATE])

    stf_s[...] = load_state(sf0_ref) if has_init else jnp.zeros(stf_s.shape, F32)
    stb_s[...] = load_state(sb0_ref) if has_init else jnp.zeros(stb_s.shape, F32)

    def forward_pass(c, carry):
        r0 = c * L
        rows = chunk_rows(c)
        dt_c = _softplus(dt_ref[rows, :] + dtb_ref[...])

        d1, d2, d3 = _split3(dt_c * a_row)
        acs_f = _dot(tri_s[0], d1) + (_dot(tri_s[0], d2) + _dot(tri_s[0], d3))
        acs_b = _dot(tri_s[1], d1) + (_dot(tri_s[1], d2) + _dot(tri_s[1], d3))
        acs = jnp.where(lax.broadcasted_iota(jnp.int32, (L, LANES), 1) < SSD_HEADS, acs_f, acs_b)
        acs_t = acs.T
        expand = exp_s[...]
        dt_e = _dot2_l(dt_c, expand)
        acs_e = _dot3_l(acs, expand)
        edge = jnp.concatenate([acs_e[L - 1:L, 0:SSD_INNER], acs_e[0:1, SSD_INNER:W2]], axis=1)
        eacs = jnp.exp(acs_e)
        cdec = jnp.exp(edge)

        prev = xbc_ref[pl.ds(pl.multiple_of(jnp.maximum(r0 - halo, 0), halo), halo), :]
        nxt = xbc_ref[pl.ds(pl.multiple_of(jnp.minimum(r0 + L, seq - halo), halo), halo), :]
        cur = xbc_ref[rows, :]
        win = jnp.concatenate([jnp.where(c > 0, prev, 0.0), cur,
                               jnp.where(c < nchunks - 1, nxt, 0.0)], axis=0)
        acc = cb_ref[...] + cur * cw_ref[SSD_CONV // 2:SSD_CONV // 2 + 1, :]
        for j in range(SSD_CONV):
            if j != SSD_CONV // 2:
                shifted = pltpu.roll(win, (SSD_CONV // 2 - j) % (L + 2 * halo), 0)[halo:halo + L, :]
                acc = acc + shifted * cw_ref[j:j + 1, :]
        act = _silu(acc)
        x_c = act[:, 0:SSD_INNER]
        b_c = act[:, SSD_INNER:SSD_INNER + SSD_BC_W]
        c_c = act[:, SSD_INNER + SSD_BC_W:XBC_W]
        xd = jnp.concatenate([x_c, x_c], axis=1) * dt_e
        xd_b = xd.astype(BF16)
        xdw = (xd * jnp.exp(edge - acs_e)).astype(BF16)
        b_b = b_c.astype(BF16)
        c_b = c_c.astype(BF16)
        s_new = _dot(b_c.T.astype(BF16), xdw) * gm_s[...]

        st_f = stf_s[...]
        y = _dot(c_b, st_f.astype(BF16)) * eacs[:, 0:SSD_INNER]
        stf_s[...] = st_f * cdec[:, 0:SSD_INNER] + s_new[:, 0:SSD_INNER]
        inc_s[c] = s_new[:, SSD_INNER:W2]
        dec_s[c] = cdec[:, SSD_INNER:W2]
        eab_s[rows, :] = eacs[:, SSD_INNER:W2]
        cbf_s[rows, :] = c_b

        cbs = []
        for g in range(SSD_GROUPS):
            cg = jnp.where(_lane_mask(c_c.shape, g * SSD_STATE, (g + 1) * SSD_STATE), c_c, 0.0)
            cbs.append(_dot_nt(cg.astype(BF16), b_b))
        for direction, causal in ((0, lower), (1, upper)):
            pairs = []
            for g in range(SSD_GROUPS):
                cb = cbs[g]
                for hp in range(2):
                    pair = g * 2 + hp
                    res = []
                    for k in range(2):
                        j = direction * SSD_HEADS + pair * 2 + k
                        diff = acs[:, j:j + 1] - acs_t[j:j + 1, :]
                        dec = jnp.where(causal, jnp.exp(diff), 0.0)
                        sc = (cb * dec).astype(BF16)
                        lo = direction * SSD_INNER + pair * LANES
                        res.append(_dot(sc, xd_b[:, lo:lo + LANES]))
                    pairs.append(jnp.where(_lane_mask(res[0].shape, 0, SSD_HD), res[0], res[1]))
            y = y + jnp.concatenate(pairs, axis=-1)
        o_ref[rows, :] = y + x_c * dexp_ref[...]
        return carry

    lax.fori_loop(0, nchunks, forward_pass, 0)
    store_state(stf_s, sf_ref)

    def backward_pass(i, carry):
        c = nchunks - 1 - i
        rows = chunk_rows(c)
        st_b = stb_s[...]
        y = o_ref[rows, :] + _dot(cbf_s[rows, :], st_b.astype(BF16)) * eab_s[rows, :]
        stb_s[...] = st_b * dec_s[c] + inc_s[c]
        yt = y * _silu(z_ref[rows, :])
        ms = jnp.mean(yt * yt, axis=-1, keepdims=True)
        o_ref[rows, :] = yt * lax.rsqrt(ms + EPS) * ng_ref[...]
        return carry

    lax.fori_loop(0, nchunks, backward_pass, 0, unroll=2)
    store_state(stb_s, sb_ref)


def _ssd(z, xbc, dt, params, init, layer, batch, seq, state_prev=(), depth=1, cast=()):
    cw, cb, alog, dtb, dexp, ng = params
    t = z.shape[0]
    has_init = init is not None
    nchunks = seq // SSD_CHUNK
    full = lambda a: pl.BlockSpec(a.shape, lambda b: (0,) * a.ndim)
    out_layer = layer if depth > 1 else 0
    if state_prev or depth == 1:
        st_spec = pl.BlockSpec((None, None, SSD_INNER, SSD_STATE), lambda b: (b, out_layer, 0, 0))
    else:
        st_spec = pl.BlockSpec((None, depth, SSD_INNER, SSD_STATE), lambda b: (b, 0, 0, 0))
    in_specs = [
        pl.BlockSpec((seq, SSD_INNER), lambda b: (b, 0)),
        pl.BlockSpec((seq, XBC_W), lambda b: (b, 0)),
        pl.BlockSpec((seq, LANES), lambda b: (b, 0)),
        full(cw), full(cb), full(alog), full(dtb), full(dexp), full(ng),
    ]
    args = [z, xbc, dt, cw, cb, alog, dtb, dexp, ng]
    if has_init:
        init_spec = pl.BlockSpec((None, None, SSD_INNER, SSD_STATE), lambda b: (b, layer, 0, 0))
        in_specs += [init_spec, init_spec]
        args += list(init)
    cast_in_specs, cast_out_specs, cast_out_shape = _cast_specs(cast, layer, batch)
    in_specs += cast_in_specs
    args += list(cast)
    aliases = {len(args) + k: 1 + k for k in range(len(state_prev))}
    in_specs += [pl.BlockSpec(memory_space=pl.ANY)] * len(state_prev)
    args += list(state_prev)
    st_shape = jax.ShapeDtypeStruct((batch, depth, SSD_INNER, SSD_STATE), F32)
    return pl.pallas_call(
        functools.partial(_ssd_kernel, has_init, len(cast), len(state_prev), seq, out_layer),
        grid=(batch,),
        in_specs=in_specs,
        out_specs=[pl.BlockSpec((seq, SSD_INNER), lambda b: (b, 0)), st_spec, st_spec] + cast_out_specs,
        out_shape=[jax.ShapeDtypeStruct((t, SSD_INNER), F32), st_shape, st_shape] + cast_out_shape,
        input_output_aliases=aliases,
        scratch_shapes=[
            pltpu.VMEM((LANES, SSD_INNER), F32), pltpu.VMEM((LANES, SSD_INNER), F32),
            pltpu.VMEM((nchunks, LANES, SSD_INNER), F32),
            pltpu.VMEM((nchunks, 1, SSD_INNER), F32),
            pltpu.VMEM((seq, SSD_INNER), F32),
            pltpu.VMEM((seq, SSD_BC_W), BF16),
            pltpu.VMEM((LANES, 2 * SSD_INNER), BF16),
            pltpu.VMEM((2, SSD_CHUNK, SSD_CHUNK), BF16),
            pltpu.VMEM((LANES, 2 * SSD_INNER), F32),
        ],
        compiler_params=pltpu.CompilerParams(
            dimension_semantics=("arbitrary",), vmem_limit_bytes=VMEM_LIMIT),
    )(*args)


def _post_kernel(alpha, d_ff, ff_chunk, x_ref, oa_ref, os_ref, g1_ref, sh2_ref, sc2_ref, g2_ref,
                 wo_ref, wfi_ref, wfo_ref, lng_ref, lnb_ref, y_ref):
    wa = oa_ref.shape[1]
    half = x_ref.shape[0] // 2
    bounds = list(range(0, d_ff, ff_chunk)) + [d_ff]
    n_chunks = len(bounds) - 1

    def mix_and_norm(r):
        o = (_dot(oa_ref[r, :].astype(BF16), wo_ref[0:wa, :])
             + _dot(os_ref[r, :].astype(BF16), wo_ref[wa:, :]))
        x1 = _layer_norm(alpha * x_ref[r, :] + g1_ref[...] * o, lng_ref[0:1, :], lnb_ref[0:1, :])
        return x1, (x1 * (1.0 + sc2_ref[...]) + sh2_ref[...]).astype(BF16)

    def up(h2, c):
        lo, hi = bounds[c], bounds[c + 1]
        return _dot(h2, wfi_ref[:, lo:hi]), _dot(h2, wfi_ref[:, d_ff + lo:d_ff + hi])

    def ffn_chunks(h2):
        f = jnp.zeros((half, x_ref.shape[1]), F32)
        g, u = up(h2, 0)
        for c in range(n_chunks):
            nxt = up(h2, c + 1) if c + 1 < n_chunks else None
            f = f + _dot((_silu(g) * u).astype(BF16), wfo_ref[bounds[c]:bounds[c + 1], :])
            if nxt is not None:
                g, u = nxt
            yield f

    def finish(r, x1, f):
        y_ref[r, :] = _layer_norm(alpha * x1 + g2_ref[...] * f, lng_ref[1:2, :], lnb_ref[1:2, :])

    ra, rb = slice(0, half), slice(half, 2 * half)
    x1a, h2a = mix_and_norm(ra)
    ffn_a = ffn_chunks(h2a)
    for _ in range(n_chunks // 2):
        fa = next(ffn_a)
    x1b, h2b = mix_and_norm(rb)
    for fa in ffn_a:
        pass
    ffn_b = ffn_chunks(h2b)
    for _ in range(n_chunks // 2):
        fb = next(ffn_b)
    finish(ra, x1a, fa)
    for fb in ffn_b:
        pass
    finish(rb, x1b, fb)


def _post(x, oa, os_, mods, layer, row_fn, wo_bf, wfi_bf, wfo_bf, lng, lnb, alpha, tm):
    t, d = x.shape
    d_ff = wfo_bf.shape[0]
    ff_chunk = 3 * MXU_TILE
    assert d_ff % MXU_TILE == 0
    per_layer = lambda a: pl.BlockSpec((None,) + a.shape[1:], lambda i: (layer, 0, 0))
    resident = lambda a: pl.BlockSpec(a.shape, lambda i: (0, 0), pipeline_mode=pl.Buffered(1))
    return pl.pallas_call(
        functools.partial(_post_kernel, alpha, d_ff, ff_chunk),
        grid=(t // tm,),
        in_specs=[
            pl.BlockSpec((tm, d), lambda i: (i, 0)),
            pl.BlockSpec((tm, oa.shape[1]), lambda i: (i, 0)),
            pl.BlockSpec((tm, os_.shape[1]), lambda i: (i, 0)),
            _mod_spec(layer, 2, row_fn, d),
            _mod_spec(layer, 3, row_fn, d),
            _mod_spec(layer, 4, row_fn, d),
            _mod_spec(layer, 5, row_fn, d),
            resident(wo_bf), resident(wfi_bf), resident(wfo_bf),
            per_layer(lng), per_layer(lnb),
        ],
        out_specs=pl.BlockSpec((tm, d), lambda i: (i, 0)),
        out_shape=jax.ShapeDtypeStruct((t, d), F32),
        compiler_params=pltpu.CompilerParams(
            dimension_semantics=("arbitrary",), vmem_limit_bytes=VMEM_LIMIT),
    )(x, oa, os_, mods, mods, mods, mods, wo_bf, wfi_bf, wfo_bf, lng, lnb)


def _rope_tables(rows, dim, copies):
    row = jnp.repeat(jnp.arange(rows), GRID_W).astype(F32)
    col = jnp.tile(jnp.arange(GRID_W), rows).astype(F32)
    n_freq = dim // 4
    inv = ROPE_THETA ** (-jnp.arange(n_freq, dtype=F32) / n_freq)
    ang = jnp.concatenate([row[:, None] * inv, col[:, None] * inv], -1)
    cos, sin = jnp.cos(ang), jnp.sin(ang)
    cos_full = jnp.repeat(cos, 2, axis=-1)
    sin_signed = jnp.stack([-sin, sin], axis=-1).reshape(sin.shape[0], dim)
    return jnp.tile(cos_full, (1, copies)), jnp.tile(sin_signed, (1, copies))


def kernel(x_prompt, x_sample, cache_diff_k, cache_diff_v, cache_gqa_k, cache_gqa_v, state_ssd_fwd, state_ssd_bwd, c, c_ctx, w_ada, b_ada, w_in, w_out, diff_lambda, diff_subln_g, qk_norm_g, ssd_conv_w, ssd_conv_b, ssd_A_log, ssd_dt_bias, ssd_D, ssd_norm_g, ln_g, ln_b, w_ffn_in, w_ffn_out):
    batch, seq, d = x_prompt.shape
    dec_batch, dec_seq, _ = x_sample.shape
    depth = w_in.shape[0]
    past = cache_diff_k.shape[2]
    alpha = (2 * depth) ** 0.25
    rows = dec_seq // GRID_W

    n_vec = 1 + dec_batch
    n_pad = -(-n_vec // 8) * 8
    cvec = jnp.concatenate([c_ctx[None, :], c, jnp.zeros((n_pad - n_vec, d), F32)], axis=0)
    mods = _modulation(cvec, w_ada, b_ada).reshape(depth, n_pad, 1, 6 * d)

    cos_d, sin_d = _rope_tables(rows, DIFF_QK, DIFF_W // DIFF_QK)
    cos_g, sin_g = _rope_tables(rows, GQA_HD, GQA_Q_HEADS)
    cosq = jnp.concatenate([cos_d, cos_g], axis=-1)
    sinq = jnp.concatenate([sin_d, sin_g], axis=-1)
    cosk = jnp.concatenate([cos_d, cos_g[:, :GQA_KV_W]], axis=-1)
    sink = jnp.concatenate([sin_d, sin_g[:, :GQA_KV_W]], axis=-1)
    tables = (cosq, sinq, cosk, sink)

    feature_major = lambda a, w: jnp.transpose(a.reshape(dec_batch, depth, past, w), (0, 1, 3, 2))
    caches = (feature_major(cache_diff_k, DIFF_W), feature_major(cache_diff_v, DIFF_W),
              feature_major(cache_gqa_k, GQA_KV_W), feature_major(cache_gqa_v, GQA_KV_W))

    xp = x_prompt.reshape(batch * seq, d)
    xs = x_sample.reshape(dec_batch * dec_seq, d)
    tm_ctx = min(512, batch * seq)
    tm_lat = min(512, dec_seq)
    tm_post_ctx = min(1024, batch * seq)
    tm_post_lat = min(1024, dec_seq)
    tq = min(512, dec_seq)
    ctx_row = lambda i: 0
    lat_row = lambda i: 1 + (i * tm_lat) // dec_seq
    lat_row_post = lambda i: 1 + (i * tm_post_lat) // dec_seq

    assert w_in.shape[2] == _C_END
    w_in_bf = w_in.astype(BF16)
    init = (state_ssd_fwd.reshape(dec_batch, depth, SSD_INNER, SSD_STATE),
            state_ssd_bwd.reshape(dec_batch, depth, SSD_INNER, SSD_STATE))

    kv_cache = ()
    gk_cache = ()
    ssd_states = ()
    for l in range(depth):
        lam_init = 0.8 - 0.6 * math.exp(-0.3 * l)
        lam_p = diff_lambda[l]
        attn_gain = jnp.concatenate([jnp.tile(diff_subln_g[l], DIFF_HEADS), jnp.ones((GQA_W,), F32)])[None, :]
        gq_t = jnp.tile(qk_norm_g[l, 0], GQA_Q_HEADS)[None, :]
        gk_t = jnp.tile(qk_norm_g[l, 1], GQA_KV_HEADS)[None, :]
        pad_row = lambda v: jnp.pad(v.reshape(1, -1), ((0, 0), (0, LANES - v.size)))
        ssd_params = (ssd_conv_w[l], ssd_conv_b[l][None, :], pad_row(ssd_A_log[l]),
                      pad_row(ssd_dt_bias[l]), jnp.repeat(ssd_D[l], SSD_HD)[None, :],
                      ssd_norm_g[l][None, :])

        qa, dk, gk, z, xbc, dt, *kv_cache = _inproj(xp, mods, l, ctx_row, w_in_bf, tm_ctx,
                                                    cache_prev=tuple(kv_cache), depth=depth, seq=seq)
        oa, gkn = _attention(qa, (dk, kv_cache[1], gk, kv_cache[2]), None, l, None, lam_p, attn_gain,
                             gq_t, gk_t, lam_init, batch, seq, seq, gkn_prev=gk_cache)
        gk_cache = (gkn,)
        os_, *rest = _ssd(z, xbc, dt, ssd_params, None, l, batch, seq,
                          state_prev=tuple(ssd_states), depth=depth,
                          cast=(w_out, w_ffn_in, w_ffn_out))
        ssd_states, (wo_bf, wfi_bf, wfo_bf) = rest[:2], rest[2:]
        xp = _post(xp, oa, os_, mods, l, ctx_row, wo_bf, wfi_bf, wfo_bf, ln_g, ln_b, alpha, tm_post_ctx)

        qa, dk, dv, gk, gv, z, xbc, dt = _inproj(xs, mods, l, lat_row, w_in_bf, tm_lat)
        (oa,) = _attention(qa, (dk, dv, gk, gv), caches, l, tables, lam_p, attn_gain, gq_t, gk_t,
                           lam_init, dec_batch, dec_seq, tq)
        os_, _, _ = _ssd(z, xbc, dt, ssd_params, init, l, dec_batch, dec_seq)
        xs = _post(xs, oa, os_, mods, l, lat_row_post, wo_bf, wfi_bf, wfo_bf, ln_g, ln_b, alpha,
                   tm_post_lat)

    def token_major(a, heads, width):
        return jnp.transpose(a.reshape(batch, depth, heads, width, seq), (0, 1, 4, 2, 3))

    state = lambda a: a.reshape(batch, depth, SSD_HEADS, SSD_HD, SSD_STATE)
    return (xp.reshape(batch, seq, d), xs.reshape(dec_batch, dec_seq, d),
            token_major(kv_cache[0], DIFF_HEADS, 2 * DIFF_QK), token_major(kv_cache[1], DIFF_HEADS, DIFF_V),
            token_major(gk_cache[0], GQA_KV_HEADS, GQA_HD), token_major(kv_cache[2], GQA_KV_HEADS, GQA_HD),
            state(ssd_states[0]), state(ssd_states[1]))
```

```python
import functools
import math

import jax
import jax.numpy as jnp
from jax import lax
from jax.experimental import pallas as pl
from jax.experimental.pallas import tpu as pltpu

F32 = jnp.float32
BF16 = jnp.bfloat16

GRID_W = 64
DIFF_HEADS = 4
DIFF_QK = 32
DIFF_V = 64
DIFF_W = DIFF_HEADS * DIFF_V
GQA_HD = 64
GQA_Q_HEADS = 4
GQA_KV_HEADS = 2
GQA_W = GQA_Q_HEADS * GQA_HD
GQA_KV_W = GQA_KV_HEADS * GQA_HD
SSD_HD = 64
SSD_HEADS = 8
SSD_INNER = SSD_HEADS * SSD_HD
SSD_GROUPS = 2
SSD_STATE = 64
SSD_BC_W = SSD_GROUPS * SSD_STATE
SSD_CONV = 5
SSD_CHUNK = 128
XBC_W = SSD_INNER + 2 * SSD_BC_W
DT_W = 2 * SSD_HEADS
ROPE_THETA = 10000.0
EPS = 1e-5
LANES = 128
MXU_TILE = 256
VMEM_LIMIT = 56 * 1024 * 1024

ROW_TILE = 512
POST_ROW_TILE = 1024
QUERY_TILE = 512
KEY_BLOCK = 256
MOD_K_TILE = 256
FF_CHUNK = 3 * MXU_TILE

_C_DQ, _C_DK, _C_DV, _C_GQ, _C_GK, _C_GV, _C_Z, _C_XBC, _C_DT, _C_END = (
    0, 256, 512, 768, 1024, 1152, 1280, 1792, 2560, 2576)


def _dot(a, b):
    return jnp.dot(a, b, preferred_element_type=F32)


def _dot_nt(a, b):
    return lax.dot_general(a, b, (((1,), (1,)), ((), ())), preferred_element_type=F32)


def _split3(a):
    a1 = a.astype(BF16)
    r1 = a - a1.astype(F32)
    a2 = r1.astype(BF16)
    a3 = (r1 - a2.astype(F32)).astype(BF16)
    return a1, a2, a3


def _dot3_l(a, b_exact):
    a1, a2, a3 = _split3(a)
    return _dot(a1, b_exact) + (_dot(a2, b_exact) + _dot(a3, b_exact))


def _sigmoid(x):
    return 1.0 / (1.0 + jnp.exp(-x))


def _silu(x):
    return x * _sigmoid(x)


def _layer_norm(x, g, b):
    mu = jnp.mean(x, axis=-1, keepdims=True)
    xc = x - mu
    var = jnp.mean(xc * xc, axis=-1, keepdims=True)
    return xc * lax.rsqrt(var + EPS) * g + b


def _group_avg_matrix(width, group):
    sh = int(math.log2(group))
    r = lax.shift_right_logical(lax.broadcasted_iota(jnp.int32, (width, width), 0), sh)
    c = lax.shift_right_logical(lax.broadcasted_iota(jnp.int32, (width, width), 1), sh)
    return jnp.where(r == c, 1.0 / group, 0.0).astype(BF16)


def _group_mean_sq(x, gmat):
    xx = x * x
    hi = xx.astype(BF16)
    lo = (xx - hi.astype(F32)).astype(BF16)
    return _dot(hi, gmat) + _dot(lo, gmat)


def _rope(x, cos, sin_signed):
    w = x.shape[-1]
    lane = lax.broadcasted_iota(jnp.int32, x.shape, 1)
    nxt = pltpu.roll(x, w - 1, 1)
    prv = pltpu.roll(x, 1, 1)
    partner = jnp.where((lane & 1) == 0, nxt, prv)
    return x * cos + partner * sin_signed


def _lane_mask(shape, lo, hi):
    lane = lax.broadcasted_iota(jnp.int32, shape, 1)
    return (lane >= lo) & (lane < hi)


def _mod_kernel(c_ref, w_ref, b_ref, o_ref):
    @pl.when(pl.program_id(1) == 0)
    def _start_from_bias():
        o_ref[...] = jnp.broadcast_to(b_ref[...], o_ref.shape)

    a = _silu(c_ref[...])
    rows = a.shape[0]
    a_hi = a.astype(BF16)
    a_hi_f = a_hi.astype(F32)
    w = w_ref[...]
    w_hi = w.astype(BF16)
    w_lo = (w - w_hi.astype(F32)).astype(BF16)
    both = _dot(jnp.concatenate([a_hi_f, a - a_hi_f], axis=0).astype(BF16), w_hi)
    o_ref[...] += both[0:rows] + (both[rows:2 * rows] + _dot(a_hi, w_lo))


def _modulation(cvec, w_ada, b_ada):
    depth, d, n = w_ada.shape
    tk = MOD_K_TILE
    rows = cvec.shape[0]
    return pl.pallas_call(
        _mod_kernel,
        grid=(depth, d // tk),
        in_specs=[
            pl.BlockSpec((rows, tk), lambda l, k: (0, k)),
            pl.BlockSpec((None, tk, n), lambda l, k: (l, k, 0)),
            pl.BlockSpec((None, 1, n), lambda l, k: (l, 0, 0)),
        ],
        out_specs=pl.BlockSpec((None, rows, n), lambda l, k: (l, 0, 0)),
        out_shape=jax.ShapeDtypeStruct((depth, rows, n), F32),
        compiler_params=pltpu.CompilerParams(
            dimension_semantics=("arbitrary", "arbitrary"), vmem_limit_bytes=VMEM_LIMIT),
    )(cvec, w_ada, b_ada.reshape(depth, 1, n))


def _put_layer_slot(ref, lead, layer, val):
    if len(ref.shape) == val.ndim + len(lead):
        ref[lead if lead else ...] = val
    else:
        for l in range(ref.shape[len(lead)]):
            ref[lead + (l,)] = val if l == layer else jnp.zeros_like(val)


def _inproj_kernel(feature_major_cache, n_cast, n_prev, layer, *refs):
    x_ref, sh_ref, sc_ref, w_ref = refs[:4]
    cast_in = refs[4:4 + n_cast]
    outs = refs[4 + n_cast + n_prev:]
    if n_cast:
        cast_out = outs[-n_cast - 2:-2] if feature_major_cache else outs[-n_cast:]
        outs = outs[:-n_cast - 2] + outs[-2:] if feature_major_cache else outs[:-n_cast]
        for src, dst in zip(cast_in, cast_out):
            dst[...] = src[...].astype(BF16)
    h = (x_ref[...] * (1.0 + sc_ref[...]) + sh_ref[...]).astype(BF16)

    def mm(lo, hi):
        return _dot(h, w_ref[:, lo:hi])

    gkv = mm(_C_GK, _C_Z)
    if feature_major_cache:
        (qa_ref, dk_ref, gk_ref, z_ref, xbc_ref, dt_ref, dkt_ref, dvt_ref, gvt_ref,
         dv_s, gv_s) = outs
        dk_ref[...] = mm(_C_DK, _C_DV)
        dv_s[...] = mm(_C_DV, _C_GQ)
        gv_s[...] = gkv[:, GQA_KV_W:]
        seq = dkt_ref.shape[-1]
        for j in range(dkt_ref.shape[0]):
            rows = slice(j * seq, (j + 1) * seq)
            _put_layer_slot(dkt_ref, (j,), layer, dk_ref[rows, :].T)
            _put_layer_slot(dvt_ref, (j,), layer, dv_s[rows, :].T)
            _put_layer_slot(gvt_ref, (j,), layer, gv_s[rows, :].T)
    else:
        qa_ref, dk_ref, dv_ref, gk_ref, gv_ref, z_ref, xbc_ref, dt_ref = outs
        dk_ref[...] = mm(_C_DK, _C_DV)
        dv_ref[...] = mm(_C_DV, _C_GQ)
        gv_ref[...] = gkv[:, GQA_KV_W:]
    qa_ref[:, 0:DIFF_W] = mm(_C_DQ, _C_DK)
    qa_ref[:, DIFF_W:DIFF_W + GQA_W] = mm(_C_GQ, _C_GK)
    gk_ref[...] = gkv[:, 0:GQA_KV_W]
    z_ref[...] = mm(_C_Z, _C_XBC)
    xbc_ref[...] = mm(_C_XBC, _C_DT)
    dt_ref[...] = jnp.zeros(dt_ref.shape, F32)
    dt_ref[:, 0:DT_W] = mm(_C_DT, _C_END)


def _cast_specs(cast, layer, steps):
    in_specs, out_specs, out_shape = [], [], []
    for w in cast:
        hold = 1
        while (w.shape[1] * hold) % steps or (w.shape[1] * hold // steps) % 16:
            hold *= 2
        rows = w.shape[1] * hold // steps
        in_specs.append(pl.BlockSpec((None, rows, w.shape[2]), lambda i, hold=hold: (layer, i // hold, 0)))
        out_specs.append(pl.BlockSpec((rows, w.shape[2]), lambda i, hold=hold: (i // hold, 0)))
        out_shape.append(jax.ShapeDtypeStruct(w.shape[1:], BF16))
    return in_specs, out_specs, out_shape


def _mod_spec(layer, which, row_fn, d):
    return pl.BlockSpec((None, None, 1, d), lambda i: (layer, row_fn(i), 0, which))


def _inproj(x, mods, layer, row_fn, w_in_bf, tm, cache_prev=None, depth=None, seq=None, cast=()):
    t, d = x.shape
    steps = t // tm
    feature_major = cache_prev is not None
    token_spec = lambda w: pl.BlockSpec((tm, w), lambda i: (i, 0))
    token_shape = lambda w: jax.ShapeDtypeStruct((t, w), F32)
    if feature_major:
        widths = (DIFF_W + GQA_W, DIFF_W, GQA_KV_W, SSD_INNER, XBC_W, LANES)
        cache_w = (DIFF_W, DIFF_W, GQA_KV_W)
        assert tm % seq == 0
        if cache_prev:
            cache_spec = lambda w: pl.BlockSpec((tm // seq, None, w, seq), lambda i: (i, layer, 0, 0))
        else:
            cache_spec = lambda w: pl.BlockSpec((tm // seq, depth, w, seq), lambda i: (i, 0, 0, 0))
        out_specs = [token_spec(w) for w in widths] + [cache_spec(w) for w in cache_w]
        out_shape = [token_shape(w) for w in widths] + [
            jax.ShapeDtypeStruct((t // seq, depth, w, seq), F32) for w in cache_w]
        aliases = {4 + k: len(widths) + k for k in range(len(cache_prev))}
    else:
        widths = (DIFF_W + GQA_W, DIFF_W, DIFF_W, GQA_KV_W, GQA_KV_W, SSD_INNER, XBC_W, LANES)
        out_specs = [token_spec(w) for w in widths]
        out_shape = [token_shape(w) for w in widths]
        cache_prev, aliases = (), {}
    aliases = {k + len(cast): v for k, v in aliases.items()}
    cast_in_specs, cast_out_specs, cast_out_shape = _cast_specs(cast, layer, steps)
    return pl.pallas_call(
        functools.partial(_inproj_kernel, feature_major, len(cast), len(cache_prev), layer),
        grid=(steps,),
        in_specs=[
            pl.BlockSpec((tm, d), lambda i: (i, 0)),
            _mod_spec(layer, 0, row_fn, d),
            _mod_spec(layer, 1, row_fn, d),
            pl.BlockSpec((None, d, _C_END), lambda i: (layer, 0, 0)),
        ] + cast_in_specs + [pl.BlockSpec(memory_space=pl.ANY)] * len(cache_prev),
        out_specs=out_specs + cast_out_specs,
        out_shape=out_shape + cast_out_shape,
        input_output_aliases=aliases,
        scratch_shapes=([pltpu.VMEM((tm, DIFF_W), F32), pltpu.VMEM((tm, GQA_KV_W), F32)]
                        if feature_major else []),
        compiler_params=pltpu.CompilerParams(
            dimension_semantics=("arbitrary",), vmem_limit_bytes=VMEM_LIMIT),
    )(x, mods, mods, w_in_bf, *cast, *cache_prev)


LOG2E = 1.4426950408889634
NEG_BIG = -1e30
N_SCORE_HEADS = 2 * DIFF_HEADS + GQA_Q_HEADS
ONES_ROWS = 16


def _diff_lambda(lam_ref, lam_init):
    lp = lam_ref[...]
    s1 = jnp.sum(lp[0:1, :] * lp[1:2, :], axis=-1, keepdims=True)
    s2 = jnp.sum(lp[2:3, :] * lp[3:4, :], axis=-1, keepdims=True)
    return jnp.exp(s1) - jnp.exp(s2) + lam_init


def _swap_halves(x):
    return pltpu.roll(x, GQA_HD, 1)


def _attn_kernel(cfg, *refs):
    lam_init, seq, past, kb, rope, feature_major, n_prev, layer = cfg
    it = iter(refs)
    qa_ref, dk_ref, dv_ref, gk_in_ref, gv_ref = next(it), next(it), next(it), next(it), next(it)
    for _ in range(n_prev):
        next(it)
    if past:
        cdk_ref, cdv_ref, cgk_ref, cgv_ref = next(it), next(it), next(it), next(it)
    if rope:
        cosq_ref, sinq_ref, cosk_ref, sink_ref = next(it), next(it), next(it), next(it)
    lam_ref, gain_ref, gq_ref, gk_ref = next(it), next(it), next(it), next(it)
    o_ref = next(it)
    gkn_ref = next(it) if feature_major else None
    kd_s, kg_s, vdt_s, vgt_s, wq_s, wg_s, s0_s, s1_s, m_s, l_s, acc_s = it
    tq = qa_ref.shape[0]
    n_blocks = (seq + past) // kb

    @pl.when(pl.program_id(1) == 0)
    def _prepare_keys():
        gmat = _group_avg_matrix(GQA_KV_W, GQA_HD)
        for i in range(seq // kb):
            rows = slice(i * kb, (i + 1) * kb)
            dk = dk_ref[rows, :]
            gk = gk_in_ref[rows, :]
            gk = gk * lax.rsqrt(_group_mean_sq(gk, gmat) + EPS) * gk_ref[...]
            if feature_major:
                gk_t = gk.T
                if len(gkn_ref.shape) == 2:
                    gkn_ref[:, rows] = gk_t
                else:
                    for l in range(gkn_ref.shape[0]):
                        gkn_ref[l, :, rows] = gk_t if l == layer else jnp.zeros_like(gk_t)
            if rope:
                ck = cosk_ref[rows, :]
                sk = sink_ref[rows, :]
                dk = _rope(dk, ck[:, 0:DIFF_W], sk[:, 0:DIFF_W])
                gk = _rope(gk, ck[:, DIFF_W:DIFF_W + GQA_KV_W], sk[:, DIFF_W:DIFF_W + GQA_KV_W])
            kd_s[rows, :] = dk.astype(BF16)
            kg_s[rows, :] = gk.astype(BF16)
            if feature_major:
                vdt_s[i] = dv_ref[:, rows].astype(BF16)
                vgt_s[i] = gv_ref[:, rows].astype(BF16)
            else:
                vdt_s[i] = dv_ref[rows, :].T.astype(BF16)
                vgt_s[i] = gv_ref[rows, :].T.astype(BF16)
        wq_s[...] = jnp.zeros(wq_s.shape, BF16)
        wg_s[...] = jnp.zeros(wg_s.shape, BF16)
        for j in range(past // kb):
            src = slice(j * kb, (j + 1) * kb)
            dst = slice(seq + j * kb, seq + (j + 1) * kb)
            kd_s[dst, :] = cdk_ref[:, src].T.astype(BF16)
            kg_s[dst, :] = cgk_ref[:, src].T.astype(BF16)
            vdt_s[seq // kb + j] = cdv_ref[:, src].astype(BF16)
            vgt_s[seq // kb + j] = cgv_ref[:, src].astype(BF16)

    qa = qa_ref[...]
    qd = qa[:, 0:DIFF_W]
    gq = qa[:, DIFF_W:DIFF_W + GQA_W]
    gq = gq * lax.rsqrt(_group_mean_sq(gq, _group_avg_matrix(GQA_W, GQA_HD)) + EPS) * gq_ref[...]
    if rope:
        cq = cosq_ref[...]
        sq = sinq_ref[...]
        qd = _rope(qd, cq[:, 0:DIFF_W], sq[:, 0:DIFF_W])
        gq = _rope(gq, cq[:, DIFF_W:DIFF_W + GQA_W], sq[:, DIFF_W:DIFF_W + GQA_W])
    qd_t = (qd * (DIFF_QK ** -0.5 * LOG2E)).T
    gq_t = (gq * (GQA_HD ** -0.5 * LOG2E)).T
    for hm in range(2 * DIFF_HEADS):
        band = slice(hm * DIFF_QK, (hm + 1) * DIFF_QK)
        wq_s[band, hm * tq:(hm + 1) * tq] = qd_t[band, :].astype(BF16)
    for h in range(GQA_Q_HEADS):
        kvh = h // (GQA_Q_HEADS // GQA_KV_HEADS)
        wg_s[kvh * GQA_HD:(kvh + 1) * GQA_HD, h * tq:(h + 1) * tq] = (
            gq_t[h * GQA_HD:(h + 1) * GQA_HD, :].astype(BF16))
    m_s[...] = jnp.full(m_s.shape, NEG_BIG, F32)
    l_s[...] = jnp.zeros(l_s.shape, F32)
    acc_s[...] = jnp.zeros(acc_s.shape, F32)
    n_diff = 2 * DIFF_HEADS * tq
    n_all = N_SCORE_HEADS * tq

    def key_rows(j):
        return pl.ds(j * kb if isinstance(j, int) else pl.multiple_of(j * kb, kb), kb)

    n_qt = tq // MXU_TILE
    tile_slabs = MXU_TILE // LANES

    def head_scores(idx, qt, k_d, k_g, s_buf):
        lo = idx * tq + qt * MXU_TILE
        if idx < 2 * DIFF_HEADS:
            s = _dot(k_d, wq_s[:, lo:lo + MXU_TILE])
        else:
            s = _dot(k_g, wg_s[:, lo - n_diff:lo - n_diff + MXU_TILE])
        for k in range(tile_slabs):
            s_buf[lo // LANES + k] = s[:, k * LANES:(k + 1) * LANES]

    def head_update(idx, qt, v_d, v_g, s_buf):
        lo = idx * tq + qt * MXU_TILE
        ps, alphas = [], []
        for k in range(tile_slabs):
            c = lo // LANES + k
            cols = slice(c * LANES, (c + 1) * LANES)
            s = s_buf[c]
            m_old = m_s[:, cols]
            m_new = jnp.maximum(m_old, jnp.max(s, axis=0, keepdims=True))
            alphas.append(jnp.exp2(m_old - m_new))
            ps.append(jnp.exp2(s - m_new).astype(BF16))
            m_s[:, cols] = m_new
        if idx < 2 * DIFF_HEADS:
            vh = idx // 2
            v_t = v_d[vh * DIFF_V:(vh + 1) * DIFF_V, :]
        else:
            vh = (idx - 2 * DIFF_HEADS) // (GQA_Q_HEADS // GQA_KV_HEADS)
            v_t = v_g[vh * GQA_HD:(vh + 1) * GQA_HD, :]
        v_ext = jnp.concatenate([v_t, jnp.ones((ONES_ROWS, kb), BF16)], axis=0)
        alpha = jnp.concatenate(alphas, axis=1)
        pv = _dot(v_ext, jnp.concatenate(ps, axis=1))
        cols = slice(lo, lo + MXU_TILE)
        qcols = slice(qt * MXU_TILE, (qt + 1) * MXU_TILE)
        acc_s[idx, :, qcols] = alpha * acc_s[idx, :, qcols] + pv[0:GQA_HD, :]
        l_s[:, cols] = alpha * l_s[:, cols] + pv[GQA_HD:GQA_HD + 1, :]

    def key_block(j, s_cur, j_next, s_next):
        v_d = vdt_s[j]
        v_g = vgt_s[j]
        if j_next is not None:
            k_d = kd_s[key_rows(j_next), :]
            k_g = kg_s[key_rows(j_next), :]
        for idx in range(N_SCORE_HEADS):
            for qt in range(n_qt):
                if j_next is not None:
                    head_scores(idx, qt, k_d, k_g, s_next)
                head_update(idx, qt, v_d, v_g, s_cur)

    for idx in range(N_SCORE_HEADS):
        for qt in range(n_qt):
            head_scores(idx, qt, kd_s[key_rows(0), :], kg_s[key_rows(0), :], s0_s)
    if n_blocks > 1:
        assert n_blocks % 2 == 0

        def block_pair(i, carry):
            key_block(2 * i, s0_s, 2 * i + 1, s1_s)
            key_block(2 * i + 1, s1_s, 2 * i + 2, s0_s)
            return carry

        lax.fori_loop(0, n_blocks // 2 - 1, block_pair, 0)
        key_block(n_blocks - 2, s0_s, n_blocks - 1, s1_s)
        key_block(n_blocks - 1, s1_s, None, None)
    else:
        key_block(0, s0_s, None, None)

    lam = _diff_lambda(lam_ref, lam_init)
    outs = []
    for h in range(DIFF_HEADS):
        c0 = slice(2 * h * tq, (2 * h + 1) * tq)
        c1 = slice((2 * h + 1) * tq, (2 * h + 2) * tq)
        o = acc_s[2 * h] * (1.0 / l_s[:, c0]) - acc_s[2 * h + 1] * (lam / l_s[:, c1])
        ms = jnp.mean(o * o, axis=0, keepdims=True)
        outs.append(o * (lax.rsqrt(ms + EPS) * (1.0 - lam_init)))
    for h in range(GQA_Q_HEADS):
        idx = 2 * DIFF_HEADS + h
        outs.append(acc_s[idx] * (1.0 / l_s[:, idx * tq:(idx + 1) * tq]))
    o_ref[...] = jnp.concatenate(outs, axis=0).T * gain_ref[...]


def _attention(qa, kv, caches, layer, tables, lam_p, gain, gq_t, gk_t, lam_init, batch, seq, tq,
               gkn_prev=None):
    t = qa.shape[0]
    nq = seq // tq
    feature_major = gkn_prev is not None
    past = caches[0].shape[3] if caches is not None else 0
    kb = min(KEY_BLOCK, seq)
    assert seq % kb == 0 and past % kb == 0 and seq % tq == 0
    n_blocks = (seq + past) // kb
    n_all = N_SCORE_HEADS * tq
    full = lambda a: pl.BlockSpec(a.shape, lambda b, q: (0,) * a.ndim)
    token_major = lambda a: pl.BlockSpec((seq, a.shape[1]), lambda b, q: (b, 0))
    by_feature = lambda a: pl.BlockSpec((None, None, a.shape[2], seq), lambda b, q: (b, layer, 0, 0))
    dk, dv, gk, gv = kv
    in_specs = [pl.BlockSpec((tq, qa.shape[1]), lambda b, q: (b * nq + q, 0)),
                token_major(dk), by_feature(dv) if feature_major else token_major(dv),
                token_major(gk), by_feature(gv) if feature_major else token_major(gv)]
    args = [qa, dk, dv, gk, gv]
    aliases = {}
    if feature_major:
        assert nq == 1
        in_specs += [pl.BlockSpec(memory_space=pl.ANY)] * len(gkn_prev)
        args += list(gkn_prev)
        aliases = {5 + k: 1 + k for k in range(len(gkn_prev))}
    if caches is not None:
        in_specs += [pl.BlockSpec((None, None, a.shape[2], past), lambda b, q: (b, layer, 0, 0))
                     for a in caches]
        args += list(caches)
    if tables is not None:
        cosq, sinq, cosk, sink = tables
        in_specs += [pl.BlockSpec((tq, cosq.shape[1]), lambda b, q: (q, 0)),
                     pl.BlockSpec((tq, sinq.shape[1]), lambda b, q: (q, 0)),
                     full(cosk), full(sink)]
        args += [cosq, sinq, cosk, sink]
    in_specs += [full(lam_p), full(gain), full(gq_t), full(gk_t)]
    args += [lam_p, gain, gq_t, gk_t]
    out_specs = [pl.BlockSpec((tq, DIFF_W + GQA_W), lambda b, q: (b * nq + q, 0))]
    out_shape = [jax.ShapeDtypeStruct((t, DIFF_W + GQA_W), F32)]
    if feature_major:
        depth = dv.shape[1]
        if gkn_prev:
            out_specs.append(pl.BlockSpec((None, None, GQA_KV_W, seq), lambda b, q: (b, layer, 0, 0)))
        else:
            out_specs.append(pl.BlockSpec((None, depth, GQA_KV_W, seq), lambda b, q: (b, 0, 0, 0)))
        out_shape.append(jax.ShapeDtypeStruct((batch, depth, GQA_KV_W, seq), F32))
    cfg = (lam_init, seq, past, kb, tables is not None, feature_major,
           len(gkn_prev) if feature_major else 0, layer)
    return pl.pallas_call(
        functools.partial(_attn_kernel, cfg),
        grid=(batch, nq),
        in_specs=in_specs,
        out_specs=out_specs,
        out_shape=out_shape,
        input_output_aliases=aliases,
        scratch_shapes=[
            pltpu.VMEM((seq + past, DIFF_W), BF16), pltpu.VMEM((seq + past, GQA_KV_W), BF16),
            pltpu.VMEM((n_blocks, DIFF_W, kb), BF16), pltpu.VMEM((n_blocks, GQA_KV_W, kb), BF16),
            pltpu.VMEM((DIFF_W, 2 * DIFF_HEADS * tq), BF16), pltpu.VMEM((GQA_KV_W, GQA_Q_HEADS * tq), BF16),
            pltpu.VMEM((n_all // LANES, kb, LANES), F32), pltpu.VMEM((n_all // LANES, kb, LANES), F32),
            pltpu.VMEM((1, n_all), F32), pltpu.VMEM((1, n_all), F32),
            pltpu.VMEM((N_SCORE_HEADS, GQA_HD, tq), F32),
        ],
        compiler_params=pltpu.CompilerParams(
            dimension_semantics=("arbitrary", "arbitrary"), vmem_limit_bytes=VMEM_LIMIT),
    )(*args)


def _softplus(x):
    return jnp.maximum(x, 0.0) + jnp.log1p(jnp.exp(-jnp.abs(x)))


def _dot2_l(a, b_exact):
    a1 = a.astype(BF16)
    a2 = (a - a1.astype(F32)).astype(BF16)
    return _dot(a1, b_exact) + _dot(a2, b_exact)


def _ssd_kernel(has_init, n_cast, n_prev, seq, out_layer, *refs):
    it = iter(refs)
    z_ref, xbc_ref, dt_ref, cw_ref, cb_ref, alog_ref, dtb_ref, dexp_ref, ng_ref = (
        next(it) for _ in range(9))
    sf0_ref, sb0_ref = (next(it), next(it)) if has_init else (None, None)
    cast_in = [next(it) for _ in range(n_cast)]
    for _ in range(n_prev):
        next(it)
    o_ref, sf_ref, sb_ref = next(it), next(it), next(it)
    for src in cast_in:
        next(it)[...] = src[...].astype(BF16)
    stf_s, stb_s, inc_s, dec_s, eab_s, cbf_s, exp_s, tri_s, gm_s = it
    L = SSD_CHUNK
    W2 = 2 * SSD_INNER
    nchunks = seq // L
    halo = 8

    ri = lax.broadcasted_iota(jnp.int32, (L, L), 0)
    ci = lax.broadcasted_iota(jnp.int32, (L, L), 1)
    lower = ri >= ci
    upper = ri <= ci
    tri_s[0] = jnp.where(lower, 1.0, 0.0).astype(BF16)
    tri_s[1] = jnp.where(upper, 1.0, 0.0).astype(BF16)
    lane_row = lax.broadcasted_iota(jnp.int32, (1, LANES), 1)
    a_row = jnp.where(lane_row < DT_W, -jnp.exp(alog_ref[...]), 0.0)
    ej = lax.broadcasted_iota(jnp.int32, (LANES, W2), 0)
    eh = lax.shift_right_logical(lax.broadcasted_iota(jnp.int32, (LANES, W2), 1), 6)
    exp_s[...] = jnp.where(ej == eh, 1.0, 0.0).astype(BF16)
    gm_s[...] = jnp.where(lax.shift_right_logical(ej, 6) == (lax.shift_right_logical(eh, 2) & 1), 1.0, 0.0)

    def chunk_rows(c):
        return pl.ds(pl.multiple_of(c * L, L), L)

    def load_state(ref):
        r = lax.broadcasted_iota(jnp.int32, (SSD_STATE, LANES), 0)
        c = lax.broadcasted_iota(jnp.int32, (SSD_STATE, LANES), 1)
        dup = jnp.where((c & (SSD_STATE - 1)) == r, 1.0, 0.0).astype(BF16)
        return _dot3_l(ref[...], dup).T * gm_s[:, 0:SSD_INNER]

    def store_state(st_ref, ref):
        st_t = st_ref[...].T
        _put_layer_slot(ref, (), out_layer, (st_t + _swap_halves(st_t))[:, 0:SSD_STATE])

    stf_s[...] = load_state(sf0_ref) if has_init else jnp.zeros(stf_s.shape, F32)
    stb_s[...] = load_state(sb0_ref) if has_init else jnp.zeros(stb_s.shape, F32)

    def forward_pass(c, carry):
        r0 = c * L
        rows = chunk_rows(c)
        dt_c = _softplus(dt_ref[rows, :] + dtb_ref[...])

        d1, d2, d3 = _split3(dt_c * a_row)
        acs_f = _dot(tri_s[0], d1) + (_dot(tri_s[0], d2) + _dot(tri_s[0], d3))
        acs_b = _dot(tri_s[1], d1) + (_dot(tri_s[1], d2) + _dot(tri_s[1], d3))
        acs = jnp.where(lax.broadcasted_iota(jnp.int32, (L, LANES), 1) < SSD_HEADS, acs_f, acs_b)
        acs_t = acs.T
        expand = exp_s[...]
        dt_e = _dot2_l(dt_c, expand)
        acs_e = _dot3_l(acs, expand)
        edge = jnp.concatenate([acs_e[L - 1:L, 0:SSD_INNER], acs_e[0:1, SSD_INNER:W2]], axis=1)
        eacs = jnp.exp(acs_e)
        cdec = jnp.exp(edge)

        prev = xbc_ref[pl.ds(pl.multiple_of(jnp.maximum(r0 - halo, 0), halo), halo), :]
        nxt = xbc_ref[pl.ds(pl.multiple_of(jnp.minimum(r0 + L, seq - halo), halo), halo), :]
        cur = xbc_ref[rows, :]
        win = jnp.concatenate([jnp.where(c > 0, prev, 0.0), cur,
                               jnp.where(c < nchunks - 1, nxt, 0.0)], axis=0)
        acc = cb_ref[...] + cur * cw_ref[SSD_CONV // 2:SSD_CONV // 2 + 1, :]
        for j in range(SSD_CONV):
            if j != SSD_CONV // 2:
                shifted = pltpu.roll(win, (SSD_CONV // 2 - j) % (L + 2 * halo), 0)[halo:halo + L, :]
                acc = acc + shifted * cw_ref[j:j + 1, :]
        act = _silu(acc)
        x_c = act[:, 0:SSD_INNER]
        b_c = act[:, SSD_INNER:SSD_INNER + SSD_BC_W]
        c_c = act[:, SSD_INNER + SSD_BC_W:XBC_W]
        xd = jnp.concatenate([x_c, x_c], axis=1) * dt_e
        xd_b = xd.astype(BF16)
        xdw = (xd * jnp.exp(edge - acs_e)).astype(BF16)
        b_b = b_c.astype(BF16)
        c_b = c_c.astype(BF16)
        s_new = _dot(b_c.T.astype(BF16), xdw) * gm_s[...]

        st_f = stf_s[...]
        y = _dot(c_b, st_f.astype(BF16)) * eacs[:, 0:SSD_INNER]
        stf_s[...] = st_f * cdec[:, 0:SSD_INNER] + s_new[:, 0:SSD_INNER]
        inc_s[c] = s_new[:, SSD_INNER:W2]
        dec_s[c] = cdec[:, SSD_INNER:W2]
        eab_s[rows, :] = eacs[:, SSD_INNER:W2]
        cbf_s[rows, :] = c_b

        cbs = []
        for g in range(SSD_GROUPS):
            cg = jnp.where(_lane_mask(c_c.shape, g * SSD_STATE, (g + 1) * SSD_STATE), c_c, 0.0)
            cbs.append(_dot_nt(cg.astype(BF16), b_b))
        for direction, causal in ((0, lower), (1, upper)):
            pairs = []
            for g in range(SSD_GROUPS):
                cb = cbs[g]
                for hp in range(2):
                    pair = g * 2 + hp
                    res = []
                    for k in range(2):
                        j = direction * SSD_HEADS + pair * 2 + k
                        diff = acs[:, j:j + 1] - acs_t[j:j + 1, :]
                        dec = jnp.where(causal, jnp.exp(diff), 0.0)
                        sc = (cb * dec).astype(BF16)
                        lo = direction * SSD_INNER + pair * LANES
                        res.append(_dot(sc, xd_b[:, lo:lo + LANES]))
                    pairs.append(jnp.where(_lane_mask(res[0].shape, 0, SSD_HD), res[0], res[1]))
            y = y + jnp.concatenate(pairs, axis=-1)
        o_ref[rows, :] = y + x_c * dexp_ref[...]
        return carry

    lax.fori_loop(0, nchunks, forward_pass, 0, unroll=2)
    store_state(stf_s, sf_ref)

    def backward_pass(i, carry):
        c = nchunks - 1 - i
        rows = chunk_rows(c)
        st_b = stb_s[...]
        y = o_ref[rows, :] + _dot(cbf_s[rows, :], st_b.astype(BF16)) * eab_s[rows, :]
        stb_s[...] = st_b * dec_s[c] + inc_s[c]
        yt = y * _silu(z_ref[rows, :])
        ms = jnp.mean(yt * yt, axis=-1, keepdims=True)
        o_ref[rows, :] = yt * lax.rsqrt(ms + EPS) * ng_ref[...]
        return carry

    lax.fori_loop(0, nchunks, backward_pass, 0, unroll=2)
    store_state(stb_s, sb_ref)


def _ssd(z, xbc, dt, params, init, layer, batch, seq, state_prev=(), depth=1, cast=()):
    cw, cb, alog, dtb, dexp, ng = params
    t = z.shape[0]
    has_init = init is not None
    nchunks = seq // SSD_CHUNK
    full = lambda a: pl.BlockSpec(a.shape, lambda b: (0,) * a.ndim)
    out_layer = layer if depth > 1 else 0
    if state_prev or depth == 1:
        st_spec = pl.BlockSpec((None, None, SSD_INNER, SSD_STATE), lambda b: (b, out_layer, 0, 0))
    else:
        st_spec = pl.BlockSpec((None, depth, SSD_INNER, SSD_STATE), lambda b: (b, 0, 0, 0))
    in_specs = [
        pl.BlockSpec((seq, SSD_INNER), lambda b: (b, 0)),
        pl.BlockSpec((seq, XBC_W), lambda b: (b, 0)),
        pl.BlockSpec((seq, LANES), lambda b: (b, 0)),
        full(cw), full(cb), full(alog), full(dtb), full(dexp), full(ng),
    ]
    args = [z, xbc, dt, cw, cb, alog, dtb, dexp, ng]
    if has_init:
        init_spec = pl.BlockSpec((None, None, SSD_INNER, SSD_STATE), lambda b: (b, layer, 0, 0))
        in_specs += [init_spec, init_spec]
        args += list(init)
    cast_in_specs, cast_out_specs, cast_out_shape = _cast_specs(cast, layer, batch)
    in_specs += cast_in_specs
    args += list(cast)
    aliases = {len(args) + k: 1 + k for k in range(len(state_prev))}
    in_specs += [pl.BlockSpec(memory_space=pl.ANY)] * len(state_prev)
    args += list(state_prev)
    st_shape = jax.ShapeDtypeStruct((batch, depth, SSD_INNER, SSD_STATE), F32)
    return pl.pallas_call(
        functools.partial(_ssd_kernel, has_init, len(cast), len(state_prev), seq, out_layer),
        grid=(batch,),
        in_specs=in_specs,
        out_specs=[pl.BlockSpec((seq, SSD_INNER), lambda b: (b, 0)), st_spec, st_spec] + cast_out_specs,
        out_shape=[jax.ShapeDtypeStruct((t, SSD_INNER), F32), st_shape, st_shape] + cast_out_shape,
        input_output_aliases=aliases,
        scratch_shapes=[
            pltpu.VMEM((LANES, SSD_INNER), F32), pltpu.VMEM((LANES, SSD_INNER), F32),
            pltpu.VMEM((nchunks, LANES, SSD_INNER), F32),
            pltpu.VMEM((nchunks, 1, SSD_INNER), F32),
            pltpu.VMEM((seq, SSD_INNER), F32),
            pltpu.VMEM((seq, SSD_BC_W), BF16),
            pltpu.VMEM((LANES, 2 * SSD_INNER), BF16),
            pltpu.VMEM((2, SSD_CHUNK, SSD_CHUNK), BF16),
            pltpu.VMEM((LANES, 2 * SSD_INNER), F32),
        ],
        compiler_params=pltpu.CompilerParams(
            dimension_semantics=("arbitrary",), vmem_limit_bytes=VMEM_LIMIT),
    )(*args)


def _post_kernel(alpha, d_ff, ff_chunk, x_ref, oa_ref, os_ref, g1_ref, sh2_ref, sc2_ref, g2_ref,
                 wo_ref, wfi_ref, wfo_ref, lng_ref, lnb_ref, y_ref):
    wa = oa_ref.shape[1]
    half = x_ref.shape[0] // 2
    bounds = list(range(0, d_ff, ff_chunk)) + [d_ff]
    n_chunks = len(bounds) - 1

    def mix_and_norm(r):
        o = (_dot(oa_ref[r, :].astype(BF16), wo_ref[0:wa, :])
             + _dot(os_ref[r, :].astype(BF16), wo_ref[wa:, :]))
        x1 = _layer_norm(alpha * x_ref[r, :] + g1_ref[...] * o, lng_ref[0:1, :], lnb_ref[0:1, :])
        return x1, (x1 * (1.0 + sc2_ref[...]) + sh2_ref[...]).astype(BF16)

    def up(h2, c):
        lo, hi = bounds[c], bounds[c + 1]
        return _dot(h2, wfi_ref[:, lo:hi]), _dot(h2, wfi_ref[:, d_ff + lo:d_ff + hi])

    def ffn_chunks(h2):
        f = jnp.zeros((half, x_ref.shape[1]), F32)
        g, u = up(h2, 0)
        for c in range(n_chunks):
            nxt = up(h2, c + 1) if c + 1 < n_chunks else None
            f = f + _dot((_silu(g) * u).astype(BF16), wfo_ref[bounds[c]:bounds[c + 1], :])
            if nxt is not None:
                g, u = nxt
            yield f

    def finish(r, x1, f):
        y_ref[r, :] = _layer_norm(alpha * x1 + g2_ref[...] * f, lng_ref[1:2, :], lnb_ref[1:2, :])

    ra, rb = slice(0, half), slice(half, 2 * half)
    x1a, h2a = mix_and_norm(ra)
    ffn_a = ffn_chunks(h2a)
    for _ in range(n_chunks // 2):
        fa = next(ffn_a)
    x1b, h2b = mix_and_norm(rb)
    for fa in ffn_a:
        pass
    ffn_b = ffn_chunks(h2b)
    for _ in range(n_chunks // 2):
        fb = next(ffn_b)
    finish(ra, x1a, fa)
    for fb in ffn_b:
        pass
    finish(rb, x1b, fb)


def _post(x, oa, os_, mods, layer, row_fn, wo_bf, wfi_bf, wfo_bf, lng, lnb, alpha, tm):
    t, d = x.shape
    d_ff = wfo_bf.shape[0]
    ff_chunk = FF_CHUNK
    assert d_ff % MXU_TILE == 0 and ff_chunk % MXU_TILE == 0
    per_layer = lambda a: pl.BlockSpec((None,) + a.shape[1:], lambda i: (layer, 0, 0))
    resident = lambda a: pl.BlockSpec(a.shape, lambda i: (0, 0), pipeline_mode=pl.Buffered(1))
    return pl.pallas_call(
        functools.partial(_post_kernel, alpha, d_ff, ff_chunk),
        grid=(t // tm,),
        in_specs=[
            pl.BlockSpec((tm, d), lambda i: (i, 0)),
            pl.BlockSpec((tm, oa.shape[1]), lambda i: (i, 0)),
            pl.BlockSpec((tm, os_.shape[1]), lambda i: (i, 0)),
            _mod_spec(layer, 2, row_fn, d),
            _mod_spec(layer, 3, row_fn, d),
            _mod_spec(layer, 4, row_fn, d),
            _mod_spec(layer, 5, row_fn, d),
            resident(wo_bf), resident(wfi_bf), resident(wfo_bf),
            per_layer(lng), per_layer(lnb),
        ],
        out_specs=pl.BlockSpec((tm, d), lambda i: (i, 0)),
        out_shape=jax.ShapeDtypeStruct((t, d), F32),
        compiler_params=pltpu.CompilerParams(
            dimension_semantics=("arbitrary",), vmem_limit_bytes=VMEM_LIMIT),
    )(x, oa, os_, mods, mods, mods, mods, wo_bf, wfi_bf, wfo_bf, lng, lnb)


def _rope_tables(rows, dim, copies):
    row = jnp.repeat(jnp.arange(rows), GRID_W).astype(F32)
    col = jnp.tile(jnp.arange(GRID_W), rows).astype(F32)
    n_freq = dim // 4
    inv = ROPE_THETA ** (-jnp.arange(n_freq, dtype=F32) / n_freq)
    ang = jnp.concatenate([row[:, None] * inv, col[:, None] * inv], -1)
    cos, sin = jnp.cos(ang), jnp.sin(ang)
    cos_full = jnp.repeat(cos, 2, axis=-1)
    sin_signed = jnp.stack([-sin, sin], axis=-1).reshape(sin.shape[0], dim)
    return jnp.tile(cos_full, (1, copies)), jnp.tile(sin_signed, (1, copies))


def kernel(x_prompt, x_sample, cache_diff_k, cache_diff_v, cache_gqa_k, cache_gqa_v, state_ssd_fwd, state_ssd_bwd, c, c_ctx, w_ada, b_ada, w_in, w_out, diff_lambda, diff_subln_g, qk_norm_g, ssd_conv_w, ssd_conv_b, ssd_A_log, ssd_dt_bias, ssd_D, ssd_norm_g, ln_g, ln_b, w_ffn_in, w_ffn_out):
    batch, seq, d = x_prompt.shape
    dec_batch, dec_seq, _ = x_sample.shape
    depth = w_in.shape[0]
    past = cache_diff_k.shape[2]
    alpha = (2 * depth) ** 0.25
    rows = dec_seq // GRID_W

    n_vec = 1 + dec_batch
    n_pad = -(-n_vec // 8) * 8
    cvec = jnp.concatenate([c_ctx[None, :], c, jnp.zeros((n_pad - n_vec, d), F32)], axis=0)
    mods = _modulation(cvec, w_ada, b_ada).reshape(depth, n_pad, 1, 6 * d)

    cos_d, sin_d = _rope_tables(rows, DIFF_QK, DIFF_W // DIFF_QK)
    cos_g, sin_g = _rope_tables(rows, GQA_HD, GQA_Q_HEADS)
    cosq = jnp.concatenate([cos_d, cos_g], axis=-1)
    sinq = jnp.concatenate([sin_d, sin_g], axis=-1)
    cosk = jnp.concatenate([cos_d, cos_g[:, :GQA_KV_W]], axis=-1)
    sink = jnp.concatenate([sin_d, sin_g[:, :GQA_KV_W]], axis=-1)
    tables = (cosq, sinq, cosk, sink)

    feature_major = lambda a, w: jnp.transpose(a.reshape(dec_batch, depth, past, w), (0, 1, 3, 2))
    caches = (feature_major(cache_diff_k, DIFF_W), feature_major(cache_diff_v, DIFF_W),
              feature_major(cache_gqa_k, GQA_KV_W), feature_major(cache_gqa_v, GQA_KV_W))

    xp = x_prompt.reshape(batch * seq, d)
    xs = x_sample.reshape(dec_batch * dec_seq, d)
    tm_ctx = min(ROW_TILE, batch * seq)
    tm_lat = min(ROW_TILE, dec_seq)
    tm_post_ctx = min(POST_ROW_TILE, batch * seq)
    tm_post_lat = min(POST_ROW_TILE, dec_seq)
    tq = min(QUERY_TILE, dec_seq)
    ctx_row = lambda i: 0
    lat_row = lambda i: 1 + (i * tm_lat) // dec_seq
    lat_row_post = lambda i: 1 + (i * tm_post_lat) // dec_seq

    assert w_in.shape[2] == _C_END
    w_in_bf = w_in.astype(BF16)
    init = (state_ssd_fwd.reshape(dec_batch, depth, SSD_INNER, SSD_STATE),
            state_ssd_bwd.reshape(dec_batch, depth, SSD_INNER, SSD_STATE))

    kv_cache = ()
    gk_cache = ()
    ssd_states = ()
    for l in range(depth):
        lam_init = 0.8 - 0.6 * math.exp(-0.3 * l)
        lam_p = diff_lambda[l]
        attn_gain = jnp.concatenate([jnp.tile(diff_subln_g[l], DIFF_HEADS), jnp.ones((GQA_W,), F32)])[None, :]
        gq_t = jnp.tile(qk_norm_g[l, 0], GQA_Q_HEADS)[None, :]
        gk_t = jnp.tile(qk_norm_g[l, 1], GQA_KV_HEADS)[None, :]
        pad_row = lambda v: jnp.pad(v.reshape(1, -1), ((0, 0), (0, LANES - v.size)))
        ssd_params = (ssd_conv_w[l], ssd_conv_b[l][None, :], pad_row(ssd_A_log[l]),
                      pad_row(ssd_dt_bias[l]), jnp.repeat(ssd_D[l], SSD_HD)[None, :],
                      ssd_norm_g[l][None, :])

        qa, dk, gk, z, xbc, dt, *kv_cache = _inproj(xp, mods, l, ctx_row, w_in_bf, tm_ctx,
                                                    cache_prev=tuple(kv_cache), depth=depth, seq=seq)
        oa, gkn = _attention(qa, (dk, kv_cache[1], gk, kv_cache[2]), None, l, None, lam_p, attn_gain,
                             gq_t, gk_t, lam_init, batch, seq, seq, gkn_prev=gk_cache)
        gk_cache = (gkn,)
        os_, *rest = _ssd(z, xbc, dt, ssd_params, None, l, batch, seq,
                          state_prev=tuple(ssd_states), depth=depth,
                          cast=(w_out, w_ffn_in, w_ffn_out))
        ssd_states, (wo_bf, wfi_bf, wfo_bf) = rest[:2], rest[2:]
        xp = _post(xp, oa, os_, mods, l, ctx_row, wo_bf, wfi_bf, wfo_bf, ln_g, ln_b, alpha, tm_post_ctx)

        qa, dk, dv, gk, gv, z, xbc, dt = _inproj(xs, mods, l, lat_row, w_in_bf, tm_lat)
        (oa,) = _attention(qa, (dk, dv, gk, gv), caches, l, tables, lam_p, attn_gain, gq_t, gk_t,
                           lam_init, dec_batch, dec_seq, tq)
        os_, _, _ = _ssd(z, xbc, dt, ssd_params, init, l, dec_batch, dec_seq)
        xs = _post(xs, oa, os_, mods, l, lat_row_post, wo_bf, wfi_bf, wfo_bf, ln_g, ln_b, alpha,
                   tm_post_lat)

    def token_major(a, heads, width):
        return jnp.transpose(a.reshape(batch, depth, heads, width, seq), (0, 1, 4, 2, 3))

    state = lambda a: a.reshape(batch, depth, SSD_HEADS, SSD_HD, SSD_STATE)
    return (xp.reshape(batch, seq, d), xs.reshape(dec_batch, dec_seq, d),
            token_major(kv_cache[0], DIFF_HEADS, 2 * DIFF_QK), token_major(kv_cache[1], DIFF_HEADS, DIFF_V),
            token_major(gk_cache[0], GQA_KV_HEADS, GQA_HD), token_major(kv_cache[2], GQA_KV_HEADS, GQA_HD),
            state(ssd_states[0]), state(ssd_states[1]))
```

```python
import functools
import math

import jax
import jax.numpy as jnp
from jax import lax
from jax.experimental import pallas as pl
from jax.experimental.pallas import tpu as pltpu

F32 = jnp.float32
BF16 = jnp.bfloat16

GRID_W = 64
DIFF_HEADS = 4
DIFF_QK = 32
DIFF_V = 64
DIFF_W = DIFF_HEADS * DIFF_V
GQA_HD = 64
GQA_Q_HEADS = 4
GQA_KV_HEADS = 2
GQA_W = GQA_Q_HEADS * GQA_HD
GQA_KV_W = GQA_KV_HEADS * GQA_HD
SSD_HD = 64
SSD_HEADS = 8
SSD_INNER = SSD_HEADS * SSD_HD
SSD_GROUPS = 2
SSD_STATE = 64
SSD_BC_W = SSD_GROUPS * SSD_STATE
SSD_CONV = 5
SSD_CHUNK = 128
XBC_W = SSD_INNER + 2 * SSD_BC_W
DT_W = 2 * SSD_HEADS
ROPE_THETA = 10000.0
EPS = 1e-5
LANES = 128
MXU_TILE = 256
VMEM_LIMIT = 56 * 1024 * 1024

ROW_TILE = 512
POST_ROW_TILE = 1024
QUERY_TILE = 512
KEY_BLOCK = 256
MOD_K_TILE = 256
FF_CHUNK = 3 * MXU_TILE

_C_DQ, _C_DK, _C_DV, _C_GQ, _C_GK, _C_GV, _C_Z, _C_XBC, _C_DT, _C_END = (
    0, 256, 512, 768, 1024, 1152, 1280, 1792, 2560, 2576)


def _dot(a, b):
    return jnp.dot(a, b, preferred_element_type=F32)


def _dot_nt(a, b):
    return lax.dot_general(a, b, (((1,), (1,)), ((), ())), preferred_element_type=F32)


def _split3(a):
    a1 = a.astype(BF16)
    r1 = a - a1.astype(F32)
    a2 = r1.astype(BF16)
    a3 = (r1 - a2.astype(F32)).astype(BF16)
    return a1, a2, a3


def _dot3_l(a, b_exact):
    a1, a2, a3 = _split3(a)
    return _dot(a1, b_exact) + (_dot(a2, b_exact) + _dot(a3, b_exact))


def _sigmoid(x):
    return 1.0 / (1.0 + jnp.exp(-x))


def _silu(x):
    return x * _sigmoid(x)


def _layer_norm(x, g, b):
    mu = jnp.mean(x, axis=-1, keepdims=True)
    xc = x - mu
    var = jnp.mean(xc * xc, axis=-1, keepdims=True)
    return xc * lax.rsqrt(var + EPS) * g + b


def _group_avg_matrix(width, group):
    sh = int(math.log2(group))
    r = lax.shift_right_logical(lax.broadcasted_iota(jnp.int32, (width, width), 0), sh)
    c = lax.shift_right_logical(lax.broadcasted_iota(jnp.int32, (width, width), 1), sh)
    return jnp.where(r == c, 1.0 / group, 0.0).astype(BF16)


def _group_mean_sq(x, gmat):
    xx = x * x
    hi = xx.astype(BF16)
    lo = (xx - hi.astype(F32)).astype(BF16)
    return _dot(hi, gmat) + _dot(lo, gmat)


def _rope(x, cos, sin_signed):
    w = x.shape[-1]
    lane = lax.broadcasted_iota(jnp.int32, x.shape, 1)
    nxt = pltpu.roll(x, w - 1, 1)
    prv = pltpu.roll(x, 1, 1)
    partner = jnp.where((lane & 1) == 0, nxt, prv)
    return x * cos + partner * sin_signed


def _lane_mask(shape, lo, hi):
    lane = lax.broadcasted_iota(jnp.int32, shape, 1)
    return (lane >= lo) & (lane < hi)


def _mod_kernel(c_ref, w_ref, b_ref, o_ref):
    @pl.when(pl.program_id(1) == 0)
    def _start_from_bias():
        o_ref[...] = jnp.broadcast_to(b_ref[...], o_ref.shape)

    a = _silu(c_ref[...])
    rows = a.shape[0]
    a_hi = a.astype(BF16)
    a_hi_f = a_hi.astype(F32)
    w = w_ref[...]
    w_hi = w.astype(BF16)
    w_lo = (w - w_hi.astype(F32)).astype(BF16)
    both = _dot(jnp.concatenate([a_hi_f, a - a_hi_f], axis=0).astype(BF16), w_hi)
    o_ref[...] += both[0:rows] + (both[rows:2 * rows] + _dot(a_hi, w_lo))


def _modulation(cvec, w_ada, b_ada):
    depth, d, n = w_ada.shape
    tk = MOD_K_TILE
    rows = cvec.shape[0]
    return pl.pallas_call(
        _mod_kernel,
        grid=(depth, d // tk),
        in_specs=[
            pl.BlockSpec((rows, tk), lambda l, k: (0, k)),
            pl.BlockSpec((None, tk, n), lambda l, k: (l, k, 0)),
            pl.BlockSpec((None, 1, n), lambda l, k: (l, 0, 0)),
        ],
        out_specs=pl.BlockSpec((None, rows, n), lambda l, k: (l, 0, 0)),
        out_shape=jax.ShapeDtypeStruct((depth, rows, n), F32),
        compiler_params=pltpu.CompilerParams(
            dimension_semantics=("arbitrary", "arbitrary"), vmem_limit_bytes=VMEM_LIMIT),
    )(cvec, w_ada, b_ada.reshape(depth, 1, n))


def _put_layer_slot(ref, lead, layer, val):
    if len(ref.shape) == val.ndim + len(lead):
        ref[lead if lead else ...] = val
    else:
        for l in range(ref.shape[len(lead)]):
            ref[lead + (l,)] = val if l == layer else jnp.zeros_like(val)


def _inproj_kernel(feature_major_cache, n_cast, n_prev, layer, *refs):
    x_ref, sh_ref, sc_ref, w_ref = refs[:4]
    cast_in = refs[4:4 + n_cast]
    outs = refs[4 + n_cast + n_prev:]
    if n_cast:
        cast_out = outs[-n_cast - 2:-2] if feature_major_cache else outs[-n_cast:]
        outs = outs[:-n_cast - 2] + outs[-2:] if feature_major_cache else outs[:-n_cast]
        for src, dst in zip(cast_in, cast_out):
            dst[...] = src[...].astype(BF16)
    h = (x_ref[...] * (1.0 + sc_ref[...]) + sh_ref[...]).astype(BF16)

    def mm(lo, hi):
        return _dot(h, w_ref[:, lo:hi])

    gkv = mm(_C_GK, _C_Z)
    if feature_major_cache:
        (qa_ref, dk_ref, gk_ref, z_ref, xbc_ref, dt_ref, dkt_ref, dvt_ref, gvt_ref,
         dv_s, gv_s) = outs
        dk_ref[...] = mm(_C_DK, _C_DV)
        dv_s[...] = mm(_C_DV, _C_GQ)
        gv_s[...] = gkv[:, GQA_KV_W:]
        seq = dkt_ref.shape[-1]
        for j in range(dkt_ref.shape[0]):
            rows = slice(j * seq, (j + 1) * seq)
            _put_layer_slot(dkt_ref, (j,), layer, dk_ref[rows, :].T)
            _put_layer_slot(dvt_ref, (j,), layer, dv_s[rows, :].T)
            _put_layer_slot(gvt_ref, (j,), layer, gv_s[rows, :].T)
    else:
        qa_ref, dk_ref, dv_ref, gk_ref, gv_ref, z_ref, xbc_ref, dt_ref = outs
        dk_ref[...] = mm(_C_DK, _C_DV)
        dv_ref[...] = mm(_C_DV, _C_GQ)
        gv_ref[...] = gkv[:, GQA_KV_W:]
    qa_ref[:, 0:DIFF_W] = mm(_C_DQ, _C_DK)
    qa_ref[:, DIFF_W:DIFF_W + GQA_W] = mm(_C_GQ, _C_GK)
    gk_ref[...] = gkv[:, 0:GQA_KV_W]
    z_ref[...] = mm(_C_Z, _C_XBC)
    xbc_ref[...] = mm(_C_XBC, _C_DT)
    dt_ref[...] = jnp.zeros(dt_ref.shape, F32)
    dt_ref[:, 0:DT_W] = mm(_C_DT, _C_END)


def _cast_specs(cast, layer, steps):
    in_specs, out_specs, out_shape = [], [], []
    for w in cast:
        hold = 1
        while (w.shape[1] * hold) % steps or (w.shape[1] * hold // steps) % 16:
            hold *= 2
        rows = w.shape[1] * hold // steps
        in_specs.append(pl.BlockSpec((None, rows, w.shape[2]), lambda i, hold=hold: (layer, i // hold, 0)))
        out_specs.append(pl.BlockSpec((rows, w.shape[2]), lambda i, hold=hold: (i // hold, 0)))
        out_shape.append(jax.ShapeDtypeStruct(w.shape[1:], BF16))
    return in_specs, out_specs, out_shape


def _mod_spec(layer, which, row_fn, d):
    return pl.BlockSpec((None, None, 1, d), lambda i: (layer, row_fn(i), 0, which))


def _inproj(x, mods, layer, row_fn, w_in_bf, tm, cache_prev=None, depth=None, seq=None, cast=()):
    t, d = x.shape
    steps = t // tm
    feature_major = cache_prev is not None
    token_spec = lambda w: pl.BlockSpec((tm, w), lambda i: (i, 0))
    token_shape = lambda w: jax.ShapeDtypeStruct((t, w), F32)
    if feature_major:
        widths = (DIFF_W + GQA_W, DIFF_W, GQA_KV_W, SSD_INNER, XBC_W, LANES)
        cache_w = (DIFF_W, DIFF_W, GQA_KV_W)
        assert tm % seq == 0
        if cache_prev:
            cache_spec = lambda w: pl.BlockSpec((tm // seq, None, w, seq), lambda i: (i, layer, 0, 0))
        else:
            cache_spec = lambda w: pl.BlockSpec((tm // seq, depth, w, seq), lambda i: (i, 0, 0, 0))
        out_specs = [token_spec(w) for w in widths] + [cache_spec(w) for w in cache_w]
        out_shape = [token_shape(w) for w in widths] + [
            jax.ShapeDtypeStruct((t // seq, depth, w, seq), F32) for w in cache_w]
        aliases = {4 + k: len(widths) + k for k in range(len(cache_prev))}
    else:
        widths = (DIFF_W + GQA_W, DIFF_W, DIFF_W, GQA_KV_W, GQA_KV_W, SSD_INNER, XBC_W, LANES)
        out_specs = [token_spec(w) for w in widths]
        out_shape = [token_shape(w) for w in widths]
        cache_prev, aliases = (), {}
    aliases = {k + len(cast): v for k, v in aliases.items()}
    cast_in_specs, cast_out_specs, cast_out_shape = _cast_specs(cast, layer, steps)
    return pl.pallas_call(
        functools.partial(_inproj_kernel, feature_major, len(cast), len(cache_prev), layer),
        grid=(steps,),
        in_specs=[
            pl.BlockSpec((tm, d), lambda i: (i, 0)),
            _mod_spec(layer, 0, row_fn, d),
            _mod_spec(layer, 1, row_fn, d),
            pl.BlockSpec((None, d, _C_END), lambda i: (layer, 0, 0)),
        ] + cast_in_specs + [pl.BlockSpec(memory_space=pl.ANY)] * len(cache_prev),
        out_specs=out_specs + cast_out_specs,
        out_shape=out_shape + cast_out_shape,
        input_output_aliases=aliases,
        scratch_shapes=([pltpu.VMEM((tm, DIFF_W), F32), pltpu.VMEM((tm, GQA_KV_W), F32)]
                        if feature_major else []),
        compiler_params=pltpu.CompilerParams(
            dimension_semantics=("arbitrary",), vmem_limit_bytes=VMEM_LIMIT),
    )(x, mods, mods, w_in_bf, *cast, *cache_prev)


LOG2E = 1.4426950408889634
NEG_BIG = -1e30
N_SCORE_HEADS = 2 * DIFF_HEADS + GQA_Q_HEADS
ONES_ROWS = 16


def _diff_lambda(lam_ref, lam_init):
    lp = lam_ref[...]
    s1 = jnp.sum(lp[0:1, :] * lp[1:2, :], axis=-1, keepdims=True)
    s2 = jnp.sum(lp[2:3, :] * lp[3:4, :], axis=-1, keepdims=True)
    return jnp.exp(s1) - jnp.exp(s2) + lam_init


def _swap_halves(x):
    return pltpu.roll(x, GQA_HD, 1)


def _attn_kernel(cfg, *refs):
    lam_init, seq, past, kb, rope, feature_major, n_prev, layer = cfg
    it = iter(refs)
    qa_ref, dk_ref, dv_ref, gk_in_ref, gv_ref = next(it), next(it), next(it), next(it), next(it)
    for _ in range(n_prev):
        next(it)
    if past:
        cdk_ref, cdv_ref, cgk_ref, cgv_ref = next(it), next(it), next(it), next(it)
    if rope:
        cosq_ref, sinq_ref, cosk_ref, sink_ref = next(it), next(it), next(it), next(it)
    lam_ref, gain_ref, gq_ref, gk_ref = next(it), next(it), next(it), next(it)
    o_ref = next(it)
    gkn_ref = next(it) if feature_major else None
    kd_s, kg_s, vdt_s, vgt_s, wq_s, wg_s, s0_s, s1_s, m_s, l_s, acc_s = it
    tq = qa_ref.shape[0]
    n_blocks = (seq + past) // kb

    @pl.when(pl.program_id(1) == 0)
    def _prepare_keys():
        gmat = _group_avg_matrix(GQA_KV_W, GQA_HD)
        for i in range(seq // kb):
            rows = slice(i * kb, (i + 1) * kb)
            dk = dk_ref[rows, :]
            gk = gk_in_ref[rows, :]
            gk = gk * lax.rsqrt(_group_mean_sq(gk, gmat) + EPS) * gk_ref[...]
            if feature_major:
                gk_t = gk.T
                if len(gkn_ref.shape) == 2:
                    gkn_ref[:, rows] = gk_t
                else:
                    for l in range(gkn_ref.shape[0]):
                        gkn_ref[l, :, rows] = gk_t if l == layer else jnp.zeros_like(gk_t)
            if rope:
                ck = cosk_ref[rows, :]
                sk = sink_ref[rows, :]
                dk = _rope(dk, ck[:, 0:DIFF_W], sk[:, 0:DIFF_W])
                gk = _rope(gk, ck[:, DIFF_W:DIFF_W + GQA_KV_W], sk[:, DIFF_W:DIFF_W + GQA_KV_W])
            kd_s[rows, :] = dk.astype(BF16)
            kg_s[rows, :] = gk.astype(BF16)
            if feature_major:
                vdt_s[i] = dv_ref[:, rows].astype(BF16)
                vgt_s[i] = gv_ref[:, rows].astype(BF16)
            else:
                vdt_s[i] = dv_ref[rows, :].T.astype(BF16)
                vgt_s[i] = gv_ref[rows, :].T.astype(BF16)
        wq_s[...] = jnp.zeros(wq_s.shape, BF16)
        wg_s[...] = jnp.zeros(wg_s.shape, BF16)
        for j in range(past // kb):
            src = slice(j * kb, (j + 1) * kb)
            dst = slice(seq + j * kb, seq + (j + 1) * kb)
            kd_s[dst, :] = cdk_ref[:, src].T.astype(BF16)
            kg_s[dst, :] = cgk_ref[:, src].T.astype(BF16)
            vdt_s[seq // kb + j] = cdv_ref[:, src].astype(BF16)
            vgt_s[seq // kb + j] = cgv_ref[:, src].astype(BF16)

    qa = qa_ref[...]
    qd = qa[:, 0:DIFF_W]
    gq = qa[:, DIFF_W:DIFF_W + GQA_W]
    gq = gq * lax.rsqrt(_group_mean_sq(gq, _group_avg_matrix(GQA_W, GQA_HD)) + EPS) * gq_ref[...]
    if rope:
        cq = cosq_ref[...]
        sq = sinq_ref[...]
        qd = _rope(qd, cq[:, 0:DIFF_W], sq[:, 0:DIFF_W])
        gq = _rope(gq, cq[:, DIFF_W:DIFF_W + GQA_W], sq[:, DIFF_W:DIFF_W + GQA_W])
    qd_t = (qd * (DIFF_QK ** -0.5 * LOG2E)).T
    gq_t = (gq * (GQA_HD ** -0.5 * LOG2E)).T
    for hm in range(2 * DIFF_HEADS):
        band = slice(hm * DIFF_QK, (hm + 1) * DIFF_QK)
        wq_s[band, hm * tq:(hm + 1) * tq] = qd_t[band, :].astype(BF16)
    for h in range(GQA_Q_HEADS):
        kvh = h // (GQA_Q_HEADS // GQA_KV_HEADS)
        wg_s[kvh * GQA_HD:(kvh + 1) * GQA_HD, h * tq:(h + 1) * tq] = (
            gq_t[h * GQA_HD:(h + 1) * GQA_HD, :].astype(BF16))
    m_s[...] = jnp.full(m_s.shape, NEG_BIG, F32)
    l_s[...] = jnp.zeros(l_s.shape, F32)
    acc_s[...] = jnp.zeros(acc_s.shape, F32)
    n_diff = 2 * DIFF_HEADS * tq
    n_all = N_SCORE_HEADS * tq

    def key_rows(j):
        return pl.ds(j * kb if isinstance(j, int) else pl.multiple_of(j * kb, kb), kb)

    n_qt = tq // MXU_TILE
    tile_slabs = MXU_TILE // LANES

    def head_scores(idx, qt, k_d, k_g, s_buf):
        lo = idx * tq + qt * MXU_TILE
        if idx < 2 * DIFF_HEADS:
            s = _dot(k_d, wq_s[:, lo:lo + MXU_TILE])
        else:
            s = _dot(k_g, wg_s[:, lo - n_diff:lo - n_diff + MXU_TILE])
        for k in range(tile_slabs):
            s_buf[lo // LANES + k] = s[:, k * LANES:(k + 1) * LANES]

    def head_update(idx, qt, v_d, v_g, s_buf):
        lo = idx * tq + qt * MXU_TILE
        ps, alphas = [], []
        for k in range(tile_slabs):
            c = lo // LANES + k
            cols = slice(c * LANES, (c + 1) * LANES)
            s = s_buf[c]
            m_old = m_s[:, cols]
            m_new = jnp.maximum(m_old, jnp.max(s, axis=0, keepdims=True))
            alphas.append(jnp.exp2(m_old - m_new))
            ps.append(jnp.exp2(s - m_new).astype(BF16))
            m_s[:, cols] = m_new
        if idx < 2 * DIFF_HEADS:
            vh = idx // 2
            v_t = v_d[vh * DIFF_V:(vh + 1) * DIFF_V, :]
        else:
            vh = (idx - 2 * DIFF_HEADS) // (GQA_Q_HEADS // GQA_KV_HEADS)
            v_t = v_g[vh * GQA_HD:(vh + 1) * GQA_HD, :]
        v_ext = jnp.concatenate([v_t, jnp.ones((ONES_ROWS, kb), BF16)], axis=0)
        alpha = jnp.concatenate(alphas, axis=1)
        pv = _dot(v_ext, jnp.concatenate(ps, axis=1))
        cols = slice(lo, lo + MXU_TILE)
        qcols = slice(qt * MXU_TILE, (qt + 1) * MXU_TILE)
        acc_s[idx, :, qcols] = alpha * acc_s[idx, :, qcols] + pv[0:GQA_HD, :]
        l_s[:, cols] = alpha * l_s[:, cols] + pv[GQA_HD:GQA_HD + 1, :]

    def key_block(j, s_cur, j_next, s_next):
        v_d = vdt_s[j]
        v_g = vgt_s[j]
        if j_next is not None:
            k_d = kd_s[key_rows(j_next), :]
            k_g = kg_s[key_rows(j_next), :]
        for idx in range(N_SCORE_HEADS):
            for qt in range(n_qt):
                if j_next is not None:
                    head_scores(idx, qt, k_d, k_g, s_next)
                head_update(idx, qt, v_d, v_g, s_cur)

    for idx in range(N_SCORE_HEADS):
        for qt in range(n_qt):
            head_scores(idx, qt, kd_s[key_rows(0), :], kg_s[key_rows(0), :], s0_s)
    if n_blocks > 1:
        assert n_blocks % 2 == 0

        def block_pair(i, carry):
            key_block(2 * i, s0_s, 2 * i + 1, s1_s)
            key_block(2 * i + 1, s1_s, 2 * i + 2, s0_s)
            return carry

        lax.fori_loop(0, n_blocks // 2 - 1, block_pair, 0, unroll=2)
        key_block(n_blocks - 2, s0_s, n_blocks - 1, s1_s)
        key_block(n_blocks - 1, s1_s, None, None)
    else:
        key_block(0, s0_s, None, None)

    lam = _diff_lambda(lam_ref, lam_init)
    outs = []
    for h in range(DIFF_HEADS):
        c0 = slice(2 * h * tq, (2 * h + 1) * tq)
        c1 = slice((2 * h + 1) * tq, (2 * h + 2) * tq)
        o = acc_s[2 * h] * (1.0 / l_s[:, c0]) - acc_s[2 * h + 1] * (lam / l_s[:, c1])
        ms = jnp.mean(o * o, axis=0, keepdims=True)
        outs.append(o * (lax.rsqrt(ms + EPS) * (1.0 - lam_init)))
    for h in range(GQA_Q_HEADS):
        idx = 2 * DIFF_HEADS + h
        outs.append(acc_s[idx] * (1.0 / l_s[:, idx * tq:(idx + 1) * tq]))
    o_ref[...] = jnp.concatenate(outs, axis=0).T * gain_ref[...]


def _attention(qa, kv, caches, layer, tables, lam_p, gain, gq_t, gk_t, lam_init, batch, seq, tq,
               gkn_prev=None):
    t = qa.shape[0]
    nq = seq // tq
    feature_major = gkn_prev is not None
    past = caches[0].shape[3] if caches is not None else 0
    kb = min(KEY_BLOCK, seq)
    assert seq % kb == 0 and past % kb == 0 and seq % tq == 0
    n_blocks = (seq + past) // kb
    n_all = N_SCORE_HEADS * tq
    full = lambda a: pl.BlockSpec(a.shape, lambda b, q: (0,) * a.ndim)
    token_major = lambda a: pl.BlockSpec((seq, a.shape[1]), lambda b, q: (b, 0))
    by_feature = lambda a: pl.BlockSpec((None, None, a.shape[2], seq), lambda b, q: (b, layer, 0, 0))
    dk, dv, gk, gv = kv
    in_specs = [pl.BlockSpec((tq, qa.shape[1]), lambda b, q: (b * nq + q, 0)),
                token_major(dk), by_feature(dv) if feature_major else token_major(dv),
                token_major(gk), by_feature(gv) if feature_major else token_major(gv)]
    args = [qa, dk, dv, gk, gv]
    aliases = {}
    if feature_major:
        assert nq == 1
        in_specs += [pl.BlockSpec(memory_space=pl.ANY)] * len(gkn_prev)
        args += list(gkn_prev)
        aliases = {5 + k: 1 + k for k in range(len(gkn_prev))}
    if caches is not None:
        in_specs += [pl.BlockSpec((None, None, a.shape[2], past), lambda b, q: (b, layer, 0, 0))
                     for a in caches]
        args += list(caches)
    if tables is not None:
        cosq, sinq, cosk, sink = tables
        in_specs += [pl.BlockSpec((tq, cosq.shape[1]), lambda b, q: (q, 0)),
                     pl.BlockSpec((tq, sinq.shape[1]), lambda b, q: (q, 0)),
                     full(cosk), full(sink)]
        args += [cosq, sinq, cosk, sink]
    in_specs += [full(lam_p), full(gain), full(gq_t), full(gk_t)]
    args += [lam_p, gain, gq_t, gk_t]
    out_specs = [pl.BlockSpec((tq, DIFF_W + GQA_W), lambda b, q: (b * nq + q, 0))]
    out_shape = [jax.ShapeDtypeStruct((t, DIFF_W + GQA_W), F32)]
    if feature_major:
        depth = dv.shape[1]
        if gkn_prev:
            out_specs.append(pl.BlockSpec((None, None, GQA_KV_W, seq), lambda b, q: (b, layer, 0, 0)))
        else:
            out_specs.append(pl.BlockSpec((None, depth, GQA_KV_W, seq), lambda b, q: (b, 0, 0, 0)))
        out_shape.append(jax.ShapeDtypeStruct((batch, depth, GQA_KV_W, seq), F32))
    cfg = (lam_init, seq, past, kb, tables is not None, feature_major,
           len(gkn_prev) if feature_major else 0, layer)
    return pl.pallas_call(
        functools.partial(_attn_kernel, cfg),
        grid=(batch, nq),
        in_specs=in_specs,
        out_specs=out_specs,
        out_shape=out_shape,
        input_output_aliases=aliases,
        scratch_shapes=[
            pltpu.VMEM((seq + past, DIFF_W), BF16), pltpu.VMEM((seq + past, GQA_KV_W), BF16),
            pltpu.VMEM((n_blocks, DIFF_W, kb), BF16), pltpu.VMEM((n_blocks, GQA_KV_W, kb), BF16),
            pltpu.VMEM((DIFF_W, 2 * DIFF_HEADS * tq), BF16), pltpu.VMEM((GQA_KV_W, GQA_Q_HEADS * tq), BF16),
            pltpu.VMEM((n_all // LANES, kb, LANES), F32), pltpu.VMEM((n_all // LANES, kb, LANES), F32),
            pltpu.VMEM((1, n_all), F32), pltpu.VMEM((1, n_all), F32),
            pltpu.VMEM((N_SCORE_HEADS, GQA_HD, tq), F32),
        ],
        compiler_params=pltpu.CompilerParams(
            dimension_semantics=("arbitrary", "arbitrary"), vmem_limit_bytes=VMEM_LIMIT),
    )(*args)


def _softplus(x):
    return jnp.maximum(x, 0.0) + jnp.log1p(jnp.exp(-jnp.abs(x)))


def _dot2_l(a, b_exact):
    a1 = a.astype(BF16)
    a2 = (a - a1.astype(F32)).astype(BF16)
    return _dot(a1, b_exact) + _dot(a2, b_exact)


def _ssd_kernel(has_init, n_cast, n_prev, seq, out_layer, *refs):
    it = iter(refs)
    z_ref, xbc_ref, dt_ref, cw_ref, cb_ref, alog_ref, dtb_ref, dexp_ref, ng_ref = (
        next(it) for _ in range(9))
    sf0_ref, sb0_ref = (next(it), next(it)) if has_init else (None, None)
    cast_in = [next(it) for _ in range(n_cast)]
    for _ in range(n_prev):
        next(it)
    o_ref, sf_ref, sb_ref = next(it), next(it), next(it)
    for src in cast_in:
        next(it)[...] = src[...].astype(BF16)
    stf_s, stb_s, inc_s, dec_s, eab_s, cbf_s, exp_s, tri_s, gm_s = it
    L = SSD_CHUNK
    W2 = 2 * SSD_INNER
    nchunks = seq // L
    halo = 8

    ri = lax.broadcasted_iota(jnp.int32, (L, L), 0)
    ci = lax.broadcasted_iota(jnp.int32, (L, L), 1)
    lower = ri >= ci
    upper = ri <= ci
    tri_s[0] = jnp.where(lower, 1.0, 0.0).astype(BF16)
    tri_s[1] = jnp.where(upper, 1.0, 0.0).astype(BF16)
    lane_row = lax.broadcasted_iota(jnp.int32, (1, LANES), 1)
    a_row = jnp.where(lane_row < DT_W, -jnp.exp(alog_ref[...]), 0.0)
    ej = lax.broadcasted_iota(jnp.int32, (LANES, W2), 0)
    eh = lax.shift_right_logical(lax.broadcasted_iota(jnp.int32, (LANES, W2), 1), 6)
    exp_s[...] = jnp.where(ej == eh, 1.0, 0.0).astype(BF16)
    gm_s[...] = jnp.where(lax.shift_right_logical(ej, 6) == (lax.shift_right_logical(eh, 2) & 1), 1.0, 0.0)

    def chunk_rows(c):
        return pl.ds(pl.multiple_of(c * L, L), L)

    def load_state(ref):
        r = lax.broadcasted_iota(jnp.int32, (SSD_STATE, LANES), 0)
        c = lax.broadcasted_iota(jnp.int32, (SSD_STATE, LANES), 1)
        dup = jnp.where((c & (SSD_STATE - 1)) == r, 1.0, 0.0).astype(BF16)
        return _dot3_l(ref[...], dup).T * gm_s[:, 0:SSD_INNER]

    def store_state(st_ref, ref):
        st_t = st_ref[...].T
        _put_layer_slot(ref, (), out_layer, (st_t + _swap_halves(st_t))[:, 0:SSD_STATE])

    stf_s[...] = load_state(sf0_ref) if has_init else jnp.zeros(stf_s.shape, F32)
    stb_s[...] = load_state(sb0_ref) if has_init else jnp.zeros(stb_s.shape, F32)

    def forward_pass(c, carry):
        r0 = c * L
        rows = chunk_rows(c)
        dt_c = _softplus(dt_ref[rows, :] + dtb_ref[...])

        d1, d2, d3 = _split3(dt_c * a_row)
        acs_f = _dot(tri_s[0], d1) + (_dot(tri_s[0], d2) + _dot(tri_s[0], d3))
        acs_b = _dot(tri_s[1], d1) + (_dot(tri_s[1], d2) + _dot(tri_s[1], d3))
        acs = jnp.where(lax.broadcasted_iota(jnp.int32, (L, LANES), 1) < SSD_HEADS, acs_f, acs_b)
        acs_t = acs.T
        expand = exp_s[...]
        dt_e = _dot2_l(dt_c, expand)
        acs_e = _dot3_l(acs, expand)
        edge = jnp.concatenate([acs_e[L - 1:L, 0:SSD_INNER], acs_e[0:1, SSD_INNER:W2]], axis=1)
        eacs = jnp.exp(acs_e)
        cdec = jnp.exp(edge)

        prev = xbc_ref[pl.ds(pl.multiple_of(jnp.maximum(r0 - halo, 0), halo), halo), :]
        nxt = xbc_ref[pl.ds(pl.multiple_of(jnp.minimum(r0 + L, seq - halo), halo), halo), :]
        cur = xbc_ref[rows, :]
        win = jnp.concatenate([jnp.where(c > 0, prev, 0.0), cur,
                               jnp.where(c < nchunks - 1, nxt, 0.0)], axis=0)
        acc = cb_ref[...] + cur * cw_ref[SSD_CONV // 2:SSD_CONV // 2 + 1, :]
        for j in range(SSD_CONV):
            if j != SSD_CONV // 2:
                shifted = pltpu.roll(win, (SSD_CONV // 2 - j) % (L + 2 * halo), 0)[halo:halo + L, :]
                acc = acc + shifted * cw_ref[j:j + 1, :]
        act = _silu(acc)
        x_c = act[:, 0:SSD_INNER]
        b_c = act[:, SSD_INNER:SSD_INNER + SSD_BC_W]
        c_c = act[:, SSD_INNER + SSD_BC_W:XBC_W]
        xd = jnp.concatenate([x_c, x_c], axis=1) * dt_e
        xd_b = xd.astype(BF16)
        xdw = (xd * jnp.exp(edge - acs_e)).astype(BF16)
        b_b = b_c.astype(BF16)
        c_b = c_c.astype(BF16)
        s_new = _dot(b_c.T.astype(BF16), xdw) * gm_s[...]

        st_f = stf_s[...]
        y = _dot(c_b, st_f.astype(BF16)) * eacs[:, 0:SSD_INNER]
        stf_s[...] = st_f * cdec[:, 0:SSD_INNER] + s_new[:, 0:SSD_INNER]
        inc_s[c] = s_new[:, SSD_INNER:W2]
        dec_s[c] = cdec[:, SSD_INNER:W2]
        eab_s[rows, :] = eacs[:, SSD_INNER:W2]
        cbf_s[rows, :] = c_b

        cbs = []
        for g in range(SSD_GROUPS):
            cg = jnp.where(_lane_mask(c_c.shape, g * SSD_STATE, (g + 1) * SSD_STATE), c_c, 0.0)
            cbs.append(_dot_nt(cg.astype(BF16), b_b))
        for direction, causal in ((0, lower), (1, upper)):
            pairs = []
            for g in range(SSD_GROUPS):
                cb = cbs[g]
                for hp in range(2):
                    pair = g * 2 + hp
                    res = []
                    for k in range(2):
                        j = direction * SSD_HEADS + pair * 2 + k
                        diff = acs[:, j:j + 1] - acs_t[j:j + 1, :]
                        dec = jnp.where(causal, jnp.exp(diff), 0.0)
                        sc = (cb * dec).astype(BF16)
                        lo = direction * SSD_INNER + pair * LANES
                        res.append(_dot(sc, xd_b[:, lo:lo + LANES]))
                    pairs.append(jnp.where(_lane_mask(res[0].shape, 0, SSD_HD), res[0], res[1]))
            y = y + jnp.concatenate(pairs, axis=-1)
        o_ref[rows, :] = y + x_c * dexp_ref[...]
        return carry

    lax.fori_loop(0, nchunks, forward_pass, 0, unroll=2)
    store_state(stf_s, sf_ref)

    def backward_pass(i, carry):
        c = nchunks - 1 - i
        rows = chunk_rows(c)
        st_b = stb_s[...]
        y = o_ref[rows, :] + _dot(cbf_s[rows, :], st_b.astype(BF16)) * eab_s[rows, :]
        stb_s[...] = st_b * dec_s[c] + inc_s[c]
        yt = y * _silu(z_ref[rows, :])
        ms = jnp.mean(yt * yt, axis=-1, keepdims=True)
        o_ref[rows, :] = yt * lax.rsqrt(ms + EPS) * ng_ref[...]
        return carry

    lax.fori_loop(0, nchunks, backward_pass, 0, unroll=2)
    store_state(stb_s, sb_ref)


def _ssd(z, xbc, dt, params, init, layer, batch, seq, state_prev=(), depth=1, cast=()):
    cw, cb, alog, dtb, dexp, ng = params
    t = z.shape[0]
    has_init = init is not None
    nchunks = seq // SSD_CHUNK
    full = lambda a: pl.BlockSpec(a.shape, lambda b: (0,) * a.ndim)
    out_layer = layer if depth > 1 else 0
    if state_prev or depth == 1:
        st_spec = pl.BlockSpec((None, None, SSD_INNER, SSD_STATE), lambda b: (b, out_layer, 0, 0))
    else:
        st_spec = pl.BlockSpec((None, depth, SSD_INNER, SSD_STATE), lambda b: (b, 0, 0, 0))
    in_specs = [
        pl.BlockSpec((seq, SSD_INNER), lambda b: (b, 0)),
        pl.BlockSpec((seq, XBC_W), lambda b: (b, 0)),
        pl.BlockSpec((seq, LANES), lambda b: (b, 0)),
        full(cw), full(cb), full(alog), full(dtb), full(dexp), full(ng),
    ]
    args = [z, xbc, dt, cw, cb, alog, dtb, dexp, ng]
    if has_init:
        init_spec = pl.BlockSpec((None, None, SSD_INNER, SSD_STATE), lambda b: (b, layer, 0, 0))
        in_specs += [init_spec, init_spec]
        args += list(init)
    cast_in_specs, cast_out_specs, cast_out_shape = _cast_specs(cast, layer, batch)
    in_specs += cast_in_specs
    args += list(cast)
    aliases = {len(args) + k: 1 + k for k in range(len(state_prev))}
    in_specs += [pl.BlockSpec(memory_space=pl.ANY)] * len(state_prev)
    args += list(state_prev)
    st_shape = jax.ShapeDtypeStruct((batch, depth, SSD_INNER, SSD_STATE), F32)
    return pl.pallas_call(
        functools.partial(_ssd_kernel, has_init, len(cast), len(state_prev), seq, out_layer),
        grid=(batch,),
        in_specs=in_specs,
        out_specs=[pl.BlockSpec((seq, SSD_INNER), lambda b: (b, 0)), st_spec, st_spec] + cast_out_specs,
        out_shape=[jax.ShapeDtypeStruct((t, SSD_INNER), F32), st_shape, st_shape] + cast_out_shape,
        input_output_aliases=aliases,
        scratch_shapes=[
            pltpu.VMEM((LANES, SSD_INNER), F32), pltpu.VMEM((LANES, SSD_INNER), F32),
            pltpu.VMEM((nchunks, LANES, SSD_INNER), F32),
            pltpu.VMEM((nchunks, 1, SSD_INNER), F32),
            pltpu.VMEM((seq, SSD_INNER), F32),
            pltpu.VMEM((seq, SSD_BC_W), BF16),
            pltpu.VMEM((LANES, 2 * SSD_INNER), BF16),
            pltpu.VMEM((2, SSD_CHUNK, SSD_CHUNK), BF16),
            pltpu.VMEM((LANES, 2 * SSD_INNER), F32),
        ],
        compiler_params=pltpu.CompilerParams(
            dimension_semantics=("arbitrary",), vmem_limit_bytes=VMEM_LIMIT),
    )(*args)


def _post_kernel(alpha, d_ff, ff_chunk, x_ref, oa_ref, os_ref, g1_ref, sh2_ref, sc2_ref, g2_ref,
                 wo_ref, wfi_ref, wfo_ref, lng_ref, lnb_ref, y_ref):
    wa = oa_ref.shape[1]
    half = x_ref.shape[0] // 2
    bounds = list(range(0, d_ff, ff_chunk)) + [d_ff]
    n_chunks = len(bounds) - 1

    def mix_and_norm(r):
        o = (_dot(oa_ref[r, :].astype(BF16), wo_ref[0:wa, :])
             + _dot(os_ref[r, :].astype(BF16), wo_ref[wa:, :]))
        x1 = _layer_norm(alpha * x_ref[r, :] + g1_ref[...] * o, lng_ref[0:1, :], lnb_ref[0:1, :])
        return x1, (x1 * (1.0 + sc2_ref[...]) + sh2_ref[...]).astype(BF16)

    def up(h2, c):
        lo, hi = bounds[c], bounds[c + 1]
        return _dot(h2, wfi_ref[:, lo:hi]), _dot(h2, wfi_ref[:, d_ff + lo:d_ff + hi])

    def ffn_chunks(h2):
        f = jnp.zeros((half, x_ref.shape[1]), F32)
        g, u = up(h2, 0)
        for c in range(n_chunks):
            nxt = up(h2, c + 1) if c + 1 < n_chunks else None
            f = f + _dot((_silu(g) * u).astype(BF16), wfo_ref[bounds[c]:bounds[c + 1], :])
            if nxt is not None:
                g, u = nxt
            yield f

    def finish(r, x1, f):
        y_ref[r, :] = _layer_norm(alpha * x1 + g2_ref[...] * f, lng_ref[1:2, :], lnb_ref[1:2, :])

    ra, rb = slice(0, half), slice(half, 2 * half)
    x1a, h2a = mix_and_norm(ra)
    ffn_a = ffn_chunks(h2a)
    for _ in range(n_chunks // 2):
        fa = next(ffn_a)
    x1b, h2b = mix_and_norm(rb)
    for fa in ffn_a:
        pass
    ffn_b = ffn_chunks(h2b)
    for _ in range(n_chunks // 2):
        fb = next(ffn_b)
    finish(ra, x1a, fa)
    for fb in ffn_b:
        pass
    finish(rb, x1b, fb)


def _post(x, oa, os_, mods, layer, row_fn, wo_bf, wfi_bf, wfo_bf, lng, lnb, alpha, tm):
    t, d = x.shape
    d_ff = wfo_bf.shape[0]
    ff_chunk = FF_CHUNK
    assert d_ff % MXU_TILE == 0 and ff_chunk % MXU_TILE == 0
    per_layer = lambda a: pl.BlockSpec((None,) + a.shape[1:], lambda i: (layer, 0, 0))
    resident = lambda a: pl.BlockSpec(a.shape, lambda i: (0, 0), pipeline_mode=pl.Buffered(1))
    return pl.pallas_call(
        functools.partial(_post_kernel, alpha, d_ff, ff_chunk),
        grid=(t // tm,),
        in_specs=[
            pl.BlockSpec((tm, d), lambda i: (i, 0)),
            pl.BlockSpec((tm, oa.shape[1]), lambda i: (i, 0)),
            pl.BlockSpec((tm, os_.shape[1]), lambda i: (i, 0)),
            _mod_spec(layer, 2, row_fn, d),
            _mod_spec(layer, 3, row_fn, d),
            _mod_spec(layer, 4, row_fn, d),
            _mod_spec(layer, 5, row_fn, d),
            resident(wo_bf), resident(wfi_bf), resident(wfo_bf),
            per_layer(lng), per_layer(lnb),
        ],
        out_specs=pl.BlockSpec((tm, d), lambda i: (i, 0)),
        out_shape=jax.ShapeDtypeStruct((t, d), F32),
        compiler_params=pltpu.CompilerParams(
            dimension_semantics=("arbitrary",), vmem_limit_bytes=VMEM_LIMIT),
    )(x, oa, os_, mods, mods, mods, mods, wo_bf, wfi_bf, wfo_bf, lng, lnb)


def _rope_tables(rows, dim, copies):
    row = jnp.repeat(jnp.arange(rows), GRID_W).astype(F32)
    col = jnp.tile(jnp.arange(GRID_W), rows).astype(F32)
    n_freq = dim // 4
    inv = ROPE_THETA ** (-jnp.arange(n_freq, dtype=F32) / n_freq)
    ang = jnp.concatenate([row[:, None] * inv, col[:, None] * inv], -1)
    cos, sin = jnp.cos(ang), jnp.sin(ang)
    cos_full = jnp.repeat(cos, 2, axis=-1)
    sin_signed = jnp.stack([-sin, sin], axis=-1).reshape(sin.shape[0], dim)
    return jnp.tile(cos_full, (1, copies)), jnp.tile(sin_signed, (1, copies))


def kernel(x_prompt, x_sample, cache_diff_k, cache_diff_v, cache_gqa_k, cache_gqa_v, state_ssd_fwd, state_ssd_bwd, c, c_ctx, w_ada, b_ada, w_in, w_out, diff_lambda, diff_subln_g, qk_norm_g, ssd_conv_w, ssd_conv_b, ssd_A_log, ssd_dt_bias, ssd_D, ssd_norm_g, ln_g, ln_b, w_ffn_in, w_ffn_out):
    batch, seq, d = x_prompt.shape
    dec_batch, dec_seq, _ = x_sample.shape
    depth = w_in.shape[0]
    past = cache_diff_k.shape[2]
    alpha = (2 * depth) ** 0.25
    rows = dec_seq // GRID_W

    n_vec = 1 + dec_batch
    n_pad = -(-n_vec // 8) * 8
    cvec = jnp.concatenate([c_ctx[None, :], c, jnp.zeros((n_pad - n_vec, d), F32)], axis=0)
    mods = _modulation(cvec, w_ada, b_ada).reshape(depth, n_pad, 1, 6 * d)

    cos_d, sin_d = _rope_tables(rows, DIFF_QK, DIFF_W // DIFF_QK)
    cos_g, sin_g = _rope_tables(rows, GQA_HD, GQA_Q_HEADS)
    cosq = jnp.concatenate([cos_d, cos_g], axis=-1)
    sinq = jnp.concatenate([sin_d, sin_g], axis=-1)
    cosk = jnp.concatenate([cos_d, cos_g[:, :GQA_KV_W]], axis=-1)
    sink = jnp.concatenate([sin_d, sin_g[:, :GQA_KV_W]], axis=-1)
    tables = (cosq, sinq, cosk, sink)

    feature_major = lambda a, w: jnp.transpose(a.reshape(dec_batch, depth, past, w), (0, 1, 3, 2))
    caches = (feature_major(cache_diff_k, DIFF_W), feature_major(cache_diff_v, DIFF_W),
              feature_major(cache_gqa_k, GQA_KV_W), feature_major(cache_gqa_v, GQA_KV_W))

    xp = x_prompt.reshape(batch * seq, d)
    xs = x_sample.reshape(dec_batch * dec_seq, d)
    tm_ctx = min(ROW_TILE, batch * seq)
    tm_lat = min(ROW_TILE, dec_seq)
    tm_post_ctx = min(POST_ROW_TILE, batch * seq)
    tm_post_lat = min(POST_ROW_TILE, dec_seq)
    tq = min(QUERY_TILE, dec_seq)
    ctx_row = lambda i: 0
    lat_row = lambda i: 1 + (i * tm_lat) // dec_seq
    lat_row_post = lambda i: 1 + (i * tm_post_lat) // dec_seq

    assert w_in.shape[2] == _C_END
    w_in_bf = w_in.astype(BF16)
    init = (state_ssd_fwd.reshape(dec_batch, depth, SSD_INNER, SSD_STATE),
            state_ssd_bwd.reshape(dec_batch, depth, SSD_INNER, SSD_STATE))

    kv_cache = ()
    gk_cache = ()
    ssd_states = ()
    for l in range(depth):
        lam_init = 0.8 - 0.6 * math.exp(-0.3 * l)
        lam_p = diff_lambda[l]
        attn_gain = jnp.concatenate([jnp.tile(diff_subln_g[l], DIFF_HEADS), jnp.ones((GQA_W,), F32)])[None, :]
        gq_t = jnp.tile(qk_norm_g[l, 0], GQA_Q_HEADS)[None, :]
        gk_t = jnp.tile(qk_norm_g[l, 1], GQA_KV_HEADS)[None, :]
        pad_row = lambda v: jnp.pad(v.reshape(1, -1), ((0, 0), (0, LANES - v.size)))
        ssd_params = (ssd_conv_w[l], ssd_conv_b[l][None, :], pad_row(ssd_A_log[l]),
                      pad_row(ssd_dt_bias[l]), jnp.repeat(ssd_D[l], SSD_HD)[None, :],
                      ssd_norm_g[l][None, :])

        qa, dk, gk, z, xbc, dt, *kv_cache = _inproj(xp, mods, l, ctx_row, w_in_bf, tm_ctx,
                                                    cache_prev=tuple(kv_cache), depth=depth, seq=seq)
        oa, gkn = _attention(qa, (dk, kv_cache[1], gk, kv_cache[2]), None, l, None, lam_p, attn_gain,
                             gq_t, gk_t, lam_init, batch, seq, seq, gkn_prev=gk_cache)
        gk_cache = (gkn,)
        os_, *rest = _ssd(z, xbc, dt, ssd_params, None, l, batch, seq,
                          state_prev=tuple(ssd_states), depth=depth,
                          cast=(w_out, w_ffn_in, w_ffn_out))
        ssd_states, (wo_bf, wfi_bf, wfo_bf) = rest[:2], rest[2:]
        xp = _post(xp, oa, os_, mods, l, ctx_row, wo_bf, wfi_bf, wfo_bf, ln_g, ln_b, alpha, tm_post_ctx)

        qa, dk, dv, gk, gv, z, xbc, dt = _inproj(xs, mods, l, lat_row, w_in_bf, tm_lat)
        (oa,) = _attention(qa, (dk, dv, gk, gv), caches, l, tables, lam_p, attn_gain, gq_t, gk_t,
                           lam_init, dec_batch, dec_seq, tq)
        os_, _, _ = _ssd(z, xbc, dt, ssd_params, init, l, dec_batch, dec_seq)
        xs = _post(xs, oa, os_, mods, l, lat_row_post, wo_bf, wfi_bf, wfo_bf, ln_g, ln_b, alpha,
                   tm_post_lat)

    def token_major(a, heads, width):
        return jnp.transpose(a.reshape(batch, depth, heads, width, seq), (0, 1, 4, 2, 3))

    state = lambda a: a.reshape(batch, depth, SSD_HEADS, SSD_HD, SSD_STATE)
    return (xp.reshape(batch, seq, d), xs.reshape(dec_batch, dec_seq, d),
            token_major(kv_cache[0], DIFF_HEADS, 2 * DIFF_QK), token_major(kv_cache[1], DIFF_HEADS, DIFF_V),
            token_major(gk_cache[0], GQA_KV_HEADS, GQA_HD), token_major(kv_cache[2], GQA_KV_HEADS, GQA_HD),
            state(ssd_states[0]), state(ssd_states[1]))
```
